```python
import jax, jax.numpy as jnp
from jax import lax
import numpy as np

D_MODEL = 1024
BATCH = 8
SEQ = 2048
DEPTH = 1

GRID_W = 64
NA_HEADS = 8
NA_HEAD_DIM = 64
NA_WIN_ROWS_MAX = 8
NA_WIN_COLS = 16
NA_WIDTH = NA_HEADS * NA_HEAD_DIM
MLA_HEADS = 8
MLA_Q_RANK = 256
MLA_KV_RANK = 128
MLA_NOPE_DIM = 64
MLA_ROPE_DIM = 32
MLA_V_DIM = 64
MLA_QK_DIM = MLA_NOPE_DIM + MLA_ROPE_DIM
MLA_WIDTH = MLA_HEADS * MLA_V_DIM
ROPE_THETA = 10000.0
ATTN_BLOCK = 128
IN_SPLITS = (NA_WIDTH, NA_WIDTH, NA_WIDTH, MLA_Q_RANK, MLA_KV_RANK, MLA_ROPE_DIM, D_MODEL, D_MODEL)
IN_COLS = 3 * NA_WIDTH + MLA_Q_RANK + MLA_KV_RANK + MLA_ROPE_DIM + 2 * D_MODEL
N_EXPERTS = 64
TOP_K = 8
N_GROUPS = 8
TOPK_GROUPS = 4
EXPERT_FF = 256
SHARED_FF = 256
ROUTED_SCALE = 2.5
MOE_CHUNK = 128
EPS = 1e-6
NEG_BIG = -1e30

kernel_name = "hybrid_na_mla_moe_encoder_block"


def rms_norm(x, g):
    xf = x.astype(jnp.float32)
    y = xf * lax.rsqrt(jnp.mean(xf * xf, axis=-1, keepdims=True) + EPS)
    return (y * g.astype(jnp.float32)).astype(x.dtype)


def apply_rope(x, positions):
    half = x.shape[-1] // 2
    inv_freq = ROPE_THETA ** (-jnp.arange(half, dtype=jnp.float32) / half)
    ang = positions.astype(jnp.float32)[:, :, None, None] * inv_freq
    cos, sin = jnp.cos(ang), jnp.sin(ang)
    xf = x.astype(jnp.float32)
    x1, x2 = xf[..., :half], xf[..., half:]
    return jnp.concatenate([x1 * cos - x2 * sin, x2 * cos + x1 * sin], axis=-1).astype(x.dtype)


def neighbourhood_attention(q, k, v, rpb):
    B, S, H, Dh = q.shape
    R = S // GRID_W
    kh = min(NA_WIN_ROWS_MAX, R)
    to_grid = lambda t: t.reshape(B, R, GRID_W, H, Dh).transpose(0, 3, 1, 2, 4)
    qg, kg, vg = to_grid(q), to_grid(k), to_grid(v)
    rows = jnp.arange(R)
    row_start = jnp.clip(rows - kh // 2, 0, R - kh)
    key_rows = row_start[:, None] + jnp.arange(kh)[None, :]
    kb = kg[:, :, key_rows]
    vb = vg[:, :, key_rows]
    s = jnp.einsum('bhrqd,bhrjkd->bhrqjk', qg, kb).astype(jnp.float32) * (Dh ** -0.5)
    cols = jnp.arange(GRID_W)
    col_start = jnp.clip(cols - NA_WIN_COLS // 2, 0, GRID_W - NA_WIN_COLS)
    in_win = (cols[None, :] >= col_start[:, None]) & (cols[None, :] < col_start[:, None] + NA_WIN_COLS)
    dr = key_rows - rows[:, None]
    dc = jnp.clip(cols[None, :] - cols[:, None], -(NA_WIN_COLS - 1), NA_WIN_COLS - 1)
    bias = rpb[:, dr[:, None, :, None] + (NA_WIN_ROWS_MAX - 1),
               dc[None, :, None, :] + (NA_WIN_COLS - 1)]
    s = s + bias.astype(jnp.float32)[None]
    s = jnp.where(in_win[:, None, :], s, NEG_BIG)
    p = jax.nn.softmax(s.reshape(B, H, R, GRID_W, kh * GRID_W), axis=-1)
    p = p.reshape(B, H, R, GRID_W, kh, GRID_W).astype(v.dtype)
    o = jnp.einsum('bhrqjk,bhrjkd->bhrqd', p, vb)
    return o.transpose(0, 2, 3, 1, 4).reshape(B, S, H * Dh)


def blocked_full_attention(q, k, v):
    B, S, H, Dqk = q.shape
    Dv = v.shape[-1]
    nb = S // ATTN_BLOCK
    qb = q.reshape(B, nb, ATTN_BLOCK, H, Dqk).transpose(1, 0, 2, 3, 4)
    scale = Dqk ** -0.5

    def one_block(q_blk):
        s = jnp.einsum('bqhd,bkhd->bhqk', q_blk, k).astype(jnp.float32) * scale
        p = jax.nn.softmax(s, axis=-1).astype(v.dtype)
        return jnp.einsum('bhqk,bkhd->bqhd', p, v)

    o = lax.map(one_block, qb)
    return o.transpose(1, 0, 2, 3, 4).reshape(B, S, H * Dv)


def moe_ffn(h, w_router, e_bias, w_eg, w_eu, w_ed, w_sg, w_su, w_sd):
    B, S, D = h.shape
    t = h.reshape(-1, D)
    T = t.shape[0]
    scores = jax.nn.sigmoid((t @ w_router).astype(jnp.float32))
    sel = scores + e_bias.astype(jnp.float32)
    grp = sel.reshape(T, N_GROUPS, N_EXPERTS // N_GROUPS)
    grp_score = jnp.sum(lax.top_k(grp, 2)[0], axis=-1)
    _, gidx = lax.top_k(grp_score, TOPK_GROUPS)
    gmask = jnp.sum(jax.nn.one_hot(gidx, N_GROUPS, dtype=jnp.float32), axis=-2)
    emask = jnp.repeat(gmask, N_EXPERTS // N_GROUPS, axis=-1)
    sel = jnp.where(emask > 0, sel, -jnp.inf)
    _, eidx = lax.top_k(sel, TOP_K)
    w = jnp.take_along_axis(scores, eidx, axis=-1)
    w = w / jnp.sum(w, axis=-1, keepdims=True) * ROUTED_SCALE
    gates = jnp.sum(jax.nn.one_hot(eidx, N_EXPERTS, dtype=jnp.float32) * w[..., None], axis=-2)
    tc = t.reshape(-1, MOE_CHUNK, D)
    gc = gates.reshape(-1, MOE_CHUNK, N_EXPERTS).astype(t.dtype)

    def expert_chunk(args):
        xc, gch = args
        hg = jnp.einsum('cd,edf->cef', xc, w_eg)
        hu = jnp.einsum('cd,edf->cef', xc, w_eu)
        a = jax.nn.silu(hg) * hu * gch[..., None]
        return jnp.einsum('cef,efd->cd', a, w_ed)

    routed = lax.map(expert_chunk, (tc, gc)).reshape(T, D)
    shared = (jax.nn.silu(t @ w_sg) * (t @ w_su)) @ w_sd
    return (routed + shared).reshape(B, S, D)


def setup_inputs(seed: int = 0) -> dict:
    key = jax.random.key(seed)
    ks = iter(jax.random.split(key, 32))
    f32 = jnp.float32
    nrm = lambda shape, s: jax.random.normal(next(ks), shape, f32) * s
    gain = lambda shape: 1.0 + 0.05 * jax.random.normal(next(ks), shape, f32)
    L, D, E, F = DEPTH, D_MODEL, N_EXPERTS, EXPERT_FF
    x = jax.random.normal(next(ks), (BATCH, SEQ, D), f32)
    c = jax.random.normal(next(ks), (BATCH, D), f32)
    positions = jnp.broadcast_to(jnp.arange(SEQ, dtype=jnp.int32)[None, :], (BATCH, SEQ))
    return {
        "x": x,
        "c": c,
        "positions": positions,
        "w_ada": nrm((L, D, 6 * D), 0.3 * D ** -0.5),
        "b_ada": nrm((L, 6 * D), 0.02),
        "g_norm1": gain((L, D)),
        "w_in": nrm((L, D, IN_COLS), D ** -0.5),
        "g_na_q": gain((L, NA_HEAD_DIM)),
        "g_na_k": gain((L, NA_HEAD_DIM)),
        "na_rpb": nrm((L, NA_HEADS, 2 * NA_WIN_ROWS_MAX - 1, 2 * NA_WIN_COLS - 1), 0.1),
        "g_q_lat": gain((L, MLA_Q_RANK)),
        "w_uq": nrm((L, MLA_Q_RANK, MLA_HEADS * MLA_QK_DIM), MLA_Q_RANK ** -0.5),
        "g_kv_lat": gain((L, MLA_KV_RANK)),
        "w_ukv": nrm((L, MLA_KV_RANK, MLA_HEADS * (MLA_NOPE_DIM + MLA_V_DIM)), MLA_KV_RANK ** -0.5),
        "g_mla_q": gain((L, MLA_QK_DIM)),
        "g_mla_k": gain((L, MLA_QK_DIM)),
        "w_proj_na": nrm((L, NA_WIDTH, D), NA_WIDTH ** -0.5),
        "w_proj_mla": nrm((L, MLA_WIDTH, D), MLA_WIDTH ** -0.5),
        "w_out": nrm((L, D, D), D ** -0.5),
        "g_norm2": gain((L, D)),
        "w_router": nrm((L, D, E), D ** -0.5),
        "e_bias": nrm((L, E), 0.01),
        "w_exp_gate": nrm((L, E, D, F), D ** -0.5),
        "w_exp_up": nrm((L, E, D, F), D ** -0.5),
        "w_exp_down": nrm((L, E, F, D), F ** -0.5),
        "w_sh_gate": nrm((L, D, SHARED_FF), D ** -0.5),
        "w_sh_up": nrm((L, D, SHARED_FF), D ** -0.5),
        "w_sh_down": nrm((L, SHARED_FF, D), SHARED_FF ** -0.5),
    }


def reference(x, c, positions, w_ada, b_ada, g_norm1, w_in, g_na_q, g_na_k, na_rpb,
              g_q_lat, w_uq, g_kv_lat, w_ukv, g_mla_q, g_mla_k, w_proj_na, w_proj_mla,
              w_out, g_norm2, w_router, e_bias, w_exp_gate, w_exp_up, w_exp_down,
              w_sh_gate, w_sh_up, w_sh_down):
    B, S, D = x.shape
    split_at = list(np.cumsum(IN_SPLITS)[:-1])
    for l in range(DEPTH):
        mod = jax.nn.silu(c) @ w_ada[l] + b_ada[l]
        shift1, scale1, gate1, shift2, scale2, gate2 = jnp.split(mod[:, None, :], 6, axis=-1)

        h = rms_norm(x, g_norm1[l]) * (1.0 + scale1) + shift1
        proj = h @ w_in[l]
        na_q, na_k, na_v, q_lat, kv_lat, k_rot, gate_na, gate_mla = jnp.split(proj, split_at, axis=-1)

        qa = rms_norm(na_q.reshape(B, S, NA_HEADS, NA_HEAD_DIM), g_na_q[l])
        ka = rms_norm(na_k.reshape(B, S, NA_HEADS, NA_HEAD_DIM), g_na_k[l])
        va = na_v.reshape(B, S, NA_HEADS, NA_HEAD_DIM)
        y_na = neighbourhood_attention(qa, ka, va, na_rpb[l])

        qm = (rms_norm(q_lat, g_q_lat[l]) @ w_uq[l]).reshape(B, S, MLA_HEADS, MLA_QK_DIM)
        q_nope, q_rope = qm[..., :MLA_NOPE_DIM], apply_rope(qm[..., MLA_NOPE_DIM:], positions)
        kvm = (rms_norm(kv_lat, g_kv_lat[l]) @ w_ukv[l]).reshape(B, S, MLA_HEADS, MLA_NOPE_DIM + MLA_V_DIM)
        k_nope, vm = kvm[..., :MLA_NOPE_DIM], kvm[..., MLA_NOPE_DIM:]
        k_rope = jnp.broadcast_to(apply_rope(k_rot[:, :, None, :], positions), (B, S, MLA_HEADS, MLA_ROPE_DIM))
        qm = rms_norm(jnp.concatenate([q_nope, q_rope], axis=-1), g_mla_q[l])
        km = rms_norm(jnp.concatenate([k_nope, k_rope], axis=-1), g_mla_k[l])
        y_mla = blocked_full_attention(qm, km, vm)

        merged = (jax.nn.sigmoid(gate_na) * (y_na @ w_proj_na[l])
                  + jax.nn.sigmoid(gate_mla) * (y_mla @ w_proj_mla[l]))
        x = x + gate1 * (merged @ w_out[l])

        h2 = rms_norm(x, g_norm2[l]) * (1.0 + scale2) + shift2
        y_ffn = moe_ffn(h2, w_router[l], e_bias[l], w_exp_gate[l], w_exp_up[l], w_exp_down[l],
                        w_sh_gate[l], w_sh_up[l], w_sh_down[l])
        x = x + gate2 * y_ffn
    return x
```

```python
import functools

import jax
import jax.numpy as jnp
import numpy as np
from jax import lax
from jax.experimental import pallas as pl
from jax.experimental.pallas import tpu as pltpu

GRID_W = 64
NA_HEADS = 8
NA_HEAD_DIM = 64
NA_WIN_ROWS = 8
NA_WIN_COLS = 16
MLA_HEADS = 8
MLA_NOPE_DIM = 64
MLA_ROPE_DIM = 32
MLA_V_DIM = 64
MLA_QK_DIM = MLA_NOPE_DIM + MLA_ROPE_DIM
ROPE_THETA = 10000.0
N_GROUPS = 8
TOPK_GROUPS = 4
TOP_K = 8
ROUTED_SCALE = 2.5
EPS = 1e-6
NEG_BIG = -1e30

LANES = 128
V7X_VMEM_LIMIT = 56 * 1024 * 1024

NA_QROWS = 4
NA_BAND = 12
NA_BLOCK_TYPES = 3

F32 = jnp.float32
BF16 = jnp.bfloat16


def _bf(x):
    return x.astype(BF16)


def _dot(a, b):
    return jnp.dot(a, b, preferred_element_type=F32)


def _dot_nt(a, b):
    return lax.dot_general(a, b, (((1,), (1,)), ((), ())), preferred_element_type=F32)


def _split(x):
    hi = _bf(x)
    lo = _bf(x - hi.astype(F32))
    return hi, lo


def _dot3(a, b):
    ah, al = _split(a)
    bh, bl = _split(b)
    return _dot(ah, bh) + (_dot(ah, bl) + _dot(al, bh))


def _dot3_nt(a, b):
    ah, al = _split(a)
    bh, bl = _split(b)
    return _dot_nt(ah, bh) + (_dot_nt(ah, bl) + _dot_nt(al, bh))


def _sigmoid(x):
    return 1.0 / (1.0 + jnp.exp(-x))


def _silu(x):
    return x * _sigmoid(x)


def _rms(x, n):
    ss = jnp.sum(x * x, axis=-1, keepdims=True)
    return x * lax.rsqrt(ss * (1.0 / n) + EPS)


def _adaln_kernel(c_ref, w_ref, b_ref, o_ref):
    c = c_ref[...]
    o_ref[...] = _dot3(_silu(c), w_ref[...]) + b_ref[...]


def _adaln(c, w, b):
    bsz, d = c.shape
    n = w.shape[1]
    tn = 1024
    return pl.pallas_call(
        _adaln_kernel,
        out_shape=jax.ShapeDtypeStruct((bsz, n), F32),
        grid=(n // tn,),
        in_specs=[
            pl.BlockSpec((bsz, d), lambda j: (0, 0)),
            pl.BlockSpec((d, tn), lambda j: (0, j)),
            pl.BlockSpec((1, tn), lambda j: (0, j)),
        ],
        out_specs=pl.BlockSpec((bsz, tn), lambda j: (0, j)),
        compiler_params=pltpu.CompilerParams(dimension_semantics=("arbitrary",)),
        name="adaln",
    )(c, w, b.reshape(1, n))


def _inproj_kernel(x_ref, shift_ref, scale_ref, g1_ref, wqkv_ref, wlat_ref, wgate_ref,
                   gq_ref, gk_ref, gql_ref, gkvl_ref, wuq_ref, wuk_ref, wuv_ref,
                   gmq_ref, gmk_ref, pos_ref, freq_ref,
                   qa_ref, ka_ref, va_ref, qm_ref, km_ref, vm_ref, sgn_ref, sgm_ref):
    d = x_ref.shape[1]
    x = x_ref[...]
    h = _rms(x, d) * g1_ref[...]
    h = h * (1.0 + scale_ref[0]) + shift_ref[0]
    hb = _bf(h)

    qkv = _dot(hb, wqkv_ref[...])
    lat = _dot(hb, wlat_ref[...])
    gts = _dot(hb, wgate_ref[...])
    sgn_ref[...] = _bf(_sigmoid(gts[:, :d]))
    sgm_ref[...] = _bf(_sigmoid(gts[:, d:]))

    na_w = NA_HEADS * NA_HEAD_DIM
    lane = lax.broadcasted_iota(jnp.int32, (1, LANES), 1)
    lo_half = lane < NA_HEAD_DIM
    for p in range(na_w // LANES):
        sl = slice(p * LANES, (p + 1) * LANES)
        for src_off, g_ref, dst_ref in ((0, gq_ref, qa_ref), (na_w, gk_ref, ka_ref)):
            t = qkv[:, src_off + p * LANES: src_off + (p + 1) * LANES]
            sq = t * t
            s_lo = jnp.sum(jnp.where(lo_half, sq, 0.0), axis=-1, keepdims=True)
            s_hi = jnp.sum(jnp.where(lo_half, 0.0, sq), axis=-1, keepdims=True)
            r = jnp.where(lo_half,
                          lax.rsqrt(s_lo * (1.0 / NA_HEAD_DIM) + EPS),
                          lax.rsqrt(s_hi * (1.0 / NA_HEAD_DIM) + EPS))
            dst_ref[:, sl] = _bf(t * r * g_ref[:, sl])
    va_ref[...] = _bf(qkv[:, 2 * na_w: 3 * na_w])

    q_rank = gql_ref.shape[1]
    kv_rank = gkvl_ref.shape[1]
    qln = _rms(lat[:, :q_rank], q_rank) * gql_ref[...]
    kvn = _bf(_rms(lat[:, q_rank:q_rank + kv_rank], kv_rank) * gkvl_ref[...])
    qpre = _dot(_bf(qln), wuq_ref[...])
    knope = _dot(kvn, wuk_ref[...])
    vm_ref[...] = _bf(_dot(kvn, wuv_ref[...]))
    krot = lat[:, q_rank + kv_rank:]

    ang = pos_ref[...].astype(F32) * freq_ref[...]
    cosv = jnp.cos(ang)
    sinv = jnp.sin(ang)
    half = MLA_ROPE_DIM // 2
    in_rope = (lane >= MLA_NOPE_DIM) & (lane < MLA_QK_DIM)
    first_half = lane < MLA_NOPE_DIM + half
    c_tab = jnp.where(lane < MLA_NOPE_DIM, 1.0, jnp.where(in_rope, cosv, 0.0))
    s_up = jnp.where(in_rope & jnp.logical_not(first_half), sinv, 0.0)
    s_dn = jnp.where(in_rope & first_half, -sinv, 0.0)

    def rope(t):
        return t * c_tab + pltpu.roll(t, half, 1) * s_up + pltpu.roll(t, LANES - half, 1) * s_dn

    kr = rope(krot)
    for hd in range(MLA_HEADS):
        sl = slice(hd * LANES, (hd + 1) * LANES)
        qh = rope(qpre[:, sl])
        qm_ref[:, sl] = _bf(_rms(qh, MLA_QK_DIM) * gmq_ref[:, sl])
        kh = knope[:, sl] + kr
        km_ref[:, sl] = _bf(_rms(kh, MLA_QK_DIM) * gmk_ref[:, sl])


def _inproj(x2, mod3, g1, wqkv, wlat, wgate, gq, gk, gql, gkvl, wuq, wuk, wuv, gmq, gmk,
            pos, freq, seq):
    t, d = x2.shape
    tm = 256
    per_b = seq // tm
    na_w = NA_HEADS * NA_HEAD_DIM
    mla_w = MLA_HEADS * LANES
    v_w = MLA_HEADS * MLA_V_DIM

    def full(a):
        return pl.BlockSpec(a.shape, lambda i: (0,) * a.ndim)

    def rows(w):
        return pl.BlockSpec((tm, w), lambda i: (i, 0))

    out_shapes = (
        jax.ShapeDtypeStruct((t, na_w), BF16), jax.ShapeDtypeStruct((t, na_w), BF16),
        jax.ShapeDtypeStruct((t, na_w), BF16),
        jax.ShapeDtypeStruct((t, mla_w), BF16), jax.ShapeDtypeStruct((t, mla_w), BF16),
        jax.ShapeDtypeStruct((t, v_w), BF16),
        jax.ShapeDtypeStruct((t, d), BF16), jax.ShapeDtypeStruct((t, d), BF16),
    )
    return pl.pallas_call(
        _inproj_kernel,
        out_shape=out_shapes,
        grid=(t // tm,),
        in_specs=[
            rows(d),
            pl.BlockSpec((1, 1, d), lambda i: (i // per_b, 0, 0)),
            pl.BlockSpec((1, 1, d), lambda i: (i // per_b, 0, 1)),
            full(g1), full(wqkv), full(wlat), full(wgate), full(gq), full(gk), full(gql),
            full(gkvl), full(wuq), full(wuk), full(wuv), full(gmq), full(gmk),
            pl.BlockSpec((tm, 1), lambda i: (i, 0)),
            full(freq),
        ],
        out_specs=(rows(na_w), rows(na_w), rows(na_w), rows(mla_w), rows(mla_w), rows(v_w),
                   rows(d), rows(d)),
        compiler_params=pltpu.CompilerParams(dimension_semantics=("arbitrary",),
                                             vmem_limit_bytes=V7X_VMEM_LIMIT),
        name="inproj",
    )(x2, mod3, mod3, g1, wqkv, wlat, wgate, gq, gk, gql, gkvl, wuq, wuk, wuv, gmq, gmk,
      pos, freq)


def _na_block_geometry(block_type, n_rows):
    if block_type == 0:
        return 0, 0
    if block_type == 1:
        r0 = NA_QROWS
        return r0, r0 - NA_WIN_ROWS // 2
    return n_rows - NA_QROWS, n_rows - NA_BAND


def _na_bias_kernel(rpb_ref, o_ref, m_ref, *, n_rows):
    hd = pl.program_id(0)
    n_dr = 2 * NA_WIN_ROWS - 1
    n_dc = 2 * NA_WIN_COLS - 1
    qc = lax.broadcasted_iota(jnp.int32, (GRID_W, LANES), 0)
    kc = lax.broadcasted_iota(jnp.int32, (GRID_W, LANES), 1) & (GRID_W - 1)
    dc = jnp.clip(kc - qc, -(NA_WIN_COLS - 1), NA_WIN_COLS - 1) + (NA_WIN_COLS - 1)
    cstart = jnp.clip(qc - NA_WIN_COLS // 2, 0, GRID_W - NA_WIN_COLS)
    col_ok = (kc >= cstart) & (kc < cstart + NA_WIN_COLS)
    for i_dr in range(n_dr):
        acc = jnp.zeros((GRID_W, LANES), F32)
        for t in range(n_dc):
            acc = jnp.where(dc == t, rpb_ref[hd, i_dr * n_dc + t], acc)
        m_ref[i_dr] = jnp.where(col_ok, acc, NEG_BIG)
    neg = jnp.full((GRID_W, LANES), NEG_BIG, F32)
    lo_half = lax.broadcasted_iota(jnp.int32, (GRID_W, LANES), 1) < GRID_W
    kh = NA_WIN_ROWS
    for bt in range(NA_BLOCK_TYPES):
        r0, start = _na_block_geometry(bt, n_rows)
        for i in range(NA_QROWS):
            r = r0 + i
            rs = min(max(r - kh // 2, 0), n_rows - kh)
            for jp in range(NA_BAND // 2):
                halves = []
                for j in (2 * jp, 2 * jp + 1):
                    krow = start + j
                    if rs <= krow < rs + kh:
                        halves.append(m_ref[krow - r + (NA_WIN_ROWS - 1)])
                    else:
                        halves.append(neg)
                tile = jnp.where(lo_half, halves[0], halves[1])
                o_ref[bt, 0, i * GRID_W:(i + 1) * GRID_W, jp * LANES:(jp + 1) * LANES] = tile


def _na_bias(rpb, n_rows):
    heads = rpb.shape[0]
    nq = NA_QROWS * GRID_W
    nk = NA_BAND * GRID_W
    rpb2 = rpb.reshape(heads, -1)
    return pl.pallas_call(
        functools.partial(_na_bias_kernel, n_rows=n_rows),
        out_shape=jax.ShapeDtypeStruct((NA_BLOCK_TYPES, heads, nq, nk), F32),
        grid=(heads,),
        in_specs=[pl.BlockSpec(memory_space=pltpu.SMEM)],
        out_specs=pl.BlockSpec((NA_BLOCK_TYPES, 1, nq, nk), lambda hd: (0, hd, 0, 0)),
        scratch_shapes=[pltpu.VMEM((2 * NA_WIN_ROWS - 1, GRID_W, LANES), F32)],
        compiler_params=pltpu.CompilerParams(dimension_semantics=("arbitrary",)),
        name="na_bias",
    )(rpb2)


def _na_kernel(q_ref, k_ref, v_ref, bias_ref, o_ref, *, n_blocks, n_rows):
    blk = pl.program_id(1)
    start_row = jnp.where(blk == 0, 0,
                          jnp.where(blk == n_blocks - 1, n_rows - NA_BAND,
                                    blk * NA_QROWS - NA_WIN_ROWS // 2))
    off = pl.multiple_of(start_row * GRID_W, GRID_W)
    nk = NA_BAND * GRID_W
    lane = lax.broadcasted_iota(jnp.int32, (1, LANES), 1)
    for p in range(NA_HEADS * NA_HEAD_DIM // LANES):
        sl = slice(p * LANES, (p + 1) * LANES)
        qp = q_ref[:, sl]
        kb = k_ref[pl.ds(off, nk), sl]
        vb = v_ref[pl.ds(off, nk), sl]
        acc = jnp.zeros((q_ref.shape[0], LANES), F32)
        for hh in range(2):
            mine = (lane < NA_HEAD_DIM) if hh == 0 else (lane >= NA_HEAD_DIM)
            qh = jnp.where(mine, qp, jnp.zeros_like(qp))
            s = _dot_nt(qh, kb) + bias_ref[0, 2 * p + hh]
            m = jnp.max(s, axis=-1, keepdims=True)
            e = jnp.exp(s - m)
            l = jnp.sum(e, axis=-1, keepdims=True)
            vh = jnp.where(mine, vb, jnp.zeros_like(vb))
            acc = acc + _dot(_bf(e), vh) / l
        o_ref[:, sl] = _bf(acc)


def _na_attention(qa, ka, va, bias, bsz, seq):
    t, w = qa.shape
    n_rows = seq // GRID_W
    n_blocks = n_rows // NA_QROWS
    tq = NA_QROWS * GRID_W
    nk = NA_BAND * GRID_W
    heads = bias.shape[1]

    def btype(blk):
        return jnp.where(blk == 0, 0, jnp.where(blk == n_blocks - 1, 2, 1))

    return pl.pallas_call(
        functools.partial(_na_kernel, n_blocks=n_blocks, n_rows=n_rows),
        out_shape=jax.ShapeDtypeStruct((t, w), BF16),
        grid=(bsz, n_blocks),
        in_specs=[
            pl.BlockSpec((tq, w), lambda b, blk: (b * n_blocks + blk, 0)),
            pl.BlockSpec((seq, w), lambda b, blk: (b, 0)),
            pl.BlockSpec((seq, w), lambda b, blk: (b, 0)),
            pl.BlockSpec((1, heads, tq, nk), lambda b, blk: (btype(blk), 0, 0, 0)),
        ],
        out_specs=pl.BlockSpec((tq, w), lambda b, blk: (b * n_blocks + blk, 0)),
        compiler_params=pltpu.CompilerParams(dimension_semantics=("arbitrary", "arbitrary"),
                                             vmem_limit_bytes=V7X_VMEM_LIMIT),
        name="na_attn",
    )(qa, ka, va, bias)


def _mla_kernel(q_ref, k_ref, v_ref, o_ref):
    lane = lax.broadcasted_iota(jnp.int32, (1, LANES), 1)
    v = v_ref[...]
    acc = jnp.zeros((q_ref.shape[0], LANES), F32)
    for hh in range(2):
        sl = slice(hh * LANES, (hh + 1) * LANES)
        s = _dot_nt(q_ref[:, sl], k_ref[:, sl])
        m = jnp.max(s, axis=-1, keepdims=True)
        e = jnp.exp(s - m)
        l = jnp.sum(e, axis=-1, keepdims=True)
        mine = (lane < MLA_V_DIM) if hh == 0 else (lane >= MLA_V_DIM)
        vh = jnp.where(mine, v, jnp.zeros_like(v))
        acc = acc + _dot(_bf(e), vh) / l
    o_ref[...] = _bf(acc)


def _mla_attention(qm, km, vm, bsz, seq):
    t = qm.shape[0]
    tq = 512
    nq = seq // tq
    pairs = MLA_HEADS // 2
    return pl.pallas_call(
        _mla_kernel,
        out_shape=jax.ShapeDtypeStruct((t, MLA_HEADS * MLA_V_DIM), BF16),
        grid=(bsz, pairs, nq),
        in_specs=[
            pl.BlockSpec((tq, 2 * LANES), lambda b, p, i: (b * nq + i, p)),
            pl.BlockSpec((seq, 2 * LANES), lambda b, p, i: (b, p)),
            pl.BlockSpec((seq, LANES), lambda b, p, i: (b, p)),
        ],
        out_specs=pl.BlockSpec((tq, LANES), lambda b, p, i: (b * nq + i, p)),
        compiler_params=pltpu.CompilerParams(
            dimension_semantics=("arbitrary", "arbitrary", "arbitrary"),
            vmem_limit_bytes=V7X_VMEM_LIMIT),
        name="mla_attn",
    )(qm, km, vm)


def _merge_kernel(x_ref, yna_ref, ymla_ref, sgn_ref, sgm_ref, wpn_ref, wpm_ref, wout_ref,
                  gate1_ref, shift2_ref, scale2_ref, g2_ref, wrt_ref,
                  x1_ref, h2_ref, lt_ref):
    d = x_ref.shape[1]
    merged = (sgn_ref[...].astype(F32) * _dot(yna_ref[...], wpn_ref[...])
              + sgm_ref[...].astype(F32) * _dot(ymla_ref[...], wpm_ref[...]))
    x1 = x_ref[...] + gate1_ref[0] * _dot(_bf(merged), wout_ref[...])
    x1_ref[...] = x1
    h2 = _rms(x1, d) * g2_ref[...]
    h2 = h2 * (1.0 + scale2_ref[0]) + shift2_ref[0]
    h2_ref[...] = _bf(h2)
    lt_ref[...] = _dot3_nt(wrt_ref[...], h2)


def _merge(x2, yna, ymla, sgn, sgm, wpn, wpm, wout, mod3, g2, wrt, seq):
    t, d = x2.shape
    tm = 512
    per_b = seq // tm
    n_exp = wrt.shape[0]

    def full(a):
        return pl.BlockSpec(a.shape, lambda i: (0,) * a.ndim)

    def rows(w):
        return pl.BlockSpec((tm, w), lambda i: (i, 0))

    def modblk(j):
        return pl.BlockSpec((1, 1, d), lambda i: (i // per_b, 0, j))

    return pl.pallas_call(
        _merge_kernel,
        out_shape=(jax.ShapeDtypeStruct((t, d), F32), jax.ShapeDtypeStruct((t, d), BF16),
                   jax.ShapeDtypeStruct((n_exp, t), F32)),
        grid=(t // tm,),
        in_specs=[rows(d), rows(yna.shape[1]), rows(ymla.shape[1]), rows(d), rows(d),
                  full(wpn), full(wpm), full(wout),
                  modblk(2), modblk(3), modblk(4), full(g2), full(wrt)],
        out_specs=(rows(d), rows(d), pl.BlockSpec((n_exp, tm), lambda i: (0, i))),
        compiler_params=pltpu.CompilerParams(dimension_semantics=("arbitrary",),
                                             vmem_limit_bytes=V7X_VMEM_LIMIT),
        name="merge",
    )(x2, yna, ymla, sgn, sgm, wpn, wpm, wout, mod3, mod3, mod3, g2, wrt)


def _route_kernel(lt_ref, eb_ref, o_ref):
    n_exp, tn = lt_ref.shape
    per_g = n_exp // N_GROUPS
    neg_inf = -jnp.inf
    sc = _sigmoid(lt_ref[...])
    sel = sc + eb_ref[...]
    sc3 = sc.reshape(N_GROUPS, per_g, tn)
    g3 = sel.reshape(N_GROUPS, per_g, tn)
    io = lax.broadcasted_iota(jnp.int32, (N_GROUPS, per_g, tn), 1)
    gio = lax.broadcasted_iota(jnp.int32, (N_GROUPS, per_g, tn), 0)
    eio = gio * per_g + io

    m1 = jnp.max(g3, axis=1, keepdims=True)
    i1 = jnp.min(jnp.where(g3 == m1, io, per_g), axis=1, keepdims=True)
    m2 = jnp.max(jnp.where(io == i1, neg_inf, g3), axis=1, keepdims=True)
    gs = m1 + m2

    g1io = lax.broadcasted_iota(jnp.int32, (N_GROUPS, 1, tn), 0)
    gsel = jnp.zeros((N_GROUPS, 1, tn), F32)
    cur = gs
    for _ in range(TOPK_GROUPS):
        m = jnp.max(cur, axis=0, keepdims=True)
        i = jnp.min(jnp.where(cur == m, g1io, N_GROUPS), axis=0, keepdims=True)
        pick = g1io == i
        gsel = jnp.where(pick, 1.0, gsel)
        cur = jnp.where(pick, neg_inf, cur)

    cur = jnp.where(gsel > 0.0, g3, neg_inf)
    chosen = jnp.zeros((N_GROUPS, per_g, tn), F32)
    for _ in range(TOP_K):
        m = jnp.max(jnp.max(cur, axis=1, keepdims=True), axis=0, keepdims=True)
        cand = jnp.where(cur == m, eio, n_exp)
        i = jnp.min(jnp.min(cand, axis=1, keepdims=True), axis=0, keepdims=True)
        pick = eio == i
        chosen = jnp.where(pick, 1.0, chosen)
        cur = jnp.where(pick, neg_inf, cur)

    w = jnp.where(chosen > 0.0, sc3, 0.0)
    tot = jnp.sum(jnp.sum(w, axis=1, keepdims=True), axis=0, keepdims=True)
    o_ref[...] = (w / tot * ROUTED_SCALE).reshape(n_exp, tn)


def _route(lt, e_bias):
    n_exp, t = lt.shape
    tn = 512
    return pl.pallas_call(
        _route_kernel,
        out_shape=jax.ShapeDtypeStruct((n_exp, t), F32),
        grid=(t // tn,),
        in_specs=[pl.BlockSpec((n_exp, tn), lambda i: (0, i)),
                  pl.BlockSpec((n_exp, 1), lambda i: (0, 0))],
        out_specs=pl.BlockSpec((n_exp, tn), lambda i: (0, i)),
        compiler_params=pltpu.CompilerParams(dimension_semantics=("arbitrary",)),
        name="route",
    )(lt, e_bias.reshape(n_exp, 1))


def _moe_kernel(h_ref, gates_ref, x1_ref, gate2_ref, wsg_ref, wsu_ref, wsd_ref,
                wg_ref, wu_ref, wd_ref, o_ref, acc_ref):
    e = pl.program_id(1)
    n_exp = pl.num_programs(1)
    h = h_ref[...]

    @pl.when(e == 0)
    def _():
        a = _silu(_dot(h, wsg_ref[...])) * _dot(h, wsu_ref[...])
        acc_ref[...] = _dot(_bf(a), wsd_ref[...])

    ff = wg_ref.shape[2]
    rows = lax.broadcasted_iota(jnp.int32, (gates_ref.shape[1], ff), 0)
    pick = jnp.where(rows == e, 1.0, 0.0).astype(BF16)
    ghi, glo = _split(gates_ref[...])
    gcol = _dot(ghi, pick) + _dot(glo, pick)
    a = _silu(_dot(h, wg_ref[0])) * _dot(h, wu_ref[0]) * gcol
    acc_ref[...] += _dot(_bf(a), wd_ref[0])

    @pl.when(e == n_exp - 1)
    def _():
        o_ref[...] = x1_ref[...] + gate2_ref[0] * acc_ref[...]


def _moe(h2, gates, x1, mod3, wsg, wsu, wsd, wg, wu, wd, seq):
    t, d = h2.shape
    n_exp, _, ff = wg.shape
    tm = 1024
    per_b = seq // tm

    def full(a):
        return pl.BlockSpec(a.shape, lambda i, e: (0,) * a.ndim)

    return pl.pallas_call(
        _moe_kernel,
        out_shape=jax.ShapeDtypeStruct((t, d), F32),
        grid=(t // tm, n_exp),
        in_specs=[
            pl.BlockSpec((tm, d), lambda i, e: (i, 0)),
            pl.BlockSpec((tm, n_exp), lambda i, e: (i, 0)),
            pl.BlockSpec((tm, d), lambda i, e: (i, 0)),
            pl.BlockSpec((1, 1, d), lambda i, e: (i // per_b, 0, 5)),
            full(wsg), full(wsu), full(wsd),
            pl.BlockSpec((1, d, ff), lambda i, e: (e, 0, 0)),
            pl.BlockSpec((1, d, ff), lambda i, e: (e, 0, 0)),
            pl.BlockSpec((1, ff, d), lambda i, e: (e, 0, 0)),
        ],
        out_specs=pl.BlockSpec((tm, d), lambda i, e: (i, 0)),
        scratch_shapes=[pltpu.VMEM((tm, d), F32)],
        compiler_params=pltpu.CompilerParams(dimension_semantics=("arbitrary", "arbitrary"),
                                             vmem_limit_bytes=V7X_VMEM_LIMIT),
        name="moe",
    )(h2, gates, x1, mod3, wsg, wsu, wsd, wg, wu, wd)


def _pad_heads(w, heads, width):
    lead = w.shape[:-1]
    w = w.reshape(lead + (heads, width))
    w = jnp.pad(w, [(0, 0)] * len(lead) + [(0, 0), (0, LANES - width)])
    return w.reshape(lead + (heads * LANES,))


def kernel(x, c, positions, w_ada, b_ada, g_norm1, w_in, g_na_q, g_na_k, na_rpb, g_q_lat, w_uq,
           g_kv_lat, w_ukv, g_mla_q, g_mla_k, w_proj_na, w_proj_mla, w_out, g_norm2, w_router,
           e_bias, w_exp_gate, w_exp_up, w_exp_down, w_sh_gate, w_sh_up, w_sh_down):
    bsz, seq, d = x.shape
    t = bsz * seq
    depth = w_ada.shape[0]
    na_w = NA_HEADS * NA_HEAD_DIM
    q_rank = g_q_lat.shape[1]
    kv_rank = g_kv_lat.shape[1]
    n_rows = seq // GRID_W

    pos = positions.reshape(t, 1)
    half = MLA_ROPE_DIM // 2
    inv_freq = ROPE_THETA ** (-jnp.arange(half, dtype=F32) / half)
    freq = jnp.zeros((LANES,), F32).at[MLA_NOPE_DIM:MLA_QK_DIM].set(jnp.tile(inv_freq, 2))
    freq = freq.reshape(1, LANES)

    x2 = x.reshape(t, d)
    for l in range(depth):
        mod3 = _adaln(c, w_ada[l], b_ada[l]).reshape(bsz, 1, 6 * d)

        wi = w_in[l]
        o_lat = 3 * na_w
        o_rot = o_lat + q_rank + kv_rank
        o_gate = o_rot + MLA_ROPE_DIM
        wqkv = _bf(wi[:, :o_lat])
        w_rot = jnp.pad(wi[:, o_rot:o_gate], ((0, 0), (MLA_NOPE_DIM, LANES - MLA_QK_DIM)))
        wlat = _bf(jnp.concatenate([wi[:, o_lat:o_rot], w_rot], axis=1))
        wgate = _bf(wi[:, o_gate:])
        gq = (jnp.tile(g_na_q[l], NA_HEADS) * NA_HEAD_DIM ** -0.5).reshape(1, na_w)
        gk = jnp.tile(g_na_k[l], NA_HEADS).reshape(1, na_w)
        wuq = _bf(_pad_heads(w_uq[l], MLA_HEADS, MLA_QK_DIM))
        wukv = w_ukv[l].reshape(kv_rank, MLA_HEADS, MLA_NOPE_DIM + MLA_V_DIM)
        wuk = _bf(_pad_heads(wukv[:, :, :MLA_NOPE_DIM].reshape(kv_rank, -1), MLA_HEADS, MLA_NOPE_DIM))
        wuv = _bf(wukv[:, :, MLA_NOPE_DIM:].reshape(kv_rank, MLA_HEADS * MLA_V_DIM))
        gmq = _pad_heads(jnp.tile(g_mla_q[l], MLA_HEADS) * MLA_QK_DIM ** -0.5, MLA_HEADS,
                         MLA_QK_DIM).reshape(1, -1)
        gmk = _pad_heads(jnp.tile(g_mla_k[l], MLA_HEADS), MLA_HEADS, MLA_QK_DIM).reshape(1, -1)

        qa, ka, va, qm, km, vm, sgn, sgm = _inproj(
            x2, mod3, g_norm1[l].reshape(1, d), wqkv, wlat, wgate, gq, gk,
            g_q_lat[l].reshape(1, q_rank), g_kv_lat[l].reshape(1, kv_rank), wuq, wuk, wuv,
            gmq, gmk, pos, freq, seq)

        bias = _na_bias(na_rpb[l], n_rows)
        y_na = _na_attention(qa, ka, va, bias, bsz, seq)
        y_mla = _mla_attention(qm, km, vm, bsz, seq)

        x1, h2, lt = _merge(x2, y_na, y_mla, sgn, sgm, _bf(w_proj_na[l]), _bf(w_proj_mla[l]),
                            _bf(w_out[l]), mod3, g_norm2[l].reshape(1, d), w_router[l].T, seq)
        gates = _route(lt, e_bias[l]).T

        x2 = _moe(h2, gates, x1, mod3, _bf(w_sh_gate[l]), _bf(w_sh_up[l]), _bf(w_sh_down[l]),
                  _bf(w_exp_gate[l]), _bf(w_exp_up[l]), _bf(w_exp_down[l]), seq)
    return x2.reshape(bsz, seq, d)
```

```python
import functools

import jax
import jax.numpy as jnp
import numpy as np
from jax import lax
from jax.experimental import pallas as pl
from jax.experimental.pallas import tpu as pltpu

GRID_W = 64
NA_HEADS = 8
NA_HEAD_DIM = 64
NA_WIN_ROWS = 8
NA_WIN_COLS = 16
MLA_HEADS = 8
MLA_NOPE_DIM = 64
MLA_ROPE_DIM = 32
MLA_V_DIM = 64
MLA_QK_DIM = MLA_NOPE_DIM + MLA_ROPE_DIM
ROPE_THETA = 10000.0
N_GROUPS = 8
TOPK_GROUPS = 4
TOP_K = 8
ROUTED_SCALE = 2.5
EPS = 1e-6
NEG_BIG = -1e30

LANES = 128
V7X_VMEM_LIMIT = 56 * 1024 * 1024

NA_QROWS = 4
NA_BAND = 12
NA_BLOCK_TYPES = 3
MOE_TB = 256
MOE_CHUNK_SHIFT = 4
MOE_CHUNK = 1 << MOE_CHUNK_SHIFT
MOE_TM = 512
MOE_ROWGROUP = 512

F32 = jnp.float32
BF16 = jnp.bfloat16


def _bf(x):
    return x.astype(BF16)


def _dot(a, b):
    return jnp.dot(a, b, preferred_element_type=F32)


def _dot_nt(a, b):
    return lax.dot_general(a, b, (((1,), (1,)), ((), ())), preferred_element_type=F32)


def _split(x):
    hi = _bf(x)
    lo = _bf(x - hi.astype(F32))
    return hi, lo


def _dot3(a, b):
    ah, al = _split(a)
    bh, bl = _split(b)
    return _dot(ah, bh) + (_dot(ah, bl) + _dot(al, bh))


def _dot3_nt(a, b):
    ah, al = _split(a)
    bh, bl = _split(b)
    return _dot_nt(ah, bh) + (_dot_nt(ah, bl) + _dot_nt(al, bh))


def _sigmoid(x):
    return 1.0 / (1.0 + jnp.exp(-x))


def _silu(x):
    return x * _sigmoid(x)


def _rms(x, n):
    ss = jnp.sum(x * x, axis=-1, keepdims=True)
    return x * lax.rsqrt(ss * (1.0 / n) + EPS)


def _adaln_kernel(c_ref, w_ref, b_ref, o_ref):
    c = c_ref[...]
    o_ref[...] = _dot3(_silu(c), w_ref[...]) + b_ref[...]


def _adaln(c, w, b):
    bsz, d = c.shape
    n = w.shape[1]
    tn = 1024
    return pl.pallas_call(
        _adaln_kernel,
        out_shape=jax.ShapeDtypeStruct((bsz, n), F32),
        grid=(n // tn,),
        in_specs=[
            pl.BlockSpec((bsz, d), lambda j: (0, 0)),
            pl.BlockSpec((d, tn), lambda j: (0, j)),
            pl.BlockSpec((1, tn), lambda j: (0, j)),
        ],
        out_specs=pl.BlockSpec((bsz, tn), lambda j: (0, j)),
        compiler_params=pltpu.CompilerParams(dimension_semantics=("arbitrary",)),
        name="adaln",
    )(c, w, b.reshape(1, n))


def _inproj_kernel(x_ref, shift_ref, scale_ref, g1_ref, wqkv_ref, wlat_ref, wgate_ref,
                   gq_ref, gk_ref, gql_ref, gkvl_ref, wuq_ref, wuk_ref, wuv_ref,
                   gmq_ref, gmk_ref, pos_ref, freq_ref,
                   qa_ref, ka_ref, va_ref, qm_ref, km_ref, vm_ref, sgn_ref, sgm_ref):
    d = x_ref.shape[1]
    x = x_ref[...]
    h = _rms(x, d) * g1_ref[...]
    h = h * (1.0 + scale_ref[0]) + shift_ref[0]
    hb = _bf(h)

    qkv = _dot(hb, wqkv_ref[...])
    lat = _dot(hb, wlat_ref[...])
    gts = _dot(hb, wgate_ref[...])
    sgn_ref[...] = _bf(_sigmoid(gts[:, :d]))
    sgm_ref[...] = _bf(_sigmoid(gts[:, d:]))

    na_w = NA_HEADS * NA_HEAD_DIM
    lane = lax.broadcasted_iota(jnp.int32, (1, LANES), 1)
    lo_half = lane < NA_HEAD_DIM
    for p in range(na_w // LANES):
        sl = slice(p * LANES, (p + 1) * LANES)
        for src_off, g_ref, dst_ref in ((0, gq_ref, qa_ref), (na_w, gk_ref, ka_ref)):
            t = qkv[:, src_off + p * LANES: src_off + (p + 1) * LANES]
            sq = t * t
            s_lo = jnp.sum(jnp.where(lo_half, sq, 0.0), axis=-1, keepdims=True)
            s_hi = jnp.sum(jnp.where(lo_half, 0.0, sq), axis=-1, keepdims=True)
            r = jnp.where(lo_half,
                          lax.rsqrt(s_lo * (1.0 / NA_HEAD_DIM) + EPS),
                          lax.rsqrt(s_hi * (1.0 / NA_HEAD_DIM) + EPS))
            dst_ref[:, sl] = _bf(t * r * g_ref[:, sl])
    va_ref[...] = _bf(qkv[:, 2 * na_w: 3 * na_w])

    q_rank = gql_ref.shape[1]
    kv_rank = gkvl_ref.shape[1]
    qln = _rms(lat[:, :q_rank], q_rank) * gql_ref[...]
    kvn = _bf(_rms(lat[:, q_rank:q_rank + kv_rank], kv_rank) * gkvl_ref[...])
    qpre = _dot(_bf(qln), wuq_ref[...])
    knope = _dot(kvn, wuk_ref[...])
    vm_ref[...] = _bf(_dot(kvn, wuv_ref[...]))
    krot = lat[:, q_rank + kv_rank:]

    ang = pos_ref[...].astype(F32) * freq_ref[...]
    cosv = jnp.cos(ang)
    sinv = jnp.sin(ang)
    half = MLA_ROPE_DIM // 2
    in_rope = (lane >= MLA_NOPE_DIM) & (lane < MLA_QK_DIM)
    first_half = lane < MLA_NOPE_DIM + half
    c_tab = jnp.where(lane < MLA_NOPE_DIM, 1.0, jnp.where(in_rope, cosv, 0.0))
    s_up = jnp.where(in_rope & jnp.logical_not(first_half), sinv, 0.0)
    s_dn = jnp.where(in_rope & first_half, -sinv, 0.0)

    def rope(t):
        return t * c_tab + pltpu.roll(t, half, 1) * s_up + pltpu.roll(t, LANES - half, 1) * s_dn

    kr = rope(krot)
    for hd in range(MLA_HEADS):
        sl = slice(hd * LANES, (hd + 1) * LANES)
        qh = rope(qpre[:, sl])
        qm_ref[:, sl] = _bf(_rms(qh, MLA_QK_DIM) * gmq_ref[:, sl])
        kh = knope[:, sl] + kr
        km_ref[:, sl] = _bf(_rms(kh, MLA_QK_DIM) * gmk_ref[:, sl])


def _inproj(x2, mod3, g1, wqkv, wlat, wgate, gq, gk, gql, gkvl, wuq, wuk, wuv, gmq, gmk,
            pos, freq, seq):
    t, d = x2.shape
    tm = 256
    per_b = seq // tm
    na_w = NA_HEADS * NA_HEAD_DIM
    mla_w = MLA_HEADS * LANES
    v_w = MLA_HEADS * MLA_V_DIM

    def full(a):
        return pl.BlockSpec(a.shape, lambda i: (0,) * a.ndim)

    def rows(w):
        return pl.BlockSpec((tm, w), lambda i: (i, 0))

    out_shapes = (
        jax.ShapeDtypeStruct((t, na_w), BF16), jax.ShapeDtypeStruct((t, na_w), BF16),
        jax.ShapeDtypeStruct((t, na_w), BF16),
        jax.ShapeDtypeStruct((t, mla_w), BF16), jax.ShapeDtypeStruct((t, mla_w), BF16),
        jax.ShapeDtypeStruct((t, v_w), BF16),
        jax.ShapeDtypeStruct((t, d), BF16), jax.ShapeDtypeStruct((t, d), BF16),
    )
    return pl.pallas_call(
        _inproj_kernel,
        out_shape=out_shapes,
        grid=(t // tm,),
        in_specs=[
            rows(d),
            pl.BlockSpec((1, 1, d), lambda i: (i // per_b, 0, 0)),
            pl.BlockSpec((1, 1, d), lambda i: (i // per_b, 0, 1)),
            full(g1), full(wqkv), full(wlat), full(wgate), full(gq), full(gk), full(gql),
            full(gkvl), full(wuq), full(wuk), full(wuv), full(gmq), full(gmk),
            pl.BlockSpec((tm, 1), lambda i: (i, 0)),
            full(freq),
        ],
        out_specs=(rows(na_w), rows(na_w), rows(na_w), rows(mla_w), rows(mla_w), rows(v_w),
                   rows(d), rows(d)),
        compiler_params=pltpu.CompilerParams(dimension_semantics=("arbitrary",),
                                             vmem_limit_bytes=V7X_VMEM_LIMIT),
        name="inproj",
    )(x2, mod3, mod3, g1, wqkv, wlat, wgate, gq, gk, gql, gkvl, wuq, wuk, wuv, gmq, gmk,
      pos, freq)


def _na_block_geometry(block_type, n_rows):
    if block_type == 0:
        return 0, 0
    if block_type == 1:
        r0 = NA_QROWS
        return r0, r0 - NA_WIN_ROWS // 2
    return n_rows - NA_QROWS, n_rows - NA_BAND


def _na_bias_kernel(rpb_ref, o_ref, m_ref, *, n_rows):
    hd = pl.program_id(0)
    n_dr = 2 * NA_WIN_ROWS - 1
    n_dc = 2 * NA_WIN_COLS - 1
    qc = lax.broadcasted_iota(jnp.int32, (GRID_W, LANES), 0)
    kc = lax.broadcasted_iota(jnp.int32, (GRID_W, LANES), 1) & (GRID_W - 1)
    dc = jnp.clip(kc - qc, -(NA_WIN_COLS - 1), NA_WIN_COLS - 1) + (NA_WIN_COLS - 1)
    cstart = jnp.clip(qc - NA_WIN_COLS // 2, 0, GRID_W - NA_WIN_COLS)
    col_ok = (kc >= cstart) & (kc < cstart + NA_WIN_COLS)
    for i_dr in range(n_dr):
        acc = jnp.zeros((GRID_W, LANES), F32)
        for t in range(n_dc):
            acc = jnp.where(dc == t, rpb_ref[hd, i_dr * n_dc + t], acc)
        m_ref[i_dr] = jnp.where(col_ok, acc, NEG_BIG)
    neg = jnp.full((GRID_W, LANES), NEG_BIG, F32)
    lo_half = lax.broadcasted_iota(jnp.int32, (GRID_W, LANES), 1) < GRID_W
    kh = NA_WIN_ROWS
    for bt in range(NA_BLOCK_TYPES):
        r0, start = _na_block_geometry(bt, n_rows)
        for i in range(NA_QROWS):
            r = r0 + i
            rs = min(max(r - kh // 2, 0), n_rows - kh)
            for jp in range(NA_BAND // 2):
                halves = []
                for j in (2 * jp, 2 * jp + 1):
                    krow = start + j
                    if rs <= krow < rs + kh:
                        halves.append(m_ref[krow - r + (NA_WIN_ROWS - 1)])
                    else:
                        halves.append(neg)
                tile = jnp.where(lo_half, halves[0], halves[1])
                o_ref[bt, 0, i * GRID_W:(i + 1) * GRID_W, jp * LANES:(jp + 1) * LANES] = tile


def _na_bias(rpb, n_rows):
    heads = rpb.shape[0]
    nq = NA_QROWS * GRID_W
    nk = NA_BAND * GRID_W
    rpb2 = rpb.reshape(heads, -1)
    return pl.pallas_call(
        functools.partial(_na_bias_kernel, n_rows=n_rows),
        out_shape=jax.ShapeDtypeStruct((NA_BLOCK_TYPES, heads, nq, nk), F32),
        grid=(heads,),
        in_specs=[pl.BlockSpec(memory_space=pltpu.SMEM)],
        out_specs=pl.BlockSpec((NA_BLOCK_TYPES, 1, nq, nk), lambda hd: (0, hd, 0, 0)),
        scratch_shapes=[pltpu.VMEM((2 * NA_WIN_ROWS - 1, GRID_W, LANES), F32)],
        compiler_params=pltpu.CompilerParams(dimension_semantics=("arbitrary",)),
        name="na_bias",
    )(rpb2)


def _na_kernel(q_ref, k_ref, v_ref, bias_ref, o_ref, *, n_blocks, n_rows):
    blk = pl.program_id(1)
    start_row = jnp.where(blk == 0, 0,
                          jnp.where(blk == n_blocks - 1, n_rows - NA_BAND,
                                    blk * NA_QROWS - NA_WIN_ROWS // 2))
    off = pl.multiple_of(start_row * GRID_W, GRID_W)
    nk = NA_BAND * GRID_W
    lane = lax.broadcasted_iota(jnp.int32, (1, LANES), 1)
    for p in range(NA_HEADS * NA_HEAD_DIM // LANES):
        sl = slice(p * LANES, (p + 1) * LANES)
        qp = q_ref[:, sl]
        kb = k_ref[pl.ds(off, nk), sl]
        vb = v_ref[pl.ds(off, nk), sl]
        acc = jnp.zeros((q_ref.shape[0], LANES), F32)
        for hh in range(2):
            mine = (lane < NA_HEAD_DIM) if hh == 0 else (lane >= NA_HEAD_DIM)
            qh = jnp.where(mine, qp, jnp.zeros_like(qp))
            s = _dot_nt(qh, kb) + bias_ref[0, 2 * p + hh]
            m = jnp.max(s, axis=-1, keepdims=True)
            e = jnp.exp(s - m)
            l = jnp.sum(e, axis=-1, keepdims=True)
            vh = jnp.where(mine, vb, jnp.zeros_like(vb))
            acc = acc + _dot(_bf(e), vh) / l
        o_ref[:, sl] = _bf(acc)


def _na_attention(qa, ka, va, bias, bsz, seq):
    t, w = qa.shape
    n_rows = seq // GRID_W
    n_blocks = n_rows // NA_QROWS
    tq = NA_QROWS * GRID_W
    nk = NA_BAND * GRID_W
    heads = bias.shape[1]

    def btype(blk):
        return jnp.where(blk == 0, 0, jnp.where(blk == n_blocks - 1, 2, 1))

    return pl.pallas_call(
        functools.partial(_na_kernel, n_blocks=n_blocks, n_rows=n_rows),
        out_shape=jax.ShapeDtypeStruct((t, w), BF16),
        grid=(bsz, n_blocks),
        in_specs=[
            pl.BlockSpec((tq, w), lambda b, blk: (b * n_blocks + blk, 0)),
            pl.BlockSpec((seq, w), lambda b, blk: (b, 0)),
            pl.BlockSpec((seq, w), lambda b, blk: (b, 0)),
            pl.BlockSpec((1, heads, tq, nk), lambda b, blk: (btype(blk), 0, 0, 0)),
        ],
        out_specs=pl.BlockSpec((tq, w), lambda b, blk: (b * n_blocks + blk, 0)),
        compiler_params=pltpu.CompilerParams(dimension_semantics=("arbitrary", "arbitrary"),
                                             vmem_limit_bytes=V7X_VMEM_LIMIT),
        name="na_attn",
    )(qa, ka, va, bias)


def _mla_kernel(q_ref, k_ref, v_ref, o_ref):
    lane = lax.broadcasted_iota(jnp.int32, (1, LANES), 1)
    v = v_ref[...]
    acc = jnp.zeros((q_ref.shape[0], LANES), F32)
    for hh in range(2):
        sl = slice(hh * LANES, (hh + 1) * LANES)
        s = _dot_nt(q_ref[:, sl], k_ref[:, sl])
        m = jnp.max(s, axis=-1, keepdims=True)
        e = jnp.exp(s - m)
        l = jnp.sum(e, axis=-1, keepdims=True)
        mine = (lane < MLA_V_DIM) if hh == 0 else (lane >= MLA_V_DIM)
        vh = jnp.where(mine, v, jnp.zeros_like(v))
        acc = acc + _dot(_bf(e), vh) / l
    o_ref[...] = _bf(acc)


def _mla_attention(qm, km, vm, bsz, seq):
    t = qm.shape[0]
    tq = 512
    nq = seq // tq
    pairs = MLA_HEADS // 2
    return pl.pallas_call(
        _mla_kernel,
        out_shape=jax.ShapeDtypeStruct((t, MLA_HEADS * MLA_V_DIM), BF16),
        grid=(bsz, pairs, nq),
        in_specs=[
            pl.BlockSpec((tq, 2 * LANES), lambda b, p, i: (b * nq + i, p)),
            pl.BlockSpec((seq, 2 * LANES), lambda b, p, i: (b, p)),
            pl.BlockSpec((seq, LANES), lambda b, p, i: (b, p)),
        ],
        out_specs=pl.BlockSpec((tq, LANES), lambda b, p, i: (b * nq + i, p)),
        compiler_params=pltpu.CompilerParams(
            dimension_semantics=("arbitrary", "arbitrary", "arbitrary"),
            vmem_limit_bytes=V7X_VMEM_LIMIT),
        name="mla_attn",
    )(qm, km, vm)


def _merge_kernel(x_ref, yna_ref, ymla_ref, sgn_ref, sgm_ref, wpn_ref, wpm_ref, wout_ref,
                  gate1_ref, shift2_ref, scale2_ref, g2_ref, wrt_ref,
                  x1_ref, h2_ref, lt_ref):
    d = x_ref.shape[1]
    merged = (sgn_ref[...].astype(F32) * _dot(yna_ref[...], wpn_ref[...])
              + sgm_ref[...].astype(F32) * _dot(ymla_ref[...], wpm_ref[...]))
    x1 = x_ref[...] + gate1_ref[0] * _dot(_bf(merged), wout_ref[...])
    x1_ref[...] = x1
    h2 = _rms(x1, d) * g2_ref[...]
    h2 = h2 * (1.0 + scale2_ref[0]) + shift2_ref[0]
    h2_ref[...] = _bf(h2)
    lt_ref[...] = _dot3_nt(wrt_ref[...], h2)


def _merge(x2, yna, ymla, sgn, sgm, wpn, wpm, wout, mod3, g2, wrt, seq):
    t, d = x2.shape
    tm = 512
    per_b = seq // tm
    n_exp = wrt.shape[0]

    def full(a):
        return pl.BlockSpec(a.shape, lambda i: (0,) * a.ndim)

    def rows(w):
        return pl.BlockSpec((tm, w), lambda i: (i, 0))

    def modblk(j):
        return pl.BlockSpec((1, 1, d), lambda i: (i // per_b, 0, j))

    return pl.pallas_call(
        _merge_kernel,
        out_shape=(jax.ShapeDtypeStruct((t, d), F32), jax.ShapeDtypeStruct((t, d), BF16),
                   jax.ShapeDtypeStruct((n_exp, t), F32)),
        grid=(t // tm,),
        in_specs=[rows(d), rows(yna.shape[1]), rows(ymla.shape[1]), rows(d), rows(d),
                  full(wpn), full(wpm), full(wout),
                  modblk(2), modblk(3), modblk(4), full(g2), full(wrt)],
        out_specs=(rows(d), rows(d), pl.BlockSpec((n_exp, tm), lambda i: (0, i))),
        compiler_params=pltpu.CompilerParams(dimension_semantics=("arbitrary",),
                                             vmem_limit_bytes=V7X_VMEM_LIMIT),
        name="merge",
    )(x2, yna, ymla, sgn, sgm, wpn, wpm, wout, mod3, mod3, mod3, g2, wrt)


def _route_kernel(lt_ref, eb_ref, o_ref, q_ref):
    n_exp, tn = lt_ref.shape
    per_g = n_exp // N_GROUPS
    neg_inf = -jnp.inf
    sc = _sigmoid(lt_ref[...])
    sel = sc + eb_ref[...]
    sc3 = sc.reshape(N_GROUPS, per_g, tn)
    g3 = sel.reshape(N_GROUPS, per_g, tn)
    io = lax.broadcasted_iota(jnp.int32, (N_GROUPS, per_g, tn), 1)
    gio = lax.broadcasted_iota(jnp.int32, (N_GROUPS, per_g, tn), 0)
    eio = gio * per_g + io

    m1 = jnp.max(g3, axis=1, keepdims=True)
    i1 = jnp.min(jnp.where(g3 == m1, io, per_g), axis=1, keepdims=True)
    m2 = jnp.max(jnp.where(io == i1, neg_inf, g3), axis=1, keepdims=True)
    gs = m1 + m2

    g1io = lax.broadcasted_iota(jnp.int32, (N_GROUPS, 1, tn), 0)
    gsel = jnp.zeros((N_GROUPS, 1, tn), F32)
    cur = gs
    for _ in range(TOPK_GROUPS):
        m = jnp.max(cur, axis=0, keepdims=True)
        i = jnp.min(jnp.where(cur == m, g1io, N_GROUPS), axis=0, keepdims=True)
        pick = g1io == i
        gsel = jnp.where(pick, 1.0, gsel)
        cur = jnp.where(pick, neg_inf, cur)

    cur = jnp.where(gsel > 0.0, g3, neg_inf)
    chosen = jnp.zeros((N_GROUPS, per_g, tn), F32)
    for _ in range(TOP_K):
        m = jnp.max(jnp.max(cur, axis=1, keepdims=True), axis=0, keepdims=True)
        cand = jnp.where(cur == m, eio, n_exp)
        i = jnp.min(jnp.min(cand, axis=1, keepdims=True), axis=0, keepdims=True)
        pick = eio == i
        chosen = jnp.where(pick, 1.0, chosen)
        cur = jnp.where(pick, neg_inf, cur)

    w = jnp.where(chosen > 0.0, sc3, 0.0)
    tot = jnp.sum(jnp.sum(w, axis=1, keepdims=True), axis=0, keepdims=True)
    gates = (w / tot * ROUTED_SCALE).reshape(n_exp, tn)
    o_ref[...] = gates
    routed = jnp.where(gates > 0.0, 1.0, 0.0).astype(BF16)
    n_row = _dot_nt(jnp.ones((8, tn), BF16), routed)[0:1]
    q_ref[0] = jnp.floor((n_row + (MOE_CHUNK - 1)) * (1.0 / MOE_CHUNK)).astype(jnp.int32)


def _route(lt, e_bias):
    n_exp, t = lt.shape
    tn = MOE_TB
    return pl.pallas_call(
        _route_kernel,
        out_shape=(jax.ShapeDtypeStruct((n_exp, t), F32),
                   jax.ShapeDtypeStruct((t // tn, 1, n_exp), jnp.int32)),
        grid=(t // tn,),
        in_specs=[pl.BlockSpec((n_exp, tn), lambda i: (0, i)),
                  pl.BlockSpec((n_exp, 1), lambda i: (0, 0))],
        out_specs=(pl.BlockSpec((n_exp, tn), lambda i: (0, i)),
                   pl.BlockSpec((1, 1, n_exp), lambda i: (i, 0, 0))),
        compiler_params=pltpu.CompilerParams(dimension_semantics=("arbitrary",)),
        name="route",
    )(lt, e_bias.reshape(n_exp, 1))


def _chunk_rows(idx):
    return pl.ds(pl.multiple_of(idx * MOE_CHUNK, MOE_CHUNK), MOE_CHUNK)


def _dispatch_kernel(q_s, qb_s, dst_s, nch_s, tailq_s, taila_s,
                     gt_ref, qrow_ref, qbrow_ref, h_ref, xs_ref, stage_ref, zero_ref, sem):
    b = pl.program_id(0)
    nb = pl.num_programs(0)
    slot = lax.rem(b, 2)
    n_exp, tb = gt_ref.shape
    rmax = stage_ref.shape[1]

    routed = gt_ref[...] > 0.0
    before = (lax.broadcasted_iota(jnp.int32, (tb, tb), 0)
              < lax.broadcasted_iota(jnp.int32, (tb, tb), 1))
    pos = _dot(jnp.where(routed, 1.0, 0.0).astype(BF16), jnp.where(before, 1.0, 0.0).astype(BF16))
    posm = _bf(jnp.where(routed, pos, -1.0))
    qrow = qrow_ref[0]
    qbrow = qbrow_ref[0]
    qbrow_f = qbrow.astype(F32)
    h = h_ref[...]
    for g in range(rmax // MOE_ROWGROUP):
        @pl.when(g * MOE_ROWGROUP < nch_s[b] * MOE_CHUNK)
        def _():
            r0 = g * MOE_ROWGROUP
            chunk = (lax.broadcasted_iota(jnp.int32, (MOE_ROWGROUP, n_exp), 0) + r0) >> MOE_CHUNK_SHIFT
            own = jnp.where(chunk >= qbrow, jnp.where(chunk < qbrow + qrow, 1.0, 0.0), 0.0)
            rank = _dot(_bf(own), posm)
            start = jnp.sum(own * qbrow_f, axis=-1, keepdims=True) * MOE_CHUNK
            rel = (lax.broadcasted_iota(jnp.int32, (MOE_ROWGROUP, 1), 0) + r0).astype(F32) - start
            onehot = jnp.where(rank == rel, 1.0, 0.0).astype(BF16)
            stage_ref[slot, r0:r0 + MOE_ROWGROUP, :] = _bf(_dot(onehot, h))

    def copy_run(src_chunk, dst_chunk, n, src_of):
        def one(c, carry):
            pltpu.make_async_copy(src_of(src_chunk + c), xs_ref.at[_chunk_rows(dst_chunk + c), :],
                                  sem.at[slot]).start()
            return carry
        lax.fori_loop(0, n, one, 0)

    def issue(e, carry):
        copy_run(qb_s[b, e], dst_s[b, e], q_s[b, e],
                 lambda i: stage_ref.at[slot, _chunk_rows(i), :])
        return carry
    lax.fori_loop(0, n_exp, issue, 0)

    def wait_chunks(n, sl):
        def one(c, carry):
            pltpu.make_async_copy(zero_ref, xs_ref.at[pl.ds(0, MOE_CHUNK), :], sem.at[sl]).wait()
            return carry
        lax.fori_loop(0, n, one, 0)

    @pl.when(b > 0)
    def _():
        wait_chunks(nch_s[b - 1], 1 - slot)

    @pl.when(b == nb - 1)
    def _():
        zero_ref[...] = jnp.zeros_like(zero_ref)

        def tails(e, carry):
            copy_run(0, taila_s[e], tailq_s[e], lambda i: zero_ref)
            return carry + tailq_s[e]
        n_tail = lax.fori_loop(0, n_exp, tails, 0)
        wait_chunks(nch_s[b] + n_tail, slot)


def _dispatch(gates_t, q, qbase, dstq, nch, tailq, taila, h2, rt):
    n_exp, t = gates_t.shape
    d = h2.shape[1]
    nb = t // MOE_TB
    rmax = _moe_stage_rows(n_exp)
    grid_spec = pltpu.PrefetchScalarGridSpec(
        num_scalar_prefetch=6,
        grid=(nb,),
        in_specs=[
            pl.BlockSpec((n_exp, MOE_TB), lambda b, *_: (0, b)),
            pl.BlockSpec((1, 1, n_exp), lambda b, *_: (b, 0, 0)),
            pl.BlockSpec((1, 1, n_exp), lambda b, *_: (b, 0, 0)),
            pl.BlockSpec((MOE_TB, d), lambda b, *_: (b, 0)),
        ],
        out_specs=pl.BlockSpec(memory_space=pl.ANY),
        scratch_shapes=[pltpu.VMEM((2, rmax, d), BF16), pltpu.VMEM((MOE_CHUNK, d), BF16),
                        pltpu.SemaphoreType.DMA((2,))],
    )
    return pl.pallas_call(
        _dispatch_kernel,
        out_shape=jax.ShapeDtypeStruct((rt, d), BF16),
        grid_spec=grid_spec,
        compiler_params=pltpu.CompilerParams(dimension_semantics=("arbitrary",),
                                             vmem_limit_bytes=V7X_VMEM_LIMIT),
        name="moe_dispatch",
    )(q, qbase, dstq, nch, tailq, taila, gates_t, q.reshape(nb, 1, n_exp),
      qbase.reshape(nb, 1, n_exp), h2)


def _expert_kernel(te_s, tot_s, xs_ref, wg_ref, wu_ref, wd_ref, ys_ref):
    @pl.when(pl.program_id(0) < tot_s[0])
    def _():
        x = xs_ref[...]
        a = _silu(_dot(x, _bf(wg_ref[0]))) * _dot(x, _bf(wu_ref[0]))
        ys_ref[...] = _bf(_dot(_bf(a), _bf(wd_ref[0])))


def _experts(tile_expert, total_tiles, xs, wg, wu, wd):
    rt, d = xs.shape
    ff = wg.shape[2]

    def row_blk(j, te, tot):
        return (jnp.minimum(j, tot[0] - 1), 0)

    def w_blk(j, te, tot):
        return (te[j], 0, 0)

    grid_spec = pltpu.PrefetchScalarGridSpec(
        num_scalar_prefetch=2,
        grid=(rt // MOE_TM,),
        in_specs=[pl.BlockSpec((MOE_TM, d), row_blk),
                  pl.BlockSpec((1, d, ff), w_blk), pl.BlockSpec((1, d, ff), w_blk),
                  pl.BlockSpec((1, ff, d), w_blk)],
        out_specs=pl.BlockSpec((MOE_TM, d), row_blk),
    )
    return pl.pallas_call(
        _expert_kernel,
        out_shape=jax.ShapeDtypeStruct((rt, d), BF16),
        grid_spec=grid_spec,
        compiler_params=pltpu.CompilerParams(dimension_semantics=("arbitrary",),
                                             vmem_limit_bytes=V7X_VMEM_LIMIT),
        name="moe_experts",
    )(tile_expert, total_tiles, xs, wg, wu, wd)


def _combine_kernel(q_s, qb_s, dst_s, nch_s,
                    gt_ref, qcol_ref, qbcol_ref, h_ref, x1_ref, gate2_ref,
                    wsg_ref, wsu_ref, wsd_ref, ys_ref, o_ref, stage_ref, sem):
    b = pl.program_id(0)
    nb = pl.num_programs(0)
    slot = lax.rem(b, 2)
    n_exp, tb = gt_ref.shape
    rmax = stage_ref.shape[1]

    def fetch(bb, sl):
        def issue(e, carry):
            def one(c, carry2):
                pltpu.make_async_copy(ys_ref.at[_chunk_rows(dst_s[bb, e] + c), :],
                                      stage_ref.at[sl, _chunk_rows(qb_s[bb, e] + c), :],
                                      sem.at[sl]).start()
                return carry2
            lax.fori_loop(0, q_s[bb, e], one, 0)
            return carry
        lax.fori_loop(0, n_exp, issue, 0)

    @pl.when(b == 0)
    def _():
        stage_ref[...] = jnp.zeros_like(stage_ref)
        fetch(0, 0)

    def wait_one(c, carry):
        pltpu.make_async_copy(ys_ref.at[pl.ds(0, MOE_CHUNK), :],
                              stage_ref.at[slot, pl.ds(0, MOE_CHUNK), :], sem.at[slot]).wait()
        return carry
    lax.fori_loop(0, nch_s[b], wait_one, 0)

    @pl.when(b + 1 < nb)
    def _():
        fetch(b + 1, 1 - slot)

    gt = gt_ref[...]
    routed = jnp.where(gt > 0.0, 1.0, 0.0).astype(BF16)
    i0 = lax.broadcasted_iota(jnp.int32, (tb, tb), 0)
    i1 = lax.broadcasted_iota(jnp.int32, (tb, tb), 1)
    eye = jnp.where(i0 == i1, 1.0, 0.0).astype(BF16)
    routed_t = _dot_nt(eye, routed)
    gates_tok = _dot_nt(eye, _bf(gt))
    earlier = jnp.where(i1 < i0, 1.0, 0.0).astype(BF16)
    pos_t = _dot(earlier, _bf(routed_t))
    posm_t = _bf(jnp.where(routed_t > 0.0, pos_t, -1.0))

    qcol = qcol_ref[0]
    qbcol = qbcol_ref[0]
    chunk = lax.broadcasted_iota(jnp.int32, (n_exp, rmax), 1) >> MOE_CHUNK_SHIFT
    own = jnp.where(chunk >= qbcol, jnp.where(chunk < qbcol + qcol, 1.0, 0.0), 0.0)
    own_b = _bf(own)
    rank = _dot(posm_t, own_b)
    wexp = _dot(_bf(gates_tok), own_b)
    start = jnp.sum(own * qbcol.astype(F32), axis=0, keepdims=True) * MOE_CHUNK
    rel = lax.broadcasted_iota(jnp.int32, (1, rmax), 1).astype(F32) - start
    weights = _bf(jnp.where(rank == rel, wexp, 0.0))
    routed_out = _dot(weights, stage_ref[slot])

    h = h_ref[...]
    a = _silu(_dot(h, wsg_ref[...])) * _dot(h, wsu_ref[...])
    shared = _dot(_bf(a), wsd_ref[...])
    o_ref[...] = x1_ref[...] + gate2_ref[0] * (routed_out + shared)


def _combine(gates_t, q, qbase, dstq, nch, h2, x1, mod3, wsg, wsu, wsd, ys, seq):
    n_exp, t = gates_t.shape
    d = h2.shape[1]
    nb = t // MOE_TB
    per_b = seq // MOE_TB

    def full(a):
        return pl.BlockSpec(a.shape, lambda b, *_: (0,) * a.ndim)

    def rows(w):
        return pl.BlockSpec((MOE_TB, w), lambda b, *_: (b, 0))

    grid_spec = pltpu.PrefetchScalarGridSpec(
        num_scalar_prefetch=4,
        grid=(nb,),
        in_specs=[
            pl.BlockSpec((n_exp, MOE_TB), lambda b, *_: (0, b)),
            pl.BlockSpec((1, n_exp, 1), lambda b, *_: (b, 0, 0)),
            pl.BlockSpec((1, n_exp, 1), lambda b, *_: (b, 0, 0)),
            rows(d), rows(d),
            pl.BlockSpec((1, 1, d), lambda b, *_: (b // per_b, 0, 5)),
            full(wsg), full(wsu), full(wsd),
            pl.BlockSpec(memory_space=pl.ANY),
        ],
        out_specs=rows(d),
        scratch_shapes=[pltpu.VMEM((2, _moe_stage_rows(n_exp), d), BF16),
                        pltpu.SemaphoreType.DMA((2,))],
    )
    return pl.pallas_call(
        _combine_kernel,
        out_shape=jax.ShapeDtypeStruct((t, d), F32),
        grid_spec=grid_spec,
        compiler_params=pltpu.CompilerParams(dimension_semantics=("arbitrary",),
                                             vmem_limit_bytes=V7X_VMEM_LIMIT),
        name="moe_combine",
    )(q, qbase, dstq, nch, gates_t, q.reshape(nb, n_exp, 1), qbase.reshape(nb, n_exp, 1),
      h2, x1, mod3, wsg, wsu, wsd, ys)


def _moe_plan(q):
    nb, n_exp = q.shape
    cpt = MOE_TM // MOE_CHUNK
    qbase = jnp.cumsum(q, axis=1) - q
    nch = jnp.sum(q, axis=1)
    per_exp = jnp.sum(q, axis=0)
    tiles = (per_exp + cpt - 1) // cpt
    seg = tiles * cpt
    off = jnp.cumsum(seg) - seg
    dstq = off[None, :] + jnp.cumsum(q, axis=0) - q
    tailq = seg - per_exp
    taila = off + per_exp
    tile_end = jnp.cumsum(tiles)
    total = tile_end[-1]
    n_tiles_max = _moe_max_tiles(nb, n_exp)
    j = jnp.minimum(jnp.arange(n_tiles_max, dtype=jnp.int32), total - 1)
    tile_expert = jnp.sum((j[:, None] >= tile_end[None, :]).astype(jnp.int32), axis=1)
    return qbase, dstq, nch, tailq, taila, tile_expert, total.reshape(1)


def _moe_stage_rows(n_exp):
    rows = MOE_TB * TOP_K + n_exp * (MOE_CHUNK - 1)
    return -(-rows // MOE_ROWGROUP) * MOE_ROWGROUP


def _moe_max_tiles(nb, n_exp):
    cpt = MOE_TM // MOE_CHUNK
    max_chunks = nb * (MOE_TB * TOP_K // MOE_CHUNK + n_exp)
    return -(-max_chunks // cpt) + n_exp


def _moe(h2, gates_t, q3, x1, mod3, wsg, wsu, wsd, wg, wu, wd, seq):
    n_exp, t = gates_t.shape
    nb = t // MOE_TB
    q = q3.reshape(nb, n_exp)
    qbase, dstq, nch, tailq, taila, tile_expert, total = _moe_plan(q)
    rt = _moe_max_tiles(nb, n_exp) * MOE_TM
    xs = _dispatch(gates_t, q, qbase, dstq, nch, tailq, taila, h2, rt)
    ys = _experts(tile_expert, total, xs, wg, wu, wd)
    return _combine(gates_t, q, qbase, dstq, nch, h2, x1, mod3, wsg, wsu, wsd, ys, seq)


def _pad_heads(w, heads, width):
    lead = w.shape[:-1]
    w = w.reshape(lead + (heads, width))
    w = jnp.pad(w, [(0, 0)] * len(lead) + [(0, 0), (0, LANES - width)])
    return w.reshape(lead + (heads * LANES,))


def kernel(x, c, positions, w_ada, b_ada, g_norm1, w_in, g_na_q, g_na_k, na_rpb, g_q_lat, w_uq,
           g_kv_lat, w_ukv, g_mla_q, g_mla_k, w_proj_na, w_proj_mla, w_out, g_norm2, w_router,
           e_bias, w_exp_gate, w_exp_up, w_exp_down, w_sh_gate, w_sh_up, w_sh_down):
    bsz, seq, d = x.shape
    t = bsz * seq
    depth = w_ada.shape[0]
    na_w = NA_HEADS * NA_HEAD_DIM
    q_rank = g_q_lat.shape[1]
    kv_rank = g_kv_lat.shape[1]
    n_rows = seq // GRID_W

    pos = positions.reshape(t, 1)
    half = MLA_ROPE_DIM // 2
    inv_freq = ROPE_THETA ** (-jnp.arange(half, dtype=F32) / half)
    freq = jnp.zeros((LANES,), F32).at[MLA_NOPE_DIM:MLA_QK_DIM].set(jnp.tile(inv_freq, 2))
    freq = freq.reshape(1, LANES)

    x2 = x.reshape(t, d)
    for l in range(depth):
        mod3 = _adaln(c, w_ada[l], b_ada[l]).reshape(bsz, 1, 6 * d)

        wi = w_in[l]
        o_lat = 3 * na_w
        o_rot = o_lat + q_rank + kv_rank
        o_gate = o_rot + MLA_ROPE_DIM
        wqkv = _bf(wi[:, :o_lat])
        w_rot = jnp.pad(wi[:, o_rot:o_gate], ((0, 0), (MLA_NOPE_DIM, LANES - MLA_QK_DIM)))
        wlat = _bf(jnp.concatenate([wi[:, o_lat:o_rot], w_rot], axis=1))
        wgate = _bf(wi[:, o_gate:])
        gq = (jnp.tile(g_na_q[l], NA_HEADS) * NA_HEAD_DIM ** -0.5).reshape(1, na_w)
        gk = jnp.tile(g_na_k[l], NA_HEADS).reshape(1, na_w)
        wuq = _bf(_pad_heads(w_uq[l], MLA_HEADS, MLA_QK_DIM))
        wukv = w_ukv[l].reshape(kv_rank, MLA_HEADS, MLA_NOPE_DIM + MLA_V_DIM)
        wuk = _bf(_pad_heads(wukv[:, :, :MLA_NOPE_DIM].reshape(kv_rank, -1), MLA_HEADS, MLA_NOPE_DIM))
        wuv = _bf(wukv[:, :, MLA_NOPE_DIM:].reshape(kv_rank, MLA_HEADS * MLA_V_DIM))
        gmq = _pad_heads(jnp.tile(g_mla_q[l], MLA_HEADS) * MLA_QK_DIM ** -0.5, MLA_HEADS,
                         MLA_QK_DIM).reshape(1, -1)
        gmk = _pad_heads(jnp.tile(g_mla_k[l], MLA_HEADS), MLA_HEADS, MLA_QK_DIM).reshape(1, -1)

        qa, ka, va, qm, km, vm, sgn, sgm = _inproj(
            x2, mod3, g_norm1[l].reshape(1, d), wqkv, wlat, wgate, gq, gk,
            g_q_lat[l].reshape(1, q_rank), g_kv_lat[l].reshape(1, kv_rank), wuq, wuk, wuv,
            gmq, gmk, pos, freq, seq)

        bias = _na_bias(na_rpb[l], n_rows)
        y_na = _na_attention(qa, ka, va, bias, bsz, seq)
        y_mla = _mla_attention(qm, km, vm, bsz, seq)

        x1, h2, lt = _merge(x2, y_na, y_mla, sgn, sgm, _bf(w_proj_na[l]), _bf(w_proj_mla[l]),
                            _bf(w_out[l]), mod3, g_norm2[l].reshape(1, d), w_router[l].T, seq)
        gates_t, q3 = _route(lt, e_bias[l])
        x2 = _moe(h2, gates_t, q3, x1, mod3, _bf(w_sh_gate[l]), _bf(w_sh_up[l]),
                  _bf(w_sh_down[l]), w_exp_gate[l], w_exp_up[l], w_exp_down[l], seq)
    return x2.reshape(bsz, seq, d)
```

```python
import functools

import jax
import jax.numpy as jnp
import numpy as np
from jax import lax
from jax.experimental import pallas as pl
from jax.experimental.pallas import tpu as pltpu

GRID_W = 64
NA_HEADS = 8
NA_HEAD_DIM = 64
NA_WIN_ROWS = 8
NA_WIN_COLS = 16
MLA_HEADS = 8
MLA_NOPE_DIM = 64
MLA_ROPE_DIM = 32
MLA_V_DIM = 64
MLA_QK_DIM = MLA_NOPE_DIM + MLA_ROPE_DIM
ROPE_THETA = 10000.0
N_GROUPS = 8
TOPK_GROUPS = 4
TOP_K = 8
ROUTED_SCALE = 2.5
EPS = 1e-6
NEG_BIG = -1e30

LANES = 128
V7X_VMEM_LIMIT = 56 * 1024 * 1024

NA_QROWS = 4
NA_BAND = 12
NA_BLOCK_TYPES = 3
MOE_TB = 256
MOE_CHUNK_SHIFT = 4
MOE_CHUNK = 1 << MOE_CHUNK_SHIFT
MOE_TM = 512
MOE_ROWGROUP = 512

F32 = jnp.float32
BF16 = jnp.bfloat16


def _bf(x):
    return x.astype(BF16)


def _dot(a, b):
    return jnp.dot(a, b, preferred_element_type=F32)


def _dot_nt(a, b):
    return lax.dot_general(a, b, (((1,), (1,)), ((), ())), preferred_element_type=F32)


def _split(x):
    hi = _bf(x)
    lo = _bf(x - hi.astype(F32))
    return hi, lo


def _dot3(a, b):
    ah, al = _split(a)
    bh, bl = _split(b)
    return _dot(ah, bh) + (_dot(ah, bl) + _dot(al, bh))


def _dot3_nt(a, b):
    ah, al = _split(a)
    bh, bl = _split(b)
    return _dot_nt(ah, bh) + (_dot_nt(ah, bl) + _dot_nt(al, bh))


def _sigmoid(x):
    return 1.0 / (1.0 + jnp.exp(-x))


def _silu(x):
    return x * _sigmoid(x)


def _rms(x, n):
    ss = jnp.sum(x * x, axis=-1, keepdims=True)
    return x * lax.rsqrt(ss * (1.0 / n) + EPS)


def _adaln_kernel(c_ref, w_ref, b_ref, o_ref):
    c = c_ref[...]
    o_ref[...] = _dot3(_silu(c), w_ref[...]) + b_ref[...]


def _adaln(c, w, b):
    bsz, d = c.shape
    n = w.shape[1]
    tn = 1024
    return pl.pallas_call(
        _adaln_kernel,
        out_shape=jax.ShapeDtypeStruct((bsz, n), F32),
        grid=(n // tn,),
        in_specs=[
            pl.BlockSpec((bsz, d), lambda j: (0, 0)),
            pl.BlockSpec((d, tn), lambda j: (0, j)),
            pl.BlockSpec((1, tn), lambda j: (0, j)),
        ],
        out_specs=pl.BlockSpec((bsz, tn), lambda j: (0, j)),
        compiler_params=pltpu.CompilerParams(dimension_semantics=("arbitrary",)),
        name="adaln",
    )(c, w, b.reshape(1, n))


def _inproj_kernel(x_ref, shift_ref, scale_ref, g1_ref, wqkv_ref, wlat_ref, wgate_ref,
                   gq_ref, gk_ref, gql_ref, gkvl_ref, wuq_ref, wuk_ref, wuv_ref,
                   gmq_ref, gmk_ref, pos_ref, freq_ref,
                   qa_ref, ka_ref, va_ref, qm_ref, km_ref, vm_ref, sgn_ref, sgm_ref):
    d = x_ref.shape[1]
    x = x_ref[...]
    h = _rms(x, d) * g1_ref[...]
    h = h * (1.0 + scale_ref[0]) + shift_ref[0]
    hb = _bf(h)

    qkv = _dot(hb, wqkv_ref[...])
    lat = _dot(hb, wlat_ref[...])
    gts = _dot(hb, wgate_ref[...])
    sgn_ref[...] = _bf(_sigmoid(gts[:, :d]))
    sgm_ref[...] = _bf(_sigmoid(gts[:, d:]))

    na_w = NA_HEADS * NA_HEAD_DIM
    lane = lax.broadcasted_iota(jnp.int32, (1, LANES), 1)
    lo_half = lane < NA_HEAD_DIM
    for p in range(na_w // LANES):
        sl = slice(p * LANES, (p + 1) * LANES)
        for src_off, g_ref, dst_ref in ((0, gq_ref, qa_ref), (na_w, gk_ref, ka_ref)):
            t = qkv[:, src_off + p * LANES: src_off + (p + 1) * LANES]
            sq = t * t
            s_lo = jnp.sum(jnp.where(lo_half, sq, 0.0), axis=-1, keepdims=True)
            s_hi = jnp.sum(jnp.where(lo_half, 0.0, sq), axis=-1, keepdims=True)
            r = jnp.where(lo_half,
                          lax.rsqrt(s_lo * (1.0 / NA_HEAD_DIM) + EPS),
                          lax.rsqrt(s_hi * (1.0 / NA_HEAD_DIM) + EPS))
            dst_ref[:, sl] = _bf(t * r * g_ref[:, sl])
    va_ref[...] = _bf(qkv[:, 2 * na_w: 3 * na_w])

    q_rank = gql_ref.shape[1]
    kv_rank = gkvl_ref.shape[1]
    qln = _rms(lat[:, :q_rank], q_rank) * gql_ref[...]
    kvn = _bf(_rms(lat[:, q_rank:q_rank + kv_rank], kv_rank) * gkvl_ref[...])
    qpre = _dot(_bf(qln), wuq_ref[...])
    knope = _dot(kvn, wuk_ref[...])
    vm_ref[...] = _bf(_dot(kvn, wuv_ref[...]))
    krot = lat[:, q_rank + kv_rank:]

    ang = pos_ref[...].astype(F32) * freq_ref[...]
    cosv = jnp.cos(ang)
    sinv = jnp.sin(ang)
    half = MLA_ROPE_DIM // 2
    in_rope = (lane >= MLA_NOPE_DIM) & (lane < MLA_QK_DIM)
    first_half = lane < MLA_NOPE_DIM + half
    c_tab = jnp.where(lane < MLA_NOPE_DIM, 1.0, jnp.where(in_rope, cosv, 0.0))
    s_up = jnp.where(in_rope & jnp.logical_not(first_half), sinv, 0.0)
    s_dn = jnp.where(in_rope & first_half, -sinv, 0.0)

    def rope(t):
        return t * c_tab + pltpu.roll(t, half, 1) * s_up + pltpu.roll(t, LANES - half, 1) * s_dn

    kr = rope(krot)
    for hd in range(MLA_HEADS):
        sl = slice(hd * LANES, (hd + 1) * LANES)
        qh = rope(qpre[:, sl])
        qm_ref[:, sl] = _bf(_rms(qh, MLA_QK_DIM) * gmq_ref[:, sl])
        kh = knope[:, sl] + kr
        km_ref[:, sl] = _bf(_rms(kh, MLA_QK_DIM) * gmk_ref[:, sl])


def _inproj(x2, mod3, g1, wqkv, wlat, wgate, gq, gk, gql, gkvl, wuq, wuk, wuv, gmq, gmk,
            pos, freq, seq):
    t, d = x2.shape
    tm = 256
    per_b = seq // tm
    na_w = NA_HEADS * NA_HEAD_DIM
    mla_w = MLA_HEADS * LANES
    v_w = MLA_HEADS * MLA_V_DIM

    def full(a):
        return pl.BlockSpec(a.shape, lambda i: (0,) * a.ndim)

    def rows(w):
        return pl.BlockSpec((tm, w), lambda i: (i, 0))

    out_shapes = (
        jax.ShapeDtypeStruct((t, na_w), BF16), jax.ShapeDtypeStruct((t, na_w), BF16),
        jax.ShapeDtypeStruct((t, na_w), BF16),
        jax.ShapeDtypeStruct((t, mla_w), BF16), jax.ShapeDtypeStruct((t, mla_w), BF16),
        jax.ShapeDtypeStruct((t, v_w), BF16),
        jax.ShapeDtypeStruct((t, d), BF16), jax.ShapeDtypeStruct((t, d), BF16),
    )
    return pl.pallas_call(
        _inproj_kernel,
        out_shape=out_shapes,
        grid=(t // tm,),
        in_specs=[
            rows(d),
            pl.BlockSpec((1, 1, d), lambda i: (i // per_b, 0, 0)),
            pl.BlockSpec((1, 1, d), lambda i: (i // per_b, 0, 1)),
            full(g1), full(wqkv), full(wlat), full(wgate), full(gq), full(gk), full(gql),
            full(gkvl), full(wuq), full(wuk), full(wuv), full(gmq), full(gmk),
            pl.BlockSpec((tm, 1), lambda i: (i, 0)),
            full(freq),
        ],
        out_specs=(rows(na_w), rows(na_w), rows(na_w), rows(mla_w), rows(mla_w), rows(v_w),
                   rows(d), rows(d)),
        compiler_params=pltpu.CompilerParams(dimension_semantics=("arbitrary",),
                                             vmem_limit_bytes=V7X_VMEM_LIMIT),
        name="inproj",
    )(x2, mod3, mod3, g1, wqkv, wlat, wgate, gq, gk, gql, gkvl, wuq, wuk, wuv, gmq, gmk,
      pos, freq)


def _na_block_geometry(block_type, n_rows):
    if block_type == 0:
        return 0, 0
    if block_type == 1:
        r0 = NA_QROWS
        return r0, r0 - NA_WIN_ROWS // 2
    return n_rows - NA_QROWS, n_rows - NA_BAND


def _na_bias_kernel(rpb_ref, o_ref, m_ref, *, n_rows):
    hd = pl.program_id(0)
    n_dr = 2 * NA_WIN_ROWS - 1
    n_dc = 2 * NA_WIN_COLS - 1
    qc = lax.broadcasted_iota(jnp.int32, (GRID_W, LANES), 0)
    kc = lax.broadcasted_iota(jnp.int32, (GRID_W, LANES), 1) & (GRID_W - 1)
    dc = jnp.clip(kc - qc, -(NA_WIN_COLS - 1), NA_WIN_COLS - 1) + (NA_WIN_COLS - 1)
    cstart = jnp.clip(qc - NA_WIN_COLS // 2, 0, GRID_W - NA_WIN_COLS)
    col_ok = (kc >= cstart) & (kc < cstart + NA_WIN_COLS)
    for i_dr in range(n_dr):
        acc = jnp.zeros((GRID_W, LANES), F32)
        for t in range(n_dc):
            acc = jnp.where(dc == t, rpb_ref[hd, i_dr * n_dc + t], acc)
        m_ref[i_dr] = jnp.where(col_ok, acc, NEG_BIG)
    neg = jnp.full((GRID_W, LANES), NEG_BIG, F32)
    lo_half = lax.broadcasted_iota(jnp.int32, (GRID_W, LANES), 1) < GRID_W
    kh = NA_WIN_ROWS
    for bt in range(NA_BLOCK_TYPES):
        r0, start = _na_block_geometry(bt, n_rows)
        for i in range(NA_QROWS):
            r = r0 + i
            rs = min(max(r - kh // 2, 0), n_rows - kh)
            for jp in range(NA_BAND // 2):
                halves = []
                for j in (2 * jp, 2 * jp + 1):
                    krow = start + j
                    if rs <= krow < rs + kh:
                        halves.append(m_ref[krow - r + (NA_WIN_ROWS - 1)])
                    else:
                        halves.append(neg)
                tile = jnp.where(lo_half, halves[0], halves[1])
                o_ref[bt, 0, i * GRID_W:(i + 1) * GRID_W, jp * LANES:(jp + 1) * LANES] = tile


def _na_bias(rpb, n_rows):
    heads = rpb.shape[0]
    nq = NA_QROWS * GRID_W
    nk = NA_BAND * GRID_W
    rpb2 = rpb.reshape(heads, -1)
    return pl.pallas_call(
        functools.partial(_na_bias_kernel, n_rows=n_rows),
        out_shape=jax.ShapeDtypeStruct((NA_BLOCK_TYPES, heads, nq, nk), F32),
        grid=(heads,),
        in_specs=[pl.BlockSpec(memory_space=pltpu.SMEM)],
        out_specs=pl.BlockSpec((NA_BLOCK_TYPES, 1, nq, nk), lambda hd: (0, hd, 0, 0)),
        scratch_shapes=[pltpu.VMEM((2 * NA_WIN_ROWS - 1, GRID_W, LANES), F32)],
        compiler_params=pltpu.CompilerParams(dimension_semantics=("arbitrary",)),
        name="na_bias",
    )(rpb2)


def _na_kernel(q_ref, k_ref, v_ref, bias_ref, o_ref, *, n_blocks, n_rows):
    blk = pl.program_id(1)
    start_row = jnp.where(blk == 0, 0,
                          jnp.where(blk == n_blocks - 1, n_rows - NA_BAND,
                                    blk * NA_QROWS - NA_WIN_ROWS // 2))
    off = pl.multiple_of(start_row * GRID_W, GRID_W)
    nk = NA_BAND * GRID_W
    lane = lax.broadcasted_iota(jnp.int32, (1, LANES), 1)
    for p in range(NA_HEADS * NA_HEAD_DIM // LANES):
        sl = slice(p * LANES, (p + 1) * LANES)
        qp = q_ref[:, sl]
        kb = k_ref[pl.ds(off, nk), sl]
        vb = v_ref[pl.ds(off, nk), sl]
        acc = jnp.zeros((q_ref.shape[0], LANES), F32)
        for hh in range(2):
            mine = (lane < NA_HEAD_DIM) if hh == 0 else (lane >= NA_HEAD_DIM)
            qh = jnp.where(mine, qp, jnp.zeros_like(qp))
            s = _dot_nt(qh, kb) + bias_ref[0, 2 * p + hh]
            m = jnp.max(s, axis=-1, keepdims=True)
            e = jnp.exp(s - m)
            l = jnp.sum(e, axis=-1, keepdims=True)
            vh = jnp.where(mine, vb, jnp.zeros_like(vb))
            acc = acc + _dot(_bf(e), vh) / l
        o_ref[:, sl] = _bf(acc)


def _na_attention(qa, ka, va, bias, bsz, seq):
    t, w = qa.shape
    n_rows = seq // GRID_W
    n_blocks = n_rows // NA_QROWS
    tq = NA_QROWS * GRID_W
    nk = NA_BAND * GRID_W
    heads = bias.shape[1]

    def btype(blk):
        return jnp.where(blk == 0, 0, jnp.where(blk == n_blocks - 1, 2, 1))

    return pl.pallas_call(
        functools.partial(_na_kernel, n_blocks=n_blocks, n_rows=n_rows),
        out_shape=jax.ShapeDtypeStruct((t, w), BF16),
        grid=(bsz, n_blocks),
        in_specs=[
            pl.BlockSpec((tq, w), lambda b, blk: (b * n_blocks + blk, 0)),
            pl.BlockSpec((seq, w), lambda b, blk: (b, 0)),
            pl.BlockSpec((seq, w), lambda b, blk: (b, 0)),
            pl.BlockSpec((1, heads, tq, nk), lambda b, blk: (btype(blk), 0, 0, 0)),
        ],
        out_specs=pl.BlockSpec((tq, w), lambda b, blk: (b * n_blocks + blk, 0)),
        compiler_params=pltpu.CompilerParams(dimension_semantics=("arbitrary", "arbitrary"),
                                             vmem_limit_bytes=V7X_VMEM_LIMIT),
        name="na_attn",
    )(qa, ka, va, bias)


def _mla_kernel(q_ref, k_ref, v_ref, o_ref):
    lane = lax.broadcasted_iota(jnp.int32, (1, LANES), 1)
    v = v_ref[...]
    acc = jnp.zeros((q_ref.shape[0], LANES), F32)
    for hh in range(2):
        sl = slice(hh * LANES, (hh + 1) * LANES)
        s = _dot_nt(q_ref[:, sl], k_ref[:, sl])
        m = jnp.max(s, axis=-1, keepdims=True)
        e = jnp.exp(s - m)
        l = jnp.sum(e, axis=-1, keepdims=True)
        mine = (lane < MLA_V_DIM) if hh == 0 else (lane >= MLA_V_DIM)
        vh = jnp.where(mine, v, jnp.zeros_like(v))
        acc = acc + _dot(_bf(e), vh) / l
    o_ref[...] = _bf(acc)


def _mla_attention(qm, km, vm, bsz, seq):
    t = qm.shape[0]
    tq = 512
    nq = seq // tq
    pairs = MLA_HEADS // 2
    return pl.pallas_call(
        _mla_kernel,
        out_shape=jax.ShapeDtypeStruct((t, MLA_HEADS * MLA_V_DIM), BF16),
        grid=(bsz, pairs, nq),
        in_specs=[
            pl.BlockSpec((tq, 2 * LANES), lambda b, p, i: (b * nq + i, p)),
            pl.BlockSpec((seq, 2 * LANES), lambda b, p, i: (b, p)),
            pl.BlockSpec((seq, LANES), lambda b, p, i: (b, p)),
        ],
        out_specs=pl.BlockSpec((tq, LANES), lambda b, p, i: (b * nq + i, p)),
        compiler_params=pltpu.CompilerParams(
            dimension_semantics=("arbitrary", "arbitrary", "arbitrary"),
            vmem_limit_bytes=V7X_VMEM_LIMIT),
        name="mla_attn",
    )(qm, km, vm)


def _merge_kernel(x_ref, yna_ref, ymla_ref, sgn_ref, sgm_ref, wpn_ref, wpm_ref, wout_ref,
                  gate1_ref, shift2_ref, scale2_ref, g2_ref, wrt_ref,
                  x1_ref, h2_ref, lt_ref):
    d = x_ref.shape[1]
    merged = (sgn_ref[...].astype(F32) * _dot(yna_ref[...], wpn_ref[...])
              + sgm_ref[...].astype(F32) * _dot(ymla_ref[...], wpm_ref[...]))
    x1 = x_ref[...] + gate1_ref[0] * _dot(_bf(merged), wout_ref[...])
    x1_ref[...] = x1
    h2 = _rms(x1, d) * g2_ref[...]
    h2 = h2 * (1.0 + scale2_ref[0]) + shift2_ref[0]
    h2_ref[...] = _bf(h2)
    lt_ref[...] = _dot3_nt(wrt_ref[...], h2)


def _merge(x2, yna, ymla, sgn, sgm, wpn, wpm, wout, mod3, g2, wrt, seq):
    t, d = x2.shape
    tm = 512
    per_b = seq // tm
    n_exp = wrt.shape[0]

    def full(a):
        return pl.BlockSpec(a.shape, lambda i: (0,) * a.ndim)

    def rows(w):
        return pl.BlockSpec((tm, w), lambda i: (i, 0))

    def modblk(j):
        return pl.BlockSpec((1, 1, d), lambda i: (i // per_b, 0, j))

    return pl.pallas_call(
        _merge_kernel,
        out_shape=(jax.ShapeDtypeStruct((t, d), F32), jax.ShapeDtypeStruct((t, d), BF16),
                   jax.ShapeDtypeStruct((n_exp, t), F32)),
        grid=(t // tm,),
        in_specs=[rows(d), rows(yna.shape[1]), rows(ymla.shape[1]), rows(d), rows(d),
                  full(wpn), full(wpm), full(wout),
                  modblk(2), modblk(3), modblk(4), full(g2), full(wrt)],
        out_specs=(rows(d), rows(d), pl.BlockSpec((n_exp, tm), lambda i: (0, i))),
        compiler_params=pltpu.CompilerParams(dimension_semantics=("arbitrary",),
                                             vmem_limit_bytes=V7X_VMEM_LIMIT),
        name="merge",
    )(x2, yna, ymla, sgn, sgm, wpn, wpm, wout, mod3, mod3, mod3, g2, wrt)


def _route_kernel(lt_ref, eb_ref, o_ref, q_ref):
    n_exp, tn = lt_ref.shape
    per_g = n_exp // N_GROUPS
    neg_inf = -jnp.inf
    sc = _sigmoid(lt_ref[...])
    sel = sc + eb_ref[...]
    sc3 = sc.reshape(N_GROUPS, per_g, tn)
    g3 = sel.reshape(N_GROUPS, per_g, tn)
    io = lax.broadcasted_iota(jnp.int32, (N_GROUPS, per_g, tn), 1)
    gio = lax.broadcasted_iota(jnp.int32, (N_GROUPS, per_g, tn), 0)
    eio = gio * per_g + io

    m1 = jnp.max(g3, axis=1, keepdims=True)
    i1 = jnp.min(jnp.where(g3 == m1, io, per_g), axis=1, keepdims=True)
    m2 = jnp.max(jnp.where(io == i1, neg_inf, g3), axis=1, keepdims=True)
    gs = m1 + m2

    g1io = lax.broadcasted_iota(jnp.int32, (N_GROUPS, 1, tn), 0)
    gsel = jnp.zeros((N_GROUPS, 1, tn), F32)
    cur = gs
    for _ in range(TOPK_GROUPS):
        m = jnp.max(cur, axis=0, keepdims=True)
        i = jnp.min(jnp.where(cur == m, g1io, N_GROUPS), axis=0, keepdims=True)
        pick = g1io == i
        gsel = jnp.where(pick, 1.0, gsel)
        cur = jnp.where(pick, neg_inf, cur)

    cur = jnp.where(gsel > 0.0, g3, neg_inf)
    chosen = jnp.zeros((N_GROUPS, per_g, tn), F32)
    for _ in range(TOP_K):
        m = jnp.max(jnp.max(cur, axis=1, keepdims=True), axis=0, keepdims=True)
        cand = jnp.where(cur == m, eio, n_exp)
        i = jnp.min(jnp.min(cand, axis=1, keepdims=True), axis=0, keepdims=True)
        pick = eio == i
        chosen = jnp.where(pick, 1.0, chosen)
        cur = jnp.where(pick, neg_inf, cur)

    w = jnp.where(chosen > 0.0, sc3, 0.0)
    tot = jnp.sum(jnp.sum(w, axis=1, keepdims=True), axis=0, keepdims=True)
    gates = (w / tot * ROUTED_SCALE).reshape(n_exp, tn)
    o_ref[...] = gates
    routed = jnp.where(gates > 0.0, 1.0, 0.0).astype(BF16)
    n_row = _dot_nt(jnp.ones((8, tn), BF16), routed)[0:1]
    q_ref[0] = jnp.floor((n_row + (MOE_CHUNK - 1)) * (1.0 / MOE_CHUNK)).astype(jnp.int32)


def _route(lt, e_bias):
    n_exp, t = lt.shape
    tn = MOE_TB
    return pl.pallas_call(
        _route_kernel,
        out_shape=(jax.ShapeDtypeStruct((n_exp, t), F32),
                   jax.ShapeDtypeStruct((t // tn, 1, n_exp), jnp.int32)),
        grid=(t // tn,),
        in_specs=[pl.BlockSpec((n_exp, tn), lambda i: (0, i)),
                  pl.BlockSpec((n_exp, 1), lambda i: (0, 0))],
        out_specs=(pl.BlockSpec((n_exp, tn), lambda i: (0, i)),
                   pl.BlockSpec((1, 1, n_exp), lambda i: (i, 0, 0))),
        compiler_params=pltpu.CompilerParams(dimension_semantics=("arbitrary",)),
        name="route",
    )(lt, e_bias.reshape(n_exp, 1))


def _for_each_chunk(n, fn):
    def quad(j, carry):
        for u in range(4):
            fn(j * 4 + u)
        return carry
    lax.fori_loop(0, n >> 2, quad, 0)
    base = (n >> 2) << 2
    for u in range(3):
        @pl.when(base + u < n)
        def _():
            fn(base + u)


def _dispatch_kernel(dtab_s, nch_s, tailq_s, taila_s,
                     gt_ref, qrow_ref, qbrow_ref, h_ref, xs_ref, stage_ref, zero_ref, sem):
    b = pl.program_id(0)
    nb = pl.num_programs(0)
    slot = lax.rem(b, 2)
    n_exp, tb = gt_ref.shape
    rmax = stage_ref.shape[1] * MOE_CHUNK
    cpg = MOE_ROWGROUP // MOE_CHUNK

    routed = gt_ref[...] > 0.0
    before = (lax.broadcasted_iota(jnp.int32, (tb, tb), 0)
              < lax.broadcasted_iota(jnp.int32, (tb, tb), 1))
    pos = _dot(jnp.where(routed, 1.0, 0.0).astype(BF16), jnp.where(before, 1.0, 0.0).astype(BF16))
    posm = _bf(jnp.where(routed, pos, -1.0))
    qrow = qrow_ref[0]
    qbrow = qbrow_ref[0]
    qbrow_f = qbrow.astype(F32)
    h = h_ref[...]
    for g in range(rmax // MOE_ROWGROUP):
        @pl.when(g * MOE_ROWGROUP < nch_s[b] * MOE_CHUNK)
        def _():
            r0 = g * MOE_ROWGROUP
            chunk = (lax.broadcasted_iota(jnp.int32, (MOE_ROWGROUP, n_exp), 0) + r0) >> MOE_CHUNK_SHIFT
            own = jnp.where(chunk >= qbrow, jnp.where(chunk < qbrow + qrow, 1.0, 0.0), 0.0)
            rank = _dot(_bf(own), posm)
            start = jnp.sum(own * qbrow_f, axis=-1, keepdims=True) * MOE_CHUNK
            rel = (lax.broadcasted_iota(jnp.int32, (MOE_ROWGROUP, 1), 0) + r0).astype(F32) - start
            onehot = jnp.where(rank == rel, 1.0, 0.0).astype(BF16)
            rows = _bf(_dot(onehot, h))
            stage_ref[slot, g * cpg:(g + 1) * cpg] = rows.reshape(cpg, MOE_CHUNK, rows.shape[1])

    def start_chunk(i):
        pltpu.make_async_copy(stage_ref.at[slot, i], xs_ref.at[dtab_s[b, i]], sem.at[slot]).start()
    _for_each_chunk(nch_s[b], start_chunk)

    def wait_chunks(n, sl):
        def wait_chunk(i):
            pltpu.make_async_copy(zero_ref, xs_ref.at[0], sem.at[sl]).wait()
        _for_each_chunk(n, wait_chunk)

    @pl.when(b > 0)
    def _():
        wait_chunks(nch_s[b - 1], 1 - slot)

    @pl.when(b == nb - 1)
    def _():
        zero_ref[...] = jnp.zeros_like(zero_ref)

        def tails(e, carry):
            def one(c, carry2):
                pltpu.make_async_copy(zero_ref, xs_ref.at[taila_s[e] + c], sem.at[slot]).start()
                return carry2
            lax.fori_loop(0, tailq_s[e], one, 0)
            return carry + tailq_s[e]
        n_tail = lax.fori_loop(0, n_exp, tails, 0)
        wait_chunks(nch_s[b] + n_tail, slot)


def _dispatch(gates_t, q, qbase, dtab, nch, tailq, taila, h2, n_chunks):
    n_exp, t = gates_t.shape
    d = h2.shape[1]
    nb = t // MOE_TB
    rmax = _moe_stage_rows(n_exp)
    grid_spec = pltpu.PrefetchScalarGridSpec(
        num_scalar_prefetch=4,
        grid=(nb,),
        in_specs=[
            pl.BlockSpec((n_exp, MOE_TB), lambda b, *_: (0, b)),
            pl.BlockSpec((1, 1, n_exp), lambda b, *_: (b, 0, 0)),
            pl.BlockSpec((1, 1, n_exp), lambda b, *_: (b, 0, 0)),
            pl.BlockSpec((MOE_TB, d), lambda b, *_: (b, 0)),
        ],
        out_specs=pl.BlockSpec(memory_space=pl.ANY),
        scratch_shapes=[pltpu.VMEM((2, rmax // MOE_CHUNK, MOE_CHUNK, d), BF16),
                        pltpu.VMEM((MOE_CHUNK, d), BF16), pltpu.SemaphoreType.DMA((2,))],
    )
    return pl.pallas_call(
        _dispatch_kernel,
        out_shape=jax.ShapeDtypeStruct((n_chunks, MOE_CHUNK, d), BF16),
        grid_spec=grid_spec,
        compiler_params=pltpu.CompilerParams(dimension_semantics=("arbitrary",),
                                             vmem_limit_bytes=V7X_VMEM_LIMIT),
        name="moe_dispatch",
    )(dtab, nch, tailq, taila, gates_t, q.reshape(nb, 1, n_exp), qbase.reshape(nb, 1, n_exp), h2)


def _expert_kernel(te_s, tot_s, xs_ref, wg_ref, wu_ref, wd_ref, ys_ref):
    @pl.when(pl.program_id(0) < tot_s[0])
    def _():
        x = xs_ref[...]
        a = _silu(_dot(x, _bf(wg_ref[0]))) * _dot(x, _bf(wu_ref[0]))
        ys_ref[...] = _bf(_dot(_bf(a), _bf(wd_ref[0])))


def _experts(tile_expert, total_tiles, xs, wg, wu, wd):
    rt, d = xs.shape
    ff = wg.shape[2]

    def row_blk(j, te, tot):
        return (jnp.minimum(j, tot[0] - 1), 0)

    def w_blk(j, te, tot):
        return (te[j], 0, 0)

    grid_spec = pltpu.PrefetchScalarGridSpec(
        num_scalar_prefetch=2,
        grid=(rt // MOE_TM,),
        in_specs=[pl.BlockSpec((MOE_TM, d), row_blk),
                  pl.BlockSpec((1, d, ff), w_blk), pl.BlockSpec((1, d, ff), w_blk),
                  pl.BlockSpec((1, ff, d), w_blk)],
        out_specs=pl.BlockSpec((MOE_TM, d), row_blk),
    )
    return pl.pallas_call(
        _expert_kernel,
        out_shape=jax.ShapeDtypeStruct((rt, d), BF16),
        grid_spec=grid_spec,
        compiler_params=pltpu.CompilerParams(dimension_semantics=("arbitrary",),
                                             vmem_limit_bytes=V7X_VMEM_LIMIT),
        name="moe_experts",
    )(tile_expert, total_tiles, xs, wg, wu, wd)


def _combine_kernel(dtab_s, nch_s,
                    gt_ref, qcol_ref, qbcol_ref, h_ref, x1_ref, gate2_ref,
                    wsg_ref, wsu_ref, wsd_ref, ys_ref, o_ref, stage_ref, sem):
    b = pl.program_id(0)
    nb = pl.num_programs(0)
    slot = lax.rem(b, 2)
    n_exp, tb = gt_ref.shape
    rmax = stage_ref.shape[1] * MOE_CHUNK

    def fetch(bb, sl):
        def start_chunk(i):
            pltpu.make_async_copy(ys_ref.at[dtab_s[bb, i]], stage_ref.at[sl, i], sem.at[sl]).start()
        _for_each_chunk(nch_s[bb], start_chunk)

    @pl.when(b == 0)
    def _():
        stage_ref[...] = jnp.zeros_like(stage_ref)
        fetch(0, 0)

    def wait_chunk(i):
        pltpu.make_async_copy(ys_ref.at[0], stage_ref.at[slot, 0], sem.at[slot]).wait()
    _for_each_chunk(nch_s[b], wait_chunk)

    @pl.when(b + 1 < nb)
    def _():
        fetch(b + 1, 1 - slot)

    gt = gt_ref[...]
    routed = jnp.where(gt > 0.0, 1.0, 0.0).astype(BF16)
    i0 = lax.broadcasted_iota(jnp.int32, (tb, tb), 0)
    i1 = lax.broadcasted_iota(jnp.int32, (tb, tb), 1)
    eye = jnp.where(i0 == i1, 1.0, 0.0).astype(BF16)
    routed_t = _dot_nt(eye, routed)
    gates_tok = _dot_nt(eye, _bf(gt))
    earlier = jnp.where(i1 < i0, 1.0, 0.0).astype(BF16)
    pos_t = _dot(earlier, _bf(routed_t))
    posm_t = _bf(jnp.where(routed_t > 0.0, pos_t, -1.0))

    qcol = qcol_ref[0]
    qbcol = qbcol_ref[0]
    chunk = lax.broadcasted_iota(jnp.int32, (n_exp, rmax), 1) >> MOE_CHUNK_SHIFT
    own = jnp.where(chunk >= qbcol, jnp.where(chunk < qbcol + qcol, 1.0, 0.0), 0.0)
    own_b = _bf(own)
    rank = _dot(posm_t, own_b)
    wexp = _dot(_bf(gates_tok), own_b)
    start = jnp.sum(own * qbcol.astype(F32), axis=0, keepdims=True) * MOE_CHUNK
    rel = lax.broadcasted_iota(jnp.int32, (1, rmax), 1).astype(F32) - start
    weights = _bf(jnp.where(rank == rel, wexp, 0.0))
    routed_out = _dot(weights, stage_ref[slot].reshape(rmax, o_ref.shape[1]))

    h = h_ref[...]
    a = _silu(_dot(h, wsg_ref[...])) * _dot(h, wsu_ref[...])
    shared = _dot(_bf(a), wsd_ref[...])
    o_ref[...] = x1_ref[...] + gate2_ref[0] * (routed_out + shared)


def _combine(gates_t, q, qbase, dtab, nch, h2, x1, mod3, wsg, wsu, wsd, ys, seq):
    n_exp, t = gates_t.shape
    d = h2.shape[1]
    nb = t // MOE_TB
    per_b = seq // MOE_TB
    stage_chunks = _moe_stage_rows(n_exp) // MOE_CHUNK

    def full(a):
        return pl.BlockSpec(a.shape, lambda b, *_: (0,) * a.ndim)

    def rows(w):
        return pl.BlockSpec((MOE_TB, w), lambda b, *_: (b, 0))

    grid_spec = pltpu.PrefetchScalarGridSpec(
        num_scalar_prefetch=2,
        grid=(nb,),
        in_specs=[
            pl.BlockSpec((n_exp, MOE_TB), lambda b, *_: (0, b)),
            pl.BlockSpec((1, n_exp, 1), lambda b, *_: (b, 0, 0)),
            pl.BlockSpec((1, n_exp, 1), lambda b, *_: (b, 0, 0)),
            rows(d), rows(d),
            pl.BlockSpec((1, 1, d), lambda b, *_: (b // per_b, 0, 5)),
            full(wsg), full(wsu), full(wsd),
            pl.BlockSpec(memory_space=pl.ANY),
        ],
        out_specs=rows(d),
        scratch_shapes=[pltpu.VMEM((2, stage_chunks, MOE_CHUNK, d), BF16),
                        pltpu.SemaphoreType.DMA((2,))],
    )
    return pl.pallas_call(
        _combine_kernel,
        out_shape=jax.ShapeDtypeStruct((t, d), F32),
        grid_spec=grid_spec,
        compiler_params=pltpu.CompilerParams(dimension_semantics=("arbitrary",),
                                             vmem_limit_bytes=V7X_VMEM_LIMIT),
        name="moe_combine",
    )(dtab, nch, gates_t, q.reshape(nb, n_exp, 1), qbase.reshape(nb, n_exp, 1),
      h2, x1, mod3, wsg, wsu, wsd, ys)


def _moe_plan(q):
    nb, n_exp = q.shape
    cpt = MOE_TM // MOE_CHUNK
    qbase = jnp.cumsum(q, axis=1) - q
    nch = jnp.sum(q, axis=1)
    per_exp = jnp.sum(q, axis=0)
    tiles = (per_exp + cpt - 1) // cpt
    seg = tiles * cpt
    off = jnp.cumsum(seg) - seg
    dstq = off[None, :] + jnp.cumsum(q, axis=0) - q
    i = jnp.arange(MOE_TB * TOP_K // MOE_CHUNK + n_exp, dtype=jnp.int32)
    owner = jnp.sum((i[None, :, None] >= (qbase + q)[:, None, :]).astype(jnp.int32), axis=2)
    owner = jnp.minimum(owner, n_exp - 1)
    dtab = (jnp.take_along_axis(dstq, owner, axis=1) + i[None, :]
            - jnp.take_along_axis(qbase, owner, axis=1))
    tailq = seg - per_exp
    taila = off + per_exp
    tile_end = jnp.cumsum(tiles)
    total = tile_end[-1]
    n_tiles_max = _moe_max_tiles(nb, n_exp)
    j = jnp.minimum(jnp.arange(n_tiles_max, dtype=jnp.int32), total - 1)
    tile_expert = jnp.sum((j[:, None] >= tile_end[None, :]).astype(jnp.int32), axis=1)
    return qbase, dtab, nch, tailq, taila, tile_expert, total.reshape(1)


def _moe_stage_rows(n_exp):
    rows = MOE_TB * TOP_K + n_exp * (MOE_CHUNK - 1)
    return -(-rows // MOE_ROWGROUP) * MOE_ROWGROUP


def _moe_max_tiles(nb, n_exp):
    cpt = MOE_TM // MOE_CHUNK
    max_chunks = nb * (MOE_TB * TOP_K // MOE_CHUNK + n_exp)
    return -(-max_chunks // cpt) + n_exp


def _moe(h2, gates_t, q3, x1, mod3, wsg, wsu, wsd, wg, wu, wd, seq):
    n_exp, t = gates_t.shape
    nb = t // MOE_TB
    q = q3.reshape(nb, n_exp)
    qbase, dtab, nch, tailq, taila, tile_expert, total = _moe_plan(q)
    d = h2.shape[1]
    n_chunks = _moe_max_tiles(nb, n_exp) * (MOE_TM // MOE_CHUNK)
    xs = _dispatch(gates_t, q, qbase, dtab, nch, tailq, taila, h2, n_chunks)
    ys = _experts(tile_expert, total, xs.reshape(n_chunks * MOE_CHUNK, d), wg, wu, wd)
    ys = ys.reshape(n_chunks, MOE_CHUNK, d)
    return _combine(gates_t, q, qbase, dtab, nch, h2, x1, mod3, wsg, wsu, wsd, ys, seq)


def _pad_heads(w, heads, width):
    lead = w.shape[:-1]
    w = w.reshape(lead + (heads, width))
    w = jnp.pad(w, [(0, 0)] * len(lead) + [(0, 0), (0, LANES - width)])
    return w.reshape(lead + (heads * LANES,))


def kernel(x, c, positions, w_ada, b_ada, g_norm1, w_in, g_na_q, g_na_k, na_rpb, g_q_lat, w_uq,
           g_kv_lat, w_ukv, g_mla_q, g_mla_k, w_proj_na, w_proj_mla, w_out, g_norm2, w_router,
           e_bias, w_exp_gate, w_exp_up, w_exp_down, w_sh_gate, w_sh_up, w_sh_down):
    bsz, seq, d = x.shape
    t = bsz * seq
    depth = w_ada.shape[0]
    na_w = NA_HEADS * NA_HEAD_DIM
    q_rank = g_q_lat.shape[1]
    kv_rank = g_kv_lat.shape[1]
    n_rows = seq // GRID_W

    pos = positions.reshape(t, 1)
    half = MLA_ROPE_DIM // 2
    inv_freq = ROPE_THETA ** (-jnp.arange(half, dtype=F32) / half)
    freq = jnp.zeros((LANES,), F32).at[MLA_NOPE_DIM:MLA_QK_DIM].set(jnp.tile(inv_freq, 2))
    freq = freq.reshape(1, LANES)

    x2 = x.reshape(t, d)
    for l in range(depth):
        mod3 = _adaln(c, w_ada[l], b_ada[l]).reshape(bsz, 1, 6 * d)

        wi = w_in[l]
        o_lat = 3 * na_w
        o_rot = o_lat + q_rank + kv_rank
        o_gate = o_rot + MLA_ROPE_DIM
        wqkv = _bf(wi[:, :o_lat])
        w_rot = jnp.pad(wi[:, o_rot:o_gate], ((0, 0), (MLA_NOPE_DIM, LANES - MLA_QK_DIM)))
        wlat = _bf(jnp.concatenate([wi[:, o_lat:o_rot], w_rot], axis=1))
        wgate = _bf(wi[:, o_gate:])
        gq = (jnp.tile(g_na_q[l], NA_HEADS) * NA_HEAD_DIM ** -0.5).reshape(1, na_w)
        gk = jnp.tile(g_na_k[l], NA_HEADS).reshape(1, na_w)
        wuq = _bf(_pad_heads(w_uq[l], MLA_HEADS, MLA_QK_DIM))
        wukv = w_ukv[l].reshape(kv_rank, MLA_HEADS, MLA_NOPE_DIM + MLA_V_DIM)
        wuk = _bf(_pad_heads(wukv[:, :, :MLA_NOPE_DIM].reshape(kv_rank, -1), MLA_HEADS, MLA_NOPE_DIM))
        wuv = _bf(wukv[:, :, MLA_NOPE_DIM:].reshape(kv_rank, MLA_HEADS * MLA_V_DIM))
        gmq = _pad_heads(jnp.tile(g_mla_q[l], MLA_HEADS) * MLA_QK_DIM ** -0.5, MLA_HEADS,
                         MLA_QK_DIM).reshape(1, -1)
        gmk = _pad_heads(jnp.tile(g_mla_k[l], MLA_HEADS), MLA_HEADS, MLA_QK_DIM).reshape(1, -1)

        qa, ka, va, qm, km, vm, sgn, sgm = _inproj(
            x2, mod3, g_norm1[l].reshape(1, d), wqkv, wlat, wgate, gq, gk,
            g_q_lat[l].reshape(1, q_rank), g_kv_lat[l].reshape(1, kv_rank), wuq, wuk, wuv,
            gmq, gmk, pos, freq, seq)

        bias = _na_bias(na_rpb[l], n_rows)
        y_na = _na_attention(qa, ka, va, bias, bsz, seq)
        y_mla = _mla_attention(qm, km, vm, bsz, seq)

        x1, h2, lt = _merge(x2, y_na, y_mla, sgn, sgm, _bf(w_proj_na[l]), _bf(w_proj_mla[l]),
                            _bf(w_out[l]), mod3, g_norm2[l].reshape(1, d), w_router[l].T, seq)
        gates_t, q3 = _route(lt, e_bias[l])
        x2 = _moe(h2, gates_t, q3, x1, mod3, _bf(w_sh_gate[l]), _bf(w_sh_up[l]),
                  _bf(w_sh_down[l]), w_exp_gate[l], w_exp_up[l], w_exp_down[l], seq)
    return x2.reshape(bsz, seq, d)
```

```python
import functools

import jax
import jax.numpy as jnp
import numpy as np
from jax import lax
from jax.experimental import pallas as pl
from jax.experimental.pallas import tpu as pltpu

GRID_W = 64
NA_HEADS = 8
NA_HEAD_DIM = 64
NA_WIN_ROWS = 8
NA_WIN_COLS = 16
MLA_HEADS = 8
MLA_NOPE_DIM = 64
MLA_ROPE_DIM = 32
MLA_V_DIM = 64
MLA_QK_DIM = MLA_NOPE_DIM + MLA_ROPE_DIM
ROPE_THETA = 10000.0
N_GROUPS = 8
TOPK_GROUPS = 4
TOP_K = 8
ROUTED_SCALE = 2.5
EPS = 1e-6
NEG_BIG = -1e30

LANES = 128
V7X_VMEM_LIMIT = 56 * 1024 * 1024

NA_QROWS = 4
NA_BAND = 12
NA_BLOCK_TYPES = 3
MOE_TB = 256
MOE_CHUNK_SHIFT = 4
MOE_CHUNK = 1 << MOE_CHUNK_SHIFT
MOE_TM = 1024
MOE_ROWGROUP = 512

F32 = jnp.float32
BF16 = jnp.bfloat16


def _bf(x):
    return x.astype(BF16)


def _dot(a, b):
    return jnp.dot(a, b, preferred_element_type=F32)


def _dot_nt(a, b):
    return lax.dot_general(a, b, (((1,), (1,)), ((), ())), preferred_element_type=F32)


def _split(x):
    hi = _bf(x)
    lo = _bf(x - hi.astype(F32))
    return hi, lo


def _dot3(a, b):
    ah, al = _split(a)
    bh, bl = _split(b)
    return _dot(ah, bh) + (_dot(ah, bl) + _dot(al, bh))


def _dot3_nt(a, b):
    ah, al = _split(a)
    bh, bl = _split(b)
    return _dot_nt(ah, bh) + (_dot_nt(ah, bl) + _dot_nt(al, bh))


def _sigmoid(x):
    return 1.0 / (1.0 + jnp.exp(-x))


def _silu(x):
    return x * _sigmoid(x)


def _rms(x, n):
    ss = jnp.sum(x * x, axis=-1, keepdims=True)
    return x * lax.rsqrt(ss * (1.0 / n) + EPS)


def _adaln_kernel(c_ref, w_ref, b_ref, o_ref):
    c = c_ref[...]
    o_ref[...] = _dot3(_silu(c), w_ref[...]) + b_ref[...]


def _adaln(c, w, b):
    bsz, d = c.shape
    n = w.shape[1]
    tn = 1024
    return pl.pallas_call(
        _adaln_kernel,
        out_shape=jax.ShapeDtypeStruct((bsz, n), F32),
        grid=(n // tn,),
        in_specs=[
            pl.BlockSpec((bsz, d), lambda j: (0, 0)),
            pl.BlockSpec((d, tn), lambda j: (0, j)),
            pl.BlockSpec((1, tn), lambda j: (0, j)),
        ],
        out_specs=pl.BlockSpec((bsz, tn), lambda j: (0, j)),
        compiler_params=pltpu.CompilerParams(dimension_semantics=("arbitrary",)),
        name="adaln",
    )(c, w, b.reshape(1, n))


def _inproj_kernel(x_ref, shift_ref, scale_ref, g1_ref, wqkv_ref, wlat_ref, wgate_ref,
                   gq_ref, gk_ref, gql_ref, gkvl_ref, wuq_ref, wuk_ref, wuv_ref,
                   gmq_ref, gmk_ref, pos_ref, freq_ref,
                   qa_ref, ka_ref, va_ref, qm_ref, km_ref, vm_ref, sgn_ref, sgm_ref):
    d = x_ref.shape[1]
    x = x_ref[...]
    h = _rms(x, d) * g1_ref[...]
    h = h * (1.0 + scale_ref[0]) + shift_ref[0]
    hb = _bf(h)

    qkv = _dot(hb, wqkv_ref[...])
    lat = _dot(hb, wlat_ref[...])
    gts = _dot(hb, wgate_ref[...])
    sgn_ref[...] = _bf(_sigmoid(gts[:, :d]))
    sgm_ref[...] = _bf(_sigmoid(gts[:, d:]))

    na_w = NA_HEADS * NA_HEAD_DIM
    lane = lax.broadcasted_iota(jnp.int32, (1, LANES), 1)
    lo_half = lane < NA_HEAD_DIM
    for p in range(na_w // LANES):
        sl = slice(p * LANES, (p + 1) * LANES)
        for src_off, g_ref, dst_ref in ((0, gq_ref, qa_ref), (na_w, gk_ref, ka_ref)):
            t = qkv[:, src_off + p * LANES: src_off + (p + 1) * LANES]
            sq = t * t
            s_lo = jnp.sum(jnp.where(lo_half, sq, 0.0), axis=-1, keepdims=True)
            s_hi = jnp.sum(jnp.where(lo_half, 0.0, sq), axis=-1, keepdims=True)
            r = jnp.where(lo_half,
                          lax.rsqrt(s_lo * (1.0 / NA_HEAD_DIM) + EPS),
                          lax.rsqrt(s_hi * (1.0 / NA_HEAD_DIM) + EPS))
            dst_ref[:, sl] = _bf(t * r * g_ref[:, sl])
    va_ref[...] = _bf(qkv[:, 2 * na_w: 3 * na_w])

    q_rank = gql_ref.shape[1]
    kv_rank = gkvl_ref.shape[1]
    qln = _rms(lat[:, :q_rank], q_rank) * gql_ref[...]
    kvn = _bf(_rms(lat[:, q_rank:q_rank + kv_rank], kv_rank) * gkvl_ref[...])
    qpre = _dot(_bf(qln), wuq_ref[...])
    knope = _dot(kvn, wuk_ref[...])
    vm_ref[...] = _bf(_dot(kvn, wuv_ref[...]))
    krot = lat[:, q_rank + kv_rank:]

    ang = pos_ref[...].astype(F32) * freq_ref[...]
    cosv = jnp.cos(ang)
    sinv = jnp.sin(ang)
    half = MLA_ROPE_DIM // 2
    in_rope = (lane >= MLA_NOPE_DIM) & (lane < MLA_QK_DIM)
    first_half = lane < MLA_NOPE_DIM + half
    c_tab = jnp.where(lane < MLA_NOPE_DIM, 1.0, jnp.where(in_rope, cosv, 0.0))
    s_up = jnp.where(in_rope & jnp.logical_not(first_half), sinv, 0.0)
    s_dn = jnp.where(in_rope & first_half, -sinv, 0.0)

    def rope(t):
        return t * c_tab + pltpu.roll(t, half, 1) * s_up + pltpu.roll(t, LANES - half, 1) * s_dn

    kr = rope(krot)
    for hd in range(MLA_HEADS):
        sl = slice(hd * LANES, (hd + 1) * LANES)
        qh = rope(qpre[:, sl])
        qm_ref[:, sl] = _bf(_rms(qh, MLA_QK_DIM) * gmq_ref[:, sl])
        kh = knope[:, sl] + kr
        km_ref[:, sl] = _bf(_rms(kh, MLA_QK_DIM) * gmk_ref[:, sl])


def _inproj(x2, mod3, g1, wqkv, wlat, wgate, gq, gk, gql, gkvl, wuq, wuk, wuv, gmq, gmk,
            pos, freq, seq):
    t, d = x2.shape
    tm = 256
    per_b = seq // tm
    na_w = NA_HEADS * NA_HEAD_DIM
    mla_w = MLA_HEADS * LANES
    v_w = MLA_HEADS * MLA_V_DIM

    def full(a):
        return pl.BlockSpec(a.shape, lambda i: (0,) * a.ndim)

    def rows(w):
        return pl.BlockSpec((tm, w), lambda i: (i, 0))

    out_shapes = (
        jax.ShapeDtypeStruct((t, na_w), BF16), jax.ShapeDtypeStruct((t, na_w), BF16),
        jax.ShapeDtypeStruct((t, na_w), BF16),
        jax.ShapeDtypeStruct((t, mla_w), BF16), jax.ShapeDtypeStruct((t, mla_w), BF16),
        jax.ShapeDtypeStruct((t, v_w), BF16),
        jax.ShapeDtypeStruct((t, d), BF16), jax.ShapeDtypeStruct((t, d), BF16),
    )
    return pl.pallas_call(
        _inproj_kernel,
        out_shape=out_shapes,
        grid=(t // tm,),
        in_specs=[
            rows(d),
            pl.BlockSpec((1, 1, d), lambda i: (i // per_b, 0, 0)),
            pl.BlockSpec((1, 1, d), lambda i: (i // per_b, 0, 1)),
            full(g1), full(wqkv), full(wlat), full(wgate), full(gq), full(gk), full(gql),
            full(gkvl), full(wuq), full(wuk), full(wuv), full(gmq), full(gmk),
            pl.BlockSpec((tm, 1), lambda i: (i, 0)),
            full(freq),
        ],
        out_specs=(rows(na_w), rows(na_w), rows(na_w), rows(mla_w), rows(mla_w), rows(v_w),
                   rows(d), rows(d)),
        compiler_params=pltpu.CompilerParams(dimension_semantics=("arbitrary",),
                                             vmem_limit_bytes=V7X_VMEM_LIMIT),
        name="inproj",
    )(x2, mod3, mod3, g1, wqkv, wlat, wgate, gq, gk, gql, gkvl, wuq, wuk, wuv, gmq, gmk,
      pos, freq)


def _na_block_geometry(block_type, n_rows):
    if block_type == 0:
        return 0, 0
    if block_type == 1:
        r0 = NA_QROWS
        return r0, r0 - NA_WIN_ROWS // 2
    return n_rows - NA_QROWS, n_rows - NA_BAND


def _na_bias_kernel(rpb_ref, o_ref, m_ref, *, n_rows):
    hd = pl.program_id(0)
    n_dr = 2 * NA_WIN_ROWS - 1
    n_dc = 2 * NA_WIN_COLS - 1
    qc = lax.broadcasted_iota(jnp.int32, (GRID_W, LANES), 0)
    kc = lax.broadcasted_iota(jnp.int32, (GRID_W, LANES), 1) & (GRID_W - 1)
    dc = jnp.clip(kc - qc, -(NA_WIN_COLS - 1), NA_WIN_COLS - 1) + (NA_WIN_COLS - 1)
    cstart = jnp.clip(qc - NA_WIN_COLS // 2, 0, GRID_W - NA_WIN_COLS)
    col_ok = (kc >= cstart) & (kc < cstart + NA_WIN_COLS)
    for i_dr in range(n_dr):
        acc = jnp.zeros((GRID_W, LANES), F32)
        for t in range(n_dc):
            acc = jnp.where(dc == t, rpb_ref[hd, i_dr * n_dc + t], acc)
        m_ref[i_dr] = jnp.where(col_ok, acc, NEG_BIG)
    neg = jnp.full((GRID_W, LANES), NEG_BIG, F32)
    lo_half = lax.broadcasted_iota(jnp.int32, (GRID_W, LANES), 1) < GRID_W
    kh = NA_WIN_ROWS
    for bt in range(NA_BLOCK_TYPES):
        r0, start = _na_block_geometry(bt, n_rows)
        for i in range(NA_QROWS):
            r = r0 + i
            rs = min(max(r - kh // 2, 0), n_rows - kh)
            for jp in range(NA_BAND // 2):
                halves = []
                for j in (2 * jp, 2 * jp + 1):
                    krow = start + j
                    if rs <= krow < rs + kh:
                        halves.append(m_ref[krow - r + (NA_WIN_ROWS - 1)])
                    else:
                        halves.append(neg)
                tile = jnp.where(lo_half, halves[0], halves[1])
                o_ref[bt, 0, i * GRID_W:(i + 1) * GRID_W, jp * LANES:(jp + 1) * LANES] = tile


def _na_bias(rpb, n_rows):
    heads = rpb.shape[0]
    nq = NA_QROWS * GRID_W
    nk = NA_BAND * GRID_W
    rpb2 = rpb.reshape(heads, -1)
    return pl.pallas_call(
        functools.partial(_na_bias_kernel, n_rows=n_rows),
        out_shape=jax.ShapeDtypeStruct((NA_BLOCK_TYPES, heads, nq, nk), F32),
        grid=(heads,),
        in_specs=[pl.BlockSpec(memory_space=pltpu.SMEM)],
        out_specs=pl.BlockSpec((NA_BLOCK_TYPES, 1, nq, nk), lambda hd: (0, hd, 0, 0)),
        scratch_shapes=[pltpu.VMEM((2 * NA_WIN_ROWS - 1, GRID_W, LANES), F32)],
        compiler_params=pltpu.CompilerParams(dimension_semantics=("arbitrary",)),
        name="na_bias",
    )(rpb2)


def _na_kernel(q_ref, k_ref, v_ref, bias_ref, o_ref, *, n_blocks, n_rows):
    blk = pl.program_id(1)
    start_row = jnp.where(blk == 0, 0,
                          jnp.where(blk == n_blocks - 1, n_rows - NA_BAND,
                                    blk * NA_QROWS - NA_WIN_ROWS // 2))
    off = pl.multiple_of(start_row * GRID_W, GRID_W)
    nk = NA_BAND * GRID_W
    lane = lax.broadcasted_iota(jnp.int32, (1, LANES), 1)
    for p in range(NA_HEADS * NA_HEAD_DIM // LANES):
        sl = slice(p * LANES, (p + 1) * LANES)
        qp = q_ref[:, sl]
        kb = k_ref[pl.ds(off, nk), sl]
        vb = v_ref[pl.ds(off, nk), sl]
        acc = jnp.zeros((q_ref.shape[0], LANES), F32)
        for hh in range(2):
            mine = (lane < NA_HEAD_DIM) if hh == 0 else (lane >= NA_HEAD_DIM)
            qh = jnp.where(mine, qp, jnp.zeros_like(qp))
            s = _dot_nt(qh, kb) + bias_ref[0, 2 * p + hh]
            m = jnp.max(s, axis=-1, keepdims=True)
            e = jnp.exp(s - m)
            l = jnp.sum(e, axis=-1, keepdims=True)
            vh = jnp.where(mine, vb, jnp.zeros_like(vb))
            acc = acc + _dot(_bf(e), vh) / l
        o_ref[:, sl] = _bf(acc)


def _na_attention(qa, ka, va, bias, bsz, seq):
    t, w = qa.shape
    n_rows = seq // GRID_W
    n_blocks = n_rows // NA_QROWS
    tq = NA_QROWS * GRID_W
    nk = NA_BAND * GRID_W
    heads = bias.shape[1]

    def btype(blk):
        return jnp.where(blk == 0, 0, jnp.where(blk == n_blocks - 1, 2, 1))

    return pl.pallas_call(
        functools.partial(_na_kernel, n_blocks=n_blocks, n_rows=n_rows),
        out_shape=jax.ShapeDtypeStruct((t, w), BF16),
        grid=(bsz, n_blocks),
        in_specs=[
            pl.BlockSpec((tq, w), lambda b, blk: (b * n_blocks + blk, 0)),
            pl.BlockSpec((seq, w), lambda b, blk: (b, 0)),
            pl.BlockSpec((seq, w), lambda b, blk: (b, 0)),
            pl.BlockSpec((1, heads, tq, nk), lambda b, blk: (btype(blk), 0, 0, 0)),
        ],
        out_specs=pl.BlockSpec((tq, w), lambda b, blk: (b * n_blocks + blk, 0)),
        compiler_params=pltpu.CompilerParams(dimension_semantics=("arbitrary", "arbitrary"),
                                             vmem_limit_bytes=V7X_VMEM_LIMIT),
        name="na_attn",
    )(qa, ka, va, bias)


def _mla_kernel(q_ref, k_ref, v_ref, o_ref):
    lane = lax.broadcasted_iota(jnp.int32, (1, LANES), 1)
    v = v_ref[...]
    acc = jnp.zeros((q_ref.shape[0], LANES), F32)
    for hh in range(2):
        sl = slice(hh * LANES, (hh + 1) * LANES)
        s = _dot_nt(q_ref[:, sl], k_ref[:, sl])
        m = jnp.max(s, axis=-1, keepdims=True)
        e = jnp.exp(s - m)
        l = jnp.sum(e, axis=-1, keepdims=True)
        mine = (lane < MLA_V_DIM) if hh == 0 else (lane >= MLA_V_DIM)
        vh = jnp.where(mine, v, jnp.zeros_like(v))
        acc = acc + _dot(_bf(e), vh) / l
    o_ref[...] = _bf(acc)


def _mla_attention(qm, km, vm, bsz, seq):
    t = qm.shape[0]
    tq = 512
    nq = seq // tq
    pairs = MLA_HEADS // 2
    return pl.pallas_call(
        _mla_kernel,
        out_shape=jax.ShapeDtypeStruct((t, MLA_HEADS * MLA_V_DIM), BF16),
        grid=(bsz, pairs, nq),
        in_specs=[
            pl.BlockSpec((tq, 2 * LANES), lambda b, p, i: (b * nq + i, p)),
            pl.BlockSpec((seq, 2 * LANES), lambda b, p, i: (b, p)),
            pl.BlockSpec((seq, LANES), lambda b, p, i: (b, p)),
        ],
        out_specs=pl.BlockSpec((tq, LANES), lambda b, p, i: (b * nq + i, p)),
        compiler_params=pltpu.CompilerParams(
            dimension_semantics=("arbitrary", "arbitrary", "arbitrary"),
            vmem_limit_bytes=V7X_VMEM_LIMIT),
        name="mla_attn",
    )(qm, km, vm)


def _merge_kernel(x_ref, yna_ref, ymla_ref, sgn_ref, sgm_ref, wpn_ref, wpm_ref, wout_ref,
                  gate1_ref, shift2_ref, scale2_ref, g2_ref, wrt_ref,
                  x1_ref, h2_ref, lt_ref):
    d = x_ref.shape[1]
    merged = (sgn_ref[...].astype(F32) * _dot(yna_ref[...], wpn_ref[...])
              + sgm_ref[...].astype(F32) * _dot(ymla_ref[...], wpm_ref[...]))
    x1 = x_ref[...] + gate1_ref[0] * _dot(_bf(merged), wout_ref[...])
    x1_ref[...] = x1
    h2 = _rms(x1, d) * g2_ref[...]
    h2 = h2 * (1.0 + scale2_ref[0]) + shift2_ref[0]
    h2_ref[...] = _bf(h2)
    lt_ref[...] = _dot3_nt(wrt_ref[...], h2)


def _merge(x2, yna, ymla, sgn, sgm, wpn, wpm, wout, mod3, g2, wrt, seq):
    t, d = x2.shape
    tm = 512
    per_b = seq // tm
    n_exp = wrt.shape[0]

    def full(a):
        return pl.BlockSpec(a.shape, lambda i: (0,) * a.ndim)

    def rows(w):
        return pl.BlockSpec((tm, w), lambda i: (i, 0))

    def modblk(j):
        return pl.BlockSpec((1, 1, d), lambda i: (i // per_b, 0, j))

    return pl.pallas_call(
        _merge_kernel,
        out_shape=(jax.ShapeDtypeStruct((t, d), F32), jax.ShapeDtypeStruct((t, d), BF16),
                   jax.ShapeDtypeStruct((n_exp, t), F32)),
        grid=(t // tm,),
        in_specs=[rows(d), rows(yna.shape[1]), rows(ymla.shape[1]), rows(d), rows(d),
                  full(wpn), full(wpm), full(wout),
                  modblk(2), modblk(3), modblk(4), full(g2), full(wrt)],
        out_specs=(rows(d), rows(d), pl.BlockSpec((n_exp, tm), lambda i: (0, i))),
        compiler_params=pltpu.CompilerParams(dimension_semantics=("arbitrary",),
                                             vmem_limit_bytes=V7X_VMEM_LIMIT),
        name="merge",
    )(x2, yna, ymla, sgn, sgm, wpn, wpm, wout, mod3, mod3, mod3, g2, wrt)


def _route_kernel(lt_ref, eb_ref, o_ref, q_ref):
    n_exp, tn = lt_ref.shape
    per_g = n_exp // N_GROUPS
    neg_inf = -jnp.inf
    sc = _sigmoid(lt_ref[...])
    sel = sc + eb_ref[...]
    sc3 = sc.reshape(N_GROUPS, per_g, tn)
    g3 = sel.reshape(N_GROUPS, per_g, tn)
    io = lax.broadcasted_iota(jnp.int32, (N_GROUPS, per_g, tn), 1)
    gio = lax.broadcasted_iota(jnp.int32, (N_GROUPS, per_g, tn), 0)
    eio = gio * per_g + io

    m1 = jnp.max(g3, axis=1, keepdims=True)
    i1 = jnp.min(jnp.where(g3 == m1, io, per_g), axis=1, keepdims=True)
    m2 = jnp.max(jnp.where(io == i1, neg_inf, g3), axis=1, keepdims=True)
    gs = m1 + m2

    g1io = lax.broadcasted_iota(jnp.int32, (N_GROUPS, 1, tn), 0)
    gsel = jnp.zeros((N_GROUPS, 1, tn), F32)
    cur = gs
    for _ in range(TOPK_GROUPS):
        m = jnp.max(cur, axis=0, keepdims=True)
        i = jnp.min(jnp.where(cur == m, g1io, N_GROUPS), axis=0, keepdims=True)
        pick = g1io == i
        gsel = jnp.where(pick, 1.0, gsel)
        cur = jnp.where(pick, neg_inf, cur)

    cur = jnp.where(gsel > 0.0, g3, neg_inf)
    chosen = jnp.zeros((N_GROUPS, per_g, tn), F32)
    for _ in range(TOP_K):
        m = jnp.max(jnp.max(cur, axis=1, keepdims=True), axis=0, keepdims=True)
        cand = jnp.where(cur == m, eio, n_exp)
        i = jnp.min(jnp.min(cand, axis=1, keepdims=True), axis=0, keepdims=True)
        pick = eio == i
        chosen = jnp.where(pick, 1.0, chosen)
        cur = jnp.where(pick, neg_inf, cur)

    w = jnp.where(chosen > 0.0, sc3, 0.0)
    tot = jnp.sum(jnp.sum(w, axis=1, keepdims=True), axis=0, keepdims=True)
    gates = (w / tot * ROUTED_SCALE).reshape(n_exp, tn)
    o_ref[...] = gates
    routed = jnp.where(gates > 0.0, 1.0, 0.0).astype(BF16)
    n_row = _dot_nt(jnp.ones((8, tn), BF16), routed)[0:1]
    q_ref[0] = jnp.floor((n_row + (MOE_CHUNK - 1)) * (1.0 / MOE_CHUNK)).astype(jnp.int32)


def _route(lt, e_bias):
    n_exp, t = lt.shape
    tn = MOE_TB
    return pl.pallas_call(
        _route_kernel,
        out_shape=(jax.ShapeDtypeStruct((n_exp, t), F32),
                   jax.ShapeDtypeStruct((t // tn, 1, n_exp), jnp.int32)),
        grid=(t // tn,),
        in_specs=[pl.BlockSpec((n_exp, tn), lambda i: (0, i)),
                  pl.BlockSpec((n_exp, 1), lambda i: (0, 0))],
        out_specs=(pl.BlockSpec((n_exp, tn), lambda i: (0, i)),
                   pl.BlockSpec((1, 1, n_exp), lambda i: (i, 0, 0))),
        compiler_params=pltpu.CompilerParams(dimension_semantics=("arbitrary",)),
        name="route",
    )(lt, e_bias.reshape(n_exp, 1))


def _for_each_chunk(n, fn):
    def quad(j, carry):
        for u in range(4):
            fn(j * 4 + u)
        return carry
    lax.fori_loop(0, n >> 2, quad, 0)
    base = (n >> 2) << 2
    for u in range(3):
        @pl.when(base + u < n)
        def _():
            fn(base + u)


def _dispatch_kernel(dtab_s, nch_s, total_s,
                     gt_ref, qrow_ref, qbrow_ref, h_ref, xs_ref, stage_ref, zero_ref, sem):
    b = pl.program_id(0)
    nb = pl.num_programs(0)
    slot = lax.rem(b, 2)
    n_exp, tb = gt_ref.shape
    rmax = stage_ref.shape[1] * MOE_CHUNK
    cpg = MOE_ROWGROUP // MOE_CHUNK

    routed = gt_ref[...] > 0.0
    before = (lax.broadcasted_iota(jnp.int32, (tb, tb), 0)
              < lax.broadcasted_iota(jnp.int32, (tb, tb), 1))
    pos = _dot(jnp.where(routed, 1.0, 0.0).astype(BF16), jnp.where(before, 1.0, 0.0).astype(BF16))
    posm = _bf(jnp.where(routed, pos, -1.0))
    qrow = qrow_ref[0]
    qbrow = qbrow_ref[0]
    qbrow_f = qbrow.astype(F32)
    h = h_ref[...]
    for g in range(rmax // MOE_ROWGROUP):
        @pl.when(g * MOE_ROWGROUP < nch_s[b] * MOE_CHUNK)
        def _():
            r0 = g * MOE_ROWGROUP
            chunk = (lax.broadcasted_iota(jnp.int32, (MOE_ROWGROUP, n_exp), 0) + r0) >> MOE_CHUNK_SHIFT
            own = jnp.where(chunk >= qbrow, jnp.where(chunk < qbrow + qrow, 1.0, 0.0), 0.0)
            rank = _dot(_bf(own), posm)
            start = jnp.sum(own * qbrow_f, axis=-1, keepdims=True) * MOE_CHUNK
            rel = (lax.broadcasted_iota(jnp.int32, (MOE_ROWGROUP, 1), 0) + r0).astype(F32) - start
            onehot = jnp.where(rank == rel, 1.0, 0.0).astype(BF16)
            rows = _bf(_dot(onehot, h))
            stage_ref[slot, g * cpg:(g + 1) * cpg] = rows.reshape(cpg, MOE_CHUNK, rows.shape[1])

    def start_chunk(i):
        pltpu.make_async_copy(stage_ref.at[slot, i], xs_ref.at[dtab_s[b, i]], sem.at[slot]).start()
    _for_each_chunk(nch_s[b], start_chunk)

    def wait_chunks(n, sl):
        def wait_chunk(i):
            pltpu.make_async_copy(zero_ref, xs_ref.at[0], sem.at[sl]).wait()
        _for_each_chunk(n, wait_chunk)

    @pl.when(b > 0)
    def _():
        wait_chunks(nch_s[b - 1], 1 - slot)

    @pl.when(b == nb - 1)
    def _():
        zero_ref[...] = jnp.zeros_like(zero_ref)
        n_tail = MOE_TM // MOE_CHUNK
        for c in range(n_tail):
            pltpu.make_async_copy(zero_ref, xs_ref.at[total_s[0] + c], sem.at[slot]).start()
        wait_chunks(nch_s[b] + n_tail, slot)


def _dispatch(gates_t, q, qbase, dtab, nch, total, h2, n_chunks):
    n_exp, t = gates_t.shape
    d = h2.shape[1]
    nb = t // MOE_TB
    rmax = _moe_stage_rows(n_exp)
    grid_spec = pltpu.PrefetchScalarGridSpec(
        num_scalar_prefetch=3,
        grid=(nb,),
        in_specs=[
            pl.BlockSpec((n_exp, MOE_TB), lambda b, *_: (0, b)),
            pl.BlockSpec((1, 1, n_exp), lambda b, *_: (b, 0, 0)),
            pl.BlockSpec((1, 1, n_exp), lambda b, *_: (b, 0, 0)),
            pl.BlockSpec((MOE_TB, d), lambda b, *_: (b, 0)),
        ],
        out_specs=pl.BlockSpec(memory_space=pl.ANY),
        scratch_shapes=[pltpu.VMEM((2, rmax // MOE_CHUNK, MOE_CHUNK, d), BF16),
                        pltpu.VMEM((MOE_CHUNK, d), BF16), pltpu.SemaphoreType.DMA((2,))],
    )
    return pl.pallas_call(
        _dispatch_kernel,
        out_shape=jax.ShapeDtypeStruct((n_chunks, MOE_CHUNK, d), BF16),
        grid_spec=grid_spec,
        compiler_params=pltpu.CompilerParams(dimension_semantics=("arbitrary",),
                                             vmem_limit_bytes=V7X_VMEM_LIMIT),
        name="moe_dispatch",
    )(dtab, nch, total, gates_t, q.reshape(nb, 1, n_exp), qbase.reshape(nb, 1, n_exp), h2)


def _tile_pieces():
    cpt = MOE_TM // MOE_CHUNK
    return [1 << s for s in range(cpt.bit_length() - 1, -1, -1)]


def _expert_kernel(off_s, len_s, next_s, first_s, xs_ref, wg_ref, wu_ref, wd_ref, ys_ref,
                   xbuf, ybuf, wg_b, wu_b, wd_b, state, sem_in, sem_out):
    e = pl.program_id(0)
    n_exp = pl.num_programs(0)
    cpt = MOE_TM // MOE_CHUNK
    d = xbuf.shape[3]
    pieces = _tile_pieces()

    def tile_in(ee, tt, sl):
        return pltpu.make_async_copy(xs_ref.at[pl.ds(off_s[ee] + tt * cpt, cpt)], xbuf.at[sl],
                                     sem_in.at[sl])

    def for_each_piece(valid, fn):
        for k, piece in enumerate(pieces):
            @pl.when((valid & piece) != 0)
            def _():
                fn(k, piece, valid & ~(2 * piece - 1))

    def tile_out(sl, dst_chunk, k, piece, start):
        return pltpu.make_async_copy(ybuf.at[sl, pl.ds(start, piece)],
                                     ys_ref.at[pl.ds(dst_chunk + start, piece)], sem_out.at[sl, k])

    def drain(sl):
        for_each_piece(state[1 + sl], lambda k, piece, start: tile_out(sl, 0, k, piece, start).wait())
        state[1 + sl] = 0

    @pl.when(e == 0)
    def _():
        state[0] = 0
        state[1] = 0
        state[2] = 0

        @pl.when(first_s[0] < n_exp)
        def _():
            tile_in(first_s[0], 0, 0).start()

    n_valid = len_s[e]
    n_tiles = (n_valid + cpt - 1) // cpt

    @pl.when(n_valid > 0)
    def _():
        wg_b[...] = _bf(wg_ref[0])
        wu_b[...] = _bf(wu_ref[0])
        wd_b[...] = _bf(wd_ref[0])

        def tile(t, carry):
            sl = state[0] & 1
            tile_in(e, t, sl).wait()
            more = t + 1 < n_tiles
            nxt_e = jnp.where(more, e, next_s[e])
            nxt_t = jnp.where(more, t + 1, 0)

            @pl.when(nxt_e < n_exp)
            def _():
                tile_in(nxt_e, nxt_t, 1 - sl).start()

            drain(sl)
            x = xbuf[sl].reshape(MOE_TM, d)
            a = _silu(_dot(x, wg_b[...])) * _dot(x, wu_b[...])
            ybuf[sl] = _bf(_dot(_bf(a), wd_b[...])).reshape(cpt, MOE_CHUNK, d)
            valid = jnp.minimum(n_valid - t * cpt, cpt)
            dst = off_s[e] + t * cpt
            for_each_piece(valid, lambda k, piece, start: tile_out(sl, dst, k, piece, start).start())
            state[1 + sl] = valid
            state[0] = state[0] + 1
            return carry
        lax.fori_loop(0, n_tiles, tile, 0)

    @pl.when(e == n_exp - 1)
    def _():
        drain(0)
        drain(1)


def _experts(off, per_exp, nxt, first, xs, wg, wu, wd):
    n_chunks, _, d = xs.shape
    n_exp, _, ff = wg.shape
    cpt = MOE_TM // MOE_CHUNK

    def w_blk(e, *_):
        return (e, 0, 0)

    grid_spec = pltpu.PrefetchScalarGridSpec(
        num_scalar_prefetch=4,
        grid=(n_exp,),
        in_specs=[pl.BlockSpec(memory_space=pl.ANY),
                  pl.BlockSpec((1, d, ff), w_blk), pl.BlockSpec((1, d, ff), w_blk),
                  pl.BlockSpec((1, ff, d), w_blk)],
        out_specs=pl.BlockSpec(memory_space=pl.ANY),
        scratch_shapes=[pltpu.VMEM((2, cpt, MOE_CHUNK, d), BF16),
                        pltpu.VMEM((2, cpt, MOE_CHUNK, d), BF16),
                        pltpu.VMEM((d, ff), BF16), pltpu.VMEM((d, ff), BF16),
                        pltpu.VMEM((ff, d), BF16),
                        pltpu.SMEM((3,), jnp.int32),
                        pltpu.SemaphoreType.DMA((2,)),
                        pltpu.SemaphoreType.DMA((2, len(_tile_pieces())))],
    )
    return pl.pallas_call(
        _expert_kernel,
        out_shape=jax.ShapeDtypeStruct((n_chunks, MOE_CHUNK, d), BF16),
        grid_spec=grid_spec,
        compiler_params=pltpu.CompilerParams(dimension_semantics=("arbitrary",),
                                             vmem_limit_bytes=V7X_VMEM_LIMIT),
        name="moe_experts",
    )(off, per_exp, nxt, first, xs, wg, wu, wd)


def _combine_kernel(dtab_s, nch_s,
                    gt_ref, qcol_ref, qbcol_ref, h_ref, x1_ref, gate2_ref,
                    wsg_ref, wsu_ref, wsd_ref, ys_ref, o_ref, stage_ref, sem):
    b = pl.program_id(0)
    nb = pl.num_programs(0)
    slot = lax.rem(b, 2)
    n_exp, tb = gt_ref.shape
    rmax = stage_ref.shape[1] * MOE_CHUNK

    def fetch(bb, sl):
        def start_chunk(i):
            pltpu.make_async_copy(ys_ref.at[dtab_s[bb, i]], stage_ref.at[sl, i], sem.at[sl]).start()
        _for_each_chunk(nch_s[bb], start_chunk)

    @pl.when(b == 0)
    def _():
        stage_ref[...] = jnp.zeros_like(stage_ref)
        fetch(0, 0)

    def wait_chunk(i):
        pltpu.make_async_copy(ys_ref.at[0], stage_ref.at[slot, 0], sem.at[slot]).wait()
    _for_each_chunk(nch_s[b], wait_chunk)

    @pl.when(b + 1 < nb)
    def _():
        fetch(b + 1, 1 - slot)

    gt = gt_ref[...]
    routed = jnp.where(gt > 0.0, 1.0, 0.0).astype(BF16)
    i0 = lax.broadcasted_iota(jnp.int32, (tb, tb), 0)
    i1 = lax.broadcasted_iota(jnp.int32, (tb, tb), 1)
    eye = jnp.where(i0 == i1, 1.0, 0.0).astype(BF16)
    routed_t = _dot_nt(eye, routed)
    gates_tok = _dot_nt(eye, _bf(gt))
    earlier = jnp.where(i1 < i0, 1.0, 0.0).astype(BF16)
    pos_t = _dot(earlier, _bf(routed_t))
    posm_t = _bf(jnp.where(routed_t > 0.0, pos_t, -1.0))

    qcol = qcol_ref[0]
    qbcol = qbcol_ref[0]
    chunk = lax.broadcasted_iota(jnp.int32, (n_exp, rmax), 1) >> MOE_CHUNK_SHIFT
    own = jnp.where(chunk >= qbcol, jnp.where(chunk < qbcol + qcol, 1.0, 0.0), 0.0)
    own_b = _bf(own)
    rank = _dot(posm_t, own_b)
    wexp = _dot(_bf(gates_tok), own_b)
    start = jnp.sum(own * qbcol.astype(F32), axis=0, keepdims=True) * MOE_CHUNK
    rel = lax.broadcasted_iota(jnp.int32, (1, rmax), 1).astype(F32) - start
    weights = _bf(jnp.where(rank == rel, wexp, 0.0))
    routed_out = _dot(weights, stage_ref[slot].reshape(rmax, o_ref.shape[1]))

    h = h_ref[...]
    a = _silu(_dot(h, wsg_ref[...])) * _dot(h, wsu_ref[...])
    shared = _dot(_bf(a), wsd_ref[...])
    o_ref[...] = x1_ref[...] + gate2_ref[0] * (routed_out + shared)


def _combine(gates_t, q, qbase, dtab, nch, h2, x1, mod3, wsg, wsu, wsd, ys, seq):
    n_exp, t = gates_t.shape
    d = h2.shape[1]
    nb = t // MOE_TB
    per_b = seq // MOE_TB
    stage_chunks = _moe_stage_rows(n_exp) // MOE_CHUNK

    def full(a):
        return pl.BlockSpec(a.shape, lambda b, *_: (0,) * a.ndim)

    def rows(w):
        return pl.BlockSpec((MOE_TB, w), lambda b, *_: (b, 0))

    grid_spec = pltpu.PrefetchScalarGridSpec(
        num_scalar_prefetch=2,
        grid=(nb,),
        in_specs=[
            pl.BlockSpec((n_exp, MOE_TB), lambda b, *_: (0, b)),
            pl.BlockSpec((1, n_exp, 1), lambda b, *_: (b, 0, 0)),
            pl.BlockSpec((1, n_exp, 1), lambda b, *_: (b, 0, 0)),
            rows(d), rows(d),
            pl.BlockSpec((1, 1, d), lambda b, *_: (b // per_b, 0, 5)),
            full(wsg), full(wsu), full(wsd),
            pl.BlockSpec(memory_space=pl.ANY),
        ],
        out_specs=rows(d),
        scratch_shapes=[pltpu.VMEM((2, stage_chunks, MOE_CHUNK, d), BF16),
                        pltpu.SemaphoreType.DMA((2,))],
    )
    return pl.pallas_call(
        _combine_kernel,
        out_shape=jax.ShapeDtypeStruct((t, d), F32),
        grid_spec=grid_spec,
        compiler_params=pltpu.CompilerParams(dimension_semantics=("arbitrary",),
                                             vmem_limit_bytes=V7X_VMEM_LIMIT),
        name="moe_combine",
    )(dtab, nch, gates_t, q.reshape(nb, n_exp, 1), qbase.reshape(nb, n_exp, 1),
      h2, x1, mod3, wsg, wsu, wsd, ys)


def _moe_plan(q):
    nb, n_exp = q.shape
    qbase = jnp.cumsum(q, axis=1) - q
    nch = jnp.sum(q, axis=1)
    per_exp = jnp.sum(q, axis=0)
    off = jnp.cumsum(per_exp) - per_exp
    dstq = off[None, :] + jnp.cumsum(q, axis=0) - q
    i = jnp.arange(_moe_block_chunks(n_exp), dtype=jnp.int32)
    ii = i[None, :, None]
    owned = (ii >= qbase[:, None, :]) & (ii < (qbase + q)[:, None, :])
    dtab = i[None, :] + jnp.sum(jnp.where(owned, (dstq - qbase)[:, None, :], 0), axis=2)
    ids = jnp.arange(n_exp, dtype=jnp.int32)
    later = (ids[None, :] > ids[:, None]) & (per_exp[None, :] > 0)
    nxt = jnp.min(jnp.where(later, ids[None, :], n_exp), axis=1)
    first = jnp.min(jnp.where(per_exp > 0, ids, n_exp)).reshape(1)
    total = jnp.sum(per_exp).reshape(1)
    return qbase, dtab, nch, off, per_exp, nxt, first, total


def _moe_block_chunks(n_exp):
    return MOE_TB * TOP_K // MOE_CHUNK + n_exp


def _moe_stage_rows(n_exp):
    return -(-_moe_block_chunks(n_exp) * MOE_CHUNK // MOE_ROWGROUP) * MOE_ROWGROUP


def _moe(h2, gates_t, q3, x1, mod3, wsg, wsu, wsd, wg, wu, wd, seq):
    n_exp, t = gates_t.shape
    nb = t // MOE_TB
    q = q3.reshape(nb, n_exp)
    qbase, dtab, nch, off, per_exp, nxt, first, total = _moe_plan(q)
    n_chunks = nb * _moe_block_chunks(n_exp) + MOE_TM // MOE_CHUNK
    xs = _dispatch(gates_t, q, qbase, dtab, nch, total, h2, n_chunks)
    ys = _experts(off, per_exp, nxt, first, xs, wg, wu, wd)
    return _combine(gates_t, q, qbase, dtab, nch, h2, x1, mod3, wsg, wsu, wsd, ys, seq)


def _pad_heads(w, heads, width):
    lead = w.shape[:-1]
    w = w.reshape(lead + (heads, width))
    w = jnp.pad(w, [(0, 0)] * len(lead) + [(0, 0), (0, LANES - width)])
    return w.reshape(lead + (heads * LANES,))


def kernel(x, c, positions, w_ada, b_ada, g_norm1, w_in, g_na_q, g_na_k, na_rpb, g_q_lat, w_uq,
           g_kv_lat, w_ukv, g_mla_q, g_mla_k, w_proj_na, w_proj_mla, w_out, g_norm2, w_router,
           e_bias, w_exp_gate, w_exp_up, w_exp_down, w_sh_gate, w_sh_up, w_sh_down):
    bsz, seq, d = x.shape
    t = bsz * seq
    depth = w_ada.shape[0]
    na_w = NA_HEADS * NA_HEAD_DIM
    q_rank = g_q_lat.shape[1]
    kv_rank = g_kv_lat.shape[1]
    n_rows = seq // GRID_W

    pos = positions.reshape(t, 1)
    half = MLA_ROPE_DIM // 2
    inv_freq = ROPE_THETA ** (-jnp.arange(half, dtype=F32) / half)
    freq = jnp.zeros((LANES,), F32).at[MLA_NOPE_DIM:MLA_QK_DIM].set(jnp.tile(inv_freq, 2))
    freq = freq.reshape(1, LANES)

    x2 = x.reshape(t, d)
    for l in range(depth):
        mod3 = _adaln(c, w_ada[l], b_ada[l]).reshape(bsz, 1, 6 * d)

        wi = w_in[l]
        o_lat = 3 * na_w
        o_rot = o_lat + q_rank + kv_rank
        o_gate = o_rot + MLA_ROPE_DIM
        wqkv = _bf(wi[:, :o_lat])
        w_rot = jnp.pad(wi[:, o_rot:o_gate], ((0, 0), (MLA_NOPE_DIM, LANES - MLA_QK_DIM)))
        wlat = _bf(jnp.concatenate([wi[:, o_lat:o_rot], w_rot], axis=1))
        wgate = _bf(wi[:, o_gate:])
        gq = (jnp.tile(g_na_q[l], NA_HEADS) * NA_HEAD_DIM ** -0.5).reshape(1, na_w)
        gk = jnp.tile(g_na_k[l], NA_HEADS).reshape(1, na_w)
        wuq = _bf(_pad_heads(w_uq[l], MLA_HEADS, MLA_QK_DIM))
        wukv = w_ukv[l].reshape(kv_rank, MLA_HEADS, MLA_NOPE_DIM + MLA_V_DIM)
        wuk = _bf(_pad_heads(wukv[:, :, :MLA_NOPE_DIM].reshape(kv_rank, -1), MLA_HEADS, MLA_NOPE_DIM))
        wuv = _bf(wukv[:, :, MLA_NOPE_DIM:].reshape(kv_rank, MLA_HEADS * MLA_V_DIM))
        gmq = _pad_heads(jnp.tile(g_mla_q[l], MLA_HEADS) * MLA_QK_DIM ** -0.5, MLA_HEADS,
                         MLA_QK_DIM).reshape(1, -1)
        gmk = _pad_heads(jnp.tile(g_mla_k[l], MLA_HEADS), MLA_HEADS, MLA_QK_DIM).reshape(1, -1)

        qa, ka, va, qm, km, vm, sgn, sgm = _inproj(
            x2, mod3, g_norm1[l].reshape(1, d), wqkv, wlat, wgate, gq, gk,
            g_q_lat[l].reshape(1, q_rank), g_kv_lat[l].reshape(1, kv_rank), wuq, wuk, wuv,
            gmq, gmk, pos, freq, seq)

        bias = _na_bias(na_rpb[l], n_rows)
        y_na = _na_attention(qa, ka, va, bias, bsz, seq)
        y_mla = _mla_attention(qm, km, vm, bsz, seq)

        x1, h2, lt = _merge(x2, y_na, y_mla, sgn, sgm, _bf(w_proj_na[l]), _bf(w_proj_mla[l]),
                            _bf(w_out[l]), mod3, g_norm2[l].reshape(1, d), w_router[l].T, seq)
        gates_t, q3 = _route(lt, e_bias[l])
        x2 = _moe(h2, gates_t, q3, x1, mod3, _bf(w_sh_gate[l]), _bf(w_sh_up[l]),
                  _bf(w_sh_down[l]), w_exp_gate[l], w_exp_up[l], w_exp_down[l], seq)
    return x2.reshape(bsz, seq, d)
```

```python
import functools

import jax
import jax.numpy as jnp
import numpy as np
from jax import lax
from jax.experimental import pallas as pl
from jax.experimental.pallas import tpu as pltpu

GRID_W = 64
NA_HEADS = 8
NA_HEAD_DIM = 64
NA_WIN_ROWS = 8
NA_WIN_COLS = 16
MLA_HEADS = 8
MLA_NOPE_DIM = 64
MLA_ROPE_DIM = 32
MLA_V_DIM = 64
MLA_QK_DIM = MLA_NOPE_DIM + MLA_ROPE_DIM
ROPE_THETA = 10000.0
N_GROUPS = 8
TOPK_GROUPS = 4
TOP_K = 8
ROUTED_SCALE = 2.5
EPS = 1e-6
NEG_BIG = -1e30

LANES = 128
V7X_VMEM_LIMIT = 56 * 1024 * 1024

NA_QROWS = 4
NA_BAND = 12
NA_BLOCK_TYPES = 3
MLA_PAIRS_PER_STEP = 4
LOG2E = 1.4426950408889634
MOE_TB = 256
MOE_CHUNK_SHIFT = 4
MOE_CHUNK = 1 << MOE_CHUNK_SHIFT
MOE_TM = 1024
MOE_ROWGROUP = 512

F32 = jnp.float32
BF16 = jnp.bfloat16


def _bf(x):
    return x.astype(BF16)


def _dot(a, b):
    return jnp.dot(a, b, preferred_element_type=F32)


def _dot_nt(a, b):
    return lax.dot_general(a, b, (((1,), (1,)), ((), ())), preferred_element_type=F32)


def _split(x):
    hi = _bf(x)
    lo = _bf(x - hi.astype(F32))
    return hi, lo


def _dot3(a, b):
    ah, al = _split(a)
    bh, bl = _split(b)
    return _dot(ah, bh) + (_dot(ah, bl) + _dot(al, bh))


def _dot3_nt(a, b):
    ah, al = _split(a)
    bh, bl = _split(b)
    return _dot_nt(ah, bh) + (_dot_nt(ah, bl) + _dot_nt(al, bh))


def _sigmoid(x):
    return 1.0 / (1.0 + jnp.exp(-x))


def _silu(x):
    return x * _sigmoid(x)


def _rms(x, n):
    ss = jnp.sum(x * x, axis=-1, keepdims=True)
    return x * lax.rsqrt(ss * (1.0 / n) + EPS)


def _adaln_kernel(c_ref, w_ref, b_ref, o_ref):
    c = c_ref[...]
    o_ref[...] = _dot3(_silu(c), w_ref[...]) + b_ref[...]


def _adaln(c, w, b):
    bsz, d = c.shape
    n = w.shape[1]
    tn = 1024
    return pl.pallas_call(
        _adaln_kernel,
        out_shape=jax.ShapeDtypeStruct((bsz, n), F32),
        grid=(n // tn,),
        in_specs=[
            pl.BlockSpec((bsz, d), lambda j: (0, 0)),
            pl.BlockSpec((d, tn), lambda j: (0, j)),
            pl.BlockSpec((1, tn), lambda j: (0, j)),
        ],
        out_specs=pl.BlockSpec((bsz, tn), lambda j: (0, j)),
        compiler_params=pltpu.CompilerParams(dimension_semantics=("arbitrary",)),
        name="adaln",
    )(c, w, b.reshape(1, n))


def _inproj_kernel(x_ref, shift_ref, scale_ref, g1_ref, wqkv_ref, wlat_ref, wgate_ref,
                   gq_ref, gk_ref, gql_ref, gkvl_ref, wuq_ref, wuk_ref, wuv_ref,
                   gmq_ref, gmk_ref, pos_ref, freq_ref,
                   qa_ref, ka_ref, va_ref, qm_ref, km_ref, vm_ref, sgn_ref, sgm_ref):
    d = x_ref.shape[1]
    x = x_ref[...]
    h = _rms(x, d) * g1_ref[...]
    h = h * (1.0 + scale_ref[0]) + shift_ref[0]
    hb = _bf(h)

    qkv = _dot(hb, wqkv_ref[...])
    lat = _dot(hb, wlat_ref[...])
    gts = _dot(hb, wgate_ref[...])
    sgn_ref[...] = _bf(_sigmoid(gts[:, :d]))
    sgm_ref[...] = _bf(_sigmoid(gts[:, d:]))

    na_w = NA_HEADS * NA_HEAD_DIM
    lane = lax.broadcasted_iota(jnp.int32, (1, LANES), 1)
    lo_half = lane < NA_HEAD_DIM
    for p in range(na_w // LANES):
        sl = slice(p * LANES, (p + 1) * LANES)
        for src_off, g_ref, dst_ref in ((0, gq_ref, qa_ref), (na_w, gk_ref, ka_ref)):
            t = qkv[:, src_off + p * LANES: src_off + (p + 1) * LANES]
            sq = t * t
            s_lo = jnp.sum(jnp.where(lo_half, sq, 0.0), axis=-1, keepdims=True)
            s_hi = jnp.sum(jnp.where(lo_half, 0.0, sq), axis=-1, keepdims=True)
            r = jnp.where(lo_half,
                          lax.rsqrt(s_lo * (1.0 / NA_HEAD_DIM) + EPS),
                          lax.rsqrt(s_hi * (1.0 / NA_HEAD_DIM) + EPS))
            dst_ref[:, sl] = _bf(t * r * g_ref[:, sl])
    va_ref[...] = _bf(qkv[:, 2 * na_w: 3 * na_w])

    q_rank = gql_ref.shape[1]
    kv_rank = gkvl_ref.shape[1]
    qln = _rms(lat[:, :q_rank], q_rank) * gql_ref[...]
    kvn = _bf(_rms(lat[:, q_rank:q_rank + kv_rank], kv_rank) * gkvl_ref[...])
    qpre = _dot(_bf(qln), wuq_ref[...])
    knope = _dot(kvn, wuk_ref[...])
    vm_ref[...] = _bf(_dot(kvn, wuv_ref[...]))
    krot = lat[:, q_rank + kv_rank:]

    tm = x_ref.shape[0]
    half = MLA_ROPE_DIM // 2
    ang_t = freq_ref[...] * pos_ref[...].astype(F32)
    cos_t = jnp.cos(ang_t)
    sin_t = jnp.sin(ang_t)
    l_i = lax.broadcasted_iota(jnp.int32, (LANES, half), 0)
    j_i = lax.broadcasted_iota(jnp.int32, (LANES, half), 1)
    hit = jnp.where((l_i >= MLA_NOPE_DIM) & (l_i < MLA_QK_DIM)
                    & (((l_i - MLA_NOPE_DIM) & (half - 1)) == j_i), 1.0, 0.0)
    first_half = l_i < MLA_NOPE_DIM + half
    eye = jnp.where(lax.broadcasted_iota(jnp.int32, (tm, tm), 0)
                    == lax.broadcasted_iota(jnp.int32, (tm, tm), 1), 1.0, 0.0).astype(BF16)

    def table(sel, vals, fill_nope):
        hi, lo = _split(vals)
        w = _dot(_bf(sel), hi) + _dot(_bf(sel), lo)
        if fill_nope:
            w = jnp.where(lax.broadcasted_iota(jnp.int32, (LANES, tm), 0) < MLA_NOPE_DIM, 1.0, w)
        hi, lo = _split(w)
        return _dot_nt(eye, hi) + _dot_nt(eye, lo)

    c_tab = table(hit, cos_t, True)
    s_up = table(jnp.where(first_half, 0.0, hit), sin_t, False)
    s_dn = table(jnp.where(first_half, -hit, 0.0), sin_t, False)

    def rope(t):
        return t * c_tab + pltpu.roll(t, half, 1) * s_up + pltpu.roll(t, LANES - half, 1) * s_dn

    kr = rope(krot)
    for hd in range(MLA_HEADS):
        sl = slice(hd * LANES, (hd + 1) * LANES)
        qh = rope(qpre[:, sl])
        qm_ref[:, sl] = _bf(_rms(qh, MLA_QK_DIM) * gmq_ref[:, sl])
        kh = knope[:, sl] + kr
        km_ref[:, sl] = _bf(_rms(kh, MLA_QK_DIM) * gmk_ref[:, sl])


def _inproj(x2, mod3, g1, wqkv, wlat, wgate, gq, gk, gql, gkvl, wuq, wuk, wuv, gmq, gmk,
            pos, freq, seq):
    t, d = x2.shape
    tm = 256
    per_b = seq // tm
    na_w = NA_HEADS * NA_HEAD_DIM
    mla_w = MLA_HEADS * LANES
    v_w = MLA_HEADS * MLA_V_DIM

    def full(a):
        return pl.BlockSpec(a.shape, lambda i: (0,) * a.ndim)

    def rows(w):
        return pl.BlockSpec((tm, w), lambda i: (i, 0))

    out_shapes = (
        jax.ShapeDtypeStruct((t, na_w), BF16), jax.ShapeDtypeStruct((t, na_w), BF16),
        jax.ShapeDtypeStruct((t, na_w), BF16),
        jax.ShapeDtypeStruct((t, mla_w), BF16), jax.ShapeDtypeStruct((t, mla_w), BF16),
        jax.ShapeDtypeStruct((t, v_w), BF16),
        jax.ShapeDtypeStruct((t, d), BF16), jax.ShapeDtypeStruct((t, d), BF16),
    )
    return pl.pallas_call(
        _inproj_kernel,
        out_shape=out_shapes,
        grid=(t // tm,),
        in_specs=[
            rows(d),
            pl.BlockSpec((1, 1, d), lambda i: (i // per_b, 0, 0)),
            pl.BlockSpec((1, 1, d), lambda i: (i // per_b, 0, 1)),
            full(g1), full(wqkv), full(wlat), full(wgate), full(gq), full(gk), full(gql),
            full(gkvl), full(wuq), full(wuk), full(wuv), full(gmq), full(gmk),
            pl.BlockSpec((1, tm), lambda i: (0, i)),
            full(freq),
        ],
        out_specs=(rows(na_w), rows(na_w), rows(na_w), rows(mla_w), rows(mla_w), rows(v_w),
                   rows(d), rows(d)),
        compiler_params=pltpu.CompilerParams(dimension_semantics=("arbitrary",),
                                             vmem_limit_bytes=V7X_VMEM_LIMIT),
        name="inproj",
    )(x2, mod3, mod3, g1, wqkv, wlat, wgate, gq, gk, gql, gkvl, wuq, wuk, wuv, gmq, gmk,
      pos, freq)


def _na_block_geometry(block_type, n_rows):
    if block_type == 0:
        return 0, 0
    if block_type == 1:
        r0 = NA_QROWS
        return r0, r0 - NA_WIN_ROWS // 2
    return n_rows - NA_QROWS, n_rows - NA_BAND


def _na_bias_kernel(rpb_ref, o_ref, m_ref, *, n_rows):
    hd = pl.program_id(0)
    n_dr = 2 * NA_WIN_ROWS - 1
    n_dc = 2 * NA_WIN_COLS - 1
    qc = lax.broadcasted_iota(jnp.int32, (GRID_W, LANES), 0)
    kc = lax.broadcasted_iota(jnp.int32, (GRID_W, LANES), 1) & (GRID_W - 1)
    dc = jnp.clip(kc - qc, -(NA_WIN_COLS - 1), NA_WIN_COLS - 1) + (NA_WIN_COLS - 1)
    cstart = jnp.clip(qc - NA_WIN_COLS // 2, 0, GRID_W - NA_WIN_COLS)
    col_ok = (kc >= cstart) & (kc < cstart + NA_WIN_COLS)
    for i_dr in range(n_dr):
        acc = jnp.zeros((GRID_W, LANES), F32)
        for t in range(n_dc):
            acc = jnp.where(dc == t, rpb_ref[hd, i_dr * n_dc + t], acc)
        m_ref[i_dr] = jnp.where(col_ok, acc * LOG2E, NEG_BIG)
    neg = jnp.full((GRID_W, LANES), NEG_BIG, F32)
    lo_half = lax.broadcasted_iota(jnp.int32, (GRID_W, LANES), 1) < GRID_W
    kh = NA_WIN_ROWS
    for bt in range(NA_BLOCK_TYPES):
        r0, start = _na_block_geometry(bt, n_rows)
        for i in range(NA_QROWS):
            r = r0 + i
            rs = min(max(r - kh // 2, 0), n_rows - kh)
            for jp in range(NA_BAND // 2):
                halves = []
                for j in (2 * jp, 2 * jp + 1):
                    krow = start + j
                    if rs <= krow < rs + kh:
                        halves.append(m_ref[krow - r + (NA_WIN_ROWS - 1)])
                    else:
                        halves.append(neg)
                tile = jnp.where(lo_half, halves[0], halves[1])
                o_ref[bt, 0, i * GRID_W:(i + 1) * GRID_W, jp * LANES:(jp + 1) * LANES] = tile


def _na_bias(rpb, n_rows):
    heads = rpb.shape[0]
    nq = NA_QROWS * GRID_W
    nk = NA_BAND * GRID_W
    rpb2 = rpb.reshape(heads, -1)
    return pl.pallas_call(
        functools.partial(_na_bias_kernel, n_rows=n_rows),
        out_shape=jax.ShapeDtypeStruct((NA_BLOCK_TYPES, heads, nq, nk), F32),
        grid=(heads,),
        in_specs=[pl.BlockSpec(memory_space=pltpu.SMEM)],
        out_specs=pl.BlockSpec((NA_BLOCK_TYPES, 1, nq, nk), lambda hd: (0, hd, 0, 0)),
        scratch_shapes=[pltpu.VMEM((2 * NA_WIN_ROWS - 1, GRID_W, LANES), F32)],
        compiler_params=pltpu.CompilerParams(dimension_semantics=("arbitrary",)),
        name="na_bias",
    )(rpb2)


def _softmax_pv(s, v_pair, hh, half):
    lane = lax.broadcasted_iota(jnp.int32, (1, LANES), 1)
    mine = (lane < half) if hh == 0 else (lane >= half)
    den_lane = half if hh == 0 else 0
    m = jnp.max(s, axis=-1, keepdims=True)
    p = _bf(jnp.exp2(s - m))
    ones_row = jnp.where(lane == den_lane, 1.0, 0.0).astype(BF16)
    o = _dot(p, jnp.where(mine, v_pair, ones_row))
    den = jnp.sum(jnp.where(lane == den_lane, o, 0.0), axis=-1, keepdims=True)
    return jnp.where(mine, o / den, 0.0)


def _na_kernel(q_ref, k_ref, v_ref, bias_ref, o_ref, *, n_blocks, n_rows):
    blk = pl.program_id(1)
    start_row = jnp.where(blk == 0, 0,
                          jnp.where(blk == n_blocks - 1, n_rows - NA_BAND,
                                    blk * NA_QROWS - NA_WIN_ROWS // 2))
    off = pl.multiple_of(start_row * GRID_W, GRID_W)
    nk = NA_BAND * GRID_W
    lane = lax.broadcasted_iota(jnp.int32, (1, LANES), 1)
    for p in range(NA_HEADS * NA_HEAD_DIM // LANES):
        sl = slice(p * LANES, (p + 1) * LANES)
        qp = q_ref[:, sl]
        kb = k_ref[pl.ds(off, nk), sl]
        vb = v_ref[pl.ds(off, nk), sl]
        acc = jnp.zeros((q_ref.shape[0], LANES), F32)
        for hh in range(2):
            mine = (lane < NA_HEAD_DIM) if hh == 0 else (lane >= NA_HEAD_DIM)
            qh = jnp.where(mine, qp, jnp.zeros_like(qp))
            s = _dot_nt(qh, kb) + bias_ref[0, 2 * p + hh]
            acc = acc + _softmax_pv(s, vb, hh, NA_HEAD_DIM)
        o_ref[:, sl] = _bf(acc)


def _na_attention(qa, ka, va, bias, bsz, seq):
    t, w = qa.shape
    n_rows = seq // GRID_W
    n_blocks = n_rows // NA_QROWS
    tq = NA_QROWS * GRID_W
    nk = NA_BAND * GRID_W
    heads = bias.shape[1]

    def btype(blk):
        return jnp.where(blk == 0, 0, jnp.where(blk == n_blocks - 1, 2, 1))

    return pl.pallas_call(
        functools.partial(_na_kernel, n_blocks=n_blocks, n_rows=n_rows),
        out_shape=jax.ShapeDtypeStruct((t, w), BF16),
        grid=(bsz, n_blocks),
        in_specs=[
            pl.BlockSpec((tq, w), lambda b, blk: (b * n_blocks + blk, 0)),
            pl.BlockSpec((seq, w), lambda b, blk: (b, 0)),
            pl.BlockSpec((seq, w), lambda b, blk: (b, 0)),
            pl.BlockSpec((1, heads, tq, nk), lambda b, blk: (btype(blk), 0, 0, 0)),
        ],
        out_specs=pl.BlockSpec((tq, w), lambda b, blk: (b * n_blocks + blk, 0)),
        compiler_params=pltpu.CompilerParams(dimension_semantics=("arbitrary", "arbitrary"),
                                             vmem_limit_bytes=V7X_VMEM_LIMIT),
        name="na_attn",
    )(qa, ka, va, bias)


def _mla_kernel(q_ref, k_ref, v_ref, o_ref):
    for pp in range(MLA_PAIRS_PER_STEP):
        v_pair = v_ref[:, pp * LANES:(pp + 1) * LANES]
        acc = jnp.zeros((q_ref.shape[0], LANES), F32)
        for hh in range(2):
            sl = slice((2 * pp + hh) * LANES, (2 * pp + hh + 1) * LANES)
            s = _dot_nt(q_ref[:, sl], k_ref[:, sl])
            acc = acc + _softmax_pv(s, v_pair, hh, MLA_V_DIM)
        o_ref[:, pp * LANES:(pp + 1) * LANES] = _bf(acc)


def _mla_attention(qm, km, vm, bsz, seq):
    t = qm.shape[0]
    tq = 512
    nq = seq // tq
    groups = MLA_HEADS // (2 * MLA_PAIRS_PER_STEP)
    qk_w = 2 * MLA_PAIRS_PER_STEP * LANES
    v_w = MLA_PAIRS_PER_STEP * LANES
    return pl.pallas_call(
        _mla_kernel,
        out_shape=jax.ShapeDtypeStruct((t, MLA_HEADS * MLA_V_DIM), BF16),
        grid=(bsz, groups, nq),
        in_specs=[
            pl.BlockSpec((tq, qk_w), lambda b, p, i: (b * nq + i, p)),
            pl.BlockSpec((seq, qk_w), lambda b, p, i: (b, p)),
            pl.BlockSpec((seq, v_w), lambda b, p, i: (b, p)),
        ],
        out_specs=pl.BlockSpec((tq, v_w), lambda b, p, i: (b * nq + i, p)),
        compiler_params=pltpu.CompilerParams(
            dimension_semantics=("arbitrary", "arbitrary", "arbitrary"),
            vmem_limit_bytes=V7X_VMEM_LIMIT),
        name="mla_attn",
    )(qm, km, vm)


def _merge_kernel(x_ref, yna_ref, ymla_ref, sgn_ref, sgm_ref, wpn_ref, wpm_ref, wout_ref,
                  gate1_ref, shift2_ref, scale2_ref, g2_ref, wrt_ref,
                  x1_ref, h2_ref, lt_ref):
    d = x_ref.shape[1]
    merged = (sgn_ref[...].astype(F32) * _dot(yna_ref[...], wpn_ref[...])
              + sgm_ref[...].astype(F32) * _dot(ymla_ref[...], wpm_ref[...]))
    x1 = x_ref[...] + gate1_ref[0] * _dot(_bf(merged), wout_ref[...])
    x1_ref[...] = x1
    h2 = _rms(x1, d) * g2_ref[...]
    h2 = h2 * (1.0 + scale2_ref[0]) + shift2_ref[0]
    h2_ref[...] = _bf(h2)
    lt_ref[...] = _dot3_nt(wrt_ref[...], h2)


def _merge(x2, yna, ymla, sgn, sgm, wpn, wpm, wout, mod3, g2, wrt, seq):
    t, d = x2.shape
    tm = 512
    per_b = seq // tm
    n_exp = wrt.shape[0]

    def full(a):
        return pl.BlockSpec(a.shape, lambda i: (0,) * a.ndim)

    def rows(w):
        return pl.BlockSpec((tm, w), lambda i: (i, 0))

    def modblk(j):
        return pl.BlockSpec((1, 1, d), lambda i: (i // per_b, 0, j))

    return pl.pallas_call(
        _merge_kernel,
        out_shape=(jax.ShapeDtypeStruct((t, d), F32), jax.ShapeDtypeStruct((t, d), BF16),
                   jax.ShapeDtypeStruct((n_exp, t), F32)),
        grid=(t // tm,),
        in_specs=[rows(d), rows(yna.shape[1]), rows(ymla.shape[1]), rows(d), rows(d),
                  full(wpn), full(wpm), full(wout),
                  modblk(2), modblk(3), modblk(4), full(g2), full(wrt)],
        out_specs=(rows(d), rows(d), pl.BlockSpec((n_exp, tm), lambda i: (0, i))),
        compiler_params=pltpu.CompilerParams(dimension_semantics=("arbitrary",),
                                             vmem_limit_bytes=V7X_VMEM_LIMIT),
        name="merge",
    )(x2, yna, ymla, sgn, sgm, wpn, wpm, wout, mod3, mod3, mod3, g2, wrt)


def _route_kernel(lt_ref, eb_ref, o_ref, q_ref):
    n_exp, tn = lt_ref.shape
    per_g = n_exp // N_GROUPS
    neg_inf = -jnp.inf
    sc = _sigmoid(lt_ref[...])
    sel = sc + eb_ref[...]
    sc3 = sc.reshape(N_GROUPS, per_g, tn)
    g3 = sel.reshape(N_GROUPS, per_g, tn)
    io = lax.broadcasted_iota(jnp.int32, (N_GROUPS, per_g, tn), 1)
    gio = lax.broadcasted_iota(jnp.int32, (N_GROUPS, per_g, tn), 0)
    eio = gio * per_g + io

    m1 = jnp.max(g3, axis=1, keepdims=True)
    i1 = jnp.min(jnp.where(g3 == m1, io, per_g), axis=1, keepdims=True)
    m2 = jnp.max(jnp.where(io == i1, neg_inf, g3), axis=1, keepdims=True)
    gs = m1 + m2

    g1io = lax.broadcasted_iota(jnp.int32, (N_GROUPS, 1, tn), 0)
    gsel = jnp.zeros((N_GROUPS, 1, tn), F32)
    cur = gs
    for _ in range(TOPK_GROUPS):
        m = jnp.max(cur, axis=0, keepdims=True)
        i = jnp.min(jnp.where(cur == m, g1io, N_GROUPS), axis=0, keepdims=True)
        pick = g1io == i
        gsel = jnp.where(pick, 1.0, gsel)
        cur = jnp.where(pick, neg_inf, cur)

    cur = jnp.where(gsel > 0.0, g3, neg_inf)
    chosen = jnp.zeros((N_GROUPS, per_g, tn), F32)
    for _ in range(TOP_K):
        m = jnp.max(jnp.max(cur, axis=1, keepdims=True), axis=0, keepdims=True)
        cand = jnp.where(cur == m, eio, n_exp)
        i = jnp.min(jnp.min(cand, axis=1, keepdims=True), axis=0, keepdims=True)
        pick = eio == i
        chosen = jnp.where(pick, 1.0, chosen)
        cur = jnp.where(pick, neg_inf, cur)

    w = jnp.where(chosen > 0.0, sc3, 0.0)
    tot = jnp.sum(jnp.sum(w, axis=1, keepdims=True), axis=0, keepdims=True)
    gates = (w / tot * ROUTED_SCALE).reshape(n_exp, tn)
    o_ref[...] = gates
    routed = jnp.where(gates > 0.0, 1.0, 0.0).astype(BF16)
    n_row = _dot_nt(jnp.ones((8, tn), BF16), routed)[0:1]
    q_ref[0] = jnp.floor((n_row + (MOE_CHUNK - 1)) * (1.0 / MOE_CHUNK)).astype(jnp.int32)


def _route(lt, e_bias):
    n_exp, t = lt.shape
    tn = MOE_TB
    return pl.pallas_call(
        _route_kernel,
        out_shape=(jax.ShapeDtypeStruct((n_exp, t), F32),
                   jax.ShapeDtypeStruct((t // tn, 1, n_exp), jnp.int32)),
        grid=(t // tn,),
        in_specs=[pl.BlockSpec((n_exp, tn), lambda i: (0, i)),
                  pl.BlockSpec((n_exp, 1), lambda i: (0, 0))],
        out_specs=(pl.BlockSpec((n_exp, tn), lambda i: (0, i)),
                   pl.BlockSpec((1, 1, n_exp), lambda i: (i, 0, 0))),
        compiler_params=pltpu.CompilerParams(dimension_semantics=("arbitrary",)),
        name="route",
    )(lt, e_bias.reshape(n_exp, 1))


def _for_each_chunk(n, fn):
    def quad(j, carry):
        for u in range(4):
            fn(j * 4 + u)
        return carry
    lax.fori_loop(0, n >> 2, quad, 0)
    base = (n >> 2) << 2
    for u in range(3):
        @pl.when(base + u < n)
        def _():
            fn(base + u)


def _dispatch_kernel(dtab_s, nch_s, total_s,
                     gt_ref, qrow_ref, qbrow_ref, h_ref, xs_ref, stage_ref, zero_ref, sem):
    b = pl.program_id(0)
    nb = pl.num_programs(0)
    slot = lax.rem(b, 2)
    n_exp, tb = gt_ref.shape
    rmax = stage_ref.shape[1] * MOE_CHUNK
    cpg = MOE_ROWGROUP // MOE_CHUNK

    routed = gt_ref[...] > 0.0
    before = (lax.broadcasted_iota(jnp.int32, (tb, tb), 0)
              < lax.broadcasted_iota(jnp.int32, (tb, tb), 1))
    pos = _dot(jnp.where(routed, 1.0, 0.0).astype(BF16), jnp.where(before, 1.0, 0.0).astype(BF16))
    posm = _bf(jnp.where(routed, pos, -1.0))
    qrow = qrow_ref[0]
    qbrow = qbrow_ref[0]
    qbrow_f = qbrow.astype(F32)
    h = h_ref[...]
    for g in range(rmax // MOE_ROWGROUP):
        @pl.when(g * MOE_ROWGROUP < nch_s[b] * MOE_CHUNK)
        def _():
            r0 = g * MOE_ROWGROUP
            chunk = (lax.broadcasted_iota(jnp.int32, (MOE_ROWGROUP, n_exp), 0) + r0) >> MOE_CHUNK_SHIFT
            own = jnp.where(chunk >= qbrow, jnp.where(chunk < qbrow + qrow, 1.0, 0.0), 0.0)
            rank = _dot(_bf(own), posm)
            start = jnp.sum(own * qbrow_f, axis=-1, keepdims=True) * MOE_CHUNK
            rel = (lax.broadcasted_iota(jnp.int32, (MOE_ROWGROUP, 1), 0) + r0).astype(F32) - start
            onehot = jnp.where(rank == rel, 1.0, 0.0).astype(BF16)
            rows = _bf(_dot(onehot, h))
            stage_ref[slot, g * cpg:(g + 1) * cpg] = rows.reshape(cpg, MOE_CHUNK, rows.shape[1])

    def start_chunk(i):
        pltpu.make_async_copy(stage_ref.at[slot, i], xs_ref.at[dtab_s[b, i]], sem.at[slot]).start()
    _for_each_chunk(nch_s[b], start_chunk)

    def wait_chunks(n, sl):
        def wait_chunk(i):
            pltpu.make_async_copy(zero_ref, xs_ref.at[0], sem.at[sl]).wait()
        _for_each_chunk(n, wait_chunk)

    @pl.when(b > 0)
    def _():
        wait_chunks(nch_s[b - 1], 1 - slot)

    @pl.when(b == nb - 1)
    def _():
        zero_ref[...] = jnp.zeros_like(zero_ref)
        n_tail = MOE_TM // MOE_CHUNK
        for c in range(n_tail):
            pltpu.make_async_copy(zero_ref, xs_ref.at[total_s[0] + c], sem.at[slot]).start()
        wait_chunks(nch_s[b] + n_tail, slot)


def _dispatch(gates_t, q, qbase, dtab, nch, total, h2, n_chunks):
    n_exp, t = gates_t.shape
    d = h2.shape[1]
    nb = t // MOE_TB
    rmax = _moe_stage_rows(n_exp)
    grid_spec = pltpu.PrefetchScalarGridSpec(
        num_scalar_prefetch=3,
        grid=(nb,),
        in_specs=[
            pl.BlockSpec((n_exp, MOE_TB), lambda b, *_: (0, b)),
            pl.BlockSpec((1, 1, n_exp), lambda b, *_: (b, 0, 0)),
            pl.BlockSpec((1, 1, n_exp), lambda b, *_: (b, 0, 0)),
            pl.BlockSpec((MOE_TB, d), lambda b, *_: (b, 0)),
        ],
        out_specs=pl.BlockSpec(memory_space=pl.ANY),
        scratch_shapes=[pltpu.VMEM((2, rmax // MOE_CHUNK, MOE_CHUNK, d), BF16),
                        pltpu.VMEM((MOE_CHUNK, d), BF16), pltpu.SemaphoreType.DMA((2,))],
    )
    return pl.pallas_call(
        _dispatch_kernel,
        out_shape=jax.ShapeDtypeStruct((n_chunks, MOE_CHUNK, d), BF16),
        grid_spec=grid_spec,
        compiler_params=pltpu.CompilerParams(dimension_semantics=("arbitrary",),
                                             vmem_limit_bytes=V7X_VMEM_LIMIT),
        name="moe_dispatch",
    )(dtab, nch, total, gates_t, q.reshape(nb, 1, n_exp), qbase.reshape(nb, 1, n_exp), h2)


def _tile_pieces():
    cpt = MOE_TM // MOE_CHUNK
    return [1 << s for s in range(cpt.bit_length() - 1, -1, -1)]


def _expert_kernel(off_s, len_s, next_s, first_s, xs_ref, wg_ref, wu_ref, wd_ref, ys_ref,
                   xbuf, ybuf, wg_b, wu_b, wd_b, state, sem_in, sem_out):
    e = pl.program_id(0)
    n_exp = pl.num_programs(0)
    cpt = MOE_TM // MOE_CHUNK
    d = xbuf.shape[3]
    pieces = _tile_pieces()

    def tile_in(ee, tt, sl):
        return pltpu.make_async_copy(xs_ref.at[pl.ds(off_s[ee] + tt * cpt, cpt)], xbuf.at[sl],
                                     sem_in.at[sl])

    def for_each_piece(valid, fn):
        for k, piece in enumerate(pieces):
            @pl.when((valid & piece) != 0)
            def _():
                fn(k, piece, valid & ~(2 * piece - 1))

    def tile_out(sl, dst_chunk, k, piece, start):
        return pltpu.make_async_copy(ybuf.at[sl, pl.ds(start, piece)],
                                     ys_ref.at[pl.ds(dst_chunk + start, piece)], sem_out.at[sl, k])

    def drain(sl):
        for_each_piece(state[1 + sl], lambda k, piece, start: tile_out(sl, 0, k, piece, start).wait())
        state[1 + sl] = 0

    @pl.when(e == 0)
    def _():
        state[0] = 0
        state[1] = 0
        state[2] = 0

        @pl.when(first_s[0] < n_exp)
        def _():
            tile_in(first_s[0], 0, 0).start()

    n_valid = len_s[e]
    n_tiles = (n_valid + cpt - 1) // cpt

    @pl.when(n_valid > 0)
    def _():
        wg_b[...] = _bf(wg_ref[0])
        wu_b[...] = _bf(wu_ref[0])
        wd_b[...] = _bf(wd_ref[0])

        def tile(t, carry):
            sl = state[0] & 1
            tile_in(e, t, sl).wait()
            more = t + 1 < n_tiles
            nxt_e = jnp.where(more, e, next_s[e])
            nxt_t = jnp.where(more, t + 1, 0)

            @pl.when(nxt_e < n_exp)
            def _():
                tile_in(nxt_e, nxt_t, 1 - sl).start()

            drain(sl)
            x = xbuf[sl].reshape(MOE_TM, d)
            a = _silu(_dot(x, wg_b[...])) * _dot(x, wu_b[...])
            ybuf[sl] = _bf(_dot(_bf(a), wd_b[...])).reshape(cpt, MOE_CHUNK, d)
            valid = jnp.minimum(n_valid - t * cpt, cpt)
            dst = off_s[e] + t * cpt
            for_each_piece(valid, lambda k, piece, start: tile_out(sl, dst, k, piece, start).start())
            state[1 + sl] = valid
            state[0] = state[0] + 1
            return carry
        lax.fori_loop(0, n_tiles, tile, 0)

    @pl.when(e == n_exp - 1)
    def _():
        drain(0)
        drain(1)


def _experts(off, per_exp, nxt, first, xs, wg, wu, wd):
    n_chunks, _, d = xs.shape
    n_exp, _, ff = wg.shape
    cpt = MOE_TM // MOE_CHUNK

    def w_blk(e, *_):
        return (e, 0, 0)

    grid_spec = pltpu.PrefetchScalarGridSpec(
        num_scalar_prefetch=4,
        grid=(n_exp,),
        in_specs=[pl.BlockSpec(memory_space=pl.ANY),
                  pl.BlockSpec((1, d, ff), w_blk), pl.BlockSpec((1, d, ff), w_blk),
                  pl.BlockSpec((1, ff, d), w_blk)],
        out_specs=pl.BlockSpec(memory_space=pl.ANY),
        scratch_shapes=[pltpu.VMEM((2, cpt, MOE_CHUNK, d), BF16),
                        pltpu.VMEM((2, cpt, MOE_CHUNK, d), BF16),
                        pltpu.VMEM((d, ff), BF16), pltpu.VMEM((d, ff), BF16),
                        pltpu.VMEM((ff, d), BF16),
                        pltpu.SMEM((3,), jnp.int32),
                        pltpu.SemaphoreType.DMA((2,)),
                        pltpu.SemaphoreType.DMA((2, len(_tile_pieces())))],
    )
    return pl.pallas_call(
        _expert_kernel,
        out_shape=jax.ShapeDtypeStruct((n_chunks, MOE_CHUNK, d), BF16),
        grid_spec=grid_spec,
        compiler_params=pltpu.CompilerParams(dimension_semantics=("arbitrary",),
                                             vmem_limit_bytes=V7X_VMEM_LIMIT),
        name="moe_experts",
    )(off, per_exp, nxt, first, xs, wg, wu, wd)


def _combine_kernel(dtab_s, nch_s,
                    gt_ref, qcol_ref, qbcol_ref, h_ref, x1_ref, gate2_ref,
                    wsg_ref, wsu_ref, wsd_ref, ys_ref, o_ref, stage_ref, sem):
    b = pl.program_id(0)
    nb = pl.num_programs(0)
    slot = lax.rem(b, 2)
    n_exp, tb = gt_ref.shape
    rmax = stage_ref.shape[1] * MOE_CHUNK

    def fetch(bb, sl):
        def start_chunk(i):
            pltpu.make_async_copy(ys_ref.at[dtab_s[bb, i]], stage_ref.at[sl, i], sem.at[sl]).start()
        _for_each_chunk(nch_s[bb], start_chunk)

    @pl.when(b == 0)
    def _():
        stage_ref[...] = jnp.zeros_like(stage_ref)
        fetch(0, 0)

    def wait_chunk(i):
        pltpu.make_async_copy(ys_ref.at[0], stage_ref.at[slot, 0], sem.at[slot]).wait()
    _for_each_chunk(nch_s[b], wait_chunk)

    @pl.when(b + 1 < nb)
    def _():
        fetch(b + 1, 1 - slot)

    gt = gt_ref[...]
    routed = jnp.where(gt > 0.0, 1.0, 0.0).astype(BF16)
    i0 = lax.broadcasted_iota(jnp.int32, (tb, tb), 0)
    i1 = lax.broadcasted_iota(jnp.int32, (tb, tb), 1)
    eye = jnp.where(i0 == i1, 1.0, 0.0).astype(BF16)
    routed_t = _dot_nt(eye, routed)
    gates_tok = _dot_nt(eye, _bf(gt))
    earlier = jnp.where(i1 < i0, 1.0, 0.0).astype(BF16)
    pos_t = _dot(earlier, _bf(routed_t))
    posm_t = _bf(jnp.where(routed_t > 0.0, pos_t, -1.0))

    qcol = qcol_ref[0]
    qbcol = qbcol_ref[0]
    chunk = lax.broadcasted_iota(jnp.int32, (n_exp, rmax), 1) >> MOE_CHUNK_SHIFT
    own = jnp.where(chunk >= qbcol, jnp.where(chunk < qbcol + qcol, 1.0, 0.0), 0.0)
    own_b = _bf(own)
    rank = _dot(posm_t, own_b)
    wexp = _dot(_bf(gates_tok), own_b)
    start = jnp.sum(own * qbcol.astype(F32), axis=0, keepdims=True) * MOE_CHUNK
    rel = lax.broadcasted_iota(jnp.int32, (1, rmax), 1).astype(F32) - start
    weights = _bf(jnp.where(rank == rel, wexp, 0.0))
    routed_out = _dot(weights, stage_ref[slot].reshape(rmax, o_ref.shape[1]))

    h = h_ref[...]
    a = _silu(_dot(h, wsg_ref[...])) * _dot(h, wsu_ref[...])
    shared = _dot(_bf(a), wsd_ref[...])
    o_ref[...] = x1_ref[...] + gate2_ref[0] * (routed_out + shared)


def _combine(gates_t, q, qbase, dtab, nch, h2, x1, mod3, wsg, wsu, wsd, ys, seq):
    n_exp, t = gates_t.shape
    d = h2.shape[1]
    nb = t // MOE_TB
    per_b = seq // MOE_TB
    stage_chunks = _moe_stage_rows(n_exp) // MOE_CHUNK

    def full(a):
        return pl.BlockSpec(a.shape, lambda b, *_: (0,) * a.ndim)

    def rows(w):
        return pl.BlockSpec((MOE_TB, w), lambda b, *_: (b, 0))

    grid_spec = pltpu.PrefetchScalarGridSpec(
        num_scalar_prefetch=2,
        grid=(nb,),
        in_specs=[
            pl.BlockSpec((n_exp, MOE_TB), lambda b, *_: (0, b)),
            pl.BlockSpec((1, n_exp, 1), lambda b, *_: (b, 0, 0)),
            pl.BlockSpec((1, n_exp, 1), lambda b, *_: (b, 0, 0)),
            rows(d), rows(d),
            pl.BlockSpec((1, 1, d), lambda b, *_: (b // per_b, 0, 5)),
            full(wsg), full(wsu), full(wsd),
            pl.BlockSpec(memory_space=pl.ANY),
        ],
        out_specs=rows(d),
        scratch_shapes=[pltpu.VMEM((2, stage_chunks, MOE_CHUNK, d), BF16),
                        pltpu.SemaphoreType.DMA((2,))],
    )
    return pl.pallas_call(
        _combine_kernel,
        out_shape=jax.ShapeDtypeStruct((t, d), F32),
        grid_spec=grid_spec,
        compiler_params=pltpu.CompilerParams(dimension_semantics=("arbitrary",),
                                             vmem_limit_bytes=V7X_VMEM_LIMIT),
        name="moe_combine",
    )(dtab, nch, gates_t, q.reshape(nb, n_exp, 1), qbase.reshape(nb, n_exp, 1),
      h2, x1, mod3, wsg, wsu, wsd, ys)


def _moe_plan(q):
    nb, n_exp = q.shape
    qbase = jnp.cumsum(q, axis=1) - q
    nch = jnp.sum(q, axis=1)
    per_exp = jnp.sum(q, axis=0)
    off = jnp.cumsum(per_exp) - per_exp
    dstq = off[None, :] + jnp.cumsum(q, axis=0) - q
    i = jnp.arange(_moe_block_chunks(n_exp), dtype=jnp.int32)
    ii = i[None, :, None]
    owned = (ii >= qbase[:, None, :]) & (ii < (qbase + q)[:, None, :])
    dtab = i[None, :] + jnp.sum(jnp.where(owned, (dstq - qbase)[:, None, :], 0), axis=2)
    ids = jnp.arange(n_exp, dtype=jnp.int32)
    later = (ids[None, :] > ids[:, None]) & (per_exp[None, :] > 0)
    nxt = jnp.min(jnp.where(later, ids[None, :], n_exp), axis=1)
    first = jnp.min(jnp.where(per_exp > 0, ids, n_exp)).reshape(1)
    total = jnp.sum(per_exp).reshape(1)
    return qbase, dtab, nch, off, per_exp, nxt, first, total


def _moe_block_chunks(n_exp):
    return MOE_TB * TOP_K // MOE_CHUNK + n_exp


def _moe_stage_rows(n_exp):
    return -(-_moe_block_chunks(n_exp) * MOE_CHUNK // MOE_ROWGROUP) * MOE_ROWGROUP


def _moe(h2, gates_t, q3, x1, mod3, wsg, wsu, wsd, wg, wu, wd, seq):
    n_exp, t = gates_t.shape
    nb = t // MOE_TB
    q = q3.reshape(nb, n_exp)
    qbase, dtab, nch, off, per_exp, nxt, first, total = _moe_plan(q)
    n_chunks = nb * _moe_block_chunks(n_exp) + MOE_TM // MOE_CHUNK
    xs = _dispatch(gates_t, q, qbase, dtab, nch, total, h2, n_chunks)
    ys = _experts(off, per_exp, nxt, first, xs, wg, wu, wd)
    return _combine(gates_t, q, qbase, dtab, nch, h2, x1, mod3, wsg, wsu, wsd, ys, seq)


def _pad_heads(w, heads, width):
    lead = w.shape[:-1]
    w = w.reshape(lead + (heads, width))
    w = jnp.pad(w, [(0, 0)] * len(lead) + [(0, 0), (0, LANES - width)])
    return w.reshape(lead + (heads * LANES,))


def kernel(x, c, positions, w_ada, b_ada, g_norm1, w_in, g_na_q, g_na_k, na_rpb, g_q_lat, w_uq,
           g_kv_lat, w_ukv, g_mla_q, g_mla_k, w_proj_na, w_proj_mla, w_out, g_norm2, w_router,
           e_bias, w_exp_gate, w_exp_up, w_exp_down, w_sh_gate, w_sh_up, w_sh_down):
    bsz, seq, d = x.shape
    t = bsz * seq
    depth = w_ada.shape[0]
    na_w = NA_HEADS * NA_HEAD_DIM
    q_rank = g_q_lat.shape[1]
    kv_rank = g_kv_lat.shape[1]
    n_rows = seq // GRID_W

    pos = positions.reshape(1, t)
    half = MLA_ROPE_DIM // 2
    freq = (ROPE_THETA ** (-jnp.arange(half, dtype=F32) / half)).reshape(half, 1)

    x2 = x.reshape(t, d)
    for l in range(depth):
        mod3 = _adaln(c, w_ada[l], b_ada[l]).reshape(bsz, 1, 6 * d)

        wi = w_in[l]
        o_lat = 3 * na_w
        o_rot = o_lat + q_rank + kv_rank
        o_gate = o_rot + MLA_ROPE_DIM
        wqkv = _bf(wi[:, :o_lat])
        w_rot = jnp.pad(wi[:, o_rot:o_gate], ((0, 0), (MLA_NOPE_DIM, LANES - MLA_QK_DIM)))
        wlat = _bf(jnp.concatenate([wi[:, o_lat:o_rot], w_rot], axis=1))
        wgate = _bf(wi[:, o_gate:])
        gq = (jnp.tile(g_na_q[l], NA_HEADS) * (NA_HEAD_DIM ** -0.5 * LOG2E)).reshape(1, na_w)
        gk = jnp.tile(g_na_k[l], NA_HEADS).reshape(1, na_w)
        wuq = _bf(_pad_heads(w_uq[l], MLA_HEADS, MLA_QK_DIM))
        wukv = w_ukv[l].reshape(kv_rank, MLA_HEADS, MLA_NOPE_DIM + MLA_V_DIM)
        wuk = _bf(_pad_heads(wukv[:, :, :MLA_NOPE_DIM].reshape(kv_rank, -1), MLA_HEADS, MLA_NOPE_DIM))
        wuv = _bf(wukv[:, :, MLA_NOPE_DIM:].reshape(kv_rank, MLA_HEADS * MLA_V_DIM))
        gmq = _pad_heads(jnp.tile(g_mla_q[l], MLA_HEADS) * (MLA_QK_DIM ** -0.5 * LOG2E),
                         MLA_HEADS, MLA_QK_DIM).reshape(1, -1)
        gmk = _pad_heads(jnp.tile(g_mla_k[l], MLA_HEADS), MLA_HEADS, MLA_QK_DIM).reshape(1, -1)

        qa, ka, va, qm, km, vm, sgn, sgm = _inproj(
            x2, mod3, g_norm1[l].reshape(1, d), wqkv, wlat, wgate, gq, gk,
            g_q_lat[l].reshape(1, q_rank), g_kv_lat[l].reshape(1, kv_rank), wuq, wuk, wuv,
            gmq, gmk, pos, freq, seq)

        bias = _na_bias(na_rpb[l], n_rows)
        y_na = _na_attention(qa, ka, va, bias, bsz, seq)
        y_mla = _mla_attention(qm, km, vm, bsz, seq)

        x1, h2, lt = _merge(x2, y_na, y_mla, sgn, sgm, _bf(w_proj_na[l]), _bf(w_proj_mla[l]),
                            _bf(w_out[l]), mod3, g_norm2[l].reshape(1, d), w_router[l].T, seq)
        gates_t, q3 = _route(lt, e_bias[l])
        x2 = _moe(h2, gates_t, q3, x1, mod3, _bf(w_sh_gate[l]), _bf(w_sh_up[l]),
                  _bf(w_sh_down[l]), w_exp_gate[l], w_exp_up[l], w_exp_down[l], seq)
    return x2.reshape(bsz, seq, d)
```

```python
import functools

import jax
import jax.numpy as jnp
import numpy as np
from jax import lax
from jax.experimental import pallas as pl
from jax.experimental.pallas import tpu as pltpu

GRID_W = 64
NA_HEADS = 8
NA_HEAD_DIM = 64
NA_WIN_ROWS = 8
NA_WIN_COLS = 16
MLA_HEADS = 8
MLA_NOPE_DIM = 64
MLA_ROPE_DIM = 32
MLA_V_DIM = 64
MLA_QK_DIM = MLA_NOPE_DIM + MLA_ROPE_DIM
ROPE_THETA = 10000.0
N_GROUPS = 8
TOPK_GROUPS = 4
TOP_K = 8
ROUTED_SCALE = 2.5
EPS = 1e-6
NEG_BIG = -1e30

LANES = 128
V7X_VMEM_LIMIT = 56 * 1024 * 1024

NA_QROWS = 4
NA_BAND = 12
NA_BLOCK_TYPES = 3
MLA_PAIRS_PER_STEP = 4
LOG2E = 1.4426950408889634
MOE_TB = 256
MOE_CHUNK_SHIFT = 4
MOE_CHUNK = 1 << MOE_CHUNK_SHIFT
MOE_TM = 1024
MOE_ROWGROUP = 512

F32 = jnp.float32
BF16 = jnp.bfloat16


def _bf(x):
    return x.astype(BF16)


def _dot(a, b):
    return jnp.dot(a, b, preferred_element_type=F32)


def _dot_nt(a, b):
    return lax.dot_general(a, b, (((1,), (1,)), ((), ())), preferred_element_type=F32)


def _split(x):
    hi = _bf(x)
    lo = _bf(x - hi.astype(F32))
    return hi, lo


def _dot3(a, b):
    ah, al = _split(a)
    bh, bl = _split(b)
    return _dot(ah, bh) + (_dot(ah, bl) + _dot(al, bh))


def _dot3_nt(a, b):
    ah, al = _split(a)
    bh, bl = _split(b)
    return _dot_nt(ah, bh) + (_dot_nt(ah, bl) + _dot_nt(al, bh))


def _sigmoid(x):
    return 1.0 / (1.0 + jnp.exp(-x))


def _silu(x):
    return x * _sigmoid(x)


def _rms(x, n):
    ss = jnp.sum(x * x, axis=-1, keepdims=True)
    return x * lax.rsqrt(ss * (1.0 / n) + EPS)


def _adaln_kernel(c_ref, w_ref, b_ref, o_ref):
    c = c_ref[...]
    o_ref[...] = _dot3(_silu(c), w_ref[...]) + b_ref[...]


def _adaln(c, w, b):
    bsz, d = c.shape
    n = w.shape[1]
    tn = 1024
    return pl.pallas_call(
        _adaln_kernel,
        out_shape=jax.ShapeDtypeStruct((bsz, n), F32),
        grid=(n // tn,),
        in_specs=[
            pl.BlockSpec((bsz, d), lambda j: (0, 0)),
            pl.BlockSpec((d, tn), lambda j: (0, j)),
            pl.BlockSpec((1, tn), lambda j: (0, j)),
        ],
        out_specs=pl.BlockSpec((bsz, tn), lambda j: (0, j)),
        compiler_params=pltpu.CompilerParams(dimension_semantics=("arbitrary",)),
        name="adaln",
    )(c, w, b.reshape(1, n))


def _inproj_kernel(x_ref, shift_ref, scale_ref, g1_ref, wqkv_ref, wlat_ref, wgate_ref,
                   gq_ref, gk_ref, gql_ref, gkvl_ref, wuq_ref, wuk_ref, wuv_ref,
                   gmq_ref, gmk_ref, pos_ref, freq_ref,
                   qa_ref, ka_ref, va_ref, qm_ref, km_ref, vm_ref, sgn_ref, sgm_ref):
    d = x_ref.shape[1]
    x = x_ref[...]
    h = _rms(x, d) * g1_ref[...]
    h = h * (1.0 + scale_ref[0]) + shift_ref[0]
    hb = _bf(h)

    qkv = _dot(hb, wqkv_ref[...])
    lat = _dot(hb, wlat_ref[...])
    gts = _dot(hb, wgate_ref[...])
    sgn_ref[...] = _bf(_sigmoid(gts[:, :d]))
    sgm_ref[...] = _bf(_sigmoid(gts[:, d:]))

    na_w = NA_HEADS * NA_HEAD_DIM
    lane = lax.broadcasted_iota(jnp.int32, (1, LANES), 1)
    lo_half = lane < NA_HEAD_DIM
    for p in range(na_w // LANES):
        sl = slice(p * LANES, (p + 1) * LANES)
        for src_off, g_ref, dst_ref in ((0, gq_ref, qa_ref), (na_w, gk_ref, ka_ref)):
            t = qkv[:, src_off + p * LANES: src_off + (p + 1) * LANES]
            sq = t * t
            s_lo = jnp.sum(jnp.where(lo_half, sq, 0.0), axis=-1, keepdims=True)
            s_hi = jnp.sum(jnp.where(lo_half, 0.0, sq), axis=-1, keepdims=True)
            r = jnp.where(lo_half,
                          lax.rsqrt(s_lo * (1.0 / NA_HEAD_DIM) + EPS),
                          lax.rsqrt(s_hi * (1.0 / NA_HEAD_DIM) + EPS))
            dst_ref[:, sl] = _bf(t * r * g_ref[:, sl])
    va_ref[...] = _bf(qkv[:, 2 * na_w: 3 * na_w])

    q_rank = gql_ref.shape[1]
    kv_rank = gkvl_ref.shape[1]
    qln = _rms(lat[:, :q_rank], q_rank) * gql_ref[...]
    kvn = _bf(_rms(lat[:, q_rank:q_rank + kv_rank], kv_rank) * gkvl_ref[...])
    qpre = _dot(_bf(qln), wuq_ref[...])
    knope = _dot(kvn, wuk_ref[...])
    vm_ref[...] = _bf(_dot(kvn, wuv_ref[...]))
    krot = lat[:, q_rank + kv_rank:]

    tm = x_ref.shape[0]
    half = MLA_ROPE_DIM // 2
    ang_t = freq_ref[...] * pos_ref[...].astype(F32)
    cos_t = jnp.cos(ang_t)
    sin_t = jnp.sin(ang_t)
    l_i = lax.broadcasted_iota(jnp.int32, (LANES, half), 0)
    j_i = lax.broadcasted_iota(jnp.int32, (LANES, half), 1)
    hit = jnp.where((l_i >= MLA_NOPE_DIM) & (l_i < MLA_QK_DIM)
                    & (((l_i - MLA_NOPE_DIM) & (half - 1)) == j_i), 1.0, 0.0)
    first_half = l_i < MLA_NOPE_DIM + half
    eye = jnp.where(lax.broadcasted_iota(jnp.int32, (tm, tm), 0)
                    == lax.broadcasted_iota(jnp.int32, (tm, tm), 1), 1.0, 0.0).astype(BF16)

    def table(sel, vals, fill_nope):
        hi, lo = _split(vals)
        w = _dot(_bf(sel), hi) + _dot(_bf(sel), lo)
        if fill_nope:
            w = jnp.where(lax.broadcasted_iota(jnp.int32, (LANES, tm), 0) < MLA_NOPE_DIM, 1.0, w)
        hi, lo = _split(w)
        return _dot_nt(eye, hi) + _dot_nt(eye, lo)

    c_tab = table(hit, cos_t, True)
    s_up = table(jnp.where(first_half, 0.0, hit), sin_t, False)
    s_dn = table(jnp.where(first_half, -hit, 0.0), sin_t, False)

    def rope(t):
        return t * c_tab + pltpu.roll(t, half, 1) * s_up + pltpu.roll(t, LANES - half, 1) * s_dn

    kr = rope(krot)
    for hd in range(MLA_HEADS):
        sl = slice(hd * LANES, (hd + 1) * LANES)
        qh = rope(qpre[:, sl])
        qm_ref[:, sl] = _bf(_rms(qh, MLA_QK_DIM) * gmq_ref[:, sl])
        kh = knope[:, sl] + kr
        km_ref[:, sl] = _bf(_rms(kh, MLA_QK_DIM) * gmk_ref[:, sl])


def _inproj(x2, mod3, g1, wqkv, wlat, wgate, gq, gk, gql, gkvl, wuq, wuk, wuv, gmq, gmk,
            pos, freq, seq):
    t, d = x2.shape
    tm = 256
    per_b = seq // tm
    na_w = NA_HEADS * NA_HEAD_DIM
    mla_w = MLA_HEADS * LANES
    v_w = MLA_HEADS * MLA_V_DIM

    def full(a):
        return pl.BlockSpec(a.shape, lambda i: (0,) * a.ndim)

    def rows(w):
        return pl.BlockSpec((tm, w), lambda i: (i, 0))

    out_shapes = (
        jax.ShapeDtypeStruct((t, na_w), BF16), jax.ShapeDtypeStruct((t, na_w), BF16),
        jax.ShapeDtypeStruct((t, na_w), BF16),
        jax.ShapeDtypeStruct((t, mla_w), BF16), jax.ShapeDtypeStruct((t, mla_w), BF16),
        jax.ShapeDtypeStruct((t, v_w), BF16),
        jax.ShapeDtypeStruct((t, d), BF16), jax.ShapeDtypeStruct((t, d), BF16),
    )
    return pl.pallas_call(
        _inproj_kernel,
        out_shape=out_shapes,
        grid=(t // tm,),
        in_specs=[
            rows(d),
            pl.BlockSpec((1, 1, d), lambda i: (i // per_b, 0, 0)),
            pl.BlockSpec((1, 1, d), lambda i: (i // per_b, 0, 1)),
            full(g1), full(wqkv), full(wlat), full(wgate), full(gq), full(gk), full(gql),
            full(gkvl), full(wuq), full(wuk), full(wuv), full(gmq), full(gmk),
            pl.BlockSpec((1, tm), lambda i: (0, i)),
            full(freq),
        ],
        out_specs=(rows(na_w), rows(na_w), rows(na_w), rows(mla_w), rows(mla_w), rows(v_w),
                   rows(d), rows(d)),
        compiler_params=pltpu.CompilerParams(dimension_semantics=("arbitrary",),
                                             vmem_limit_bytes=V7X_VMEM_LIMIT),
        name="inproj",
    )(x2, mod3, mod3, g1, wqkv, wlat, wgate, gq, gk, gql, gkvl, wuq, wuk, wuv, gmq, gmk,
      pos, freq)


def _na_block_geometry(block_type, n_rows):
    if block_type == 0:
        return 0, 0
    if block_type == 1:
        r0 = NA_QROWS
        return r0, r0 - NA_WIN_ROWS // 2
    return n_rows - NA_QROWS, n_rows - NA_BAND


def _na_bias_kernel(rpb_ref, o_ref, m_ref, *, n_rows):
    hd = pl.program_id(0)
    n_dr = 2 * NA_WIN_ROWS - 1
    n_dc = 2 * NA_WIN_COLS - 1
    qc = lax.broadcasted_iota(jnp.int32, (GRID_W, LANES), 0)
    kc = lax.broadcasted_iota(jnp.int32, (GRID_W, LANES), 1) & (GRID_W - 1)
    dc = jnp.clip(kc - qc, -(NA_WIN_COLS - 1), NA_WIN_COLS - 1) + (NA_WIN_COLS - 1)
    cstart = jnp.clip(qc - NA_WIN_COLS // 2, 0, GRID_W - NA_WIN_COLS)
    col_ok = (kc >= cstart) & (kc < cstart + NA_WIN_COLS)
    for i_dr in range(n_dr):
        acc = jnp.zeros((GRID_W, LANES), F32)
        for t in range(n_dc):
            acc = jnp.where(dc == t, rpb_ref[hd, i_dr * n_dc + t], acc)
        m_ref[i_dr] = jnp.where(col_ok, acc * LOG2E, NEG_BIG)
    neg = jnp.full((GRID_W, LANES), NEG_BIG, F32)
    lo_half = lax.broadcasted_iota(jnp.int32, (GRID_W, LANES), 1) < GRID_W
    kh = NA_WIN_ROWS
    for bt in range(NA_BLOCK_TYPES):
        r0, start = _na_block_geometry(bt, n_rows)
        for i in range(NA_QROWS):
            r = r0 + i
            rs = min(max(r - kh // 2, 0), n_rows - kh)
            for jp in range(NA_BAND // 2):
                halves = []
                for j in (2 * jp, 2 * jp + 1):
                    krow = start + j
                    if rs <= krow < rs + kh:
                        halves.append(m_ref[krow - r + (NA_WIN_ROWS - 1)])
                    else:
                        halves.append(neg)
                tile = jnp.where(lo_half, halves[0], halves[1])
                o_ref[bt, 0, i * GRID_W:(i + 1) * GRID_W, jp * LANES:(jp + 1) * LANES] = tile


def _na_bias(rpb, n_rows):
    heads = rpb.shape[0]
    nq = NA_QROWS * GRID_W
    nk = NA_BAND * GRID_W
    rpb2 = rpb.reshape(heads, -1)
    return pl.pallas_call(
        functools.partial(_na_bias_kernel, n_rows=n_rows),
        out_shape=jax.ShapeDtypeStruct((NA_BLOCK_TYPES, heads, nq, nk), F32),
        grid=(heads,),
        in_specs=[pl.BlockSpec(memory_space=pltpu.SMEM)],
        out_specs=pl.BlockSpec((NA_BLOCK_TYPES, 1, nq, nk), lambda hd: (0, hd, 0, 0)),
        scratch_shapes=[pltpu.VMEM((2 * NA_WIN_ROWS - 1, GRID_W, LANES), F32)],
        compiler_params=pltpu.CompilerParams(dimension_semantics=("arbitrary",)),
        name="na_bias",
    )(rpb2)


def _softmax_pv(s, v_pair, hh, half):
    lane = lax.broadcasted_iota(jnp.int32, (1, LANES), 1)
    mine = (lane < half) if hh == 0 else (lane >= half)
    den_lane = half if hh == 0 else 0
    m = jnp.max(s, axis=-1, keepdims=True)
    p = _bf(jnp.exp2(s - m))
    ones_row = jnp.where(lane == den_lane, 1.0, 0.0).astype(BF16)
    o = _dot(p, jnp.where(mine, v_pair, ones_row))
    den = jnp.sum(jnp.where(lane == den_lane, o, 0.0), axis=-1, keepdims=True)
    return jnp.where(mine, o / den, 0.0)


def _na_kernel(q_ref, k_ref, v_ref, bias_ref, o_ref, *, n_blocks, n_rows):
    blk = pl.program_id(1)
    start_row = jnp.where(blk == 0, 0,
                          jnp.where(blk == n_blocks - 1, n_rows - NA_BAND,
                                    blk * NA_QROWS - NA_WIN_ROWS // 2))
    off = pl.multiple_of(start_row * GRID_W, GRID_W)
    nk = NA_BAND * GRID_W
    tq = q_ref.shape[0]
    lo_half = lax.broadcasted_iota(jnp.int32, (1, LANES), 1) < NA_HEAD_DIM
    for p in range(NA_HEADS * NA_HEAD_DIM // LANES):
        sl = slice(p * LANES, (p + 1) * LANES)
        qp = q_ref[:, sl]
        kb = k_ref[pl.ds(off, nk), sl]
        vb = v_ref[pl.ds(off, nk), sl]
        zero = jnp.zeros_like(qp)
        q2 = jnp.concatenate([jnp.where(lo_half, qp, zero), jnp.where(lo_half, zero, qp)], axis=0)
        s = _dot_nt(q2, kb) + bias_ref[0, 2 * p:2 * p + 2].reshape(2 * tq, nk)
        m = jnp.max(s, axis=-1, keepdims=True)
        e = jnp.exp2(s - m)
        den = jnp.sum(e, axis=-1, keepdims=True)
        o = _dot(_bf(e), vb) / den
        o_ref[:, sl] = _bf(jnp.where(lo_half, o[:tq], o[tq:]))


def _na_attention(qa, ka, va, bias, bsz, seq):
    t, w = qa.shape
    n_rows = seq // GRID_W
    n_blocks = n_rows // NA_QROWS
    tq = NA_QROWS * GRID_W
    nk = NA_BAND * GRID_W
    heads = bias.shape[1]

    def btype(blk):
        return jnp.where(blk == 0, 0, jnp.where(blk == n_blocks - 1, 2, 1))

    return pl.pallas_call(
        functools.partial(_na_kernel, n_blocks=n_blocks, n_rows=n_rows),
        out_shape=jax.ShapeDtypeStruct((t, w), BF16),
        grid=(bsz, n_blocks),
        in_specs=[
            pl.BlockSpec((tq, w), lambda b, blk: (b * n_blocks + blk, 0)),
            pl.BlockSpec((seq, w), lambda b, blk: (b, 0)),
            pl.BlockSpec((seq, w), lambda b, blk: (b, 0)),
            pl.BlockSpec((1, heads, tq, nk), lambda b, blk: (btype(blk), 0, 0, 0)),
        ],
        out_specs=pl.BlockSpec((tq, w), lambda b, blk: (b * n_blocks + blk, 0)),
        compiler_params=pltpu.CompilerParams(dimension_semantics=("arbitrary", "arbitrary"),
                                             vmem_limit_bytes=V7X_VMEM_LIMIT),
        name="na_attn",
    )(qa, ka, va, bias)


def _mla_kernel(q_ref, k_ref, v_ref, o_ref):
    for pp in range(MLA_PAIRS_PER_STEP):
        v_pair = v_ref[:, pp * LANES:(pp + 1) * LANES]
        acc = jnp.zeros((q_ref.shape[0], LANES), F32)
        for hh in range(2):
            sl = slice((2 * pp + hh) * LANES, (2 * pp + hh + 1) * LANES)
            s = _dot_nt(q_ref[:, sl], k_ref[:, sl])
            acc = acc + _softmax_pv(s, v_pair, hh, MLA_V_DIM)
        o_ref[:, pp * LANES:(pp + 1) * LANES] = _bf(acc)


def _mla_attention(qm, km, vm, bsz, seq):
    t = qm.shape[0]
    tq = 512
    nq = seq // tq
    groups = MLA_HEADS // (2 * MLA_PAIRS_PER_STEP)
    qk_w = 2 * MLA_PAIRS_PER_STEP * LANES
    v_w = MLA_PAIRS_PER_STEP * LANES
    return pl.pallas_call(
        _mla_kernel,
        out_shape=jax.ShapeDtypeStruct((t, MLA_HEADS * MLA_V_DIM), BF16),
        grid=(bsz, groups, nq),
        in_specs=[
            pl.BlockSpec((tq, qk_w), lambda b, p, i: (b * nq + i, p)),
            pl.BlockSpec((seq, qk_w), lambda b, p, i: (b, p)),
            pl.BlockSpec((seq, v_w), lambda b, p, i: (b, p)),
        ],
        out_specs=pl.BlockSpec((tq, v_w), lambda b, p, i: (b * nq + i, p)),
        compiler_params=pltpu.CompilerParams(
            dimension_semantics=("arbitrary", "arbitrary", "arbitrary"),
            vmem_limit_bytes=V7X_VMEM_LIMIT),
        name="mla_attn",
    )(qm, km, vm)


def _merge_kernel(x_ref, yna_ref, ymla_ref, sgn_ref, sgm_ref, wpn_ref, wpm_ref, wout_ref,
                  gate1_ref, shift2_ref, scale2_ref, g2_ref, wrt_ref,
                  x1_ref, h2_ref, lt_ref):
    d = x_ref.shape[1]
    merged = (sgn_ref[...].astype(F32) * _dot(yna_ref[...], wpn_ref[...])
              + sgm_ref[...].astype(F32) * _dot(ymla_ref[...], wpm_ref[...]))
    x1 = x_ref[...] + gate1_ref[0] * _dot(_bf(merged), wout_ref[...])
    x1_ref[...] = x1
    h2 = _rms(x1, d) * g2_ref[...]
    h2 = h2 * (1.0 + scale2_ref[0]) + shift2_ref[0]
    h2_ref[...] = _bf(h2)
    lt_ref[...] = _dot3_nt(wrt_ref[...], h2)


def _merge(x2, yna, ymla, sgn, sgm, wpn, wpm, wout, mod3, g2, wrt, seq):
    t, d = x2.shape
    tm = 512
    per_b = seq // tm
    n_exp = wrt.shape[0]

    def full(a):
        return pl.BlockSpec(a.shape, lambda i: (0,) * a.ndim)

    def rows(w):
        return pl.BlockSpec((tm, w), lambda i: (i, 0))

    def modblk(j):
        return pl.BlockSpec((1, 1, d), lambda i: (i // per_b, 0, j))

    return pl.pallas_call(
        _merge_kernel,
        out_shape=(jax.ShapeDtypeStruct((t, d), F32), jax.ShapeDtypeStruct((t, d), BF16),
                   jax.ShapeDtypeStruct((n_exp, t), F32)),
        grid=(t // tm,),
        in_specs=[rows(d), rows(yna.shape[1]), rows(ymla.shape[1]), rows(d), rows(d),
                  full(wpn), full(wpm), full(wout),
                  modblk(2), modblk(3), modblk(4), full(g2), full(wrt)],
        out_specs=(rows(d), rows(d), pl.BlockSpec((n_exp, tm), lambda i: (0, i))),
        compiler_params=pltpu.CompilerParams(dimension_semantics=("arbitrary",),
                                             vmem_limit_bytes=V7X_VMEM_LIMIT),
        name="merge",
    )(x2, yna, ymla, sgn, sgm, wpn, wpm, wout, mod3, mod3, mod3, g2, wrt)


def _route_kernel(lt_ref, eb_ref, o_ref, q_ref):
    n_exp, tn = lt_ref.shape
    per_g = n_exp // N_GROUPS
    neg_inf = -jnp.inf
    sc = _sigmoid(lt_ref[...])
    sel = sc + eb_ref[...]
    sc3 = sc.reshape(N_GROUPS, per_g, tn)
    g3 = sel.reshape(N_GROUPS, per_g, tn)
    io = lax.broadcasted_iota(jnp.int32, (N_GROUPS, per_g, tn), 1)
    gio = lax.broadcasted_iota(jnp.int32, (N_GROUPS, per_g, tn), 0)
    eio = gio * per_g + io

    m1 = jnp.max(g3, axis=1, keepdims=True)
    i1 = jnp.min(jnp.where(g3 == m1, io, per_g), axis=1, keepdims=True)
    m2 = jnp.max(jnp.where(io == i1, neg_inf, g3), axis=1, keepdims=True)
    gs = m1 + m2

    g1io = lax.broadcasted_iota(jnp.int32, (N_GROUPS, 1, tn), 0)
    gsel = jnp.zeros((N_GROUPS, 1, tn), F32)
    cur = gs
    for _ in range(TOPK_GROUPS):
        m = jnp.max(cur, axis=0, keepdims=True)
        i = jnp.min(jnp.where(cur == m, g1io, N_GROUPS), axis=0, keepdims=True)
        pick = g1io == i
        gsel = jnp.where(pick, 1.0, gsel)
        cur = jnp.where(pick, neg_inf, cur)

    cur = jnp.where(gsel > 0.0, g3, neg_inf)
    chosen = jnp.zeros((N_GROUPS, per_g, tn), F32)
    for _ in range(TOP_K):
        m = jnp.max(jnp.max(cur, axis=1, keepdims=True), axis=0, keepdims=True)
        cand = jnp.where(cur == m, eio, n_exp)
        i = jnp.min(jnp.min(cand, axis=1, keepdims=True), axis=0, keepdims=True)
        pick = eio == i
        chosen = jnp.where(pick, 1.0, chosen)
        cur = jnp.where(pick, neg_inf, cur)

    w = jnp.where(chosen > 0.0, sc3, 0.0)
    tot = jnp.sum(jnp.sum(w, axis=1, keepdims=True), axis=0, keepdims=True)
    gates = (w / tot * ROUTED_SCALE).reshape(n_exp, tn)
    o_ref[...] = gates
    routed = jnp.where(gates > 0.0, 1.0, 0.0).astype(BF16)
    n_row = _dot_nt(jnp.ones((8, tn), BF16), routed)[0:1]
    q_ref[0] = jnp.floor((n_row + (MOE_CHUNK - 1)) * (1.0 / MOE_CHUNK)).astype(jnp.int32)


def _route(lt, e_bias):
    n_exp, t = lt.shape
    tn = MOE_TB
    return pl.pallas_call(
        _route_kernel,
        out_shape=(jax.ShapeDtypeStruct((n_exp, t), F32),
                   jax.ShapeDtypeStruct((t // tn, 1, n_exp), jnp.int32)),
        grid=(t // tn,),
        in_specs=[pl.BlockSpec((n_exp, tn), lambda i: (0, i)),
                  pl.BlockSpec((n_exp, 1), lambda i: (0, 0))],
        out_specs=(pl.BlockSpec((n_exp, tn), lambda i: (0, i)),
                   pl.BlockSpec((1, 1, n_exp), lambda i: (i, 0, 0))),
        compiler_params=pltpu.CompilerParams(dimension_semantics=("arbitrary",)),
        name="route",
    )(lt, e_bias.reshape(n_exp, 1))


def _for_each_chunk(n, fn):
    def quad(j, carry):
        for u in range(4):
            fn(j * 4 + u)
        return carry
    lax.fori_loop(0, n >> 2, quad, 0)
    base = (n >> 2) << 2
    for u in range(3):
        @pl.when(base + u < n)
        def _():
            fn(base + u)


def _dispatch_kernel(dtab_s, nch_s, total_s,
                     gt_ref, qrow_ref, qbrow_ref, h_ref, xs_ref, stage_ref, zero_ref, sem):
    b = pl.program_id(0)
    nb = pl.num_programs(0)
    slot = lax.rem(b, 2)
    n_exp, tb = gt_ref.shape
    rmax = stage_ref.shape[1] * MOE_CHUNK
    cpg = MOE_ROWGROUP // MOE_CHUNK

    routed = gt_ref[...] > 0.0
    before = (lax.broadcasted_iota(jnp.int32, (tb, tb), 0)
              < lax.broadcasted_iota(jnp.int32, (tb, tb), 1))
    pos = _dot(jnp.where(routed, 1.0, 0.0).astype(BF16), jnp.where(before, 1.0, 0.0).astype(BF16))
    posm = _bf(jnp.where(routed, pos, -1.0))
    qrow = qrow_ref[0]
    qbrow = qbrow_ref[0]
    qbrow_f = qbrow.astype(F32)
    h = h_ref[...]
    for g in range(rmax // MOE_ROWGROUP):
        @pl.when(g * MOE_ROWGROUP < nch_s[b] * MOE_CHUNK)
        def _():
            r0 = g * MOE_ROWGROUP
            chunk = (lax.broadcasted_iota(jnp.int32, (MOE_ROWGROUP, n_exp), 0) + r0) >> MOE_CHUNK_SHIFT
            own = jnp.where(chunk >= qbrow, jnp.where(chunk < qbrow + qrow, 1.0, 0.0), 0.0)
            rank = _dot(_bf(own), posm)
            start = jnp.sum(own * qbrow_f, axis=-1, keepdims=True) * MOE_CHUNK
            rel = (lax.broadcasted_iota(jnp.int32, (MOE_ROWGROUP, 1), 0) + r0).astype(F32) - start
            onehot = jnp.where(rank == rel, 1.0, 0.0).astype(BF16)
            rows = _bf(_dot(onehot, h))
            stage_ref[slot, g * cpg:(g + 1) * cpg] = rows.reshape(cpg, MOE_CHUNK, rows.shape[1])

    def start_chunk(i):
        pltpu.make_async_copy(stage_ref.at[slot, i], xs_ref.at[dtab_s[b, i]], sem.at[slot]).start()
    _for_each_chunk(nch_s[b], start_chunk)

    def wait_chunks(n, sl):
        def wait_chunk(i):
            pltpu.make_async_copy(zero_ref, xs_ref.at[0], sem.at[sl]).wait()
        _for_each_chunk(n, wait_chunk)

    @pl.when(b > 0)
    def _():
        wait_chunks(nch_s[b - 1], 1 - slot)

    @pl.when(b == nb - 1)
    def _():
        zero_ref[...] = jnp.zeros_like(zero_ref)
        n_tail = MOE_TM // MOE_CHUNK
        for c in range(n_tail):
            pltpu.make_async_copy(zero_ref, xs_ref.at[total_s[0] + c], sem.at[slot]).start()
        wait_chunks(nch_s[b] + n_tail, slot)


def _dispatch(gates_t, q, qbase, dtab, nch, total, h2, n_chunks):
    n_exp, t = gates_t.shape
    d = h2.shape[1]
    nb = t // MOE_TB
    rmax = _moe_stage_rows(n_exp)
    grid_spec = pltpu.PrefetchScalarGridSpec(
        num_scalar_prefetch=3,
        grid=(nb,),
        in_specs=[
            pl.BlockSpec((n_exp, MOE_TB), lambda b, *_: (0, b)),
            pl.BlockSpec((1, 1, n_exp), lambda b, *_: (b, 0, 0)),
            pl.BlockSpec((1, 1, n_exp), lambda b, *_: (b, 0, 0)),
            pl.BlockSpec((MOE_TB, d), lambda b, *_: (b, 0)),
        ],
        out_specs=pl.BlockSpec(memory_space=pl.ANY),
        scratch_shapes=[pltpu.VMEM((2, rmax // MOE_CHUNK, MOE_CHUNK, d), BF16),
                        pltpu.VMEM((MOE_CHUNK, d), BF16), pltpu.SemaphoreType.DMA((2,))],
    )
    return pl.pallas_call(
        _dispatch_kernel,
        out_shape=jax.ShapeDtypeStruct((n_chunks, MOE_CHUNK, d), BF16),
        grid_spec=grid_spec,
        compiler_params=pltpu.CompilerParams(dimension_semantics=("arbitrary",),
                                             vmem_limit_bytes=V7X_VMEM_LIMIT),
        name="moe_dispatch",
    )(dtab, nch, total, gates_t, q.reshape(nb, 1, n_exp), qbase.reshape(nb, 1, n_exp), h2)


def _tile_pieces():
    cpt = MOE_TM // MOE_CHUNK
    return [1 << s for s in range(cpt.bit_length() - 1, -1, -1)]


def _expert_kernel(off_s, len_s, next_s, first_s, xs_ref, wg_ref, wu_ref, wd_ref, ys_ref,
                   xbuf, ybuf, wg_b, wu_b, wd_b, state, sem_in, sem_out):
    e = pl.program_id(0)
    n_exp = pl.num_programs(0)
    cpt = MOE_TM // MOE_CHUNK
    d = xbuf.shape[3]
    pieces = _tile_pieces()

    def tile_in(ee, tt, sl):
        return pltpu.make_async_copy(xs_ref.at[pl.ds(off_s[ee] + tt * cpt, cpt)], xbuf.at[sl],
                                     sem_in.at[sl])

    def for_each_piece(valid, fn):
        for k, piece in enumerate(pieces):
            @pl.when((valid & piece) != 0)
            def _():
                fn(k, piece, valid & ~(2 * piece - 1))

    def tile_out(sl, dst_chunk, k, piece, start):
        return pltpu.make_async_copy(ybuf.at[sl, pl.ds(start, piece)],
                                     ys_ref.at[pl.ds(dst_chunk + start, piece)], sem_out.at[sl, k])

    def drain(sl):
        for_each_piece(state[1 + sl], lambda k, piece, start: tile_out(sl, 0, k, piece, start).wait())
        state[1 + sl] = 0

    @pl.when(e == 0)
    def _():
        state[0] = 0
        state[1] = 0
        state[2] = 0

        @pl.when(first_s[0] < n_exp)
        def _():
            tile_in(first_s[0], 0, 0).start()

    n_valid = len_s[e]
    n_tiles = (n_valid + cpt - 1) // cpt

    @pl.when(n_valid > 0)
    def _():
        wg_b[...] = _bf(wg_ref[0])
        wu_b[...] = _bf(wu_ref[0])
        wd_b[...] = _bf(wd_ref[0])

        def tile(t, carry):
            sl = state[0] & 1
            tile_in(e, t, sl).wait()
            more = t + 1 < n_tiles
            nxt_e = jnp.where(more, e, next_s[e])
            nxt_t = jnp.where(more, t + 1, 0)

            @pl.when(nxt_e < n_exp)
            def _():
                tile_in(nxt_e, nxt_t, 1 - sl).start()

            drain(sl)
            valid = jnp.minimum(n_valid - t * cpt, cpt)

            def ffn(n_c):
                x = xbuf[sl, :n_c].reshape(n_c * MOE_CHUNK, d)
                a = _silu(_dot(x, wg_b[...])) * _dot(x, wu_b[...])
                ybuf[sl, :n_c] = _bf(_dot(_bf(a), wd_b[...])).reshape(n_c, MOE_CHUNK, d)

            @pl.when(valid > cpt // 2)
            def _():
                ffn(cpt)

            @pl.when(valid <= cpt // 2)
            def _():
                ffn(cpt // 2)

            dst = off_s[e] + t * cpt
            for_each_piece(valid, lambda k, piece, start: tile_out(sl, dst, k, piece, start).start())
            state[1 + sl] = valid
            state[0] = state[0] + 1
            return carry
        lax.fori_loop(0, n_tiles, tile, 0)

    @pl.when(e == n_exp - 1)
    def _():
        drain(0)
        drain(1)


def _experts(off, per_exp, nxt, first, xs, wg, wu, wd):
    n_chunks, _, d = xs.shape
    n_exp, _, ff = wg.shape
    cpt = MOE_TM // MOE_CHUNK

    def w_blk(e, *_):
        return (e, 0, 0)

    grid_spec = pltpu.PrefetchScalarGridSpec(
        num_scalar_prefetch=4,
        grid=(n_exp,),
        in_specs=[pl.BlockSpec(memory_space=pl.ANY),
                  pl.BlockSpec((1, d, ff), w_blk), pl.BlockSpec((1, d, ff), w_blk),
                  pl.BlockSpec((1, ff, d), w_blk)],
        out_specs=pl.BlockSpec(memory_space=pl.ANY),
        scratch_shapes=[pltpu.VMEM((2, cpt, MOE_CHUNK, d), BF16),
                        pltpu.VMEM((2, cpt, MOE_CHUNK, d), BF16),
                        pltpu.VMEM((d, ff), BF16), pltpu.VMEM((d, ff), BF16),
                        pltpu.VMEM((ff, d), BF16),
                        pltpu.SMEM((3,), jnp.int32),
                        pltpu.SemaphoreType.DMA((2,)),
                        pltpu.SemaphoreType.DMA((2, len(_tile_pieces())))],
    )
    return pl.pallas_call(
        _expert_kernel,
        out_shape=jax.ShapeDtypeStruct((n_chunks, MOE_CHUNK, d), BF16),
        grid_spec=grid_spec,
        compiler_params=pltpu.CompilerParams(dimension_semantics=("arbitrary",),
                                             vmem_limit_bytes=V7X_VMEM_LIMIT),
        name="moe_experts",
    )(off, per_exp, nxt, first, xs, wg, wu, wd)


def _combine_kernel(dtab_s, nch_s,
                    gt_ref, qcol_ref, qbcol_ref, h_ref, x1_ref, gate2_ref,
                    wsg_ref, wsu_ref, wsd_ref, ys_ref, o_ref, stage_ref, sem):
    b = pl.program_id(0)
    nb = pl.num_programs(0)
    slot = lax.rem(b, 2)
    n_exp, tb = gt_ref.shape
    rmax = stage_ref.shape[1] * MOE_CHUNK

    def fetch(bb, sl):
        def start_chunk(i):
            pltpu.make_async_copy(ys_ref.at[dtab_s[bb, i]], stage_ref.at[sl, i], sem.at[sl]).start()
        _for_each_chunk(nch_s[bb], start_chunk)

    @pl.when(b == 0)
    def _():
        stage_ref[...] = jnp.zeros_like(stage_ref)
        fetch(0, 0)

    def wait_chunk(i):
        pltpu.make_async_copy(ys_ref.at[0], stage_ref.at[slot, 0], sem.at[slot]).wait()
    _for_each_chunk(nch_s[b], wait_chunk)

    @pl.when(b + 1 < nb)
    def _():
        fetch(b + 1, 1 - slot)

    gt = gt_ref[...]
    routed = jnp.where(gt > 0.0, 1.0, 0.0).astype(BF16)
    i0 = lax.broadcasted_iota(jnp.int32, (tb, tb), 0)
    i1 = lax.broadcasted_iota(jnp.int32, (tb, tb), 1)
    eye = jnp.where(i0 == i1, 1.0, 0.0).astype(BF16)
    routed_t = _dot_nt(eye, routed)
    gates_tok = _dot_nt(eye, _bf(gt))
    earlier = jnp.where(i1 < i0, 1.0, 0.0).astype(BF16)
    pos_t = _dot(earlier, _bf(routed_t))
    posm_t = _bf(jnp.where(routed_t > 0.0, pos_t, -1.0))

    qcol = qcol_ref[0]
    qbcol = qbcol_ref[0]
    chunk = lax.broadcasted_iota(jnp.int32, (n_exp, rmax), 1) >> MOE_CHUNK_SHIFT
    own = jnp.where(chunk >= qbcol, jnp.where(chunk < qbcol + qcol, 1.0, 0.0), 0.0)
    own_b = _bf(own)
    rank = _dot(posm_t, own_b)
    wexp = _dot(_bf(gates_tok), own_b)
    start = jnp.sum(own * qbcol.astype(F32), axis=0, keepdims=True) * MOE_CHUNK
    rel = lax.broadcasted_iota(jnp.int32, (1, rmax), 1).astype(F32) - start
    weights = _bf(jnp.where(rank == rel, wexp, 0.0))
    routed_out = _dot(weights, stage_ref[slot].reshape(rmax, o_ref.shape[1]))

    h = h_ref[...]
    a = _silu(_dot(h, wsg_ref[...])) * _dot(h, wsu_ref[...])
    shared = _dot(_bf(a), wsd_ref[...])
    o_ref[...] = x1_ref[...] + gate2_ref[0] * (routed_out + shared)


def _combine(gates_t, q, qbase, dtab, nch, h2, x1, mod3, wsg, wsu, wsd, ys, seq):
    n_exp, t = gates_t.shape
    d = h2.shape[1]
    nb = t // MOE_TB
    per_b = seq // MOE_TB
    stage_chunks = _moe_stage_rows(n_exp) // MOE_CHUNK

    def full(a):
        return pl.BlockSpec(a.shape, lambda b, *_: (0,) * a.ndim)

    def rows(w):
        return pl.BlockSpec((MOE_TB, w), lambda b, *_: (b, 0))

    grid_spec = pltpu.PrefetchScalarGridSpec(
        num_scalar_prefetch=2,
        grid=(nb,),
        in_specs=[
            pl.BlockSpec((n_exp, MOE_TB), lambda b, *_: (0, b)),
            pl.BlockSpec((1, n_exp, 1), lambda b, *_: (b, 0, 0)),
            pl.BlockSpec((1, n_exp, 1), lambda b, *_: (b, 0, 0)),
            rows(d), rows(d),
            pl.BlockSpec((1, 1, d), lambda b, *_: (b // per_b, 0, 5)),
            full(wsg), full(wsu), full(wsd),
            pl.BlockSpec(memory_space=pl.ANY),
        ],
        out_specs=rows(d),
        scratch_shapes=[pltpu.VMEM((2, stage_chunks, MOE_CHUNK, d), BF16),
                        pltpu.SemaphoreType.DMA((2,))],
    )
    return pl.pallas_call(
        _combine_kernel,
        out_shape=jax.ShapeDtypeStruct((t, d), F32),
        grid_spec=grid_spec,
        compiler_params=pltpu.CompilerParams(dimension_semantics=("arbitrary",),
                                             vmem_limit_bytes=V7X_VMEM_LIMIT),
        name="moe_combine",
    )(dtab, nch, gates_t, q.reshape(nb, n_exp, 1), qbase.reshape(nb, n_exp, 1),
      h2, x1, mod3, wsg, wsu, wsd, ys)


def _moe_plan(q):
    nb, n_exp = q.shape
    qbase = jnp.cumsum(q, axis=1) - q
    nch = jnp.sum(q, axis=1)
    per_exp = jnp.sum(q, axis=0)
    off = jnp.cumsum(per_exp) - per_exp
    dstq = off[None, :] + jnp.cumsum(q, axis=0) - q
    i = jnp.arange(_moe_block_chunks(n_exp), dtype=jnp.int32)
    ii = i[None, :, None]
    owned = (ii >= qbase[:, None, :]) & (ii < (qbase + q)[:, None, :])
    dtab = i[None, :] + jnp.sum(jnp.where(owned, (dstq - qbase)[:, None, :], 0), axis=2)
    ids = jnp.arange(n_exp, dtype=jnp.int32)
    later = (ids[None, :] > ids[:, None]) & (per_exp[None, :] > 0)
    nxt = jnp.min(jnp.where(later, ids[None, :], n_exp), axis=1)
    first = jnp.min(jnp.where(per_exp > 0, ids, n_exp)).reshape(1)
    total = jnp.sum(per_exp).reshape(1)
    return qbase, dtab, nch, off, per_exp, nxt, first, total


def _moe_block_chunks(n_exp):
    return MOE_TB * TOP_K // MOE_CHUNK + n_exp


def _moe_stage_rows(n_exp):
    return -(-_moe_block_chunks(n_exp) * MOE_CHUNK // MOE_ROWGROUP) * MOE_ROWGROUP


def _moe(h2, gates_t, q3, x1, mod3, wsg, wsu, wsd, wg, wu, wd, seq):
    n_exp, t = gates_t.shape
    nb = t // MOE_TB
    q = q3.reshape(nb, n_exp)
    qbase, dtab, nch, off, per_exp, nxt, first, total = _moe_plan(q)
    n_chunks = nb * _moe_block_chunks(n_exp) + MOE_TM // MOE_CHUNK
    xs = _dispatch(gates_t, q, qbase, dtab, nch, total, h2, n_chunks)
    ys = _experts(off, per_exp, nxt, first, xs, wg, wu, wd)
    return _combine(gates_t, q, qbase, dtab, nch, h2, x1, mod3, wsg, wsu, wsd, ys, seq)


def _pad_heads(w, heads, width):
    lead = w.shape[:-1]
    w = w.reshape(lead + (heads, width))
    w = jnp.pad(w, [(0, 0)] * len(lead) + [(0, 0), (0, LANES - width)])
    return w.reshape(lead + (heads * LANES,))


def kernel(x, c, positions, w_ada, b_ada, g_norm1, w_in, g_na_q, g_na_k, na_rpb, g_q_lat, w_uq,
           g_kv_lat, w_ukv, g_mla_q, g_mla_k, w_proj_na, w_proj_mla, w_out, g_norm2, w_router,
           e_bias, w_exp_gate, w_exp_up, w_exp_down, w_sh_gate, w_sh_up, w_sh_down):
    bsz, seq, d = x.shape
    t = bsz * seq
    depth = w_ada.shape[0]
    na_w = NA_HEADS * NA_HEAD_DIM
    q_rank = g_q_lat.shape[1]
    kv_rank = g_kv_lat.shape[1]
    n_rows = seq // GRID_W

    pos = positions.reshape(1, t)
    half = MLA_ROPE_DIM // 2
    freq = (ROPE_THETA ** (-jnp.arange(half, dtype=F32) / half)).reshape(half, 1)

    x2 = x.reshape(t, d)
    for l in range(depth):
        mod3 = _adaln(c, w_ada[l], b_ada[l]).reshape(bsz, 1, 6 * d)

        wi = w_in[l]
        o_lat = 3 * na_w
        o_rot = o_lat + q_rank + kv_rank
        o_gate = o_rot + MLA_ROPE_DIM
        wqkv = _bf(wi[:, :o_lat])
        w_rot = jnp.pad(wi[:, o_rot:o_gate], ((0, 0), (MLA_NOPE_DIM, LANES - MLA_QK_DIM)))
        wlat = _bf(jnp.concatenate([wi[:, o_lat:o_rot], w_rot], axis=1))
        wgate = _bf(wi[:, o_gate:])
        gq = (jnp.tile(g_na_q[l], NA_HEADS) * (NA_HEAD_DIM ** -0.5 * LOG2E)).reshape(1, na_w)
        gk = jnp.tile(g_na_k[l], NA_HEADS).reshape(1, na_w)
        wuq = _bf(_pad_heads(w_uq[l], MLA_HEADS, MLA_QK_DIM))
        wukv = w_ukv[l].reshape(kv_rank, MLA_HEADS, MLA_NOPE_DIM + MLA_V_DIM)
        wuk = _bf(_pad_heads(wukv[:, :, :MLA_NOPE_DIM].reshape(kv_rank, -1), MLA_HEADS, MLA_NOPE_DIM))
        wuv = _bf(wukv[:, :, MLA_NOPE_DIM:].reshape(kv_rank, MLA_HEADS * MLA_V_DIM))
        gmq = _pad_heads(jnp.tile(g_mla_q[l], MLA_HEADS) * (MLA_QK_DIM ** -0.5 * LOG2E),
                         MLA_HEADS, MLA_QK_DIM).reshape(1, -1)
        gmk = _pad_heads(jnp.tile(g_mla_k[l], MLA_HEADS), MLA_HEADS, MLA_QK_DIM).reshape(1, -1)

        qa, ka, va, qm, km, vm, sgn, sgm = _inproj(
            x2, mod3, g_norm1[l].reshape(1, d), wqkv, wlat, wgate, gq, gk,
            g_q_lat[l].reshape(1, q_rank), g_kv_lat[l].reshape(1, kv_rank), wuq, wuk, wuv,
            gmq, gmk, pos, freq, seq)

        bias = _na_bias(na_rpb[l], n_rows)
        y_na = _na_attention(qa, ka, va, bias, bsz, seq)
        y_mla = _mla_attention(qm, km, vm, bsz, seq)

        x1, h2, lt = _merge(x2, y_na, y_mla, sgn, sgm, _bf(w_proj_na[l]), _bf(w_proj_mla[l]),
                            _bf(w_out[l]), mod3, g_norm2[l].reshape(1, d), w_router[l].T, seq)
        gates_t, q3 = _route(lt, e_bias[l])
        x2 = _moe(h2, gates_t, q3, x1, mod3, _bf(w_sh_gate[l]), _bf(w_sh_up[l]),
                  _bf(w_sh_down[l]), w_exp_gate[l], w_exp_up[l], w_exp_down[l], seq)
    return x2.reshape(bsz, seq, d)
```

```python
import functools

import jax
import jax.numpy as jnp
import numpy as np
from jax import lax
from jax.experimental import pallas as pl
from jax.experimental.pallas import tpu as pltpu

GRID_W = 64
NA_HEADS = 8
NA_HEAD_DIM = 64
NA_WIN_ROWS = 8
NA_WIN_COLS = 16
MLA_HEADS = 8
MLA_NOPE_DIM = 64
MLA_ROPE_DIM = 32
MLA_V_DIM = 64
MLA_QK_DIM = MLA_NOPE_DIM + MLA_ROPE_DIM
ROPE_THETA = 10000.0
N_GROUPS = 8
TOPK_GROUPS = 4
TOP_K = 8
ROUTED_SCALE = 2.5
EPS = 1e-6
NEG_BIG = -1e30

LANES = 128
V7X_VMEM_LIMIT = 56 * 1024 * 1024

NA_QROWS = 4
NA_BAND = 12
NA_BLOCK_TYPES = 3
MLA_PAIRS_PER_STEP = 4
LOG2E = 1.4426950408889634
MOE_TB = 256
MOE_CHUNK_SHIFT = 4
MOE_CHUNK = 1 << MOE_CHUNK_SHIFT
MOE_TM = 1024
MOE_ROWGROUP = 512
MOE_XBUFS = 3

F32 = jnp.float32
BF16 = jnp.bfloat16


def _bf(x):
    return x.astype(BF16)


def _dot(a, b):
    return jnp.dot(a, b, preferred_element_type=F32)


def _dot_nt(a, b):
    return lax.dot_general(a, b, (((1,), (1,)), ((), ())), preferred_element_type=F32)


def _split(x):
    hi = _bf(x)
    lo = _bf(x - hi.astype(F32))
    return hi, lo


def _dot3(a, b):
    ah, al = _split(a)
    bh, bl = _split(b)
    return _dot(ah, bh) + (_dot(ah, bl) + _dot(al, bh))


def _dot3_nt(a, b):
    ah, al = _split(a)
    bh, bl = _split(b)
    return _dot_nt(ah, bh) + (_dot_nt(ah, bl) + _dot_nt(al, bh))


def _sigmoid(x):
    return 1.0 / (1.0 + jnp.exp(-x))


def _silu(x):
    return x * _sigmoid(x)


def _rms(x, n):
    ss = jnp.sum(x * x, axis=-1, keepdims=True)
    return x * lax.rsqrt(ss * (1.0 / n) + EPS)


def _adaln_kernel(c_ref, w_ref, b_ref, o_ref):
    c = c_ref[...]
    o_ref[...] = _dot3(_silu(c), w_ref[...]) + b_ref[...]


def _adaln(c, w, b):
    bsz, d = c.shape
    n = w.shape[1]
    tn = 1024
    return pl.pallas_call(
        _adaln_kernel,
        out_shape=jax.ShapeDtypeStruct((bsz, n), F32),
        grid=(n // tn,),
        in_specs=[
            pl.BlockSpec((bsz, d), lambda j: (0, 0)),
            pl.BlockSpec((d, tn), lambda j: (0, j)),
            pl.BlockSpec((1, tn), lambda j: (0, j)),
        ],
        out_specs=pl.BlockSpec((bsz, tn), lambda j: (0, j)),
        compiler_params=pltpu.CompilerParams(dimension_semantics=("arbitrary",)),
        name="adaln",
    )(c, w, b.reshape(1, n))


def _inproj_kernel(x_ref, shift_ref, scale_ref, g1_ref, wqkv_ref, wlat_ref, wgate_ref,
                   gq_ref, gk_ref, gql_ref, gkvl_ref, wuq_ref, wuk_ref, wuv_ref,
                   gmq_ref, gmk_ref, pos_ref, freq_ref,
                   qa_ref, ka_ref, va_ref, qm_ref, km_ref, vm_ref, sgn_ref, sgm_ref):
    d = x_ref.shape[1]
    x = x_ref[...]
    h = _rms(x, d) * g1_ref[...]
    h = h * (1.0 + scale_ref[0]) + shift_ref[0]
    hb = _bf(h)

    qkv = _dot(hb, wqkv_ref[...])
    lat = _dot(hb, wlat_ref[...])
    gts = _dot(hb, wgate_ref[...])
    sgn_ref[...] = _bf(_sigmoid(gts[:, :d]))
    sgm_ref[...] = _bf(_sigmoid(gts[:, d:]))

    na_w = NA_HEADS * NA_HEAD_DIM
    lane = lax.broadcasted_iota(jnp.int32, (1, LANES), 1)
    lo_half = lane < NA_HEAD_DIM
    for p in range(na_w // LANES):
        sl = slice(p * LANES, (p + 1) * LANES)
        for src_off, g_ref, dst_ref in ((0, gq_ref, qa_ref), (na_w, gk_ref, ka_ref)):
            t = qkv[:, src_off + p * LANES: src_off + (p + 1) * LANES]
            sq = t * t
            s_lo = jnp.sum(jnp.where(lo_half, sq, 0.0), axis=-1, keepdims=True)
            s_hi = jnp.sum(jnp.where(lo_half, 0.0, sq), axis=-1, keepdims=True)
            r = jnp.where(lo_half,
                          lax.rsqrt(s_lo * (1.0 / NA_HEAD_DIM) + EPS),
                          lax.rsqrt(s_hi * (1.0 / NA_HEAD_DIM) + EPS))
            dst_ref[:, sl] = _bf(t * r * g_ref[:, sl])
    va_ref[...] = _bf(qkv[:, 2 * na_w: 3 * na_w])

    q_rank = gql_ref.shape[1]
    kv_rank = gkvl_ref.shape[1]
    qln = _rms(lat[:, :q_rank], q_rank) * gql_ref[...]
    kvn = _bf(_rms(lat[:, q_rank:q_rank + kv_rank], kv_rank) * gkvl_ref[...])
    qpre = _dot(_bf(qln), wuq_ref[...])
    knope = _dot(kvn, wuk_ref[...])
    vm_ref[...] = _bf(_dot(kvn, wuv_ref[...]))
    krot = lat[:, q_rank + kv_rank:]

    tm = x_ref.shape[0]
    half = MLA_ROPE_DIM // 2
    ang_t = freq_ref[...] * pos_ref[...].astype(F32)
    cos_t = jnp.cos(ang_t)
    sin_t = jnp.sin(ang_t)
    l_i = lax.broadcasted_iota(jnp.int32, (LANES, half), 0)
    j_i = lax.broadcasted_iota(jnp.int32, (LANES, half), 1)
    hit = jnp.where((l_i >= MLA_NOPE_DIM) & (l_i < MLA_QK_DIM)
                    & (((l_i - MLA_NOPE_DIM) & (half - 1)) == j_i), 1.0, 0.0)
    first_half = l_i < MLA_NOPE_DIM + half
    eye = jnp.where(lax.broadcasted_iota(jnp.int32, (tm, tm), 0)
                    == lax.broadcasted_iota(jnp.int32, (tm, tm), 1), 1.0, 0.0).astype(BF16)

    def table(sel, vals, fill_nope):
        hi, lo = _split(vals)
        w = _dot(_bf(sel), hi) + _dot(_bf(sel), lo)
        if fill_nope:
            w = jnp.where(lax.broadcasted_iota(jnp.int32, (LANES, tm), 0) < MLA_NOPE_DIM, 1.0, w)
        hi, lo = _split(w)
        return _dot_nt(eye, hi) + _dot_nt(eye, lo)

    c_tab = table(hit, cos_t, True)
    s_up = table(jnp.where(first_half, 0.0, hit), sin_t, False)
    s_dn = table(jnp.where(first_half, -hit, 0.0), sin_t, False)

    def rope(t):
        return t * c_tab + pltpu.roll(t, half, 1) * s_up + pltpu.roll(t, LANES - half, 1) * s_dn

    kr = rope(krot)
    for hd in range(MLA_HEADS):
        sl = slice(hd * LANES, (hd + 1) * LANES)
        qh = rope(qpre[:, sl])
        qm_ref[:, sl] = _bf(_rms(qh, MLA_QK_DIM) * gmq_ref[:, sl])
        kh = knope[:, sl] + kr
        km_ref[:, sl] = _bf(_rms(kh, MLA_QK_DIM) * gmk_ref[:, sl])


def _inproj(x2, mod3, g1, wqkv, wlat, wgate, gq, gk, gql, gkvl, wuq, wuk, wuv, gmq, gmk,
            pos, freq, seq):
    t, d = x2.shape
    tm = 256
    per_b = seq // tm
    na_w = NA_HEADS * NA_HEAD_DIM
    mla_w = MLA_HEADS * LANES
    v_w = MLA_HEADS * MLA_V_DIM

    def full(a):
        return pl.BlockSpec(a.shape, lambda i: (0,) * a.ndim)

    def rows(w):
        return pl.BlockSpec((tm, w), lambda i: (i, 0))

    out_shapes = (
        jax.ShapeDtypeStruct((t, na_w), BF16), jax.ShapeDtypeStruct((t, na_w), BF16),
        jax.ShapeDtypeStruct((t, na_w), BF16),
        jax.ShapeDtypeStruct((t, mla_w), BF16), jax.ShapeDtypeStruct((t, mla_w), BF16),
        jax.ShapeDtypeStruct((t, v_w), BF16),
        jax.ShapeDtypeStruct((t, d), BF16), jax.ShapeDtypeStruct((t, d), BF16),
    )
    return pl.pallas_call(
        _inproj_kernel,
        out_shape=out_shapes,
        grid=(t // tm,),
        in_specs=[
            rows(d),
            pl.BlockSpec((1, 1, d), lambda i: (i // per_b, 0, 0)),
            pl.BlockSpec((1, 1, d), lambda i: (i // per_b, 0, 1)),
            full(g1), full(wqkv), full(wlat), full(wgate), full(gq), full(gk), full(gql),
            full(gkvl), full(wuq), full(wuk), full(wuv), full(gmq), full(gmk),
            pl.BlockSpec((1, tm), lambda i: (0, i)),
            full(freq),
        ],
        out_specs=(rows(na_w), rows(na_w), rows(na_w), rows(mla_w), rows(mla_w), rows(v_w),
                   rows(d), rows(d)),
        compiler_params=pltpu.CompilerParams(dimension_semantics=("arbitrary",),
                                             vmem_limit_bytes=V7X_VMEM_LIMIT),
        name="inproj",
    )(x2, mod3, mod3, g1, wqkv, wlat, wgate, gq, gk, gql, gkvl, wuq, wuk, wuv, gmq, gmk,
      pos, freq)


def _na_block_geometry(block_type, n_rows):
    if block_type == 0:
        return 0, 0
    if block_type == 1:
        r0 = NA_QROWS
        return r0, r0 - NA_WIN_ROWS // 2
    return n_rows - NA_QROWS, n_rows - NA_BAND


def _na_bias_kernel(rpb_ref, o_ref, m_ref, *, n_rows):
    hd = pl.program_id(0)
    n_dr = 2 * NA_WIN_ROWS - 1
    n_dc = 2 * NA_WIN_COLS - 1
    qc = lax.broadcasted_iota(jnp.int32, (GRID_W, LANES), 0)
    kc = lax.broadcasted_iota(jnp.int32, (GRID_W, LANES), 1) & (GRID_W - 1)
    dc = jnp.clip(kc - qc, -(NA_WIN_COLS - 1), NA_WIN_COLS - 1) + (NA_WIN_COLS - 1)
    cstart = jnp.clip(qc - NA_WIN_COLS // 2, 0, GRID_W - NA_WIN_COLS)
    col_ok = (kc >= cstart) & (kc < cstart + NA_WIN_COLS)
    for i_dr in range(n_dr):
        acc = jnp.zeros((GRID_W, LANES), F32)
        for t in range(n_dc):
            acc = jnp.where(dc == t, rpb_ref[hd, i_dr * n_dc + t], acc)
        m_ref[i_dr] = jnp.where(col_ok, acc * LOG2E, NEG_BIG)
    neg = jnp.full((GRID_W, LANES), NEG_BIG, F32)
    lo_half = lax.broadcasted_iota(jnp.int32, (GRID_W, LANES), 1) < GRID_W
    kh = NA_WIN_ROWS
    for bt in range(NA_BLOCK_TYPES):
        r0, start = _na_block_geometry(bt, n_rows)
        for i in range(NA_QROWS):
            r = r0 + i
            rs = min(max(r - kh // 2, 0), n_rows - kh)
            for jp in range(NA_BAND // 2):
                halves = []
                for j in (2 * jp, 2 * jp + 1):
                    krow = start + j
                    if rs <= krow < rs + kh:
                        halves.append(m_ref[krow - r + (NA_WIN_ROWS - 1)])
                    else:
                        halves.append(neg)
                tile = jnp.where(lo_half, halves[0], halves[1])
                o_ref[bt, 0, i * GRID_W:(i + 1) * GRID_W, jp * LANES:(jp + 1) * LANES] = tile


def _na_bias(rpb, n_rows):
    heads = rpb.shape[0]
    nq = NA_QROWS * GRID_W
    nk = NA_BAND * GRID_W
    rpb2 = rpb.reshape(heads, -1)
    return pl.pallas_call(
        functools.partial(_na_bias_kernel, n_rows=n_rows),
        out_shape=jax.ShapeDtypeStruct((NA_BLOCK_TYPES, heads, nq, nk), F32),
        grid=(heads,),
        in_specs=[pl.BlockSpec(memory_space=pltpu.SMEM)],
        out_specs=pl.BlockSpec((NA_BLOCK_TYPES, 1, nq, nk), lambda hd: (0, hd, 0, 0)),
        scratch_shapes=[pltpu.VMEM((2 * NA_WIN_ROWS - 1, GRID_W, LANES), F32)],
        compiler_params=pltpu.CompilerParams(dimension_semantics=("arbitrary",)),
        name="na_bias",
    )(rpb2)


def _softmax_pv(s, v_pair, hh, half):
    lane = lax.broadcasted_iota(jnp.int32, (1, LANES), 1)
    mine = (lane < half) if hh == 0 else (lane >= half)
    den_lane = half if hh == 0 else 0
    m = jnp.max(s, axis=-1, keepdims=True)
    p = _bf(jnp.exp2(s - m))
    ones_row = jnp.where(lane == den_lane, 1.0, 0.0).astype(BF16)
    o = _dot(p, jnp.where(mine, v_pair, ones_row))
    den = jnp.sum(jnp.where(lane == den_lane, o, 0.0), axis=-1, keepdims=True)
    return jnp.where(mine, o / den, 0.0)


def _na_kernel(q_ref, k_ref, v_ref, bias_ref, o_ref, *, n_blocks, n_rows):
    blk = pl.program_id(1)
    start_row = jnp.where(blk == 0, 0,
                          jnp.where(blk == n_blocks - 1, n_rows - NA_BAND,
                                    blk * NA_QROWS - NA_WIN_ROWS // 2))
    off = pl.multiple_of(start_row * GRID_W, GRID_W)
    nk = NA_BAND * GRID_W
    tq = q_ref.shape[0]
    lo_half = lax.broadcasted_iota(jnp.int32, (1, LANES), 1) < NA_HEAD_DIM
    for p in range(NA_HEADS * NA_HEAD_DIM // LANES):
        sl = slice(p * LANES, (p + 1) * LANES)
        qp = q_ref[:, sl]
        kb = k_ref[pl.ds(off, nk), sl]
        vb = v_ref[pl.ds(off, nk), sl]
        zero = jnp.zeros_like(qp)
        q2 = jnp.concatenate([jnp.where(lo_half, qp, zero), jnp.where(lo_half, zero, qp)], axis=0)
        s = _dot_nt(q2, kb) + bias_ref[0, 2 * p:2 * p + 2].reshape(2 * tq, nk)
        m = jnp.max(s, axis=-1, keepdims=True)
        e = jnp.exp2(s - m)
        den = jnp.sum(e, axis=-1, keepdims=True)
        o = _dot(_bf(e), vb) / den
        o_ref[:, sl] = _bf(jnp.where(lo_half, o[:tq], o[tq:]))


def _na_attention(qa, ka, va, bias, bsz, seq):
    t, w = qa.shape
    n_rows = seq // GRID_W
    n_blocks = n_rows // NA_QROWS
    tq = NA_QROWS * GRID_W
    nk = NA_BAND * GRID_W
    heads = bias.shape[1]

    def btype(blk):
        return jnp.where(blk == 0, 0, jnp.where(blk == n_blocks - 1, 2, 1))

    return pl.pallas_call(
        functools.partial(_na_kernel, n_blocks=n_blocks, n_rows=n_rows),
        out_shape=jax.ShapeDtypeStruct((t, w), BF16),
        grid=(bsz, n_blocks),
        in_specs=[
            pl.BlockSpec((tq, w), lambda b, blk: (b * n_blocks + blk, 0)),
            pl.BlockSpec((seq, w), lambda b, blk: (b, 0)),
            pl.BlockSpec((seq, w), lambda b, blk: (b, 0)),
            pl.BlockSpec((1, heads, tq, nk), lambda b, blk: (btype(blk), 0, 0, 0)),
        ],
        out_specs=pl.BlockSpec((tq, w), lambda b, blk: (b * n_blocks + blk, 0)),
        compiler_params=pltpu.CompilerParams(dimension_semantics=("arbitrary", "arbitrary"),
                                             vmem_limit_bytes=V7X_VMEM_LIMIT),
        name="na_attn",
    )(qa, ka, va, bias)


def _mla_kernel(q_ref, k_ref, v_ref, o_ref):
    for pp in range(MLA_PAIRS_PER_STEP):
        v_pair = v_ref[:, pp * LANES:(pp + 1) * LANES]
        acc = jnp.zeros((q_ref.shape[0], LANES), F32)
        for hh in range(2):
            sl = slice((2 * pp + hh) * LANES, (2 * pp + hh + 1) * LANES)
            s = _dot_nt(q_ref[:, sl], k_ref[:, sl])
            acc = acc + _softmax_pv(s, v_pair, hh, MLA_V_DIM)
        o_ref[:, pp * LANES:(pp + 1) * LANES] = _bf(acc)


def _mla_attention(qm, km, vm, bsz, seq):
    t = qm.shape[0]
    tq = 512
    nq = seq // tq
    groups = MLA_HEADS // (2 * MLA_PAIRS_PER_STEP)
    qk_w = 2 * MLA_PAIRS_PER_STEP * LANES
    v_w = MLA_PAIRS_PER_STEP * LANES
    return pl.pallas_call(
        _mla_kernel,
        out_shape=jax.ShapeDtypeStruct((t, MLA_HEADS * MLA_V_DIM), BF16),
        grid=(bsz, groups, nq),
        in_specs=[
            pl.BlockSpec((tq, qk_w), lambda b, p, i: (b * nq + i, p)),
            pl.BlockSpec((seq, qk_w), lambda b, p, i: (b, p)),
            pl.BlockSpec((seq, v_w), lambda b, p, i: (b, p)),
        ],
        out_specs=pl.BlockSpec((tq, v_w), lambda b, p, i: (b * nq + i, p)),
        compiler_params=pltpu.CompilerParams(
            dimension_semantics=("arbitrary", "arbitrary", "arbitrary"),
            vmem_limit_bytes=V7X_VMEM_LIMIT),
        name="mla_attn",
    )(qm, km, vm)


def _merge_kernel(x_ref, yna_ref, ymla_ref, sgn_ref, sgm_ref, wpn_ref, wpm_ref, wout_ref,
                  gate1_ref, shift2_ref, scale2_ref, g2_ref, wrt_ref,
                  x1_ref, h2_ref, lt_ref):
    d = x_ref.shape[1]
    merged = (sgn_ref[...].astype(F32) * _dot(yna_ref[...], wpn_ref[...])
              + sgm_ref[...].astype(F32) * _dot(ymla_ref[...], wpm_ref[...]))
    x1 = x_ref[...] + gate1_ref[0] * _dot(_bf(merged), wout_ref[...])
    x1_ref[...] = x1
    h2 = _rms(x1, d) * g2_ref[...]
    h2 = h2 * (1.0 + scale2_ref[0]) + shift2_ref[0]
    h2_ref[...] = _bf(h2)
    lt_ref[...] = _dot3_nt(wrt_ref[...], h2)


def _merge(x2, yna, ymla, sgn, sgm, wpn, wpm, wout, mod3, g2, wrt, seq):
    t, d = x2.shape
    tm = 512
    per_b = seq // tm
    n_exp = wrt.shape[0]

    def full(a):
        return pl.BlockSpec(a.shape, lambda i: (0,) * a.ndim)

    def rows(w):
        return pl.BlockSpec((tm, w), lambda i: (i, 0))

    def modblk(j):
        return pl.BlockSpec((1, 1, d), lambda i: (i // per_b, 0, j))

    return pl.pallas_call(
        _merge_kernel,
        out_shape=(jax.ShapeDtypeStruct((t, d), F32), jax.ShapeDtypeStruct((t, d), BF16),
                   jax.ShapeDtypeStruct((n_exp, t), F32)),
        grid=(t // tm,),
        in_specs=[rows(d), rows(yna.shape[1]), rows(ymla.shape[1]), rows(d), rows(d),
                  full(wpn), full(wpm), full(wout),
                  modblk(2), modblk(3), modblk(4), full(g2), full(wrt)],
        out_specs=(rows(d), rows(d), pl.BlockSpec((n_exp, tm), lambda i: (0, i))),
        compiler_params=pltpu.CompilerParams(dimension_semantics=("arbitrary",),
                                             vmem_limit_bytes=V7X_VMEM_LIMIT),
        name="merge",
    )(x2, yna, ymla, sgn, sgm, wpn, wpm, wout, mod3, mod3, mod3, g2, wrt)


def _route_kernel(lt_ref, eb_ref, o_ref, q_ref):
    n_exp, tn = lt_ref.shape
    per_g = n_exp // N_GROUPS
    neg_inf = -jnp.inf
    sc = _sigmoid(lt_ref[...])
    sel = sc + eb_ref[...]
    sc3 = sc.reshape(N_GROUPS, per_g, tn)
    g3 = sel.reshape(N_GROUPS, per_g, tn)
    io = lax.broadcasted_iota(jnp.int32, (N_GROUPS, per_g, tn), 1)
    gio = lax.broadcasted_iota(jnp.int32, (N_GROUPS, per_g, tn), 0)
    eio = gio * per_g + io

    m1 = jnp.max(g3, axis=1, keepdims=True)
    i1 = jnp.min(jnp.where(g3 == m1, io, per_g), axis=1, keepdims=True)
    m2 = jnp.max(jnp.where(io == i1, neg_inf, g3), axis=1, keepdims=True)
    gs = m1 + m2

    g1io = lax.broadcasted_iota(jnp.int32, (N_GROUPS, 1, tn), 0)
    gsel = jnp.zeros((N_GROUPS, 1, tn), F32)
    cur = gs
    for _ in range(TOPK_GROUPS):
        m = jnp.max(cur, axis=0, keepdims=True)
        i = jnp.min(jnp.where(cur == m, g1io, N_GROUPS), axis=0, keepdims=True)
        pick = g1io == i
        gsel = jnp.where(pick, 1.0, gsel)
        cur = jnp.where(pick, neg_inf, cur)

    cur = jnp.where(gsel > 0.0, g3, neg_inf)
    chosen = jnp.zeros((N_GROUPS, per_g, tn), F32)
    for _ in range(TOP_K):
        m = jnp.max(jnp.max(cur, axis=1, keepdims=True), axis=0, keepdims=True)
        cand = jnp.where(cur == m, eio, n_exp)
        i = jnp.min(jnp.min(cand, axis=1, keepdims=True), axis=0, keepdims=True)
        pick = eio == i
        chosen = jnp.where(pick, 1.0, chosen)
        cur = jnp.where(pick, neg_inf, cur)

    w = jnp.where(chosen > 0.0, sc3, 0.0)
    tot = jnp.sum(jnp.sum(w, axis=1, keepdims=True), axis=0, keepdims=True)
    gates = (w / tot * ROUTED_SCALE).reshape(n_exp, tn)
    o_ref[...] = gates
    routed = jnp.where(gates > 0.0, 1.0, 0.0).astype(BF16)
    n_row = _dot_nt(jnp.ones((8, tn), BF16), routed)[0:1]
    q_ref[0] = jnp.floor((n_row + (MOE_CHUNK - 1)) * (1.0 / MOE_CHUNK)).astype(jnp.int32)


def _route(lt, e_bias):
    n_exp, t = lt.shape
    tn = MOE_TB
    return pl.pallas_call(
        _route_kernel,
        out_shape=(jax.ShapeDtypeStruct((n_exp, t), F32),
                   jax.ShapeDtypeStruct((t // tn, 1, n_exp), jnp.int32)),
        grid=(t // tn,),
        in_specs=[pl.BlockSpec((n_exp, tn), lambda i: (0, i)),
                  pl.BlockSpec((n_exp, 1), lambda i: (0, 0))],
        out_specs=(pl.BlockSpec((n_exp, tn), lambda i: (0, i)),
                   pl.BlockSpec((1, 1, n_exp), lambda i: (i, 0, 0))),
        compiler_params=pltpu.CompilerParams(dimension_semantics=("arbitrary",)),
        name="route",
    )(lt, e_bias.reshape(n_exp, 1))


def _for_each_chunk(n, fn):
    def quad(j, carry):
        for u in range(4):
            fn(j * 4 + u)
        return carry
    lax.fori_loop(0, n >> 2, quad, 0)
    base = (n >> 2) << 2
    for u in range(3):
        @pl.when(base + u < n)
        def _():
            fn(base + u)


def _dispatch_kernel(dtab_s, nch_s, total_s,
                     gt_ref, qrow_ref, qbrow_ref, h_ref, xs_ref, stage_ref, zero_ref, sem):
    b = pl.program_id(0)
    nb = pl.num_programs(0)
    slot = lax.rem(b, 2)
    n_exp, tb = gt_ref.shape
    rmax = stage_ref.shape[1] * MOE_CHUNK
    cpg = MOE_ROWGROUP // MOE_CHUNK

    routed = gt_ref[...] > 0.0
    before = (lax.broadcasted_iota(jnp.int32, (tb, tb), 0)
              < lax.broadcasted_iota(jnp.int32, (tb, tb), 1))
    pos = _dot(jnp.where(routed, 1.0, 0.0).astype(BF16), jnp.where(before, 1.0, 0.0).astype(BF16))
    posm = _bf(jnp.where(routed, pos, -1.0))
    qrow = qrow_ref[0]
    qbrow = qbrow_ref[0]
    qbrow_f = qbrow.astype(F32)
    h = h_ref[...]
    def sort_rows(g):
        r0 = g * MOE_ROWGROUP
        chunk = (lax.broadcasted_iota(jnp.int32, (MOE_ROWGROUP, n_exp), 0) + r0) >> MOE_CHUNK_SHIFT
        own = jnp.where(chunk >= qbrow, jnp.where(chunk < qbrow + qrow, 1.0, 0.0), 0.0)
        rank = _dot(_bf(own), posm)
        start = jnp.sum(own * qbrow_f, axis=-1, keepdims=True) * MOE_CHUNK
        rel = (lax.broadcasted_iota(jnp.int32, (MOE_ROWGROUP, 1), 0) + r0).astype(F32) - start
        onehot = jnp.where(rank == rel, 1.0, 0.0).astype(BF16)
        rows = _bf(_dot(onehot, h))
        stage_ref[slot, g * cpg:(g + 1) * cpg] = rows.reshape(cpg, MOE_CHUNK, rows.shape[1])

    n_groups = rmax // MOE_ROWGROUP
    n_sure = min(n_groups, (MOE_TB * TOP_K + MOE_ROWGROUP - 1) // MOE_ROWGROUP)
    for g in range(n_sure):
        sort_rows(g)
    for g in range(n_sure, n_groups):
        @pl.when(g * MOE_ROWGROUP < nch_s[b] * MOE_CHUNK)
        def _():
            sort_rows(g)

    def start_chunk(i):
        pltpu.make_async_copy(stage_ref.at[slot, i], xs_ref.at[dtab_s[b, i]], sem.at[slot]).start()
    _for_each_chunk(nch_s[b], start_chunk)

    def wait_chunks(n, sl):
        def wait_chunk(i):
            pltpu.make_async_copy(zero_ref, xs_ref.at[0], sem.at[sl]).wait()
        _for_each_chunk(n, wait_chunk)

    @pl.when(b > 0)
    def _():
        wait_chunks(nch_s[b - 1], 1 - slot)

    @pl.when(b == nb - 1)
    def _():
        zero_ref[...] = jnp.zeros_like(zero_ref)
        n_tail = MOE_TM // MOE_CHUNK
        for c in range(n_tail):
            pltpu.make_async_copy(zero_ref, xs_ref.at[total_s[0] + c], sem.at[slot]).start()
        wait_chunks(nch_s[b] + n_tail, slot)


def _dispatch(gates_t, q, qbase, dtab, nch, total, h2, n_chunks):
    n_exp, t = gates_t.shape
    d = h2.shape[1]
    nb = t // MOE_TB
    rmax = _moe_stage_rows(n_exp)
    grid_spec = pltpu.PrefetchScalarGridSpec(
        num_scalar_prefetch=3,
        grid=(nb,),
        in_specs=[
            pl.BlockSpec((n_exp, MOE_TB), lambda b, *_: (0, b)),
            pl.BlockSpec((1, 1, n_exp), lambda b, *_: (b, 0, 0)),
            pl.BlockSpec((1, 1, n_exp), lambda b, *_: (b, 0, 0)),
            pl.BlockSpec((MOE_TB, d), lambda b, *_: (b, 0)),
        ],
        out_specs=pl.BlockSpec(memory_space=pl.ANY),
        scratch_shapes=[pltpu.VMEM((2, rmax // MOE_CHUNK, MOE_CHUNK, d), BF16),
                        pltpu.VMEM((MOE_CHUNK, d), BF16), pltpu.SemaphoreType.DMA((2,))],
    )
    return pl.pallas_call(
        _dispatch_kernel,
        out_shape=jax.ShapeDtypeStruct((n_chunks, MOE_CHUNK, d), BF16),
        grid_spec=grid_spec,
        compiler_params=pltpu.CompilerParams(dimension_semantics=("arbitrary",),
                                             vmem_limit_bytes=V7X_VMEM_LIMIT),
        name="moe_dispatch",
    )(dtab, nch, total, gates_t, q.reshape(nb, 1, n_exp), qbase.reshape(nb, 1, n_exp), h2)


def _tile_pieces():
    cpt = MOE_TM // MOE_CHUNK
    return [1 << s for s in range(cpt.bit_length() - 1, -1, -1)]


def _expert_kernel(off_s, len_s, next_s, first_s, xs_ref, wg_ref, wu_ref, wd_ref, ys_ref,
                   xbuf, ybuf, wg_b, wu_b, wd_b, state, sem_in, sem_out):
    e = pl.program_id(0)
    n_exp = pl.num_programs(0)
    cpt = MOE_TM // MOE_CHUNK
    d = xbuf.shape[3]
    pieces = _tile_pieces()

    def tile_in(ee, tt, sl):
        return pltpu.make_async_copy(xs_ref.at[pl.ds(off_s[ee] + tt * cpt, cpt)], xbuf.at[sl],
                                     sem_in.at[sl])

    def for_each_piece(valid, fn):
        for k, piece in enumerate(pieces):
            @pl.when((valid & piece) != 0)
            def _():
                fn(k, piece, valid & ~(2 * piece - 1))

    def tile_out(sl, dst_chunk, k, piece, start):
        return pltpu.make_async_copy(ybuf.at[sl, pl.ds(start, piece)],
                                     ys_ref.at[pl.ds(dst_chunk + start, piece)], sem_out.at[sl, k])

    def drain(sl):
        for_each_piece(state[1 + sl], lambda k, piece, start: tile_out(sl, 0, k, piece, start).wait())
        state[1 + sl] = 0

    def following(ee, tt):
        safe = jnp.minimum(ee, n_exp - 1)
        more = tt + 1 < (len_s[safe] + cpt - 1) // cpt
        nxt_e = jnp.where(ee >= n_exp, n_exp, jnp.where(more, ee, next_s[safe]))
        return nxt_e, jnp.where(more, tt + 1, 0)

    def prefetch(ee, tt, xsl):
        @pl.when(ee < n_exp)
        def _():
            tile_in(ee, tt, xsl).start()

    @pl.when(e == 0)
    def _():
        state[0] = 0
        state[1] = 0
        state[2] = 0
        ahead = (first_s[0], 0)
        for k in range(MOE_XBUFS - 1):
            prefetch(ahead[0], ahead[1], k)
            ahead = following(*ahead)

    n_valid = len_s[e]
    n_tiles = (n_valid + cpt - 1) // cpt

    @pl.when(n_valid > 0)
    def _():
        wg_b[...] = _bf(wg_ref[0])
        wu_b[...] = _bf(wu_ref[0])
        wd_b[...] = _bf(wd_ref[0])

        def tile(t, carry):
            g = state[0]
            xsl = lax.rem(g, MOE_XBUFS)
            sl = g & 1
            tile_in(e, t, xsl).wait()
            ahead = (e, t)
            for _ in range(MOE_XBUFS - 1):
                ahead = following(*ahead)
            prefetch(ahead[0], ahead[1], lax.rem(g + MOE_XBUFS - 1, MOE_XBUFS))

            drain(sl)
            valid = jnp.minimum(n_valid - t * cpt, cpt)

            def ffn(n_c):
                x = xbuf[xsl, :n_c].reshape(n_c * MOE_CHUNK, d)
                a = _silu(_dot(x, wg_b[...])) * _dot(x, wu_b[...])
                ybuf[sl, :n_c] = _bf(_dot(_bf(a), wd_b[...])).reshape(n_c, MOE_CHUNK, d)

            @pl.when(valid > cpt // 2)
            def _():
                ffn(cpt)

            @pl.when(valid <= cpt // 2)
            def _():
                ffn(cpt // 2)

            dst = off_s[e] + t * cpt
            for_each_piece(valid, lambda k, piece, start: tile_out(sl, dst, k, piece, start).start())
            state[1 + sl] = valid
            state[0] = state[0] + 1
            return carry
        lax.fori_loop(0, n_tiles, tile, 0)

    @pl.when(e == n_exp - 1)
    def _():
        drain(0)
        drain(1)


def _experts(off, per_exp, nxt, first, xs, wg, wu, wd):
    n_chunks, _, d = xs.shape
    n_exp, _, ff = wg.shape
    cpt = MOE_TM // MOE_CHUNK

    def w_blk(e, *_):
        return (e, 0, 0)

    grid_spec = pltpu.PrefetchScalarGridSpec(
        num_scalar_prefetch=4,
        grid=(n_exp,),
        in_specs=[pl.BlockSpec(memory_space=pl.ANY),
                  pl.BlockSpec((1, d, ff), w_blk), pl.BlockSpec((1, d, ff), w_blk),
                  pl.BlockSpec((1, ff, d), w_blk)],
        out_specs=pl.BlockSpec(memory_space=pl.ANY),
        scratch_shapes=[pltpu.VMEM((MOE_XBUFS, cpt, MOE_CHUNK, d), BF16),
                        pltpu.VMEM((2, cpt, MOE_CHUNK, d), BF16),
                        pltpu.VMEM((d, ff), BF16), pltpu.VMEM((d, ff), BF16),
                        pltpu.VMEM((ff, d), BF16),
                        pltpu.SMEM((3,), jnp.int32),
                        pltpu.SemaphoreType.DMA((MOE_XBUFS,)),
                        pltpu.SemaphoreType.DMA((2, len(_tile_pieces())))],
    )
    return pl.pallas_call(
        _expert_kernel,
        out_shape=jax.ShapeDtypeStruct((n_chunks, MOE_CHUNK, d), BF16),
        grid_spec=grid_spec,
        compiler_params=pltpu.CompilerParams(dimension_semantics=("arbitrary",),
                                             vmem_limit_bytes=V7X_VMEM_LIMIT),
        name="moe_experts",
    )(off, per_exp, nxt, first, xs, wg, wu, wd)


def _combine_kernel(dtab_s, nch_s,
                    gt_ref, qcol_ref, qbcol_ref, h_ref, x1_ref, gate2_ref,
                    wsg_ref, wsu_ref, wsd_ref, ys_ref, o_ref, stage_ref, sem):
    b = pl.program_id(0)
    nb = pl.num_programs(0)
    slot = lax.rem(b, 2)
    n_exp, tb = gt_ref.shape
    rmax = stage_ref.shape[1] * MOE_CHUNK

    def fetch(bb, sl):
        def start_chunk(i):
            pltpu.make_async_copy(ys_ref.at[dtab_s[bb, i]], stage_ref.at[sl, i], sem.at[sl]).start()
        _for_each_chunk(nch_s[bb], start_chunk)

    @pl.when(b == 0)
    def _():
        stage_ref[...] = jnp.zeros_like(stage_ref)
        fetch(0, 0)

    def wait_chunk(i):
        pltpu.make_async_copy(ys_ref.at[0], stage_ref.at[slot, 0], sem.at[slot]).wait()
    _for_each_chunk(nch_s[b], wait_chunk)

    @pl.when(b + 1 < nb)
    def _():
        fetch(b + 1, 1 - slot)

    gt = gt_ref[...]
    routed = jnp.where(gt > 0.0, 1.0, 0.0).astype(BF16)
    i0 = lax.broadcasted_iota(jnp.int32, (tb, tb), 0)
    i1 = lax.broadcasted_iota(jnp.int32, (tb, tb), 1)
    eye = jnp.where(i0 == i1, 1.0, 0.0).astype(BF16)
    routed_t = _dot_nt(eye, routed)
    gates_tok = _dot_nt(eye, _bf(gt))
    earlier = jnp.where(i1 < i0, 1.0, 0.0).astype(BF16)
    pos_t = _dot(earlier, _bf(routed_t))
    posm_t = _bf(jnp.where(routed_t > 0.0, pos_t, -1.0))

    qcol = qcol_ref[0]
    qbcol = qbcol_ref[0]
    chunk = lax.broadcasted_iota(jnp.int32, (n_exp, rmax), 1) >> MOE_CHUNK_SHIFT
    own = jnp.where(chunk >= qbcol, jnp.where(chunk < qbcol + qcol, 1.0, 0.0), 0.0)
    own_b = _bf(own)
    rank = _dot(posm_t, own_b)
    wexp = _dot(_bf(gates_tok), own_b)
    start = jnp.sum(own * qbcol.astype(F32), axis=0, keepdims=True) * MOE_CHUNK
    rel = lax.broadcasted_iota(jnp.int32, (1, rmax), 1).astype(F32) - start
    weights = _bf(jnp.where(rank == rel, wexp, 0.0))
    routed_out = _dot(weights, stage_ref[slot].reshape(rmax, o_ref.shape[1]))

    h = h_ref[...]
    a = _silu(_dot(h, wsg_ref[...])) * _dot(h, wsu_ref[...])
    shared = _dot(_bf(a), wsd_ref[...])
    o_ref[...] = x1_ref[...] + gate2_ref[0] * (routed_out + shared)


def _combine(gates_t, q, qbase, dtab, nch, h2, x1, mod3, wsg, wsu, wsd, ys, seq):
    n_exp, t = gates_t.shape
    d = h2.shape[1]
    nb = t // MOE_TB
    per_b = seq // MOE_TB
    stage_chunks = _moe_stage_rows(n_exp) // MOE_CHUNK

    def full(a):
        return pl.BlockSpec(a.shape, lambda b, *_: (0,) * a.ndim)

    def rows(w):
        return pl.BlockSpec((MOE_TB, w), lambda b, *_: (b, 0))

    grid_spec = pltpu.PrefetchScalarGridSpec(
        num_scalar_prefetch=2,
        grid=(nb,),
        in_specs=[
            pl.BlockSpec((n_exp, MOE_TB), lambda b, *_: (0, b)),
            pl.BlockSpec((1, n_exp, 1), lambda b, *_: (b, 0, 0)),
            pl.BlockSpec((1, n_exp, 1), lambda b, *_: (b, 0, 0)),
            rows(d), rows(d),
            pl.BlockSpec((1, 1, d), lambda b, *_: (b // per_b, 0, 5)),
            full(wsg), full(wsu), full(wsd),
            pl.BlockSpec(memory_space=pl.ANY),
        ],
        out_specs=rows(d),
        scratch_shapes=[pltpu.VMEM((2, stage_chunks, MOE_CHUNK, d), BF16),
                        pltpu.SemaphoreType.DMA((2,))],
    )
    return pl.pallas_call(
        _combine_kernel,
        out_shape=jax.ShapeDtypeStruct((t, d), F32),
        grid_spec=grid_spec,
        compiler_params=pltpu.CompilerParams(dimension_semantics=("arbitrary",),
                                             vmem_limit_bytes=V7X_VMEM_LIMIT),
        name="moe_combine",
    )(dtab, nch, gates_t, q.reshape(nb, n_exp, 1), qbase.reshape(nb, n_exp, 1),
      h2, x1, mod3, wsg, wsu, wsd, ys)


def _moe_plan(q):
    nb, n_exp = q.shape
    qbase = jnp.cumsum(q, axis=1) - q
    nch = jnp.sum(q, axis=1)
    per_exp = jnp.sum(q, axis=0)
    off = jnp.cumsum(per_exp) - per_exp
    dstq = off[None, :] + jnp.cumsum(q, axis=0) - q
    i = jnp.arange(_moe_block_chunks(n_exp), dtype=jnp.int32)
    ii = i[None, :, None]
    owned = (ii >= qbase[:, None, :]) & (ii < (qbase + q)[:, None, :])
    dtab = i[None, :] + jnp.sum(jnp.where(owned, (dstq - qbase)[:, None, :], 0), axis=2)
    ids = jnp.arange(n_exp, dtype=jnp.int32)
    later = (ids[None, :] > ids[:, None]) & (per_exp[None, :] > 0)
    nxt = jnp.min(jnp.where(later, ids[None, :], n_exp), axis=1)
    first = jnp.min(jnp.where(per_exp > 0, ids, n_exp)).reshape(1)
    total = jnp.sum(per_exp).reshape(1)
    return qbase, dtab, nch, off, per_exp, nxt, first, total


def _moe_block_chunks(n_exp):
    return MOE_TB * TOP_K // MOE_CHUNK + n_exp


def _moe_stage_rows(n_exp):
    return -(-_moe_block_chunks(n_exp) * MOE_CHUNK // MOE_ROWGROUP) * MOE_ROWGROUP


def _moe(h2, gates_t, q3, x1, mod3, wsg, wsu, wsd, wg, wu, wd, seq):
    n_exp, t = gates_t.shape
    nb = t // MOE_TB
    q = q3.reshape(nb, n_exp)
    qbase, dtab, nch, off, per_exp, nxt, first, total = _moe_plan(q)
    n_chunks = nb * _moe_block_chunks(n_exp) + MOE_TM // MOE_CHUNK
    xs = _dispatch(gates_t, q, qbase, dtab, nch, total, h2, n_chunks)
    ys = _experts(off, per_exp, nxt, first, xs, wg, wu, wd)
    return _combine(gates_t, q, qbase, dtab, nch, h2, x1, mod3, wsg, wsu, wsd, ys, seq)


def _pad_heads(w, heads, width):
    lead = w.shape[:-1]
    w = w.reshape(lead + (heads, width))
    w = jnp.pad(w, [(0, 0)] * len(lead) + [(0, 0), (0, LANES - width)])
    return w.reshape(lead + (heads * LANES,))


def kernel(x, c, positions, w_ada, b_ada, g_norm1, w_in, g_na_q, g_na_k, na_rpb, g_q_lat, w_uq,
           g_kv_lat, w_ukv, g_mla_q, g_mla_k, w_proj_na, w_proj_mla, w_out, g_norm2, w_router,
           e_bias, w_exp_gate, w_exp_up, w_exp_down, w_sh_gate, w_sh_up, w_sh_down):
    bsz, seq, d = x.shape
    t = bsz * seq
    depth = w_ada.shape[0]
    na_w = NA_HEADS * NA_HEAD_DIM
    q_rank = g_q_lat.shape[1]
    kv_rank = g_kv_lat.shape[1]
    n_rows = seq // GRID_W

    pos = positions.reshape(1, t)
    half = MLA_ROPE_DIM // 2
    freq = (ROPE_THETA ** (-jnp.arange(half, dtype=F32) / half)).reshape(half, 1)

    x2 = x.reshape(t, d)
    for l in range(depth):
        mod3 = _adaln(c, w_ada[l], b_ada[l]).reshape(bsz, 1, 6 * d)

        wi = w_in[l]
        o_lat = 3 * na_w
        o_rot = o_lat + q_rank + kv_rank
        o_gate = o_rot + MLA_ROPE_DIM
        wqkv = _bf(wi[:, :o_lat])
        w_rot = jnp.pad(wi[:, o_rot:o_gate], ((0, 0), (MLA_NOPE_DIM, LANES - MLA_QK_DIM)))
        wlat = _bf(jnp.concatenate([wi[:, o_lat:o_rot], w_rot], axis=1))
        wgate = _bf(wi[:, o_gate:])
        gq = (jnp.tile(g_na_q[l], NA_HEADS) * (NA_HEAD_DIM ** -0.5 * LOG2E)).reshape(1, na_w)
        gk = jnp.tile(g_na_k[l], NA_HEADS).reshape(1, na_w)
        wuq = _bf(_pad_heads(w_uq[l], MLA_HEADS, MLA_QK_DIM))
        wukv = w_ukv[l].reshape(kv_rank, MLA_HEADS, MLA_NOPE_DIM + MLA_V_DIM)
        wuk = _bf(_pad_heads(wukv[:, :, :MLA_NOPE_DIM].reshape(kv_rank, -1), MLA_HEADS, MLA_NOPE_DIM))
        wuv = _bf(wukv[:, :, MLA_NOPE_DIM:].reshape(kv_rank, MLA_HEADS * MLA_V_DIM))
        gmq = _pad_heads(jnp.tile(g_mla_q[l], MLA_HEADS) * (MLA_QK_DIM ** -0.5 * LOG2E),
                         MLA_HEADS, MLA_QK_DIM).reshape(1, -1)
        gmk = _pad_heads(jnp.tile(g_mla_k[l], MLA_HEADS), MLA_HEADS, MLA_QK_DIM).reshape(1, -1)

        qa, ka, va, qm, km, vm, sgn, sgm = _inproj(
            x2, mod3, g_norm1[l].reshape(1, d), wqkv, wlat, wgate, gq, gk,
            g_q_lat[l].reshape(1, q_rank), g_kv_lat[l].reshape(1, kv_rank), wuq, wuk, wuv,
            gmq, gmk, pos, freq, seq)

        bias = _na_bias(na_rpb[l], n_rows)
        y_na = _na_attention(qa, ka, va, bias, bsz, seq)
        y_mla = _mla_attention(qm, km, vm, bsz, seq)

        x1, h2, lt = _merge(x2, y_na, y_mla, sgn, sgm, _bf(w_proj_na[l]), _bf(w_proj_mla[l]),
                            _bf(w_out[l]), mod3, g_norm2[l].reshape(1, d), w_router[l].T, seq)
        gates_t, q3 = _route(lt, e_bias[l])
        x2 = _moe(h2, gates_t, q3, x1, mod3, _bf(w_sh_gate[l]), _bf(w_sh_up[l]),
                  _bf(w_sh_down[l]), w_exp_gate[l], w_exp_up[l], w_exp_down[l], seq)
    return x2.reshape(bsz, seq, d)
```

```python
import functools

import jax
import jax.numpy as jnp
import numpy as np
from jax import lax
from jax.experimental import pallas as pl
from jax.experimental.pallas import tpu as pltpu

GRID_W = 64
NA_HEADS = 8
NA_HEAD_DIM = 64
NA_WIN_ROWS = 8
NA_WIN_COLS = 16
MLA_HEADS = 8
MLA_NOPE_DIM = 64
MLA_ROPE_DIM = 32
MLA_V_DIM = 64
MLA_QK_DIM = MLA_NOPE_DIM + MLA_ROPE_DIM
ROPE_THETA = 10000.0
N_GROUPS = 8
TOPK_GROUPS = 4
TOP_K = 8
ROUTED_SCALE = 2.5
EPS = 1e-6
NEG_BIG = -1e30

LANES = 128
V7X_VMEM_LIMIT = 56 * 1024 * 1024

NA_QROWS = 4
NA_BAND = 12
NA_BLOCK_TYPES = 3
MERGE_SUB = 512
INPROJ_TM = 512
INPROJ_SUB = 256
MLA_PAIRS_PER_STEP = 4
LOG2E = 1.4426950408889634
MOE_TB = 256
MOE_CHUNK_SHIFT = 4
MOE_CHUNK = 1 << MOE_CHUNK_SHIFT
MOE_TM = 1024
MOE_ROWGROUP = 512
MOE_XBUFS = 4

F32 = jnp.float32
BF16 = jnp.bfloat16


def _bf(x):
    return x.astype(BF16)


def _dot(a, b):
    return jnp.dot(a, b, preferred_element_type=F32)


def _dot_nt(a, b):
    return lax.dot_general(a, b, (((1,), (1,)), ((), ())), preferred_element_type=F32)


def _split(x):
    hi = _bf(x)
    lo = _bf(x - hi.astype(F32))
    return hi, lo


def _dot3(a, b):
    ah, al = _split(a)
    bh, bl = _split(b)
    return _dot(ah, bh) + (_dot(ah, bl) + _dot(al, bh))


def _dot3_nt(a, b):
    ah, al = _split(a)
    bh, bl = _split(b)
    return _dot_nt(ah, bh) + (_dot_nt(ah, bl) + _dot_nt(al, bh))


def _sigmoid(x):
    return 1.0 / (1.0 + jnp.exp(-x))


def _silu(x):
    return x * _sigmoid(x)


def _rms(x, n):
    ss = jnp.sum(x * x, axis=-1, keepdims=True)
    return x * lax.rsqrt(ss * (1.0 / n) + EPS)


def _adaln_kernel(c_ref, w_ref, b_ref, o_ref):
    c = c_ref[...]
    o_ref[...] = _dot3(_silu(c), w_ref[...]) + b_ref[...]


def _adaln(c, w, b):
    bsz, d = c.shape
    n = w.shape[1]
    tn = 1024
    return pl.pallas_call(
        _adaln_kernel,
        out_shape=jax.ShapeDtypeStruct((bsz, n), F32),
        grid=(n // tn,),
        in_specs=[
            pl.BlockSpec((bsz, d), lambda j: (0, 0)),
            pl.BlockSpec((d, tn), lambda j: (0, j)),
            pl.BlockSpec((1, tn), lambda j: (0, j)),
        ],
        out_specs=pl.BlockSpec((bsz, tn), lambda j: (0, j)),
        compiler_params=pltpu.CompilerParams(dimension_semantics=("arbitrary",)),
        name="adaln",
    )(c, w, b.reshape(1, n))


def _inproj_kernel(x_ref, shift_ref, scale_ref, g1_ref, wqkv_ref, wlat_ref, wgate_ref,
                   gq_ref, gk_ref, gql_ref, gkvl_ref, wuq_ref, wuk_ref, wuv_ref,
                   gmq_ref, gmk_ref, pos_ref, freq_ref,
                   qa_ref, ka_ref, va_ref, qm_ref, km_ref, vm_ref, sgn_ref, sgm_ref):
    for r0 in range(0, x_ref.shape[0], INPROJ_SUB):
        _inproj_rows(pl.ds(r0, INPROJ_SUB), x_ref, shift_ref, scale_ref, g1_ref, wqkv_ref,
                     wlat_ref, wgate_ref, gq_ref, gk_ref, gql_ref, gkvl_ref, wuq_ref, wuk_ref,
                     wuv_ref, gmq_ref, gmk_ref, pos_ref, freq_ref, qa_ref, ka_ref, va_ref,
                     qm_ref, km_ref, vm_ref, sgn_ref, sgm_ref)


def _inproj_rows(rs, x_ref, shift_ref, scale_ref, g1_ref, wqkv_ref, wlat_ref, wgate_ref,
                 gq_ref, gk_ref, gql_ref, gkvl_ref, wuq_ref, wuk_ref, wuv_ref,
                 gmq_ref, gmk_ref, pos_ref, freq_ref,
                 qa_ref, ka_ref, va_ref, qm_ref, km_ref, vm_ref, sgn_ref, sgm_ref):
    d = x_ref.shape[1]
    x = x_ref[rs, :]
    h = _rms(x, d) * g1_ref[...]
    h = h * (1.0 + scale_ref[0]) + shift_ref[0]
    hb = _bf(h)

    qkv = _dot(hb, wqkv_ref[...])
    lat = _dot(hb, wlat_ref[...])
    gts = _dot(hb, wgate_ref[...])
    sgn_ref[rs, :] = _bf(_sigmoid(gts[:, :d]))
    sgm_ref[rs, :] = _bf(_sigmoid(gts[:, d:]))

    na_w = NA_HEADS * NA_HEAD_DIM
    lane = lax.broadcasted_iota(jnp.int32, (1, LANES), 1)
    lo_half = lane < NA_HEAD_DIM
    for p in range(na_w // LANES):
        sl = slice(p * LANES, (p + 1) * LANES)
        for src_off, g_ref, dst_ref in ((0, gq_ref, qa_ref), (na_w, gk_ref, ka_ref)):
            t = qkv[:, src_off + p * LANES: src_off + (p + 1) * LANES]
            sq = t * t
            s_lo = jnp.sum(jnp.where(lo_half, sq, 0.0), axis=-1, keepdims=True)
            s_hi = jnp.sum(jnp.where(lo_half, 0.0, sq), axis=-1, keepdims=True)
            r = jnp.where(lo_half,
                          lax.rsqrt(s_lo * (1.0 / NA_HEAD_DIM) + EPS),
                          lax.rsqrt(s_hi * (1.0 / NA_HEAD_DIM) + EPS))
            dst_ref[rs, sl] = _bf(t * r * g_ref[:, sl])
    va_ref[rs, :] = _bf(qkv[:, 2 * na_w: 3 * na_w])

    q_rank = gql_ref.shape[1]
    kv_rank = gkvl_ref.shape[1]
    qln = _rms(lat[:, :q_rank], q_rank) * gql_ref[...]
    kvn = _bf(_rms(lat[:, q_rank:q_rank + kv_rank], kv_rank) * gkvl_ref[...])
    qpre = _dot(_bf(qln), wuq_ref[...])
    knope = _dot(kvn, wuk_ref[...])
    vm_ref[rs, :] = _bf(_dot(kvn, wuv_ref[...]))
    krot = lat[:, q_rank + kv_rank:]

    tm = INPROJ_SUB
    half = MLA_ROPE_DIM // 2
    ang_t = freq_ref[...] * pos_ref[:, rs].astype(F32)
    cos_t = jnp.cos(ang_t)
    sin_t = jnp.sin(ang_t)
    l_i = lax.broadcasted_iota(jnp.int32, (LANES, half), 0)
    j_i = lax.broadcasted_iota(jnp.int32, (LANES, half), 1)
    hit = jnp.where((l_i >= MLA_NOPE_DIM) & (l_i < MLA_QK_DIM)
                    & (((l_i - MLA_NOPE_DIM) & (half - 1)) == j_i), 1.0, 0.0)
    first_half = l_i < MLA_NOPE_DIM + half
    eye = jnp.where(lax.broadcasted_iota(jnp.int32, (tm, tm), 0)
                    == lax.broadcasted_iota(jnp.int32, (tm, tm), 1), 1.0, 0.0).astype(BF16)

    def table(sel, vals, fill_nope):
        hi, lo = _split(vals)
        w = _dot(_bf(sel), hi) + _dot(_bf(sel), lo)
        if fill_nope:
            w = jnp.where(lax.broadcasted_iota(jnp.int32, (LANES, tm), 0) < MLA_NOPE_DIM, 1.0, w)
        hi, lo = _split(w)
        return _dot_nt(eye, hi) + _dot_nt(eye, lo)

    c_tab = table(hit, cos_t, True)
    s_up = table(jnp.where(first_half, 0.0, hit), sin_t, False)
    s_dn = table(jnp.where(first_half, -hit, 0.0), sin_t, False)

    def rope(t):
        return t * c_tab + pltpu.roll(t, half, 1) * s_up + pltpu.roll(t, LANES - half, 1) * s_dn

    kr = rope(krot)
    for hd in range(MLA_HEADS):
        sl = slice(hd * LANES, (hd + 1) * LANES)
        qh = rope(qpre[:, sl])
        qm_ref[rs, sl] = _bf(_rms(qh, MLA_QK_DIM) * gmq_ref[:, sl])
        kh = knope[:, sl] + kr
        km_ref[rs, sl] = _bf(_rms(kh, MLA_QK_DIM) * gmk_ref[:, sl])


def _inproj(x2, mod3, g1, wqkv, wlat, wgate, gq, gk, gql, gkvl, wuq, wuk, wuv, gmq, gmk,
            pos, freq, seq):
    t, d = x2.shape
    tm = INPROJ_TM
    per_b = seq // tm
    na_w = NA_HEADS * NA_HEAD_DIM
    mla_w = MLA_HEADS * LANES
    v_w = MLA_HEADS * MLA_V_DIM

    def full(a):
        return pl.BlockSpec(a.shape, lambda i: (0,) * a.ndim)

    def rows(w):
        return pl.BlockSpec((tm, w), lambda i: (i, 0))

    out_shapes = (
        jax.ShapeDtypeStruct((t, na_w), BF16), jax.ShapeDtypeStruct((t, na_w), BF16),
        jax.ShapeDtypeStruct((t, na_w), BF16),
        jax.ShapeDtypeStruct((t, mla_w), BF16), jax.ShapeDtypeStruct((t, mla_w), BF16),
        jax.ShapeDtypeStruct((t, v_w), BF16),
        jax.ShapeDtypeStruct((t, d), BF16), jax.ShapeDtypeStruct((t, d), BF16),
    )
    return pl.pallas_call(
        _inproj_kernel,
        out_shape=out_shapes,
        grid=(t // tm,),
        in_specs=[
            rows(d),
            pl.BlockSpec((1, 1, d), lambda i: (i // per_b, 0, 0)),
            pl.BlockSpec((1, 1, d), lambda i: (i // per_b, 0, 1)),
            full(g1), full(wqkv), full(wlat), full(wgate), full(gq), full(gk), full(gql),
            full(gkvl), full(wuq), full(wuk), full(wuv), full(gmq), full(gmk),
            pl.BlockSpec((1, tm), lambda i: (0, i)),
            full(freq),
        ],
        out_specs=(rows(na_w), rows(na_w), rows(na_w), rows(mla_w), rows(mla_w), rows(v_w),
                   rows(d), rows(d)),
        compiler_params=pltpu.CompilerParams(dimension_semantics=("arbitrary",),
                                             vmem_limit_bytes=V7X_VMEM_LIMIT),
        name="inproj",
    )(x2, mod3, mod3, g1, wqkv, wlat, wgate, gq, gk, gql, gkvl, wuq, wuk, wuv, gmq, gmk,
      pos, freq)


def _na_block_geometry(block_type, n_rows):
    if block_type == 0:
        return 0, 0
    if block_type == 1:
        r0 = NA_QROWS
        return r0, r0 - NA_WIN_ROWS // 2
    return n_rows - NA_QROWS, n_rows - NA_BAND


def _na_bias_kernel(rpb_ref, o_ref, m_ref, *, n_rows):
    hd = pl.program_id(0)
    n_dr = 2 * NA_WIN_ROWS - 1
    n_dc = 2 * NA_WIN_COLS - 1
    qc = lax.broadcasted_iota(jnp.int32, (GRID_W, LANES), 0)
    kc = lax.broadcasted_iota(jnp.int32, (GRID_W, LANES), 1) & (GRID_W - 1)
    dc = jnp.clip(kc - qc, -(NA_WIN_COLS - 1), NA_WIN_COLS - 1) + (NA_WIN_COLS - 1)
    cstart = jnp.clip(qc - NA_WIN_COLS // 2, 0, GRID_W - NA_WIN_COLS)
    col_ok = (kc >= cstart) & (kc < cstart + NA_WIN_COLS)
    for i_dr in range(n_dr):
        acc = jnp.zeros((GRID_W, LANES), F32)
        for t in range(n_dc):
            acc = jnp.where(dc == t, rpb_ref[hd, i_dr * n_dc + t], acc)
        m_ref[i_dr] = jnp.where(col_ok, acc * LOG2E, NEG_BIG)
    neg = jnp.full((GRID_W, LANES), NEG_BIG, F32)
    lo_half = lax.broadcasted_iota(jnp.int32, (GRID_W, LANES), 1) < GRID_W
    kh = NA_WIN_ROWS
    for bt in range(NA_BLOCK_TYPES):
        r0, start = _na_block_geometry(bt, n_rows)
        for i in range(NA_QROWS):
            r = r0 + i
            rs = min(max(r - kh // 2, 0), n_rows - kh)
            for jp in range(NA_BAND // 2):
                halves = []
                for j in (2 * jp, 2 * jp + 1):
                    krow = start + j
                    if rs <= krow < rs + kh:
                        halves.append(m_ref[krow - r + (NA_WIN_ROWS - 1)])
                    else:
                        halves.append(neg)
                tile = jnp.where(lo_half, halves[0], halves[1])
                o_ref[bt, 0, i * GRID_W:(i + 1) * GRID_W, jp * LANES:(jp + 1) * LANES] = tile


def _na_bias(rpb, n_rows):
    heads = rpb.shape[0]
    nq = NA_QROWS * GRID_W
    nk = NA_BAND * GRID_W
    rpb2 = rpb.reshape(heads, -1)
    return pl.pallas_call(
        functools.partial(_na_bias_kernel, n_rows=n_rows),
        out_shape=jax.ShapeDtypeStruct((NA_BLOCK_TYPES, heads, nq, nk), F32),
        grid=(heads,),
        in_specs=[pl.BlockSpec(memory_space=pltpu.SMEM)],
        out_specs=pl.BlockSpec((NA_BLOCK_TYPES, 1, nq, nk), lambda hd: (0, hd, 0, 0)),
        scratch_shapes=[pltpu.VMEM((2 * NA_WIN_ROWS - 1, GRID_W, LANES), F32)],
        compiler_params=pltpu.CompilerParams(dimension_semantics=("arbitrary",)),
        name="na_bias",
    )(rpb2)


def _softmax_pv(s, v_pair, hh, half):
    lane = lax.broadcasted_iota(jnp.int32, (1, LANES), 1)
    mine = (lane < half) if hh == 0 else (lane >= half)
    den_lane = half if hh == 0 else 0
    m = jnp.max(s, axis=-1, keepdims=True)
    p = _bf(jnp.exp2(s - m))
    ones_row = jnp.where(lane == den_lane, 1.0, 0.0).astype(BF16)
    o = _dot(p, jnp.where(mine, v_pair, ones_row))
    den = jnp.sum(jnp.where(lane == den_lane, o, 0.0), axis=-1, keepdims=True)
    return jnp.where(mine, o / den, 0.0)


def _na_kernel(q_ref, k_ref, v_ref, bias_ref, o_ref, *, n_blocks, n_rows):
    blk = pl.program_id(1)
    start_row = jnp.where(blk == 0, 0,
                          jnp.where(blk == n_blocks - 1, n_rows - NA_BAND,
                                    blk * NA_QROWS - NA_WIN_ROWS // 2))
    off = pl.multiple_of(start_row * GRID_W, GRID_W)
    nk = NA_BAND * GRID_W
    tq = q_ref.shape[0]
    lo_half = lax.broadcasted_iota(jnp.int32, (1, LANES), 1) < NA_HEAD_DIM
    for p in range(NA_HEADS * NA_HEAD_DIM // LANES):
        sl = slice(p * LANES, (p + 1) * LANES)
        qp = q_ref[:, sl]
        kb = k_ref[pl.ds(off, nk), sl]
        vb = v_ref[pl.ds(off, nk), sl]
        zero = jnp.zeros_like(qp)
        q2 = jnp.concatenate([jnp.where(lo_half, qp, zero), jnp.where(lo_half, zero, qp)], axis=0)
        s = _dot_nt(q2, kb) + bias_ref[0, 2 * p:2 * p + 2].reshape(2 * tq, nk)
        m = jnp.max(s, axis=-1, keepdims=True)
        e = jnp.exp2(s - m)
        den = jnp.sum(e, axis=-1, keepdims=True)
        o = _dot(_bf(e), vb) / den
        o_ref[:, sl] = _bf(jnp.where(lo_half, o[:tq], o[tq:]))


def _na_attention(qa, ka, va, bias, bsz, seq):
    t, w = qa.shape
    n_rows = seq // GRID_W
    n_blocks = n_rows // NA_QROWS
    tq = NA_QROWS * GRID_W
    nk = NA_BAND * GRID_W
    heads = bias.shape[1]

    def btype(blk):
        return jnp.where(blk == 0, 0, jnp.where(blk == n_blocks - 1, 2, 1))

    return pl.pallas_call(
        functools.partial(_na_kernel, n_blocks=n_blocks, n_rows=n_rows),
        out_shape=jax.ShapeDtypeStruct((t, w), BF16),
        grid=(bsz, n_blocks),
        in_specs=[
            pl.BlockSpec((tq, w), lambda b, blk: (b * n_blocks + blk, 0)),
            pl.BlockSpec((seq, w), lambda b, blk: (b, 0)),
            pl.BlockSpec((seq, w), lambda b, blk: (b, 0)),
            pl.BlockSpec((1, heads, tq, nk), lambda b, blk: (btype(blk), 0, 0, 0)),
        ],
        out_specs=pl.BlockSpec((tq, w), lambda b, blk: (b * n_blocks + blk, 0)),
        compiler_params=pltpu.CompilerParams(dimension_semantics=("arbitrary", "arbitrary"),
                                             vmem_limit_bytes=V7X_VMEM_LIMIT),
        name="na_attn",
    )(qa, ka, va, bias)


def _mla_kernel(q_ref, k_ref, v_ref, o_ref):
    for pp in range(MLA_PAIRS_PER_STEP):
        v_pair = v_ref[:, pp * LANES:(pp + 1) * LANES]
        acc = jnp.zeros((q_ref.shape[0], LANES), F32)
        for hh in range(2):
            sl = slice((2 * pp + hh) * LANES, (2 * pp + hh + 1) * LANES)
            s = _dot_nt(q_ref[:, sl], k_ref[:, sl])
            acc = acc + _softmax_pv(s, v_pair, hh, MLA_V_DIM)
        o_ref[:, pp * LANES:(pp + 1) * LANES] = _bf(acc)


def _mla_attention(qm, km, vm, bsz, seq):
    t = qm.shape[0]
    tq = 1024
    nq = seq // tq
    groups = MLA_HEADS // (2 * MLA_PAIRS_PER_STEP)
    qk_w = 2 * MLA_PAIRS_PER_STEP * LANES
    v_w = MLA_PAIRS_PER_STEP * LANES
    return pl.pallas_call(
        _mla_kernel,
        out_shape=jax.ShapeDtypeStruct((t, MLA_HEADS * MLA_V_DIM), BF16),
        grid=(bsz, groups, nq),
        in_specs=[
            pl.BlockSpec((tq, qk_w), lambda b, p, i: (b * nq + i, p)),
            pl.BlockSpec((seq, qk_w), lambda b, p, i: (b, p)),
            pl.BlockSpec((seq, v_w), lambda b, p, i: (b, p)),
        ],
        out_specs=pl.BlockSpec((tq, v_w), lambda b, p, i: (b * nq + i, p)),
        compiler_params=pltpu.CompilerParams(
            dimension_semantics=("arbitrary", "arbitrary", "arbitrary"),
            vmem_limit_bytes=V7X_VMEM_LIMIT),
        name="mla_attn",
    )(qm, km, vm)


def _merge_kernel(x_ref, yna_ref, ymla_ref, sgn_ref, sgm_ref, wpn_ref, wpm_ref, wout_ref,
                  gate1_ref, shift2_ref, scale2_ref, g2_ref, wr_ref,
                  x1_ref, h2_ref, lt_ref):
    d = x_ref.shape[1]
    for r0 in range(0, x_ref.shape[0], MERGE_SUB):
        rs = pl.ds(r0, MERGE_SUB)
        merged = (sgn_ref[rs, :].astype(F32) * _dot(yna_ref[rs, :], wpn_ref[...])
                  + sgm_ref[rs, :].astype(F32) * _dot(ymla_ref[rs, :], wpm_ref[...]))
        x1 = x_ref[rs, :] + gate1_ref[0] * _dot(_bf(merged), wout_ref[...])
        x1_ref[rs, :] = x1
        h2 = _rms(x1, d) * g2_ref[...]
        h2 = h2 * (1.0 + scale2_ref[0]) + shift2_ref[0]
        h2_ref[rs, :] = _bf(h2)
        lt_ref[:, rs] = _dot3_nt(wr_ref[...], h2)


def _merge(x2, yna, ymla, sgn, sgm, wpn, wpm, wout, mod3, g2, wr, seq):
    t, d = x2.shape
    tm = 512
    per_b = seq // tm
    n_exp = wr.shape[0]

    def full(a):
        return pl.BlockSpec(a.shape, lambda i: (0,) * a.ndim)

    def rows(w):
        return pl.BlockSpec((tm, w), lambda i: (i, 0))

    def modblk(j):
        return pl.BlockSpec((1, 1, d), lambda i: (i // per_b, 0, j))

    return pl.pallas_call(
        _merge_kernel,
        out_shape=(jax.ShapeDtypeStruct((t, d), F32), jax.ShapeDtypeStruct((t, d), BF16),
                   jax.ShapeDtypeStruct((n_exp, t), F32)),
        grid=(t // tm,),
        in_specs=[rows(d), rows(yna.shape[1]), rows(ymla.shape[1]), rows(d), rows(d),
                  full(wpn), full(wpm), full(wout),
                  modblk(2), modblk(3), modblk(4), full(g2), full(wr)],
        out_specs=(rows(d), rows(d), pl.BlockSpec((n_exp, tm), lambda i: (0, i))),
        compiler_params=pltpu.CompilerParams(dimension_semantics=("arbitrary",),
                                             vmem_limit_bytes=V7X_VMEM_LIMIT),
        name="merge",
    )(x2, yna, ymla, sgn, sgm, wpn, wpm, wout, mod3, mod3, mod3, g2, wr)


def _route_kernel(lt_ref, eb_ref, o_ref, q_ref):
    n_exp, tn = lt_ref.shape
    per_g = n_exp // N_GROUPS
    neg_inf = -jnp.inf
    sc = _sigmoid(lt_ref[...])
    sel = sc + eb_ref[...]
    sc3 = sc.reshape(N_GROUPS, per_g, tn)
    g3 = sel.reshape(N_GROUPS, per_g, tn)
    io = lax.broadcasted_iota(jnp.int32, (N_GROUPS, per_g, tn), 1)
    gio = lax.broadcasted_iota(jnp.int32, (N_GROUPS, per_g, tn), 0)
    eio = gio * per_g + io

    m1 = jnp.max(g3, axis=1, keepdims=True)
    i1 = jnp.min(jnp.where(g3 == m1, io, per_g), axis=1, keepdims=True)
    m2 = jnp.max(jnp.where(io == i1, neg_inf, g3), axis=1, keepdims=True)
    gs = m1 + m2

    g1io = lax.broadcasted_iota(jnp.int32, (N_GROUPS, 1, tn), 0)
    gsel = jnp.zeros((N_GROUPS, 1, tn), F32)
    cur = gs
    for _ in range(TOPK_GROUPS):
        m = jnp.max(cur, axis=0, keepdims=True)
        i = jnp.min(jnp.where(cur == m, g1io, N_GROUPS), axis=0, keepdims=True)
        pick = g1io == i
        gsel = jnp.where(pick, 1.0, gsel)
        cur = jnp.where(pick, neg_inf, cur)

    cur = jnp.where(gsel > 0.0, g3, neg_inf)
    chosen = jnp.zeros((N_GROUPS, per_g, tn), F32)
    for _ in range(TOP_K):
        m = jnp.max(jnp.max(cur, axis=1, keepdims=True), axis=0, keepdims=True)
        cand = jnp.where(cur == m, eio, n_exp)
        i = jnp.min(jnp.min(cand, axis=1, keepdims=True), axis=0, keepdims=True)
        pick = eio == i
        chosen = jnp.where(pick, 1.0, chosen)
        cur = jnp.where(pick, neg_inf, cur)

    w = jnp.where(chosen > 0.0, sc3, 0.0)
    tot = jnp.sum(jnp.sum(w, axis=1, keepdims=True), axis=0, keepdims=True)
    gates = (w / tot * ROUTED_SCALE).reshape(n_exp, tn)
    o_ref[...] = gates
    routed = jnp.where(gates > 0.0, 1.0, 0.0).astype(BF16)
    n_row = _dot_nt(jnp.ones((8, tn), BF16), routed)[0:1]
    q_ref[0] = jnp.floor((n_row + (MOE_CHUNK - 1)) * (1.0 / MOE_CHUNK)).astype(jnp.int32)


def _route(lt, e_bias):
    n_exp, t = lt.shape
    tn = MOE_TB
    return pl.pallas_call(
        _route_kernel,
        out_shape=(jax.ShapeDtypeStruct((n_exp, t), F32),
                   jax.ShapeDtypeStruct((t // tn, 1, n_exp), jnp.int32)),
        grid=(t // tn,),
        in_specs=[pl.BlockSpec((n_exp, tn), lambda i: (0, i)),
                  pl.BlockSpec((n_exp, 1), lambda i: (0, 0))],
        out_specs=(pl.BlockSpec((n_exp, tn), lambda i: (0, i)),
                   pl.BlockSpec((1, 1, n_exp), lambda i: (i, 0, 0))),
        compiler_params=pltpu.CompilerParams(dimension_semantics=("arbitrary",)),
        name="route",
    )(lt, e_bias.reshape(n_exp, 1))


def _for_each_chunk(n, fn):
    def quad(j, carry):
        for u in range(4):
            fn(j * 4 + u)
        return carry
    lax.fori_loop(0, n >> 2, quad, 0)
    base = (n >> 2) << 2
    for u in range(3):
        @pl.when(base + u < n)
        def _():
            fn(base + u)


def _dispatch_kernel(dtab_s, nch_s, total_s,
                     gt_ref, qrow_ref, qbrow_ref, h_ref, xs_ref, stage_ref, zero_ref, sem):
    b = pl.program_id(0)
    nb = pl.num_programs(0)
    slot = lax.rem(b, 2)
    n_exp, tb = gt_ref.shape
    rmax = stage_ref.shape[1] * MOE_CHUNK
    cpg = MOE_ROWGROUP // MOE_CHUNK

    routed = gt_ref[...] > 0.0
    before = (lax.broadcasted_iota(jnp.int32, (tb, tb), 0)
              < lax.broadcasted_iota(jnp.int32, (tb, tb), 1))
    pos = _dot(jnp.where(routed, 1.0, 0.0).astype(BF16), jnp.where(before, 1.0, 0.0).astype(BF16))
    posm = _bf(jnp.where(routed, pos, -1.0))
    qrow = qrow_ref[0]
    qbrow = qbrow_ref[0]
    qbrow_f = qbrow.astype(F32)
    h = h_ref[...]
    def sort_rows(g):
        r0 = g * MOE_ROWGROUP
        chunk = (lax.broadcasted_iota(jnp.int32, (MOE_ROWGROUP, n_exp), 0) + r0) >> MOE_CHUNK_SHIFT
        own = jnp.where(chunk >= qbrow, jnp.where(chunk < qbrow + qrow, 1.0, 0.0), 0.0)
        rank = _dot(_bf(own), posm)
        start = jnp.sum(own * qbrow_f, axis=-1, keepdims=True) * MOE_CHUNK
        rel = (lax.broadcasted_iota(jnp.int32, (MOE_ROWGROUP, 1), 0) + r0).astype(F32) - start
        onehot = jnp.where(rank == rel, 1.0, 0.0).astype(BF16)
        rows = _bf(_dot(onehot, h))
        stage_ref[slot, g * cpg:(g + 1) * cpg] = rows.reshape(cpg, MOE_CHUNK, rows.shape[1])

    n_groups = rmax // MOE_ROWGROUP
    n_sure = min(n_groups, (MOE_TB * TOP_K + MOE_ROWGROUP - 1) // MOE_ROWGROUP)
    for g in range(n_sure):
        sort_rows(g)
    for g in range(n_sure, n_groups):
        @pl.when(g * MOE_ROWGROUP < nch_s[b] * MOE_CHUNK)
        def _():
            sort_rows(g)

    def start_chunk(i):
        pltpu.make_async_copy(stage_ref.at[slot, i], xs_ref.at[dtab_s[b, i]], sem.at[slot]).start()
    _for_each_chunk(nch_s[b], start_chunk)

    def wait_chunks(n, sl):
        def wait_chunk(i):
            pltpu.make_async_copy(zero_ref, xs_ref.at[0], sem.at[sl]).wait()
        _for_each_chunk(n, wait_chunk)

    @pl.when(b > 0)
    def _():
        wait_chunks(nch_s[b - 1], 1 - slot)

    @pl.when(b == nb - 1)
    def _():
        zero_ref[...] = jnp.zeros_like(zero_ref)
        n_tail = MOE_TM // MOE_CHUNK
        for c in range(n_tail):
            pltpu.make_async_copy(zero_ref, xs_ref.at[total_s[0] + c], sem.at[slot]).start()
        wait_chunks(nch_s[b] + n_tail, slot)


def _dispatch(gates_t, q, qbase, dtab, nch, total, h2, n_chunks):
    n_exp, t = gates_t.shape
    d = h2.shape[1]
    nb = t // MOE_TB
    rmax = _moe_stage_rows(n_exp)
    grid_spec = pltpu.PrefetchScalarGridSpec(
        num_scalar_prefetch=3,
        grid=(nb,),
        in_specs=[
            pl.BlockSpec((n_exp, MOE_TB), lambda b, *_: (0, b)),
            pl.BlockSpec((1, 1, n_exp), lambda b, *_: (b, 0, 0)),
            pl.BlockSpec((1, 1, n_exp), lambda b, *_: (b, 0, 0)),
            pl.BlockSpec((MOE_TB, d), lambda b, *_: (b, 0)),
        ],
        out_specs=pl.BlockSpec(memory_space=pl.ANY),
        scratch_shapes=[pltpu.VMEM((2, rmax // MOE_CHUNK, MOE_CHUNK, d), BF16),
                        pltpu.VMEM((MOE_CHUNK, d), BF16), pltpu.SemaphoreType.DMA((2,))],
    )
    return pl.pallas_call(
        _dispatch_kernel,
        out_shape=jax.ShapeDtypeStruct((n_chunks, MOE_CHUNK, d), BF16),
        grid_spec=grid_spec,
        compiler_params=pltpu.CompilerParams(dimension_semantics=("arbitrary",),
                                             vmem_limit_bytes=V7X_VMEM_LIMIT),
        name="moe_dispatch",
    )(dtab, nch, total, gates_t, q.reshape(nb, 1, n_exp), qbase.reshape(nb, 1, n_exp), h2)


def _tile_pieces():
    cpt = MOE_TM // MOE_CHUNK
    return [1 << s for s in range(cpt.bit_length() - 1, -1, -1)]


def _expert_kernel(off_s, len_s, next_s, first_s, xs_ref, wg_ref, wu_ref, wd_ref, ys_ref,
                   xbuf, ybuf, wg_b, wu_b, wd_b, state, sem_in, sem_out):
    e = pl.program_id(0)
    n_exp = pl.num_programs(0)
    cpt = MOE_TM // MOE_CHUNK
    d = xbuf.shape[3]
    pieces = _tile_pieces()

    def tile_in(ee, tt, sl):
        return pltpu.make_async_copy(xs_ref.at[pl.ds(off_s[ee] + tt * cpt, cpt)], xbuf.at[sl],
                                     sem_in.at[sl])

    def for_each_piece(valid, fn):
        for k, piece in enumerate(pieces):
            @pl.when((valid & piece) != 0)
            def _():
                fn(k, piece, valid & ~(2 * piece - 1))

    def tile_out(sl, dst_chunk, k, piece, start):
        return pltpu.make_async_copy(ybuf.at[sl, pl.ds(start, piece)],
                                     ys_ref.at[pl.ds(dst_chunk + start, piece)], sem_out.at[sl, k])

    def drain(sl):
        for_each_piece(state[1 + sl], lambda k, piece, start: tile_out(sl, 0, k, piece, start).wait())
        state[1 + sl] = 0

    def following(ee, tt):
        safe = jnp.minimum(ee, n_exp - 1)
        more = tt + 1 < (len_s[safe] + cpt - 1) // cpt
        nxt_e = jnp.where(ee >= n_exp, n_exp, jnp.where(more, ee, next_s[safe]))
        return nxt_e, jnp.where(more, tt + 1, 0)

    def prefetch(ee, tt, xsl):
        @pl.when(ee < n_exp)
        def _():
            tile_in(ee, tt, xsl).start()

    @pl.when(e == 0)
    def _():
        state[0] = 0
        state[1] = 0
        state[2] = 0
        ahead = (first_s[0], 0)
        for k in range(MOE_XBUFS - 1):
            prefetch(ahead[0], ahead[1], k)
            ahead = following(*ahead)

    n_valid = len_s[e]
    n_tiles = (n_valid + cpt - 1) // cpt

    @pl.when(n_valid > 0)
    def _():
        wg_b[...] = _bf(wg_ref[0])
        wu_b[...] = _bf(wu_ref[0])
        wd_b[...] = _bf(wd_ref[0])

        def tile(t, carry):
            g = state[0]
            xsl = lax.rem(g, MOE_XBUFS)
            sl = g & 1
            tile_in(e, t, xsl).wait()
            ahead = (e, t)
            for _ in range(MOE_XBUFS - 1):
                ahead = following(*ahead)
            prefetch(ahead[0], ahead[1], lax.rem(g + MOE_XBUFS - 1, MOE_XBUFS))

            drain(sl)
            valid = jnp.minimum(n_valid - t * cpt, cpt)

            def ffn(n_c):
                x = xbuf[xsl, :n_c].reshape(n_c * MOE_CHUNK, d)
                a = _silu(_dot(x, wg_b[...])) * _dot(x, wu_b[...])
                ybuf[sl, :n_c] = _bf(_dot(_bf(a), wd_b[...])).reshape(n_c, MOE_CHUNK, d)

            @pl.when(valid > cpt // 2)
            def _():
                ffn(cpt)

            @pl.when(valid <= cpt // 2)
            def _():
                ffn(cpt // 2)

            dst = off_s[e] + t * cpt
            for_each_piece(valid, lambda k, piece, start: tile_out(sl, dst, k, piece, start).start())
            state[1 + sl] = valid
            state[0] = state[0] + 1
            return carry
        lax.fori_loop(0, n_tiles, tile, 0)

    @pl.when(e == n_exp - 1)
    def _():
        drain(0)
        drain(1)


def _experts(off, per_exp, nxt, first, xs, wg, wu, wd):
    n_chunks, _, d = xs.shape
    n_exp, _, ff = wg.shape
    cpt = MOE_TM // MOE_CHUNK

    def w_blk(e, *_):
        return (e, 0, 0)

    grid_spec = pltpu.PrefetchScalarGridSpec(
        num_scalar_prefetch=4,
        grid=(n_exp,),
        in_specs=[pl.BlockSpec(memory_space=pl.ANY),
                  pl.BlockSpec((1, d, ff), w_blk), pl.BlockSpec((1, d, ff), w_blk),
                  pl.BlockSpec((1, ff, d), w_blk)],
        out_specs=pl.BlockSpec(memory_space=pl.ANY),
        scratch_shapes=[pltpu.VMEM((MOE_XBUFS, cpt, MOE_CHUNK, d), BF16),
                        pltpu.VMEM((2, cpt, MOE_CHUNK, d), BF16),
                        pltpu.VMEM((d, ff), BF16), pltpu.VMEM((d, ff), BF16),
                        pltpu.VMEM((ff, d), BF16),
                        pltpu.SMEM((3,), jnp.int32),
                        pltpu.SemaphoreType.DMA((MOE_XBUFS,)),
                        pltpu.SemaphoreType.DMA((2, len(_tile_pieces())))],
    )
    return pl.pallas_call(
        _expert_kernel,
        out_shape=jax.ShapeDtypeStruct((n_chunks, MOE_CHUNK, d), BF16),
        grid_spec=grid_spec,
        compiler_params=pltpu.CompilerParams(dimension_semantics=("arbitrary",),
                                             vmem_limit_bytes=V7X_VMEM_LIMIT),
        name="moe_experts",
    )(off, per_exp, nxt, first, xs, wg, wu, wd)


def _combine_kernel(dtab_s, nch_s,
                    gt_ref, qcol_ref, qbcol_ref, h_ref, x1_ref, gate2_ref,
                    wsg_ref, wsu_ref, wsd_ref, ys_ref, o_ref, stage_ref, sem):
    b = pl.program_id(0)
    nb = pl.num_programs(0)
    slot = lax.rem(b, 2)
    n_exp, tb = gt_ref.shape
    rmax = stage_ref.shape[1] * MOE_CHUNK

    def fetch(bb, sl):
        def start_chunk(i):
            pltpu.make_async_copy(ys_ref.at[dtab_s[bb, i]], stage_ref.at[sl, i], sem.at[sl]).start()
        _for_each_chunk(nch_s[bb], start_chunk)

    @pl.when(b == 0)
    def _():
        stage_ref[...] = jnp.zeros_like(stage_ref)
        fetch(0, 0)

    def wait_chunk(i):
        pltpu.make_async_copy(ys_ref.at[0], stage_ref.at[slot, 0], sem.at[slot]).wait()
    _for_each_chunk(nch_s[b], wait_chunk)

    @pl.when(b + 1 < nb)
    def _():
        fetch(b + 1, 1 - slot)

    gt = gt_ref[...]
    routed = jnp.where(gt > 0.0, 1.0, 0.0).astype(BF16)
    i0 = lax.broadcasted_iota(jnp.int32, (tb, tb), 0)
    i1 = lax.broadcasted_iota(jnp.int32, (tb, tb), 1)
    eye = jnp.where(i0 == i1, 1.0, 0.0).astype(BF16)
    routed_t = _dot_nt(eye, routed)
    gates_tok = _dot_nt(eye, _bf(gt))
    earlier = jnp.where(i1 < i0, 1.0, 0.0).astype(BF16)
    pos_t = _dot(earlier, _bf(routed_t))
    posm_t = _bf(jnp.where(routed_t > 0.0, pos_t, -1.0))

    qcol = qcol_ref[0]
    qbcol = qbcol_ref[0]
    chunk = lax.broadcasted_iota(jnp.int32, (n_exp, rmax), 1) >> MOE_CHUNK_SHIFT
    own = jnp.where(chunk >= qbcol, jnp.where(chunk < qbcol + qcol, 1.0, 0.0), 0.0)
    own_b = _bf(own)
    rank = _dot(posm_t, own_b)
    wexp = _dot(_bf(gates_tok), own_b)
    start = jnp.sum(own * qbcol.astype(F32), axis=0, keepdims=True) * MOE_CHUNK
    rel = lax.broadcasted_iota(jnp.int32, (1, rmax), 1).astype(F32) - start
    weights = _bf(jnp.where(rank == rel, wexp, 0.0))
    routed_out = _dot(weights, stage_ref[slot].reshape(rmax, o_ref.shape[1]))

    h = h_ref[...]
    a = _silu(_dot(h, wsg_ref[...])) * _dot(h, wsu_ref[...])
    shared = _dot(_bf(a), wsd_ref[...])
    o_ref[...] = x1_ref[...] + gate2_ref[0] * (routed_out + shared)


def _combine(gates_t, q, qbase, dtab, nch, h2, x1, mod3, wsg, wsu, wsd, ys, seq):
    n_exp, t = gates_t.shape
    d = h2.shape[1]
    nb = t // MOE_TB
    per_b = seq // MOE_TB
    stage_chunks = _moe_stage_rows(n_exp) // MOE_CHUNK

    def full(a):
        return pl.BlockSpec(a.shape, lambda b, *_: (0,) * a.ndim)

    def rows(w):
        return pl.BlockSpec((MOE_TB, w), lambda b, *_: (b, 0))

    grid_spec = pltpu.PrefetchScalarGridSpec(
        num_scalar_prefetch=2,
        grid=(nb,),
        in_specs=[
            pl.BlockSpec((n_exp, MOE_TB), lambda b, *_: (0, b)),
            pl.BlockSpec((1, n_exp, 1), lambda b, *_: (b, 0, 0)),
            pl.BlockSpec((1, n_exp, 1), lambda b, *_: (b, 0, 0)),
            rows(d), rows(d),
            pl.BlockSpec((1, 1, d), lambda b, *_: (b // per_b, 0, 5)),
            full(wsg), full(wsu), full(wsd),
            pl.BlockSpec(memory_space=pl.ANY),
        ],
        out_specs=rows(d),
        scratch_shapes=[pltpu.VMEM((2, stage_chunks, MOE_CHUNK, d), BF16),
                        pltpu.SemaphoreType.DMA((2,))],
    )
    return pl.pallas_call(
        _combine_kernel,
        out_shape=jax.ShapeDtypeStruct((t, d), F32),
        grid_spec=grid_spec,
        compiler_params=pltpu.CompilerParams(dimension_semantics=("arbitrary",),
                                             vmem_limit_bytes=V7X_VMEM_LIMIT),
        name="moe_combine",
    )(dtab, nch, gates_t, q.reshape(nb, n_exp, 1), qbase.reshape(nb, n_exp, 1),
      h2, x1, mod3, wsg, wsu, wsd, ys)


def _moe_plan(q):
    nb, n_exp = q.shape
    qbase = jnp.cumsum(q, axis=1) - q
    nch = jnp.sum(q, axis=1)
    per_exp = jnp.sum(q, axis=0)
    off = jnp.cumsum(per_exp) - per_exp
    dstq = off[None, :] + jnp.cumsum(q, axis=0) - q
    i = jnp.arange(_moe_block_chunks(n_exp), dtype=jnp.int32)
    ii = i[None, :, None]
    owned = (ii >= qbase[:, None, :]) & (ii < (qbase + q)[:, None, :])
    dtab = i[None, :] + jnp.sum(jnp.where(owned, (dstq - qbase)[:, None, :], 0), axis=2)
    ids = jnp.arange(n_exp, dtype=jnp.int32)
    later = (ids[None, :] > ids[:, None]) & (per_exp[None, :] > 0)
    nxt = jnp.min(jnp.where(later, ids[None, :], n_exp), axis=1)
    first = jnp.min(jnp.where(per_exp > 0, ids, n_exp)).reshape(1)
    total = jnp.sum(per_exp).reshape(1)
    return qbase, dtab, nch, off, per_exp, nxt, first, total


def _moe_block_chunks(n_exp):
    return MOE_TB * TOP_K // MOE_CHUNK + n_exp


def _moe_stage_rows(n_exp):
    return -(-_moe_block_chunks(n_exp) * MOE_CHUNK // MOE_ROWGROUP) * MOE_ROWGROUP


def _moe(h2, gates_t, q3, x1, mod3, wsg, wsu, wsd, wg, wu, wd, seq):
    n_exp, t = gates_t.shape
    nb = t // MOE_TB
    q = q3.reshape(nb, n_exp)
    qbase, dtab, nch, off, per_exp, nxt, first, total = _moe_plan(q)
    n_chunks = nb * _moe_block_chunks(n_exp) + MOE_TM // MOE_CHUNK
    xs = _dispatch(gates_t, q, qbase, dtab, nch, total, h2, n_chunks)
    ys = _experts(off, per_exp, nxt, first, xs, wg, wu, wd)
    return _combine(gates_t, q, qbase, dtab, nch, h2, x1, mod3, wsg, wsu, wsd, ys, seq)


def _pad_heads(w, heads, width):
    lead = w.shape[:-1]
    w = w.reshape(lead + (heads, width))
    w = jnp.pad(w, [(0, 0)] * len(lead) + [(0, 0), (0, LANES - width)])
    return w.reshape(lead + (heads * LANES,))


def kernel(x, c, positions, w_ada, b_ada, g_norm1, w_in, g_na_q, g_na_k, na_rpb, g_q_lat, w_uq,
           g_kv_lat, w_ukv, g_mla_q, g_mla_k, w_proj_na, w_proj_mla, w_out, g_norm2, w_router,
           e_bias, w_exp_gate, w_exp_up, w_exp_down, w_sh_gate, w_sh_up, w_sh_down):
    bsz, seq, d = x.shape
    t = bsz * seq
    depth = w_ada.shape[0]
    na_w = NA_HEADS * NA_HEAD_DIM
    q_rank = g_q_lat.shape[1]
    kv_rank = g_kv_lat.shape[1]
    n_rows = seq // GRID_W

    pos = positions.reshape(1, t)
    half = MLA_ROPE_DIM // 2
    freq = (ROPE_THETA ** (-jnp.arange(half, dtype=F32) / half)).reshape(half, 1)

    x2 = x.reshape(t, d)
    for l in range(depth):
        mod3 = _adaln(c, w_ada[l], b_ada[l]).reshape(bsz, 1, 6 * d)

        wi = w_in[l]
        o_lat = 3 * na_w
        o_rot = o_lat + q_rank + kv_rank
        o_gate = o_rot + MLA_ROPE_DIM
        wqkv = _bf(wi[:, :o_lat])
        w_rot = jnp.pad(wi[:, o_rot:o_gate], ((0, 0), (MLA_NOPE_DIM, LANES - MLA_QK_DIM)))
        wlat = _bf(jnp.concatenate([wi[:, o_lat:o_rot], w_rot], axis=1))
        wgate = _bf(wi[:, o_gate:])
        gq = (jnp.tile(g_na_q[l], NA_HEADS) * (NA_HEAD_DIM ** -0.5 * LOG2E)).reshape(1, na_w)
        gk = jnp.tile(g_na_k[l], NA_HEADS).reshape(1, na_w)
        wuq = _bf(_pad_heads(w_uq[l], MLA_HEADS, MLA_QK_DIM))
        wukv = w_ukv[l].reshape(kv_rank, MLA_HEADS, MLA_NOPE_DIM + MLA_V_DIM)
        wuk = _bf(_pad_heads(wukv[:, :, :MLA_NOPE_DIM].reshape(kv_rank, -1), MLA_HEADS, MLA_NOPE_DIM))
        wuv = _bf(wukv[:, :, MLA_NOPE_DIM:].reshape(kv_rank, MLA_HEADS * MLA_V_DIM))
        gmq = _pad_heads(jnp.tile(g_mla_q[l], MLA_HEADS) * (MLA_QK_DIM ** -0.5 * LOG2E),
                         MLA_HEADS, MLA_QK_DIM).reshape(1, -1)
        gmk = _pad_heads(jnp.tile(g_mla_k[l], MLA_HEADS), MLA_HEADS, MLA_QK_DIM).reshape(1, -1)

        qa, ka, va, qm, km, vm, sgn, sgm = _inproj(
            x2, mod3, g_norm1[l].reshape(1, d), wqkv, wlat, wgate, gq, gk,
            g_q_lat[l].reshape(1, q_rank), g_kv_lat[l].reshape(1, kv_rank), wuq, wuk, wuv,
            gmq, gmk, pos, freq, seq)

        bias = _na_bias(na_rpb[l], n_rows)
        y_na = _na_attention(qa, ka, va, bias, bsz, seq)
        y_mla = _mla_attention(qm, km, vm, bsz, seq)

        x1, h2, lt = _merge(x2, y_na, y_mla, sgn, sgm, _bf(w_proj_na[l]), _bf(w_proj_mla[l]),
                            _bf(w_out[l]), mod3, g_norm2[l].reshape(1, d), w_router[l].T, seq)
        gates_t, q3 = _route(lt, e_bias[l])
        x2 = _moe(h2, gates_t, q3, x1, mod3, _bf(w_sh_gate[l]), _bf(w_sh_up[l]),
                  _bf(w_sh_down[l]), w_exp_gate[l], w_exp_up[l], w_exp_down[l], seq)
    return x2.reshape(bsz, seq, d)
```

```python
import functools

import jax
import jax.numpy as jnp
import numpy as np
from jax import lax
from jax.experimental import pallas as pl
from jax.experimental.pallas import tpu as pltpu

GRID_W = 64
NA_HEADS = 8
NA_HEAD_DIM = 64
NA_WIN_ROWS = 8
NA_WIN_COLS = 16
MLA_HEADS = 8
MLA_NOPE_DIM = 64
MLA_ROPE_DIM = 32
MLA_V_DIM = 64
MLA_QK_DIM = MLA_NOPE_DIM + MLA_ROPE_DIM
ROPE_THETA = 10000.0
N_GROUPS = 8
TOPK_GROUPS = 4
TOP_K = 8
ROUTED_SCALE = 2.5
EPS = 1e-6
NEG_BIG = -1e30

LANES = 128
V7X_VMEM_LIMIT = 56 * 1024 * 1024

NA_QROWS = 4
NA_BAND = 12
NA_BLOCK_TYPES = 3
MERGE_SUB = 512
INPROJ_TM = 512
INPROJ_SUB = 256
MLA_PAIRS_PER_STEP = 4
LOG2E = 1.4426950408889634
MOE_TB = 256
ROUTE_TN = 1024
MOE_CHUNK_SHIFT = 4
MOE_CHUNK = 1 << MOE_CHUNK_SHIFT
MOE_TM = 1024
MOE_ROWGROUP = 512
MOE_XBUFS = 3

F32 = jnp.float32
BF16 = jnp.bfloat16


def _bf(x):
    return x.astype(BF16)


def _dot(a, b):
    return jnp.dot(a, b, preferred_element_type=F32)


def _dot_nt(a, b):
    return lax.dot_general(a, b, (((1,), (1,)), ((), ())), preferred_element_type=F32)


def _split(x):
    hi = _bf(x)
    lo = _bf(x - hi.astype(F32))
    return hi, lo


def _dot3(a, b):
    ah, al = _split(a)
    bh, bl = _split(b)
    return _dot(ah, bh) + (_dot(ah, bl) + _dot(al, bh))


def _dot3_nt(a, b):
    ah, al = _split(a)
    bh, bl = _split(b)
    return _dot_nt(ah, bh) + (_dot_nt(ah, bl) + _dot_nt(al, bh))


def _sigmoid(x):
    return 1.0 / (1.0 + jnp.exp(-x))


def _silu(x):
    return x * _sigmoid(x)


def _rms(x, n):
    ss = jnp.sum(x * x, axis=-1, keepdims=True)
    return x * lax.rsqrt(ss * (1.0 / n) + EPS)


def _adaln_kernel(c_ref, w_ref, b_ref, o_ref):
    c = c_ref[...]
    o_ref[...] = _dot3(_silu(c), w_ref[...]) + b_ref[...]


def _adaln(c, w, b):
    bsz, d = c.shape
    n = w.shape[1]
    tn = 1024
    return pl.pallas_call(
        _adaln_kernel,
        out_shape=jax.ShapeDtypeStruct((bsz, n), F32),
        grid=(n // tn,),
        in_specs=[
            pl.BlockSpec((bsz, d), lambda j: (0, 0)),
            pl.BlockSpec((d, tn), lambda j: (0, j)),
            pl.BlockSpec((1, tn), lambda j: (0, j)),
        ],
        out_specs=pl.BlockSpec((bsz, tn), lambda j: (0, j)),
        compiler_params=pltpu.CompilerParams(dimension_semantics=("arbitrary",)),
        name="adaln",
    )(c, w, b.reshape(1, n))


def _inproj_kernel(x_ref, shift_ref, scale_ref, g1_ref, wqkv_ref, wlat_ref, wgate_ref,
                   gq_ref, gk_ref, gql_ref, gkvl_ref, wuq_ref, wuk_ref, wuv_ref,
                   gmq_ref, gmk_ref, pos_ref, freq_ref,
                   qa_ref, ka_ref, va_ref, qm_ref, km_ref, vm_ref, sgn_ref, sgm_ref):
    for r0 in range(0, x_ref.shape[0], INPROJ_SUB):
        _inproj_rows(pl.ds(r0, INPROJ_SUB), x_ref, shift_ref, scale_ref, g1_ref, wqkv_ref,
                     wlat_ref, wgate_ref, gq_ref, gk_ref, gql_ref, gkvl_ref, wuq_ref, wuk_ref,
                     wuv_ref, gmq_ref, gmk_ref, pos_ref, freq_ref, qa_ref, ka_ref, va_ref,
                     qm_ref, km_ref, vm_ref, sgn_ref, sgm_ref)


def _inproj_rows(rs, x_ref, shift_ref, scale_ref, g1_ref, wqkv_ref, wlat_ref, wgate_ref,
                 gq_ref, gk_ref, gql_ref, gkvl_ref, wuq_ref, wuk_ref, wuv_ref,
                 gmq_ref, gmk_ref, pos_ref, freq_ref,
                 qa_ref, ka_ref, va_ref, qm_ref, km_ref, vm_ref, sgn_ref, sgm_ref):
    d = x_ref.shape[1]
    x = x_ref[rs, :]
    h = _rms(x, d) * g1_ref[...]
    h = h * (1.0 + scale_ref[0]) + shift_ref[0]
    hb = _bf(h)

    qkv = _dot(hb, wqkv_ref[...])
    lat = _dot(hb, wlat_ref[...])
    gts = _dot(hb, wgate_ref[...])
    sgn_ref[rs, :] = _bf(_sigmoid(gts[:, :d]))
    sgm_ref[rs, :] = _bf(_sigmoid(gts[:, d:]))

    na_w = NA_HEADS * NA_HEAD_DIM
    lane = lax.broadcasted_iota(jnp.int32, (1, LANES), 1)
    lo_half = lane < NA_HEAD_DIM
    for p in range(na_w // LANES):
        sl = slice(p * LANES, (p + 1) * LANES)
        for src_off, g_ref, dst_ref in ((0, gq_ref, qa_ref), (na_w, gk_ref, ka_ref)):
            t = qkv[:, src_off + p * LANES: src_off + (p + 1) * LANES]
            sq = t * t
            s_lo = jnp.sum(jnp.where(lo_half, sq, 0.0), axis=-1, keepdims=True)
            s_hi = jnp.sum(jnp.where(lo_half, 0.0, sq), axis=-1, keepdims=True)
            r = jnp.where(lo_half,
                          lax.rsqrt(s_lo * (1.0 / NA_HEAD_DIM) + EPS),
                          lax.rsqrt(s_hi * (1.0 / NA_HEAD_DIM) + EPS))
            dst_ref[rs, sl] = _bf(t * r * g_ref[:, sl])
    va_ref[rs, :] = _bf(qkv[:, 2 * na_w: 3 * na_w])

    q_rank = gql_ref.shape[1]
    kv_rank = gkvl_ref.shape[1]
    qln = _rms(lat[:, :q_rank], q_rank) * gql_ref[...]
    kvn = _bf(_rms(lat[:, q_rank:q_rank + kv_rank], kv_rank) * gkvl_ref[...])
    qpre = _dot(_bf(qln), wuq_ref[...])
    knope = _dot(kvn, wuk_ref[...])
    vm_ref[rs, :] = _bf(_dot(kvn, wuv_ref[...]))
    krot = lat[:, q_rank + kv_rank:]

    tm = INPROJ_SUB
    half = MLA_ROPE_DIM // 2
    ang_t = freq_ref[...] * pos_ref[:, rs].astype(F32)
    cos_t = jnp.cos(ang_t)
    sin_t = jnp.sin(ang_t)
    l_i = lax.broadcasted_iota(jnp.int32, (LANES, half), 0)
    j_i = lax.broadcasted_iota(jnp.int32, (LANES, half), 1)
    hit = jnp.where((l_i >= MLA_NOPE_DIM) & (l_i < MLA_QK_DIM)
                    & (((l_i - MLA_NOPE_DIM) & (half - 1)) == j_i), 1.0, 0.0)
    first_half = l_i < MLA_NOPE_DIM + half
    eye = jnp.where(lax.broadcasted_iota(jnp.int32, (tm, tm), 0)
                    == lax.broadcasted_iota(jnp.int32, (tm, tm), 1), 1.0, 0.0).astype(BF16)

    def table(sel, vals, fill_nope):
        hi, lo = _split(vals)
        w = _dot(_bf(sel), hi) + _dot(_bf(sel), lo)
        if fill_nope:
            w = jnp.where(lax.broadcasted_iota(jnp.int32, (LANES, tm), 0) < MLA_NOPE_DIM, 1.0, w)
        hi, lo = _split(w)
        return _dot_nt(eye, hi) + _dot_nt(eye, lo)

    c_tab = table(hit, cos_t, True)
    s_up = table(jnp.where(first_half, 0.0, hit), sin_t, False)
    s_dn = table(jnp.where(first_half, -hit, 0.0), sin_t, False)

    def rope(t):
        return t * c_tab + pltpu.roll(t, half, 1) * s_up + pltpu.roll(t, LANES - half, 1) * s_dn

    kr = rope(krot)
    for hd in range(MLA_HEADS):
        sl = slice(hd * LANES, (hd + 1) * LANES)
        qh = rope(qpre[:, sl])
        qm_ref[rs, sl] = _bf(_rms(qh, MLA_QK_DIM) * gmq_ref[:, sl])
        kh = knope[:, sl] + kr
        km_ref[rs, sl] = _bf(_rms(kh, MLA_QK_DIM) * gmk_ref[:, sl])


def _inproj(x2, mod3, g1, wqkv, wlat, wgate, gq, gk, gql, gkvl, wuq, wuk, wuv, gmq, gmk,
            pos, freq, seq):
    t, d = x2.shape
    tm = INPROJ_TM
    per_b = seq // tm
    na_w = NA_HEADS * NA_HEAD_DIM
    mla_w = MLA_HEADS * LANES
    v_w = MLA_HEADS * MLA_V_DIM

    def full(a):
        return pl.BlockSpec(a.shape, lambda i: (0,) * a.ndim)

    def rows(w):
        return pl.BlockSpec((tm, w), lambda i: (i, 0))

    out_shapes = (
        jax.ShapeDtypeStruct((t, na_w), BF16), jax.ShapeDtypeStruct((t, na_w), BF16),
        jax.ShapeDtypeStruct((t, na_w), BF16),
        jax.ShapeDtypeStruct((t, mla_w), BF16), jax.ShapeDtypeStruct((t, mla_w), BF16),
        jax.ShapeDtypeStruct((t, v_w), BF16),
        jax.ShapeDtypeStruct((t, d), BF16), jax.ShapeDtypeStruct((t, d), BF16),
    )
    return pl.pallas_call(
        _inproj_kernel,
        out_shape=out_shapes,
        grid=(t // tm,),
        in_specs=[
            rows(d),
            pl.BlockSpec((1, 1, d), lambda i: (i // per_b, 0, 0)),
            pl.BlockSpec((1, 1, d), lambda i: (i // per_b, 0, 1)),
            full(g1), full(wqkv), full(wlat), full(wgate), full(gq), full(gk), full(gql),
            full(gkvl), full(wuq), full(wuk), full(wuv), full(gmq), full(gmk),
            pl.BlockSpec((1, tm), lambda i: (0, i)),
            full(freq),
        ],
        out_specs=(rows(na_w), rows(na_w), rows(na_w), rows(mla_w), rows(mla_w), rows(v_w),
                   rows(d), rows(d)),
        compiler_params=pltpu.CompilerParams(dimension_semantics=("arbitrary",),
                                             vmem_limit_bytes=V7X_VMEM_LIMIT),
        name="inproj",
    )(x2, mod3, mod3, g1, wqkv, wlat, wgate, gq, gk, gql, gkvl, wuq, wuk, wuv, gmq, gmk,
      pos, freq)


def _na_block_geometry(block_type, n_rows):
    if block_type == 0:
        return 0, 0
    if block_type == 1:
        r0 = NA_QROWS
        return r0, r0 - NA_WIN_ROWS // 2
    return n_rows - NA_QROWS, n_rows - NA_BAND


def _na_bias_kernel(rpb_ref, o_ref, m_ref, *, n_rows):
    hd = pl.program_id(0)
    n_dr = 2 * NA_WIN_ROWS - 1
    n_dc = 2 * NA_WIN_COLS - 1
    qc = lax.broadcasted_iota(jnp.int32, (GRID_W, LANES), 0)
    kc = lax.broadcasted_iota(jnp.int32, (GRID_W, LANES), 1) & (GRID_W - 1)
    dc = jnp.clip(kc - qc, -(NA_WIN_COLS - 1), NA_WIN_COLS - 1) + (NA_WIN_COLS - 1)
    cstart = jnp.clip(qc - NA_WIN_COLS // 2, 0, GRID_W - NA_WIN_COLS)
    col_ok = (kc >= cstart) & (kc < cstart + NA_WIN_COLS)
    for i_dr in range(n_dr):
        acc = jnp.zeros((GRID_W, LANES), F32)
        for t in range(n_dc):
            acc = jnp.where(dc == t, rpb_ref[hd, i_dr * n_dc + t], acc)
        m_ref[i_dr] = jnp.where(col_ok, acc * LOG2E, NEG_BIG)
    neg = jnp.full((GRID_W, LANES), NEG_BIG, F32)
    lo_half = lax.broadcasted_iota(jnp.int32, (GRID_W, LANES), 1) < GRID_W
    kh = NA_WIN_ROWS
    for bt in range(NA_BLOCK_TYPES):
        r0, start = _na_block_geometry(bt, n_rows)
        for i in range(NA_QROWS):
            r = r0 + i
            rs = min(max(r - kh // 2, 0), n_rows - kh)
            for jp in range(NA_BAND // 2):
                halves = []
                for j in (2 * jp, 2 * jp + 1):
                    krow = start + j
                    if rs <= krow < rs + kh:
                        halves.append(m_ref[krow - r + (NA_WIN_ROWS - 1)])
                    else:
                        halves.append(neg)
                tile = jnp.where(lo_half, halves[0], halves[1])
                o_ref[bt, 0, i * GRID_W:(i + 1) * GRID_W, jp * LANES:(jp + 1) * LANES] = tile


def _na_bias(rpb, n_rows):
    heads = rpb.shape[0]
    nq = NA_QROWS * GRID_W
    nk = NA_BAND * GRID_W
    rpb2 = rpb.reshape(heads, -1)
    return pl.pallas_call(
        functools.partial(_na_bias_kernel, n_rows=n_rows),
        out_shape=jax.ShapeDtypeStruct((NA_BLOCK_TYPES, heads, nq, nk), F32),
        grid=(heads,),
        in_specs=[pl.BlockSpec(memory_space=pltpu.SMEM)],
        out_specs=pl.BlockSpec((NA_BLOCK_TYPES, 1, nq, nk), lambda hd: (0, hd, 0, 0)),
        scratch_shapes=[pltpu.VMEM((2 * NA_WIN_ROWS - 1, GRID_W, LANES), F32)],
        compiler_params=pltpu.CompilerParams(dimension_semantics=("arbitrary",)),
        name="na_bias",
    )(rpb2)


def _softmax_pv(s, v_pair, hh, half):
    lane = lax.broadcasted_iota(jnp.int32, (1, LANES), 1)
    mine = (lane < half) if hh == 0 else (lane >= half)
    den_lane = half if hh == 0 else 0
    m = jnp.max(s, axis=-1, keepdims=True)
    p = _bf(jnp.exp2(s - m))
    ones_row = jnp.where(lane == den_lane, 1.0, 0.0).astype(BF16)
    o = _dot(p, jnp.where(mine, v_pair, ones_row))
    den = jnp.sum(jnp.where(lane == den_lane, o, 0.0), axis=-1, keepdims=True)
    return jnp.where(mine, o / den, 0.0)


def _na_kernel(q_ref, k_ref, v_ref, bias_ref, o_ref, *, n_blocks, n_rows):
    blk = pl.program_id(1)
    start_row = jnp.where(blk == 0, 0,
                          jnp.where(blk == n_blocks - 1, n_rows - NA_BAND,
                                    blk * NA_QROWS - NA_WIN_ROWS // 2))
    off = pl.multiple_of(start_row * GRID_W, GRID_W)
    nk = NA_BAND * GRID_W
    tq = q_ref.shape[0]
    lo_half = lax.broadcasted_iota(jnp.int32, (1, LANES), 1) < NA_HEAD_DIM
    for p in range(NA_HEADS * NA_HEAD_DIM // LANES):
        sl = slice(p * LANES, (p + 1) * LANES)
        qp = q_ref[:, sl]
        kb = k_ref[pl.ds(off, nk), sl]
        vb = v_ref[pl.ds(off, nk), sl]
        zero = jnp.zeros_like(qp)
        q2 = jnp.concatenate([jnp.where(lo_half, qp, zero), jnp.where(lo_half, zero, qp)], axis=0)
        s = _dot_nt(q2, kb) + bias_ref[0, 2 * p:2 * p + 2].reshape(2 * tq, nk)
        m = jnp.max(s, axis=-1, keepdims=True)
        e = jnp.exp2(s - m)
        den = jnp.sum(e, axis=-1, keepdims=True)
        o = _dot(_bf(e), vb) / den
        o_ref[:, sl] = _bf(jnp.where(lo_half, o[:tq], o[tq:]))


def _na_attention(qa, ka, va, bias, bsz, seq):
    t, w = qa.shape
    n_rows = seq // GRID_W
    n_blocks = n_rows // NA_QROWS
    tq = NA_QROWS * GRID_W
    nk = NA_BAND * GRID_W
    heads = bias.shape[1]

    def btype(blk):
        return jnp.where(blk == 0, 0, jnp.where(blk == n_blocks - 1, 2, 1))

    return pl.pallas_call(
        functools.partial(_na_kernel, n_blocks=n_blocks, n_rows=n_rows),
        out_shape=jax.ShapeDtypeStruct((t, w), BF16),
        grid=(bsz, n_blocks),
        in_specs=[
            pl.BlockSpec((tq, w), lambda b, blk: (b * n_blocks + blk, 0)),
            pl.BlockSpec((seq, w), lambda b, blk: (b, 0)),
            pl.BlockSpec((seq, w), lambda b, blk: (b, 0)),
            pl.BlockSpec((1, heads, tq, nk), lambda b, blk: (btype(blk), 0, 0, 0)),
        ],
        out_specs=pl.BlockSpec((tq, w), lambda b, blk: (b * n_blocks + blk, 0)),
        compiler_params=pltpu.CompilerParams(dimension_semantics=("arbitrary", "arbitrary"),
                                             vmem_limit_bytes=V7X_VMEM_LIMIT),
        name="na_attn",
    )(qa, ka, va, bias)


def _mla_kernel(q_ref, k_ref, v_ref, o_ref):
    for pp in range(MLA_PAIRS_PER_STEP):
        v_pair = v_ref[:, pp * LANES:(pp + 1) * LANES]
        acc = jnp.zeros((q_ref.shape[0], LANES), F32)
        for hh in range(2):
            sl = slice((2 * pp + hh) * LANES, (2 * pp + hh + 1) * LANES)
            s = _dot_nt(q_ref[:, sl], k_ref[:, sl])
            acc = acc + _softmax_pv(s, v_pair, hh, MLA_V_DIM)
        o_ref[:, pp * LANES:(pp + 1) * LANES] = _bf(acc)


def _mla_attention(qm, km, vm, bsz, seq):
    t = qm.shape[0]
    tq = 512
    nq = seq // tq
    groups = MLA_HEADS // (2 * MLA_PAIRS_PER_STEP)
    qk_w = 2 * MLA_PAIRS_PER_STEP * LANES
    v_w = MLA_PAIRS_PER_STEP * LANES
    return pl.pallas_call(
        _mla_kernel,
        out_shape=jax.ShapeDtypeStruct((t, MLA_HEADS * MLA_V_DIM), BF16),
        grid=(bsz, groups, nq),
        in_specs=[
            pl.BlockSpec((tq, qk_w), lambda b, p, i: (b * nq + i, p)),
            pl.BlockSpec((seq, qk_w), lambda b, p, i: (b, p)),
            pl.BlockSpec((seq, v_w), lambda b, p, i: (b, p)),
        ],
        out_specs=pl.BlockSpec((tq, v_w), lambda b, p, i: (b * nq + i, p)),
        compiler_params=pltpu.CompilerParams(
            dimension_semantics=("arbitrary", "arbitrary", "arbitrary"),
            vmem_limit_bytes=V7X_VMEM_LIMIT),
        name="mla_attn",
    )(qm, km, vm)


def _merge_kernel(x_ref, yna_ref, ymla_ref, sgn_ref, sgm_ref, wpn_ref, wpm_ref, wout_ref,
                  gate1_ref, shift2_ref, scale2_ref, g2_ref, wr_ref,
                  x1_ref, h2_ref, lt_ref):
    d = x_ref.shape[1]
    for r0 in range(0, x_ref.shape[0], MERGE_SUB):
        rs = pl.ds(r0, MERGE_SUB)
        merged = (sgn_ref[rs, :].astype(F32) * _dot(yna_ref[rs, :], wpn_ref[...])
                  + sgm_ref[rs, :].astype(F32) * _dot(ymla_ref[rs, :], wpm_ref[...]))
        x1 = x_ref[rs, :] + gate1_ref[0] * _dot(_bf(merged), wout_ref[...])
        x1_ref[rs, :] = x1
        h2 = _rms(x1, d) * g2_ref[...]
        h2 = h2 * (1.0 + scale2_ref[0]) + shift2_ref[0]
        h2_ref[rs, :] = _bf(h2)
        lt_ref[:, rs] = _dot3_nt(wr_ref[...], h2)


def _merge(x2, yna, ymla, sgn, sgm, wpn, wpm, wout, mod3, g2, wr, seq):
    t, d = x2.shape
    tm = 512
    per_b = seq // tm
    n_exp = wr.shape[0]

    def full(a):
        return pl.BlockSpec(a.shape, lambda i: (0,) * a.ndim)

    def rows(w):
        return pl.BlockSpec((tm, w), lambda i: (i, 0))

    def modblk(j):
        return pl.BlockSpec((1, 1, d), lambda i: (i // per_b, 0, j))

    return pl.pallas_call(
        _merge_kernel,
        out_shape=(jax.ShapeDtypeStruct((t, d), F32), jax.ShapeDtypeStruct((t, d), BF16),
                   jax.ShapeDtypeStruct((n_exp, t), F32)),
        grid=(t // tm,),
        in_specs=[rows(d), rows(yna.shape[1]), rows(ymla.shape[1]), rows(d), rows(d),
                  full(wpn), full(wpm), full(wout),
                  modblk(2), modblk(3), modblk(4), full(g2), full(wr)],
        out_specs=(rows(d), rows(d), pl.BlockSpec((n_exp, tm), lambda i: (0, i))),
        compiler_params=pltpu.CompilerParams(dimension_semantics=("arbitrary",),
                                             vmem_limit_bytes=V7X_VMEM_LIMIT),
        name="merge",
    )(x2, yna, ymla, sgn, sgm, wpn, wpm, wout, mod3, mod3, mod3, g2, wr)


def _route_kernel(lt_ref, eb_ref, o_ref, q_ref):
    n_exp, tn = lt_ref.shape
    per_g = n_exp // N_GROUPS
    neg_inf = -jnp.inf
    sc = _sigmoid(lt_ref[...])
    sel = sc + eb_ref[...]
    sc3 = sc.reshape(N_GROUPS, per_g, tn)
    g3 = sel.reshape(N_GROUPS, per_g, tn)
    io = lax.broadcasted_iota(jnp.int32, (N_GROUPS, per_g, tn), 1)
    gio = lax.broadcasted_iota(jnp.int32, (N_GROUPS, per_g, tn), 0)
    eio = gio * per_g + io

    m1 = jnp.max(g3, axis=1, keepdims=True)
    i1 = jnp.min(jnp.where(g3 == m1, io, per_g), axis=1, keepdims=True)
    m2 = jnp.max(jnp.where(io == i1, neg_inf, g3), axis=1, keepdims=True)
    gs = m1 + m2

    g1io = lax.broadcasted_iota(jnp.int32, (N_GROUPS, 1, tn), 0)
    gsel = jnp.zeros((N_GROUPS, 1, tn), F32)
    cur = gs
    for _ in range(TOPK_GROUPS):
        m = jnp.max(cur, axis=0, keepdims=True)
        i = jnp.min(jnp.where(cur == m, g1io, N_GROUPS), axis=0, keepdims=True)
        pick = g1io == i
        gsel = jnp.where(pick, 1.0, gsel)
        cur = jnp.where(pick, neg_inf, cur)

    cur = jnp.where(gsel > 0.0, g3, neg_inf)
    chosen = jnp.zeros((N_GROUPS, per_g, tn), F32)
    for _ in range(TOP_K):
        m = jnp.max(jnp.max(cur, axis=1, keepdims=True), axis=0, keepdims=True)
        cand = jnp.where(cur == m, eio, n_exp)
        i = jnp.min(jnp.min(cand, axis=1, keepdims=True), axis=0, keepdims=True)
        pick = eio == i
        chosen = jnp.where(pick, 1.0, chosen)
        cur = jnp.where(pick, neg_inf, cur)

    w = jnp.where(chosen > 0.0, sc3, 0.0)
    tot = jnp.sum(jnp.sum(w, axis=1, keepdims=True), axis=0, keepdims=True)
    gates = (w / tot * ROUTED_SCALE).reshape(n_exp, tn)
    o_ref[...] = gates
    routed = jnp.where(gates > 0.0, 1.0, 0.0).astype(BF16)
    ones = jnp.ones((8, MOE_TB), BF16)
    for j in range(tn // MOE_TB):
        n_row = _dot_nt(ones, routed[:, j * MOE_TB:(j + 1) * MOE_TB])[0:1]
        q_ref[j] = jnp.floor((n_row + (MOE_CHUNK - 1)) * (1.0 / MOE_CHUNK)).astype(jnp.int32)


def _route(lt, e_bias):
    n_exp, t = lt.shape
    tn = ROUTE_TN
    bps = tn // MOE_TB
    return pl.pallas_call(
        _route_kernel,
        out_shape=(jax.ShapeDtypeStruct((n_exp, t), F32),
                   jax.ShapeDtypeStruct((t // MOE_TB, 1, n_exp), jnp.int32)),
        grid=(t // tn,),
        in_specs=[pl.BlockSpec((n_exp, tn), lambda i: (0, i)),
                  pl.BlockSpec((n_exp, 1), lambda i: (0, 0))],
        out_specs=(pl.BlockSpec((n_exp, tn), lambda i: (0, i)),
                   pl.BlockSpec((bps, 1, n_exp), lambda i: (i, 0, 0))),
        compiler_params=pltpu.CompilerParams(dimension_semantics=("arbitrary",)),
        name="route",
    )(lt, e_bias.reshape(n_exp, 1))


def _for_each_chunk(n, fn):
    def quad(j, carry):
        for u in range(4):
            fn(j * 4 + u)
        return carry
    lax.fori_loop(0, n >> 2, quad, 0)
    base = (n >> 2) << 2
    for u in range(3):
        @pl.when(base + u < n)
        def _():
            fn(base + u)


def _dispatch_kernel(dtab_s, nch_s, total_s,
                     gt_ref, qrow_ref, qbrow_ref, h_ref, xs_ref, stage_ref, zero_ref, sem):
    b = pl.program_id(0)
    nb = pl.num_programs(0)
    slot = lax.rem(b, 2)
    n_exp, tb = gt_ref.shape
    rmax = stage_ref.shape[1] * MOE_CHUNK
    cpg = MOE_ROWGROUP // MOE_CHUNK

    routed = gt_ref[...] > 0.0
    before = (lax.broadcasted_iota(jnp.int32, (tb, tb), 0)
              < lax.broadcasted_iota(jnp.int32, (tb, tb), 1))
    pos = _dot(jnp.where(routed, 1.0, 0.0).astype(BF16), jnp.where(before, 1.0, 0.0).astype(BF16))
    posm = _bf(jnp.where(routed, pos, -1.0))
    qrow = qrow_ref[0]
    qbrow = qbrow_ref[0]
    qbrow_f = qbrow.astype(F32)
    h = h_ref[...]
    def sort_rows(g):
        r0 = g * MOE_ROWGROUP
        chunk = (lax.broadcasted_iota(jnp.int32, (MOE_ROWGROUP, n_exp), 0) + r0) >> MOE_CHUNK_SHIFT
        own = jnp.where(chunk >= qbrow, jnp.where(chunk < qbrow + qrow, 1.0, 0.0), 0.0)
        rank = _dot(_bf(own), posm)
        start = jnp.sum(own * qbrow_f, axis=-1, keepdims=True) * MOE_CHUNK
        rel = (lax.broadcasted_iota(jnp.int32, (MOE_ROWGROUP, 1), 0) + r0).astype(F32) - start
        onehot = jnp.where(rank == rel, 1.0, 0.0).astype(BF16)
        rows = _bf(_dot(onehot, h))
        stage_ref[slot, g * cpg:(g + 1) * cpg] = rows.reshape(cpg, MOE_CHUNK, rows.shape[1])

    n_groups = rmax // MOE_ROWGROUP
    n_sure = min(n_groups, (MOE_TB * TOP_K + MOE_ROWGROUP - 1) // MOE_ROWGROUP)
    for g in range(n_sure):
        sort_rows(g)
    for g in range(n_sure, n_groups):
        @pl.when(g * MOE_ROWGROUP < nch_s[b] * MOE_CHUNK)
        def _():
            sort_rows(g)

    def start_chunk(i):
        pltpu.make_async_copy(stage_ref.at[slot, i], xs_ref.at[dtab_s[b, i]], sem.at[slot]).start()
    _for_each_chunk(nch_s[b], start_chunk)

    def wait_chunks(n, sl):
        def wait_chunk(i):
            pltpu.make_async_copy(zero_ref, xs_ref.at[0], sem.at[sl]).wait()
        _for_each_chunk(n, wait_chunk)

    @pl.when(b > 0)
    def _():
        wait_chunks(nch_s[b - 1], 1 - slot)

    @pl.when(b == nb - 1)
    def _():
        zero_ref[...] = jnp.zeros_like(zero_ref)
        n_tail = MOE_TM // MOE_CHUNK
        for c in range(n_tail):
            pltpu.make_async_copy(zero_ref, xs_ref.at[total_s[0] + c], sem.at[slot]).start()
        wait_chunks(nch_s[b] + n_tail, slot)


def _dispatch(gates_t, q, qbase, dtab, nch, total, h2, n_chunks):
    n_exp, t = gates_t.shape
    d = h2.shape[1]
    nb = t // MOE_TB
    rmax = _moe_stage_rows(n_exp)
    grid_spec = pltpu.PrefetchScalarGridSpec(
        num_scalar_prefetch=3,
        grid=(nb,),
        in_specs=[
            pl.BlockSpec((n_exp, MOE_TB), lambda b, *_: (0, b)),
            pl.BlockSpec((1, 1, n_exp), lambda b, *_: (b, 0, 0)),
            pl.BlockSpec((1, 1, n_exp), lambda b, *_: (b, 0, 0)),
            pl.BlockSpec((MOE_TB, d), lambda b, *_: (b, 0)),
        ],
        out_specs=pl.BlockSpec(memory_space=pl.ANY),
        scratch_shapes=[pltpu.VMEM((2, rmax // MOE_CHUNK, MOE_CHUNK, d), BF16),
                        pltpu.VMEM((MOE_CHUNK, d), BF16), pltpu.SemaphoreType.DMA((2,))],
    )
    return pl.pallas_call(
        _dispatch_kernel,
        out_shape=jax.ShapeDtypeStruct((n_chunks, MOE_CHUNK, d), BF16),
        grid_spec=grid_spec,
        compiler_params=pltpu.CompilerParams(dimension_semantics=("arbitrary",),
                                             vmem_limit_bytes=V7X_VMEM_LIMIT),
        name="moe_dispatch",
    )(dtab, nch, total, gates_t, q.reshape(nb, 1, n_exp), qbase.reshape(nb, 1, n_exp), h2)


def _tile_pieces():
    cpt = MOE_TM // MOE_CHUNK
    return [1 << s for s in range(cpt.bit_length() - 1, -1, -1)]


def _expert_kernel(off_s, len_s, next_s, first_s, xs_ref, wg_ref, wu_ref, wd_ref, ys_ref,
                   xbuf, ybuf, wg_b, wu_b, wd_b, state, sem_in, sem_out):
    e = pl.program_id(0)
    n_exp = pl.num_programs(0)
    cpt = MOE_TM // MOE_CHUNK
    d = xbuf.shape[3]
    pieces = _tile_pieces()

    def tile_in(ee, tt, sl):
        return pltpu.make_async_copy(xs_ref.at[pl.ds(off_s[ee] + tt * cpt, cpt)], xbuf.at[sl],
                                     sem_in.at[sl])

    def for_each_piece(valid, fn):
        for k, piece in enumerate(pieces):
            @pl.when((valid & piece) != 0)
            def _():
                fn(k, piece, valid & ~(2 * piece - 1))

    def tile_out(sl, dst_chunk, k, piece, start):
        return pltpu.make_async_copy(ybuf.at[sl, pl.ds(start, piece)],
                                     ys_ref.at[pl.ds(dst_chunk + start, piece)], sem_out.at[sl, k])

    def drain(sl):
        for_each_piece(state[1 + sl], lambda k, piece, start: tile_out(sl, 0, k, piece, start).wait())
        state[1 + sl] = 0

    def following(ee, tt):
        safe = jnp.minimum(ee, n_exp - 1)
        more = tt + 1 < (len_s[safe] + cpt - 1) // cpt
        nxt_e = jnp.where(ee >= n_exp, n_exp, jnp.where(more, ee, next_s[safe]))
        return nxt_e, jnp.where(more, tt + 1, 0)

    def prefetch(ee, tt, xsl):
        @pl.when(ee < n_exp)
        def _():
            tile_in(ee, tt, xsl).start()

    @pl.when(e == 0)
    def _():
        state[0] = 0
        state[1] = 0
        state[2] = 0
        ahead = (first_s[0], 0)
        for k in range(MOE_XBUFS - 1):
            prefetch(ahead[0], ahead[1], k)
            ahead = following(*ahead)

    n_valid = len_s[e]
    n_tiles = (n_valid + cpt - 1) // cpt

    @pl.when(n_valid > 0)
    def _():
        wg_b[...] = _bf(wg_ref[0])
        wu_b[...] = _bf(wu_ref[0])
        wd_b[...] = _bf(wd_ref[0])

        def tile(t, carry):
            g = state[0]
            xsl = lax.rem(g, MOE_XBUFS)
            sl = g & 1
            tile_in(e, t, xsl).wait()
            ahead = (e, t)
            for _ in range(MOE_XBUFS - 1):
                ahead = following(*ahead)
            prefetch(ahead[0], ahead[1], lax.rem(g + MOE_XBUFS - 1, MOE_XBUFS))

            drain(sl)
            valid = jnp.minimum(n_valid - t * cpt, cpt)

            def ffn(n_c):
                x = xbuf[xsl, :n_c].reshape(n_c * MOE_CHUNK, d)
                a = _silu(_dot(x, wg_b[...])) * _dot(x, wu_b[...])
                ybuf[sl, :n_c] = _bf(_dot(_bf(a), wd_b[...])).reshape(n_c, MOE_CHUNK, d)

            @pl.when(valid > cpt // 2)
            def _():
                ffn(cpt)

            @pl.when(valid <= cpt // 2)
            def _():
                ffn(cpt // 2)

            dst = off_s[e] + t * cpt
            for_each_piece(valid, lambda k, piece, start: tile_out(sl, dst, k, piece, start).start())
            state[1 + sl] = valid
            state[0] = state[0] + 1
            return carry
        lax.fori_loop(0, n_tiles, tile, 0)

    @pl.when(e == n_exp - 1)
    def _():
        drain(0)
        drain(1)


def _experts(off, per_exp, nxt, first, xs, wg, wu, wd):
    n_chunks, _, d = xs.shape
    n_exp, _, ff = wg.shape
    cpt = MOE_TM // MOE_CHUNK

    def w_blk(e, *_):
        return (e, 0, 0)

    grid_spec = pltpu.PrefetchScalarGridSpec(
        num_scalar_prefetch=4,
        grid=(n_exp,),
        in_specs=[pl.BlockSpec(memory_space=pl.ANY),
                  pl.BlockSpec((1, d, ff), w_blk), pl.BlockSpec((1, d, ff), w_blk),
                  pl.BlockSpec((1, ff, d), w_blk)],
        out_specs=pl.BlockSpec(memory_space=pl.ANY),
        scratch_shapes=[pltpu.VMEM((MOE_XBUFS, cpt, MOE_CHUNK, d), BF16),
                        pltpu.VMEM((2, cpt, MOE_CHUNK, d), BF16),
                        pltpu.VMEM((d, ff), BF16), pltpu.VMEM((d, ff), BF16),
                        pltpu.VMEM((ff, d), BF16),
                        pltpu.SMEM((3,), jnp.int32),
                        pltpu.SemaphoreType.DMA((MOE_XBUFS,)),
                        pltpu.SemaphoreType.DMA((2, len(_tile_pieces())))],
    )
    return pl.pallas_call(
        _expert_kernel,
        out_shape=jax.ShapeDtypeStruct((n_chunks, MOE_CHUNK, d), BF16),
        grid_spec=grid_spec,
        compiler_params=pltpu.CompilerParams(dimension_semantics=("arbitrary",),
                                             vmem_limit_bytes=V7X_VMEM_LIMIT),
        name="moe_experts",
    )(off, per_exp, nxt, first, xs, wg, wu, wd)


def _combine_kernel(dtab_s, nch_s,
                    gt_ref, qcol_ref, qbcol_ref, h_ref, x1_ref, gate2_ref,
                    wsg_ref, wsu_ref, wsd_ref, ys_ref, o_ref, stage_ref, sem):
    b = pl.program_id(0)
    nb = pl.num_programs(0)
    slot = lax.rem(b, 2)
    n_exp, tb = gt_ref.shape
    rmax = stage_ref.shape[1] * MOE_CHUNK

    def fetch(bb, sl):
        def start_chunk(i):
            pltpu.make_async_copy(ys_ref.at[dtab_s[bb, i]], stage_ref.at[sl, i], sem.at[sl]).start()
        _for_each_chunk(nch_s[bb], start_chunk)

    @pl.when(b == 0)
    def _():
        stage_ref[...] = jnp.zeros_like(stage_ref)
        fetch(0, 0)

    def wait_chunk(i):
        pltpu.make_async_copy(ys_ref.at[0], stage_ref.at[slot, 0], sem.at[slot]).wait()
    _for_each_chunk(nch_s[b], wait_chunk)

    @pl.when(b + 1 < nb)
    def _():
        fetch(b + 1, 1 - slot)

    gt = gt_ref[...]
    routed = jnp.where(gt > 0.0, 1.0, 0.0).astype(BF16)
    i0 = lax.broadcasted_iota(jnp.int32, (tb, tb), 0)
    i1 = lax.broadcasted_iota(jnp.int32, (tb, tb), 1)
    eye = jnp.where(i0 == i1, 1.0, 0.0).astype(BF16)
    routed_t = _dot_nt(eye, routed)
    gates_tok = _dot_nt(eye, _bf(gt))
    earlier = jnp.where(i1 < i0, 1.0, 0.0).astype(BF16)
    pos_t = _dot(earlier, _bf(routed_t))
    posm_t = _bf(jnp.where(routed_t > 0.0, pos_t, -1.0))

    qcol = qcol_ref[0]
    qbcol = qbcol_ref[0]
    h = h_ref[...]
    a = _silu(_dot(h, wsg_ref[...])) * _dot(h, wsu_ref[...])
    shared = _dot(_bf(a), wsd_ref[...])

    def finish(rows):
        chunk = lax.broadcasted_iota(jnp.int32, (n_exp, rows), 1) >> MOE_CHUNK_SHIFT
        own = jnp.where(chunk >= qbcol, jnp.where(chunk < qbcol + qcol, 1.0, 0.0), 0.0)
        own_b = _bf(own)
        rank = _dot(posm_t, own_b)
        wexp = _dot(_bf(gates_tok), own_b)
        start = jnp.sum(own * qbcol.astype(F32), axis=0, keepdims=True) * MOE_CHUNK
        rel = lax.broadcasted_iota(jnp.int32, (1, rows), 1).astype(F32) - start
        weights = _bf(jnp.where(rank == rel, wexp, 0.0))
        staged = stage_ref[slot, :rows // MOE_CHUNK].reshape(rows, o_ref.shape[1])
        o_ref[...] = x1_ref[...] + gate2_ref[0] * (_dot(weights, staged) + shared)

    short = rmax - MOE_ROWGROUP

    @pl.when(nch_s[b] * MOE_CHUNK <= short)
    def _():
        finish(short)

    @pl.when(nch_s[b] * MOE_CHUNK > short)
    def _():
        finish(rmax)


def _combine(gates_t, q, qbase, dtab, nch, h2, x1, mod3, wsg, wsu, wsd, ys, seq):
    n_exp, t = gates_t.shape
    d = h2.shape[1]
    nb = t // MOE_TB
    per_b = seq // MOE_TB
    stage_chunks = _moe_stage_rows(n_exp) // MOE_CHUNK

    def full(a):
        return pl.BlockSpec(a.shape, lambda b, *_: (0,) * a.ndim)

    def rows(w):
        return pl.BlockSpec((MOE_TB, w), lambda b, *_: (b, 0))

    grid_spec = pltpu.PrefetchScalarGridSpec(
        num_scalar_prefetch=2,
        grid=(nb,),
        in_specs=[
            pl.BlockSpec((n_exp, MOE_TB), lambda b, *_: (0, b)),
            pl.BlockSpec((1, n_exp, 1), lambda b, *_: (b, 0, 0)),
            pl.BlockSpec((1, n_exp, 1), lambda b, *_: (b, 0, 0)),
            rows(d), rows(d),
            pl.BlockSpec((1, 1, d), lambda b, *_: (b // per_b, 0, 5)),
            full(wsg), full(wsu), full(wsd),
            pl.BlockSpec(memory_space=pl.ANY),
        ],
        out_specs=rows(d),
        scratch_shapes=[pltpu.VMEM((2, stage_chunks, MOE_CHUNK, d), BF16),
                        pltpu.SemaphoreType.DMA((2,))],
    )
    return pl.pallas_call(
        _combine_kernel,
        out_shape=jax.ShapeDtypeStruct((t, d), F32),
        grid_spec=grid_spec,
        compiler_params=pltpu.CompilerParams(dimension_semantics=("arbitrary",),
                                             vmem_limit_bytes=V7X_VMEM_LIMIT),
        name="moe_combine",
    )(dtab, nch, gates_t, q.reshape(nb, n_exp, 1), qbase.reshape(nb, n_exp, 1),
      h2, x1, mod3, wsg, wsu, wsd, ys)


def _moe_plan(q):
    nb, n_exp = q.shape
    qbase = jnp.cumsum(q, axis=1) - q
    nch = jnp.sum(q, axis=1)
    per_exp = jnp.sum(q, axis=0)
    off = jnp.cumsum(per_exp) - per_exp
    dstq = off[None, :] + jnp.cumsum(q, axis=0) - q
    i = jnp.arange(_moe_block_chunks(n_exp), dtype=jnp.int32)
    ii = i[None, :, None]
    owned = (ii >= qbase[:, None, :]) & (ii < (qbase + q)[:, None, :])
    dtab = i[None, :] + jnp.sum(jnp.where(owned, (dstq - qbase)[:, None, :], 0), axis=2)
    ids = jnp.arange(n_exp, dtype=jnp.int32)
    later = (ids[None, :] > ids[:, None]) & (per_exp[None, :] > 0)
    nxt = jnp.min(jnp.where(later, ids[None, :], n_exp), axis=1)
    first = jnp.min(jnp.where(per_exp > 0, ids, n_exp)).reshape(1)
    total = jnp.sum(per_exp).reshape(1)
    return qbase, dtab, nch, off, per_exp, nxt, first, total


def _moe_block_chunks(n_exp):
    return MOE_TB * TOP_K // MOE_CHUNK + n_exp


def _moe_stage_rows(n_exp):
    return -(-_moe_block_chunks(n_exp) * MOE_CHUNK // MOE_ROWGROUP) * MOE_ROWGROUP


def _moe(h2, gates_t, q3, x1, mod3, wsg, wsu, wsd, wg, wu, wd, seq):
    n_exp, t = gates_t.shape
    nb = t // MOE_TB
    q = q3.reshape(nb, n_exp)
    qbase, dtab, nch, off, per_exp, nxt, first, total = _moe_plan(q)
    n_chunks = nb * _moe_block_chunks(n_exp) + MOE_TM // MOE_CHUNK
    xs = _dispatch(gates_t, q, qbase, dtab, nch, total, h2, n_chunks)
    ys = _experts(off, per_exp, nxt, first, xs, wg, wu, wd)
    return _combine(gates_t, q, qbase, dtab, nch, h2, x1, mod3, wsg, wsu, wsd, ys, seq)


def _pad_heads(w, heads, width):
    lead = w.shape[:-1]
    w = w.reshape(lead + (heads, width))
    w = jnp.pad(w, [(0, 0)] * len(lead) + [(0, 0), (0, LANES - width)])
    return w.reshape(lead + (heads * LANES,))


def kernel(x, c, positions, w_ada, b_ada, g_norm1, w_in, g_na_q, g_na_k, na_rpb, g_q_lat, w_uq,
           g_kv_lat, w_ukv, g_mla_q, g_mla_k, w_proj_na, w_proj_mla, w_out, g_norm2, w_router,
           e_bias, w_exp_gate, w_exp_up, w_exp_down, w_sh_gate, w_sh_up, w_sh_down):
    bsz, seq, d = x.shape
    t = bsz * seq
    depth = w_ada.shape[0]
    na_w = NA_HEADS * NA_HEAD_DIM
    q_rank = g_q_lat.shape[1]
    kv_rank = g_kv_lat.shape[1]
    n_rows = seq // GRID_W

    pos = positions.reshape(1, t)
    half = MLA_ROPE_DIM // 2
    freq = (ROPE_THETA ** (-jnp.arange(half, dtype=F32) / half)).reshape(half, 1)

    x2 = x.reshape(t, d)
    for l in range(depth):
        mod3 = _adaln(c, w_ada[l], b_ada[l]).reshape(bsz, 1, 6 * d)

        wi = w_in[l]
        o_lat = 3 * na_w
        o_rot = o_lat + q_rank + kv_rank
        o_gate = o_rot + MLA_ROPE_DIM
        wqkv = _bf(wi[:, :o_lat])
        w_rot = jnp.pad(wi[:, o_rot:o_gate], ((0, 0), (MLA_NOPE_DIM, LANES - MLA_QK_DIM)))
        wlat = _bf(jnp.concatenate([wi[:, o_lat:o_rot], w_rot], axis=1))
        wgate = _bf(wi[:, o_gate:])
        gq = (jnp.tile(g_na_q[l], NA_HEADS) * (NA_HEAD_DIM ** -0.5 * LOG2E)).reshape(1, na_w)
        gk = jnp.tile(g_na_k[l], NA_HEADS).reshape(1, na_w)
        wuq = _bf(_pad_heads(w_uq[l], MLA_HEADS, MLA_QK_DIM))
        wukv = w_ukv[l].reshape(kv_rank, MLA_HEADS, MLA_NOPE_DIM + MLA_V_DIM)
        wuk = _bf(_pad_heads(wukv[:, :, :MLA_NOPE_DIM].reshape(kv_rank, -1), MLA_HEADS, MLA_NOPE_DIM))
        wuv = _bf(wukv[:, :, MLA_NOPE_DIM:].reshape(kv_rank, MLA_HEADS * MLA_V_DIM))
        gmq = _pad_heads(jnp.tile(g_mla_q[l], MLA_HEADS) * (MLA_QK_DIM ** -0.5 * LOG2E),
                         MLA_HEADS, MLA_QK_DIM).reshape(1, -1)
        gmk = _pad_heads(jnp.tile(g_mla_k[l], MLA_HEADS), MLA_HEADS, MLA_QK_DIM).reshape(1, -1)

        qa, ka, va, qm, km, vm, sgn, sgm = _inproj(
            x2, mod3, g_norm1[l].reshape(1, d), wqkv, wlat, wgate, gq, gk,
            g_q_lat[l].reshape(1, q_rank), g_kv_lat[l].reshape(1, kv_rank), wuq, wuk, wuv,
            gmq, gmk, pos, freq, seq)

        bias = _na_bias(na_rpb[l], n_rows)
        y_na = _na_attention(qa, ka, va, bias, bsz, seq)
        y_mla = _mla_attention(qm, km, vm, bsz, seq)

        x1, h2, lt = _merge(x2, y_na, y_mla, sgn, sgm, _bf(w_proj_na[l]), _bf(w_proj_mla[l]),
                            _bf(w_out[l]), mod3, g_norm2[l].reshape(1, d), w_router[l].T, seq)
        gates_t, q3 = _route(lt, e_bias[l])
        x2 = _moe(h2, gates_t, q3, x1, mod3, _bf(w_sh_gate[l]), _bf(w_sh_up[l]),
                  _bf(w_sh_down[l]), w_exp_gate[l], w_exp_up[l], w_exp_down[l], seq)
    return x2.reshape(bsz, seq, d)
```

```python
import functools

import jax
import jax.numpy as jnp
import numpy as np
from jax import lax
from jax.experimental import pallas as pl
from jax.experimental.pallas import tpu as pltpu

GRID_W = 64
NA_HEADS = 8
NA_HEAD_DIM = 64
NA_WIN_ROWS = 8
NA_WIN_COLS = 16
MLA_HEADS = 8
MLA_NOPE_DIM = 64
MLA_ROPE_DIM = 32
MLA_V_DIM = 64
MLA_QK_DIM = MLA_NOPE_DIM + MLA_ROPE_DIM
ROPE_THETA = 10000.0
N_GROUPS = 8
TOPK_GROUPS = 4
TOP_K = 8
ROUTED_SCALE = 2.5
EPS = 1e-6
NEG_BIG = -1e30

LANES = 128
V7X_VMEM_LIMIT = 56 * 1024 * 1024

NA_QROWS = 4
NA_BAND = 12
NA_BLOCK_TYPES = 3
MERGE_SUB = 512
INPROJ_TM = 512
INPROJ_SUB = 256
MLA_PAIRS_PER_STEP = 4
LOG2E = 1.4426950408889634
MOE_TB = 256
ROUTE_TN = 1024
MOE_CHUNK_SHIFT = 4
MOE_CHUNK = 1 << MOE_CHUNK_SHIFT
MOE_TM = 1024
MOE_ROWGROUP = 512
MOE_COPY_UNROLL = 4
MOE_XBUFS = 3

F32 = jnp.float32
BF16 = jnp.bfloat16


def _bf(x):
    return x.astype(BF16)


def _dot(a, b):
    return jnp.dot(a, b, preferred_element_type=F32)


def _dot_nt(a, b):
    return lax.dot_general(a, b, (((1,), (1,)), ((), ())), preferred_element_type=F32)


def _split(x):
    hi = _bf(x)
    lo = _bf(x - hi.astype(F32))
    return hi, lo


def _dot3(a, b):
    ah, al = _split(a)
    bh, bl = _split(b)
    return _dot(ah, bh) + (_dot(ah, bl) + _dot(al, bh))


def _dot3_nt(a, b):
    ah, al = _split(a)
    bh, bl = _split(b)
    return _dot_nt(ah, bh) + (_dot_nt(ah, bl) + _dot_nt(al, bh))


def _sigmoid(x):
    return 1.0 / (1.0 + jnp.exp(-x))


def _silu(x):
    return x * _sigmoid(x)


def _rms(x, n):
    ss = jnp.sum(x * x, axis=-1, keepdims=True)
    return x * lax.rsqrt(ss * (1.0 / n) + EPS)


def _adaln_kernel(c_ref, w_ref, b_ref, o_ref):
    c = c_ref[...]
    o_ref[...] = _dot3(_silu(c), w_ref[...]) + b_ref[...]


def _adaln(c, w, b):
    bsz, d = c.shape
    n = w.shape[1]
    tn = 1024
    return pl.pallas_call(
        _adaln_kernel,
        out_shape=jax.ShapeDtypeStruct((bsz, n), F32),
        grid=(n // tn,),
        in_specs=[
            pl.BlockSpec((bsz, d), lambda j: (0, 0)),
            pl.BlockSpec((d, tn), lambda j: (0, j)),
            pl.BlockSpec((1, tn), lambda j: (0, j)),
        ],
        out_specs=pl.BlockSpec((bsz, tn), lambda j: (0, j)),
        compiler_params=pltpu.CompilerParams(dimension_semantics=("arbitrary",)),
        name="adaln",
    )(c, w, b.reshape(1, n))


def _inproj_kernel(x_ref, shift_ref, scale_ref, g1_ref, wqkv_ref, wlat_ref, wgate_ref,
                   gq_ref, gk_ref, gql_ref, gkvl_ref, wuq_ref, wuk_ref, wuv_ref,
                   gmq_ref, gmk_ref, pos_ref, freq_ref,
                   qa_ref, ka_ref, va_ref, qm_ref, km_ref, vm_ref, sgn_ref, sgm_ref):
    for r0 in range(0, x_ref.shape[0], INPROJ_SUB):
        _inproj_rows(pl.ds(r0, INPROJ_SUB), x_ref, shift_ref, scale_ref, g1_ref, wqkv_ref,
                     wlat_ref, wgate_ref, gq_ref, gk_ref, gql_ref, gkvl_ref, wuq_ref, wuk_ref,
                     wuv_ref, gmq_ref, gmk_ref, pos_ref, freq_ref, qa_ref, ka_ref, va_ref,
                     qm_ref, km_ref, vm_ref, sgn_ref, sgm_ref)


def _inproj_rows(rs, x_ref, shift_ref, scale_ref, g1_ref, wqkv_ref, wlat_ref, wgate_ref,
                 gq_ref, gk_ref, gql_ref, gkvl_ref, wuq_ref, wuk_ref, wuv_ref,
                 gmq_ref, gmk_ref, pos_ref, freq_ref,
                 qa_ref, ka_ref, va_ref, qm_ref, km_ref, vm_ref, sgn_ref, sgm_ref):
    d = x_ref.shape[1]
    x = x_ref[rs, :]
    h = _rms(x, d) * g1_ref[...]
    h = h * (1.0 + scale_ref[0]) + shift_ref[0]
    hb = _bf(h)

    qkv = _dot(hb, wqkv_ref[...])
    lat = _dot(hb, wlat_ref[...])
    gts = _dot(hb, wgate_ref[...])
    sgn_ref[rs, :] = _bf(_sigmoid(gts[:, :d]))
    sgm_ref[rs, :] = _bf(_sigmoid(gts[:, d:]))

    na_w = NA_HEADS * NA_HEAD_DIM
    lane = lax.broadcasted_iota(jnp.int32, (1, LANES), 1)
    lo_half = lane < NA_HEAD_DIM
    for p in range(na_w // LANES):
        sl = slice(p * LANES, (p + 1) * LANES)
        for src_off, g_ref, dst_ref in ((0, gq_ref, qa_ref), (na_w, gk_ref, ka_ref)):
            t = qkv[:, src_off + p * LANES: src_off + (p + 1) * LANES]
            sq = t * t
            s_lo = jnp.sum(jnp.where(lo_half, sq, 0.0), axis=-1, keepdims=True)
            s_hi = jnp.sum(jnp.where(lo_half, 0.0, sq), axis=-1, keepdims=True)
            r = jnp.where(lo_half,
                          lax.rsqrt(s_lo * (1.0 / NA_HEAD_DIM) + EPS),
                          lax.rsqrt(s_hi * (1.0 / NA_HEAD_DIM) + EPS))
            dst_ref[rs, sl] = _bf(t * r * g_ref[:, sl])
    va_ref[rs, :] = _bf(qkv[:, 2 * na_w: 3 * na_w])

    q_rank = gql_ref.shape[1]
    kv_rank = gkvl_ref.shape[1]
    qln = _rms(lat[:, :q_rank], q_rank) * gql_ref[...]
    kvn = _bf(_rms(lat[:, q_rank:q_rank + kv_rank], kv_rank) * gkvl_ref[...])
    qpre = _dot(_bf(qln), wuq_ref[...])
    knope = _dot(kvn, wuk_ref[...])
    vm_ref[rs, :] = _bf(_dot(kvn, wuv_ref[...]))
    krot = lat[:, q_rank + kv_rank:]

    tm = INPROJ_SUB
    half = MLA_ROPE_DIM // 2
    ang_t = freq_ref[...] * pos_ref[:, rs].astype(F32)
    cos_t = jnp.cos(ang_t)
    sin_t = jnp.sin(ang_t)
    l_i = lax.broadcasted_iota(jnp.int32, (LANES, half), 0)
    j_i = lax.broadcasted_iota(jnp.int32, (LANES, half), 1)
    hit = jnp.where((l_i >= MLA_NOPE_DIM) & (l_i < MLA_QK_DIM)
                    & (((l_i - MLA_NOPE_DIM) & (half - 1)) == j_i), 1.0, 0.0)
    first_half = l_i < MLA_NOPE_DIM + half
    eye = jnp.where(lax.broadcasted_iota(jnp.int32, (tm, tm), 0)
                    == lax.broadcasted_iota(jnp.int32, (tm, tm), 1), 1.0, 0.0).astype(BF16)

    def table(sel, vals, fill_nope):
        hi, lo = _split(vals)
        w = _dot(_bf(sel), hi) + _dot(_bf(sel), lo)
        if fill_nope:
            w = jnp.where(lax.broadcasted_iota(jnp.int32, (LANES, tm), 0) < MLA_NOPE_DIM, 1.0, w)
        hi, lo = _split(w)
        return _dot_nt(eye, hi) + _dot_nt(eye, lo)

    c_tab = table(hit, cos_t, True)
    s_up = table(jnp.where(first_half, 0.0, hit), sin_t, False)
    s_dn = table(jnp.where(first_half, -hit, 0.0), sin_t, False)

    def rope(t):
        return t * c_tab + pltpu.roll(t, half, 1) * s_up + pltpu.roll(t, LANES - half, 1) * s_dn

    kr = rope(krot)
    for hd in range(MLA_HEADS):
        sl = slice(hd * LANES, (hd + 1) * LANES)
        qh = rope(qpre[:, sl])
        qm_ref[rs, sl] = _bf(_rms(qh, MLA_QK_DIM) * gmq_ref[:, sl])
        kh = knope[:, sl] + kr
        km_ref[rs, sl] = _bf(_rms(kh, MLA_QK_DIM) * gmk_ref[:, sl])


def _inproj(x2, mod3, g1, wqkv, wlat, wgate, gq, gk, gql, gkvl, wuq, wuk, wuv, gmq, gmk,
            pos, freq, seq):
    t, d = x2.shape
    tm = INPROJ_TM
    per_b = seq // tm
    na_w = NA_HEADS * NA_HEAD_DIM
    mla_w = MLA_HEADS * LANES
    v_w = MLA_HEADS * MLA_V_DIM

    def full(a):
        return pl.BlockSpec(a.shape, lambda i: (0,) * a.ndim)

    def rows(w):
        return pl.BlockSpec((tm, w), lambda i: (i, 0))

    out_shapes = (
        jax.ShapeDtypeStruct((t, na_w), BF16), jax.ShapeDtypeStruct((t, na_w), BF16),
        jax.ShapeDtypeStruct((t, na_w), BF16),
        jax.ShapeDtypeStruct((t, mla_w), BF16), jax.ShapeDtypeStruct((t, mla_w), BF16),
        jax.ShapeDtypeStruct((t, v_w), BF16),
        jax.ShapeDtypeStruct((t, d), BF16), jax.ShapeDtypeStruct((t, d), BF16),
    )
    return pl.pallas_call(
        _inproj_kernel,
        out_shape=out_shapes,
        grid=(t // tm,),
        in_specs=[
            rows(d),
            pl.BlockSpec((1, 1, d), lambda i: (i // per_b, 0, 0)),
            pl.BlockSpec((1, 1, d), lambda i: (i // per_b, 0, 1)),
            full(g1), full(wqkv), full(wlat), full(wgate), full(gq), full(gk), full(gql),
            full(gkvl), full(wuq), full(wuk), full(wuv), full(gmq), full(gmk),
            pl.BlockSpec((1, tm), lambda i: (0, i)),
            full(freq),
        ],
        out_specs=(rows(na_w), rows(na_w), rows(na_w), rows(mla_w), rows(mla_w), rows(v_w),
                   rows(d), rows(d)),
        compiler_params=pltpu.CompilerParams(dimension_semantics=("arbitrary",),
                                             vmem_limit_bytes=V7X_VMEM_LIMIT),
        name="inproj",
    )(x2, mod3, mod3, g1, wqkv, wlat, wgate, gq, gk, gql, gkvl, wuq, wuk, wuv, gmq, gmk,
      pos, freq)


def _na_block_geometry(block_type, n_rows):
    if block_type == 0:
        return 0, 0
    if block_type == 1:
        r0 = NA_QROWS
        return r0, r0 - NA_WIN_ROWS // 2
    return n_rows - NA_QROWS, n_rows - NA_BAND


def _na_bias_kernel(rpb_ref, o_ref, m_ref, *, n_rows):
    hd = pl.program_id(0)
    n_dr = 2 * NA_WIN_ROWS - 1
    n_dc = 2 * NA_WIN_COLS - 1
    qc = lax.broadcasted_iota(jnp.int32, (GRID_W, LANES), 0)
    kc = lax.broadcasted_iota(jnp.int32, (GRID_W, LANES), 1) & (GRID_W - 1)
    dc = jnp.clip(kc - qc, -(NA_WIN_COLS - 1), NA_WIN_COLS - 1) + (NA_WIN_COLS - 1)
    cstart = jnp.clip(qc - NA_WIN_COLS // 2, 0, GRID_W - NA_WIN_COLS)
    col_ok = (kc >= cstart) & (kc < cstart + NA_WIN_COLS)
    for i_dr in range(n_dr):
        acc = jnp.zeros((GRID_W, LANES), F32)
        for t in range(n_dc):
            acc = jnp.where(dc == t, rpb_ref[hd, i_dr * n_dc + t], acc)
        m_ref[i_dr] = jnp.where(col_ok, acc * LOG2E, NEG_BIG)
    neg = jnp.full((GRID_W, LANES), NEG_BIG, F32)
    lo_half = lax.broadcasted_iota(jnp.int32, (GRID_W, LANES), 1) < GRID_W
    kh = NA_WIN_ROWS
    for bt in range(NA_BLOCK_TYPES):
        r0, start = _na_block_geometry(bt, n_rows)
        for i in range(NA_QROWS):
            r = r0 + i
            rs = min(max(r - kh // 2, 0), n_rows - kh)
            for jp in range(NA_BAND // 2):
                halves = []
                for j in (2 * jp, 2 * jp + 1):
                    krow = start + j
                    if rs <= krow < rs + kh:
                        halves.append(m_ref[krow - r + (NA_WIN_ROWS - 1)])
                    else:
                        halves.append(neg)
                tile = jnp.where(lo_half, halves[0], halves[1])
                o_ref[bt, 0, i * GRID_W:(i + 1) * GRID_W, jp * LANES:(jp + 1) * LANES] = tile


def _na_bias(rpb, n_rows):
    heads = rpb.shape[0]
    nq = NA_QROWS * GRID_W
    nk = NA_BAND * GRID_W
    rpb2 = rpb.reshape(heads, -1)
    return pl.pallas_call(
        functools.partial(_na_bias_kernel, n_rows=n_rows),
        out_shape=jax.ShapeDtypeStruct((NA_BLOCK_TYPES, heads, nq, nk), F32),
        grid=(heads,),
        in_specs=[pl.BlockSpec(memory_space=pltpu.SMEM)],
        out_specs=pl.BlockSpec((NA_BLOCK_TYPES, 1, nq, nk), lambda hd: (0, hd, 0, 0)),
        scratch_shapes=[pltpu.VMEM((2 * NA_WIN_ROWS - 1, GRID_W, LANES), F32)],
        compiler_params=pltpu.CompilerParams(dimension_semantics=("arbitrary",)),
        name="na_bias",
    )(rpb2)


def _softmax_pv(s, v_pair, hh, half):
    lane = lax.broadcasted_iota(jnp.int32, (1, LANES), 1)
    mine = (lane < half) if hh == 0 else (lane >= half)
    den_lane = half if hh == 0 else 0
    m = jnp.max(s, axis=-1, keepdims=True)
    p = _bf(jnp.exp2(s - m))
    ones_row = jnp.where(lane == den_lane, 1.0, 0.0).astype(BF16)
    o = _dot(p, jnp.where(mine, v_pair, ones_row))
    den = jnp.sum(jnp.where(lane == den_lane, o, 0.0), axis=-1, keepdims=True)
    return jnp.where(mine, o / den, 0.0)


def _na_kernel(q_ref, k_ref, v_ref, bias_ref, o_ref, *, n_blocks, n_rows):
    blk = pl.program_id(1)
    start_row = jnp.where(blk == 0, 0,
                          jnp.where(blk == n_blocks - 1, n_rows - NA_BAND,
                                    blk * NA_QROWS - NA_WIN_ROWS // 2))
    off = pl.multiple_of(start_row * GRID_W, GRID_W)
    nk = NA_BAND * GRID_W
    tq = q_ref.shape[0]
    lo_half = lax.broadcasted_iota(jnp.int32, (1, LANES), 1) < NA_HEAD_DIM
    for p in range(NA_HEADS * NA_HEAD_DIM // LANES):
        sl = slice(p * LANES, (p + 1) * LANES)
        qp = q_ref[:, sl]
        kb = k_ref[pl.ds(off, nk), sl]
        vb = v_ref[pl.ds(off, nk), sl]
        zero = jnp.zeros_like(qp)
        q2 = jnp.concatenate([jnp.where(lo_half, qp, zero), jnp.where(lo_half, zero, qp)], axis=0)
        s = _dot_nt(q2, kb) + bias_ref[0, 2 * p:2 * p + 2].reshape(2 * tq, nk)
        m = jnp.max(s, axis=-1, keepdims=True)
        e = jnp.exp2(s - m)
        den = jnp.sum(e, axis=-1, keepdims=True)
        o = _dot(_bf(e), vb) / den
        o_ref[:, sl] = _bf(jnp.where(lo_half, o[:tq], o[tq:]))


def _na_attention(qa, ka, va, bias, bsz, seq):
    t, w = qa.shape
    n_rows = seq // GRID_W
    n_blocks = n_rows // NA_QROWS
    tq = NA_QROWS * GRID_W
    nk = NA_BAND * GRID_W
    heads = bias.shape[1]

    def btype(blk):
        return jnp.where(blk == 0, 0, jnp.where(blk == n_blocks - 1, 2, 1))

    return pl.pallas_call(
        functools.partial(_na_kernel, n_blocks=n_blocks, n_rows=n_rows),
        out_shape=jax.ShapeDtypeStruct((t, w), BF16),
        grid=(bsz, n_blocks),
        in_specs=[
            pl.BlockSpec((tq, w), lambda b, blk: (b * n_blocks + blk, 0)),
            pl.BlockSpec((seq, w), lambda b, blk: (b, 0)),
            pl.BlockSpec((seq, w), lambda b, blk: (b, 0)),
            pl.BlockSpec((1, heads, tq, nk), lambda b, blk: (btype(blk), 0, 0, 0)),
        ],
        out_specs=pl.BlockSpec((tq, w), lambda b, blk: (b * n_blocks + blk, 0)),
        compiler_params=pltpu.CompilerParams(dimension_semantics=("arbitrary", "arbitrary"),
                                             vmem_limit_bytes=V7X_VMEM_LIMIT),
        name="na_attn",
    )(qa, ka, va, bias)


def _mla_kernel(q_ref, k_ref, v_ref, o_ref):
    for pp in range(MLA_PAIRS_PER_STEP):
        v_pair = v_ref[:, pp * LANES:(pp + 1) * LANES]
        acc = jnp.zeros((q_ref.shape[0], LANES), F32)
        for hh in range(2):
            sl = slice((2 * pp + hh) * LANES, (2 * pp + hh + 1) * LANES)
            s = _dot_nt(q_ref[:, sl], k_ref[:, sl])
            acc = acc + _softmax_pv(s, v_pair, hh, MLA_V_DIM)
        o_ref[:, pp * LANES:(pp + 1) * LANES] = _bf(acc)


def _mla_attention(qm, km, vm, bsz, seq):
    t = qm.shape[0]
    tq = 512
    nq = seq // tq
    groups = MLA_HEADS // (2 * MLA_PAIRS_PER_STEP)
    qk_w = 2 * MLA_PAIRS_PER_STEP * LANES
    v_w = MLA_PAIRS_PER_STEP * LANES
    return pl.pallas_call(
        _mla_kernel,
        out_shape=jax.ShapeDtypeStruct((t, MLA_HEADS * MLA_V_DIM), BF16),
        grid=(bsz, groups, nq),
        in_specs=[
            pl.BlockSpec((tq, qk_w), lambda b, p, i: (b * nq + i, p)),
            pl.BlockSpec((seq, qk_w), lambda b, p, i: (b, p)),
            pl.BlockSpec((seq, v_w), lambda b, p, i: (b, p)),
        ],
        out_specs=pl.BlockSpec((tq, v_w), lambda b, p, i: (b * nq + i, p)),
        compiler_params=pltpu.CompilerParams(
            dimension_semantics=("arbitrary", "arbitrary", "arbitrary"),
            vmem_limit_bytes=V7X_VMEM_LIMIT),
        name="mla_attn",
    )(qm, km, vm)


def _merge_kernel(x_ref, yna_ref, ymla_ref, sgn_ref, sgm_ref, wpn_ref, wpm_ref, wout_ref,
                  gate1_ref, shift2_ref, scale2_ref, g2_ref, wr_ref,
                  x1_ref, h2_ref, lt_ref):
    d = x_ref.shape[1]
    for r0 in range(0, x_ref.shape[0], MERGE_SUB):
        rs = pl.ds(r0, MERGE_SUB)
        merged = (sgn_ref[rs, :].astype(F32) * _dot(yna_ref[rs, :], wpn_ref[...])
                  + sgm_ref[rs, :].astype(F32) * _dot(ymla_ref[rs, :], wpm_ref[...]))
        x1 = x_ref[rs, :] + gate1_ref[0] * _dot(_bf(merged), wout_ref[...])
        x1_ref[rs, :] = x1
        h2 = _rms(x1, d) * g2_ref[...]
        h2 = h2 * (1.0 + scale2_ref[0]) + shift2_ref[0]
        h2_ref[rs, :] = _bf(h2)
        lt_ref[:, rs] = _dot3_nt(wr_ref[...], h2)


def _merge(x2, yna, ymla, sgn, sgm, wpn, wpm, wout, mod3, g2, wr, seq):
    t, d = x2.shape
    tm = 512
    per_b = seq // tm
    n_exp = wr.shape[0]

    def full(a):
        return pl.BlockSpec(a.shape, lambda i: (0,) * a.ndim)

    def rows(w):
        return pl.BlockSpec((tm, w), lambda i: (i, 0))

    def modblk(j):
        return pl.BlockSpec((1, 1, d), lambda i: (i // per_b, 0, j))

    return pl.pallas_call(
        _merge_kernel,
        out_shape=(jax.ShapeDtypeStruct((t, d), F32), jax.ShapeDtypeStruct((t, d), BF16),
                   jax.ShapeDtypeStruct((n_exp, t), F32)),
        grid=(t // tm,),
        in_specs=[rows(d), rows(yna.shape[1]), rows(ymla.shape[1]), rows(d), rows(d),
                  full(wpn), full(wpm), full(wout),
                  modblk(2), modblk(3), modblk(4), full(g2), full(wr)],
        out_specs=(rows(d), rows(d), pl.BlockSpec((n_exp, tm), lambda i: (0, i))),
        compiler_params=pltpu.CompilerParams(dimension_semantics=("arbitrary",),
                                             vmem_limit_bytes=V7X_VMEM_LIMIT),
        name="merge",
    )(x2, yna, ymla, sgn, sgm, wpn, wpm, wout, mod3, mod3, mod3, g2, wr)


def _route_kernel(lt_ref, eb_ref, o_ref, q_ref):
    n_exp, tn = lt_ref.shape
    per_g = n_exp // N_GROUPS
    neg_inf = -jnp.inf
    sc = _sigmoid(lt_ref[...])
    sel = sc + eb_ref[...]
    sc3 = sc.reshape(N_GROUPS, per_g, tn)
    g3 = sel.reshape(N_GROUPS, per_g, tn)
    io = lax.broadcasted_iota(jnp.int32, (N_GROUPS, per_g, tn), 1)
    gio = lax.broadcasted_iota(jnp.int32, (N_GROUPS, per_g, tn), 0)
    eio = gio * per_g + io

    m1 = jnp.max(g3, axis=1, keepdims=True)
    i1 = jnp.min(jnp.where(g3 == m1, io, per_g), axis=1, keepdims=True)
    m2 = jnp.max(jnp.where(io == i1, neg_inf, g3), axis=1, keepdims=True)
    gs = m1 + m2

    g1io = lax.broadcasted_iota(jnp.int32, (N_GROUPS, 1, tn), 0)
    gsel = jnp.zeros((N_GROUPS, 1, tn), F32)
    cur = gs
    for _ in range(TOPK_GROUPS):
        m = jnp.max(cur, axis=0, keepdims=True)
        i = jnp.min(jnp.where(cur == m, g1io, N_GROUPS), axis=0, keepdims=True)
        pick = g1io == i
        gsel = jnp.where(pick, 1.0, gsel)
        cur = jnp.where(pick, neg_inf, cur)

    cur = jnp.where(gsel > 0.0, g3, neg_inf)
    chosen = jnp.zeros((N_GROUPS, per_g, tn), F32)
    for _ in range(TOP_K):
        m = jnp.max(jnp.max(cur, axis=1, keepdims=True), axis=0, keepdims=True)
        cand = jnp.where(cur == m, eio, n_exp)
        i = jnp.min(jnp.min(cand, axis=1, keepdims=True), axis=0, keepdims=True)
        pick = eio == i
        chosen = jnp.where(pick, 1.0, chosen)
        cur = jnp.where(pick, neg_inf, cur)

    w = jnp.where(chosen > 0.0, sc3, 0.0)
    tot = jnp.sum(jnp.sum(w, axis=1, keepdims=True), axis=0, keepdims=True)
    gates = (w / tot * ROUTED_SCALE).reshape(n_exp, tn)
    o_ref[...] = gates
    routed = jnp.where(gates > 0.0, 1.0, 0.0).astype(BF16)
    ones = jnp.ones((8, MOE_TB), BF16)
    for j in range(tn // MOE_TB):
        n_row = _dot_nt(ones, routed[:, j * MOE_TB:(j + 1) * MOE_TB])[0:1]
        q_ref[j] = jnp.floor((n_row + (MOE_CHUNK - 1)) * (1.0 / MOE_CHUNK)).astype(jnp.int32)


def _route(lt, e_bias):
    n_exp, t = lt.shape
    tn = ROUTE_TN
    bps = tn // MOE_TB
    return pl.pallas_call(
        _route_kernel,
        out_shape=(jax.ShapeDtypeStruct((n_exp, t), F32),
                   jax.ShapeDtypeStruct((t // MOE_TB, 1, n_exp), jnp.int32)),
        grid=(t // tn,),
        in_specs=[pl.BlockSpec((n_exp, tn), lambda i: (0, i)),
                  pl.BlockSpec((n_exp, 1), lambda i: (0, 0))],
        out_specs=(pl.BlockSpec((n_exp, tn), lambda i: (0, i)),
                   pl.BlockSpec((bps, 1, n_exp), lambda i: (i, 0, 0))),
        compiler_params=pltpu.CompilerParams(dimension_semantics=("arbitrary",)),
        name="route",
    )(lt, e_bias.reshape(n_exp, 1))


def _for_each_chunk(n, fn):
    shift = MOE_COPY_UNROLL.bit_length() - 1

    def group(j, carry):
        for u in range(MOE_COPY_UNROLL):
            fn(j * MOE_COPY_UNROLL + u)
        return carry
    lax.fori_loop(0, n >> shift, group, 0)
    base = (n >> shift) << shift
    for u in range(MOE_COPY_UNROLL - 1):
        @pl.when(base + u < n)
        def _():
            fn(base + u)


def _dispatch_kernel(dtab_s, nch_s, total_s,
                     gt_ref, qrow_ref, qbrow_ref, h_ref, wg_ref, wu_ref, wd_ref,
                     xs_ref, wg_o, wu_o, wd_o, stage_ref, zero_ref, sem, *, cast_steps):
    b = pl.program_id(0)
    nb = pl.num_programs(0)
    slot = lax.rem(b, 2)
    n_exp, tb = gt_ref.shape
    rmax = stage_ref.shape[1] * MOE_CHUNK
    cpg = MOE_ROWGROUP // MOE_CHUNK

    @pl.when(b < cast_steps)
    def _():
        wg_o[...] = _bf(wg_ref[...])
        wu_o[...] = _bf(wu_ref[...])
        wd_o[...] = _bf(wd_ref[...])

    routed = gt_ref[...] > 0.0
    before = (lax.broadcasted_iota(jnp.int32, (tb, tb), 0)
              < lax.broadcasted_iota(jnp.int32, (tb, tb), 1))
    pos = _dot(jnp.where(routed, 1.0, 0.0).astype(BF16), jnp.where(before, 1.0, 0.0).astype(BF16))
    posm = _bf(jnp.where(routed, pos, -1.0))
    qrow = qrow_ref[0]
    qbrow = qbrow_ref[0]
    qbrow_f = qbrow.astype(F32)
    h = h_ref[...]
    def sort_rows(g):
        r0 = g * MOE_ROWGROUP
        chunk = (lax.broadcasted_iota(jnp.int32, (MOE_ROWGROUP, n_exp), 0) + r0) >> MOE_CHUNK_SHIFT
        own = jnp.where(chunk >= qbrow, jnp.where(chunk < qbrow + qrow, 1.0, 0.0), 0.0)
        rank = _dot(_bf(own), posm)
        start = jnp.sum(own * qbrow_f, axis=-1, keepdims=True) * MOE_CHUNK
        rel = (lax.broadcasted_iota(jnp.int32, (MOE_ROWGROUP, 1), 0) + r0).astype(F32) - start
        onehot = jnp.where(rank == rel, 1.0, 0.0).astype(BF16)
        rows = _bf(_dot(onehot, h))
        stage_ref[slot, g * cpg:(g + 1) * cpg] = rows.reshape(cpg, MOE_CHUNK, rows.shape[1])

    n_groups = rmax // MOE_ROWGROUP
    n_sure = min(n_groups, (MOE_TB * TOP_K + MOE_ROWGROUP - 1) // MOE_ROWGROUP)
    for g in range(n_sure):
        sort_rows(g)
    for g in range(n_sure, n_groups):
        @pl.when(g * MOE_ROWGROUP < nch_s[b] * MOE_CHUNK)
        def _():
            sort_rows(g)

    def start_chunk(i):
        pltpu.make_async_copy(stage_ref.at[slot, i], xs_ref.at[dtab_s[b, i]], sem.at[slot]).start()
    _for_each_chunk(nch_s[b], start_chunk)

    def wait_chunks(n, sl):
        def wait_chunk(i):
            pltpu.make_async_copy(zero_ref, xs_ref.at[0], sem.at[sl]).wait()
        _for_each_chunk(n, wait_chunk)

    @pl.when(b > 0)
    def _():
        wait_chunks(nch_s[b - 1], 1 - slot)

    @pl.when(b == nb - 1)
    def _():
        zero_ref[...] = jnp.zeros_like(zero_ref)
        n_tail = MOE_TM // MOE_CHUNK
        for c in range(n_tail):
            pltpu.make_async_copy(zero_ref, xs_ref.at[total_s[0] + c], sem.at[slot]).start()
        wait_chunks(nch_s[b] + n_tail, slot)


def _dispatch(gates_t, q, qbase, dtab, nch, total, h2, wg, wu, wd, n_chunks):
    n_exp, t = gates_t.shape
    d = h2.shape[1]
    ff = wg.shape[2]
    nb = t // MOE_TB
    rmax = _moe_stage_rows(n_exp)
    eps = -(-n_exp // nb)
    assert n_exp % eps == 0
    cast_steps = n_exp // eps

    def w_blk(b, *_):
        return (jnp.minimum(b, cast_steps - 1), 0, 0)

    grid_spec = pltpu.PrefetchScalarGridSpec(
        num_scalar_prefetch=3,
        grid=(nb,),
        in_specs=[
            pl.BlockSpec((n_exp, MOE_TB), lambda b, *_: (0, b)),
            pl.BlockSpec((1, 1, n_exp), lambda b, *_: (b, 0, 0)),
            pl.BlockSpec((1, 1, n_exp), lambda b, *_: (b, 0, 0)),
            pl.BlockSpec((MOE_TB, d), lambda b, *_: (b, 0)),
            pl.BlockSpec((eps, d, ff), w_blk), pl.BlockSpec((eps, d, ff), w_blk),
            pl.BlockSpec((eps, ff, d), w_blk),
        ],
        out_specs=(pl.BlockSpec(memory_space=pl.ANY),
                   pl.BlockSpec((eps, d, ff), w_blk), pl.BlockSpec((eps, d, ff), w_blk),
                   pl.BlockSpec((eps, ff, d), w_blk)),
        scratch_shapes=[pltpu.VMEM((2, rmax // MOE_CHUNK, MOE_CHUNK, d), BF16),
                        pltpu.VMEM((MOE_CHUNK, d), BF16), pltpu.SemaphoreType.DMA((2,))],
    )
    return pl.pallas_call(
        functools.partial(_dispatch_kernel, cast_steps=cast_steps),
        out_shape=(jax.ShapeDtypeStruct((n_chunks, MOE_CHUNK, d), BF16),
                   jax.ShapeDtypeStruct(wg.shape, BF16), jax.ShapeDtypeStruct(wu.shape, BF16),
                   jax.ShapeDtypeStruct(wd.shape, BF16)),
        grid_spec=grid_spec,
        compiler_params=pltpu.CompilerParams(dimension_semantics=("arbitrary",),
                                             vmem_limit_bytes=V7X_VMEM_LIMIT),
        name="moe_dispatch",
    )(dtab, nch, total, gates_t, q.reshape(nb, 1, n_exp), qbase.reshape(nb, 1, n_exp), h2,
      wg, wu, wd)


def _tile_pieces():
    cpt = MOE_TM // MOE_CHUNK
    return [1 << s for s in range(cpt.bit_length() - 1, -1, -1)]


def _expert_kernel(off_s, len_s, next_s, first_s, xs_ref, wg_ref, wu_ref, wd_ref, ys_ref,
                   xbuf, ybuf, state, sem_in, sem_out):
    e = pl.program_id(0)
    n_exp = pl.num_programs(0)
    cpt = MOE_TM // MOE_CHUNK
    d = xbuf.shape[3]
    pieces = _tile_pieces()

    def for_each_piece(valid, fn):
        for k, piece in enumerate(pieces):
            @pl.when((valid & piece) != 0)
            def _():
                fn(k, piece, valid & ~(2 * piece - 1))

    def tile_valid(ee, tt):
        return jnp.minimum(len_s[ee] - tt * cpt, cpt)

    def tile_in(ee, tt, xsl, k, piece, start):
        return pltpu.make_async_copy(xs_ref.at[pl.ds(off_s[ee] + tt * cpt + start, piece)],
                                     xbuf.at[xsl, pl.ds(start, piece)], sem_in.at[xsl, k])

    def tile_out(sl, dst_chunk, k, piece, start):
        return pltpu.make_async_copy(ybuf.at[sl, pl.ds(start, piece)],
                                     ys_ref.at[pl.ds(dst_chunk + start, piece)], sem_out.at[sl, k])

    def drain(sl):
        for_each_piece(state[1 + sl], lambda k, piece, start: tile_out(sl, 0, k, piece, start).wait())
        state[1 + sl] = 0

    def following(ee, tt):
        safe = jnp.minimum(ee, n_exp - 1)
        more = tt + 1 < (len_s[safe] + cpt - 1) // cpt
        nxt_e = jnp.where(ee >= n_exp, n_exp, jnp.where(more, ee, next_s[safe]))
        return nxt_e, jnp.where(more, tt + 1, 0)

    def prefetch(ee, tt, xsl):
        @pl.when(ee < n_exp)
        def _():
            for_each_piece(tile_valid(ee, tt),
                           lambda k, piece, start: tile_in(ee, tt, xsl, k, piece, start).start())

    @pl.when(e == 0)
    def _():
        state[0] = 0
        state[1] = 0
        state[2] = 0
        xbuf[...] = jnp.zeros_like(xbuf)
        ahead = (first_s[0], 0)
        for k in range(MOE_XBUFS - 1):
            prefetch(ahead[0], ahead[1], k)
            ahead = following(*ahead)

    n_valid = len_s[e]
    n_tiles = (n_valid + cpt - 1) // cpt

    @pl.when(n_valid > 0)
    def _():
        def tile(t, carry):
            g = state[0]
            xsl = lax.rem(g, MOE_XBUFS)
            sl = g & 1
            valid = tile_valid(e, t)
            for_each_piece(valid, lambda k, piece, start: tile_in(e, t, xsl, k, piece, start).wait())
            ahead = (e, t)
            for _ in range(MOE_XBUFS - 1):
                ahead = following(*ahead)
            prefetch(ahead[0], ahead[1], lax.rem(g + MOE_XBUFS - 1, MOE_XBUFS))

            drain(sl)

            def ffn(n_c):
                x = xbuf[xsl, :n_c].reshape(n_c * MOE_CHUNK, d)
                a = _silu(_dot(x, wg_ref[0])) * _dot(x, wu_ref[0])
                ybuf[sl, :n_c] = _bf(_dot(_bf(a), wd_ref[0])).reshape(n_c, MOE_CHUNK, d)

            @pl.when(valid > cpt // 2)
            def _():
                ffn(cpt)

            @pl.when(valid <= cpt // 2)
            def _():
                ffn(cpt // 2)

            dst = off_s[e] + t * cpt
            for_each_piece(valid, lambda k, piece, start: tile_out(sl, dst, k, piece, start).start())
            state[1 + sl] = valid
            state[0] = state[0] + 1
            return carry
        lax.fori_loop(0, n_tiles, tile, 0)

    @pl.when(e == n_exp - 1)
    def _():
        drain(0)
        drain(1)


def _experts(off, per_exp, nxt, first, xs, wg, wu, wd):
    n_chunks, _, d = xs.shape
    n_exp, _, ff = wg.shape
    cpt = MOE_TM // MOE_CHUNK

    def w_blk(e, *_):
        return (e, 0, 0)

    grid_spec = pltpu.PrefetchScalarGridSpec(
        num_scalar_prefetch=4,
        grid=(n_exp,),
        in_specs=[pl.BlockSpec(memory_space=pl.ANY),
                  pl.BlockSpec((1, d, ff), w_blk), pl.BlockSpec((1, d, ff), w_blk),
                  pl.BlockSpec((1, ff, d), w_blk)],
        out_specs=pl.BlockSpec(memory_space=pl.ANY),
        scratch_shapes=[pltpu.VMEM((MOE_XBUFS, cpt, MOE_CHUNK, d), BF16),
                        pltpu.VMEM((2, cpt, MOE_CHUNK, d), BF16),
                        pltpu.SMEM((3,), jnp.int32),
                        pltpu.SemaphoreType.DMA((MOE_XBUFS, len(_tile_pieces()))),
                        pltpu.SemaphoreType.DMA((2, len(_tile_pieces())))],
    )
    return pl.pallas_call(
        _expert_kernel,
        out_shape=jax.ShapeDtypeStruct((n_chunks, MOE_CHUNK, d), BF16),
        grid_spec=grid_spec,
        compiler_params=pltpu.CompilerParams(dimension_semantics=("arbitrary",),
                                             vmem_limit_bytes=V7X_VMEM_LIMIT),
        name="moe_experts",
    )(off, per_exp, nxt, first, xs, wg, wu, wd)


def _combine_kernel(dtab_s, nch_s,
                    gt_ref, qcol_ref, qbcol_ref, h_ref, x1_ref, gate2_ref,
                    wsg_ref, wsu_ref, wsd_ref, ys_ref, o_ref, stage_ref, sem):
    b = pl.program_id(0)
    nb = pl.num_programs(0)
    slot = lax.rem(b, 2)
    n_exp, tb = gt_ref.shape
    rmax = stage_ref.shape[1] * MOE_CHUNK

    def fetch(bb, sl):
        def start_chunk(i):
            pltpu.make_async_copy(ys_ref.at[dtab_s[bb, i]], stage_ref.at[sl, i], sem.at[sl]).start()
        _for_each_chunk(nch_s[bb], start_chunk)

    @pl.when(b == 0)
    def _():
        stage_ref[...] = jnp.zeros_like(stage_ref)
        fetch(0, 0)

    def wait_chunk(i):
        pltpu.make_async_copy(ys_ref.at[0], stage_ref.at[slot, 0], sem.at[slot]).wait()
    _for_each_chunk(nch_s[b], wait_chunk)

    @pl.when(b + 1 < nb)
    def _():
        fetch(b + 1, 1 - slot)

    gt = gt_ref[...]
    routed = jnp.where(gt > 0.0, 1.0, 0.0).astype(BF16)
    i0 = lax.broadcasted_iota(jnp.int32, (tb, tb), 0)
    i1 = lax.broadcasted_iota(jnp.int32, (tb, tb), 1)
    eye = jnp.where(i0 == i1, 1.0, 0.0).astype(BF16)
    routed_t = _dot_nt(eye, routed)
    gates_tok = _dot_nt(eye, _bf(gt))
    earlier = jnp.where(i1 < i0, 1.0, 0.0).astype(BF16)
    pos_t = _dot(earlier, _bf(routed_t))
    posm_t = _bf(jnp.where(routed_t > 0.0, pos_t, -1.0))

    qcol = qcol_ref[0]
    qbcol = qbcol_ref[0]
    h = h_ref[...]
    a = _silu(_dot(h, wsg_ref[...])) * _dot(h, wsu_ref[...])
    shared = _dot(_bf(a), wsd_ref[...])

    def finish(rows):
        chunk = lax.broadcasted_iota(jnp.int32, (n_exp, rows), 1) >> MOE_CHUNK_SHIFT
        own = jnp.where(chunk >= qbcol, jnp.where(chunk < qbcol + qcol, 1.0, 0.0), 0.0)
        own_b = _bf(own)
        rank = _dot(posm_t, own_b)
        wexp = _dot(_bf(gates_tok), own_b)
        start = jnp.sum(own * qbcol.astype(F32), axis=0, keepdims=True) * MOE_CHUNK
        rel = lax.broadcasted_iota(jnp.int32, (1, rows), 1).astype(F32) - start
        weights = _bf(jnp.where(rank == rel, wexp, 0.0))
        staged = stage_ref[slot, :rows // MOE_CHUNK].reshape(rows, o_ref.shape[1])
        o_ref[...] = x1_ref[...] + gate2_ref[0] * (_dot(weights, staged) + shared)

    short = rmax - MOE_ROWGROUP

    @pl.when(nch_s[b] * MOE_CHUNK <= short)
    def _():
        finish(short)

    @pl.when(nch_s[b] * MOE_CHUNK > short)
    def _():
        finish(rmax)


def _combine(gates_t, q, qbase, dtab, nch, h2, x1, mod3, wsg, wsu, wsd, ys, seq):
    n_exp, t = gates_t.shape
    d = h2.shape[1]
    nb = t // MOE_TB
    per_b = seq // MOE_TB
    stage_chunks = _moe_stage_rows(n_exp) // MOE_CHUNK

    def full(a):
        return pl.BlockSpec(a.shape, lambda b, *_: (0,) * a.ndim)

    def rows(w):
        return pl.BlockSpec((MOE_TB, w), lambda b, *_: (b, 0))

    grid_spec = pltpu.PrefetchScalarGridSpec(
        num_scalar_prefetch=2,
        grid=(nb,),
        in_specs=[
            pl.BlockSpec((n_exp, MOE_TB), lambda b, *_: (0, b)),
            pl.BlockSpec((1, n_exp, 1), lambda b, *_: (b, 0, 0)),
            pl.BlockSpec((1, n_exp, 1), lambda b, *_: (b, 0, 0)),
            rows(d), rows(d),
            pl.BlockSpec((1, 1, d), lambda b, *_: (b // per_b, 0, 5)),
            full(wsg), full(wsu), full(wsd),
            pl.BlockSpec(memory_space=pl.ANY),
        ],
        out_specs=rows(d),
        scratch_shapes=[pltpu.VMEM((2, stage_chunks, MOE_CHUNK, d), BF16),
                        pltpu.SemaphoreType.DMA((2,))],
    )
    return pl.pallas_call(
        _combine_kernel,
        out_shape=jax.ShapeDtypeStruct((t, d), F32),
        grid_spec=grid_spec,
        compiler_params=pltpu.CompilerParams(dimension_semantics=("arbitrary",),
                                             vmem_limit_bytes=V7X_VMEM_LIMIT),
        name="moe_combine",
    )(dtab, nch, gates_t, q.reshape(nb, n_exp, 1), qbase.reshape(nb, n_exp, 1),
      h2, x1, mod3, wsg, wsu, wsd, ys)


def _moe_plan(q):
    nb, n_exp = q.shape
    qbase = jnp.cumsum(q, axis=1) - q
    nch = jnp.sum(q, axis=1)
    per_exp = jnp.sum(q, axis=0)
    off = jnp.cumsum(per_exp) - per_exp
    dstq = off[None, :] + jnp.cumsum(q, axis=0) - q
    i = jnp.arange(_moe_block_chunks(n_exp), dtype=jnp.int32)
    ii = i[None, :, None]
    owned = (ii >= qbase[:, None, :]) & (ii < (qbase + q)[:, None, :])
    dtab = i[None, :] + jnp.sum(jnp.where(owned, (dstq - qbase)[:, None, :], 0), axis=2)
    ids = jnp.arange(n_exp, dtype=jnp.int32)
    later = (ids[None, :] > ids[:, None]) & (per_exp[None, :] > 0)
    nxt = jnp.min(jnp.where(later, ids[None, :], n_exp), axis=1)
    first = jnp.min(jnp.where(per_exp > 0, ids, n_exp)).reshape(1)
    total = jnp.sum(per_exp).reshape(1)
    return qbase, dtab, nch, off, per_exp, nxt, first, total


def _moe_block_chunks(n_exp):
    return MOE_TB * TOP_K // MOE_CHUNK + n_exp


def _moe_stage_rows(n_exp):
    return -(-_moe_block_chunks(n_exp) * MOE_CHUNK // MOE_ROWGROUP) * MOE_ROWGROUP


def _moe(h2, gates_t, q3, x1, mod3, wsg, wsu, wsd, wg, wu, wd, seq):
    n_exp, t = gates_t.shape
    nb = t // MOE_TB
    q = q3.reshape(nb, n_exp)
    qbase, dtab, nch, off, per_exp, nxt, first, total = _moe_plan(q)
    n_chunks = nb * _moe_block_chunks(n_exp) + MOE_TM // MOE_CHUNK
    xs, wg_b, wu_b, wd_b = _dispatch(gates_t, q, qbase, dtab, nch, total, h2, wg, wu, wd, n_chunks)
    ys = _experts(off, per_exp, nxt, first, xs, wg_b, wu_b, wd_b)
    return _combine(gates_t, q, qbase, dtab, nch, h2, x1, mod3, wsg, wsu, wsd, ys, seq)


def _pad_heads(w, heads, width):
    lead = w.shape[:-1]
    w = w.reshape(lead + (heads, width))
    w = jnp.pad(w, [(0, 0)] * len(lead) + [(0, 0), (0, LANES - width)])
    return w.reshape(lead + (heads * LANES,))


def kernel(x, c, positions, w_ada, b_ada, g_norm1, w_in, g_na_q, g_na_k, na_rpb, g_q_lat, w_uq,
           g_kv_lat, w_ukv, g_mla_q, g_mla_k, w_proj_na, w_proj_mla, w_out, g_norm2, w_router,
           e_bias, w_exp_gate, w_exp_up, w_exp_down, w_sh_gate, w_sh_up, w_sh_down):
    bsz, seq, d = x.shape
    t = bsz * seq
    depth = w_ada.shape[0]
    na_w = NA_HEADS * NA_HEAD_DIM
    q_rank = g_q_lat.shape[1]
    kv_rank = g_kv_lat.shape[1]
    n_rows = seq // GRID_W

    pos = positions.reshape(1, t)
    half = MLA_ROPE_DIM // 2
    freq = (ROPE_THETA ** (-jnp.arange(half, dtype=F32) / half)).reshape(half, 1)

    x2 = x.reshape(t, d)
    for l in range(depth):
        mod3 = _adaln(c, w_ada[l], b_ada[l]).reshape(bsz, 1, 6 * d)

        wi = w_in[l]
        o_lat = 3 * na_w
        o_rot = o_lat + q_rank + kv_rank
        o_gate = o_rot + MLA_ROPE_DIM
        wqkv = _bf(wi[:, :o_lat])
        w_rot = jnp.pad(wi[:, o_rot:o_gate], ((0, 0), (MLA_NOPE_DIM, LANES - MLA_QK_DIM)))
        wlat = _bf(jnp.concatenate([wi[:, o_lat:o_rot], w_rot], axis=1))
        wgate = _bf(wi[:, o_gate:])
        gq = (jnp.tile(g_na_q[l], NA_HEADS) * (NA_HEAD_DIM ** -0.5 * LOG2E)).reshape(1, na_w)
        gk = jnp.tile(g_na_k[l], NA_HEADS).reshape(1, na_w)
        wuq = _bf(_pad_heads(w_uq[l], MLA_HEADS, MLA_QK_DIM))
        wukv = w_ukv[l].reshape(kv_rank, MLA_HEADS, MLA_NOPE_DIM + MLA_V_DIM)
        wuk = _bf(_pad_heads(wukv[:, :, :MLA_NOPE_DIM].reshape(kv_rank, -1), MLA_HEADS, MLA_NOPE_DIM))
        wuv = _bf(wukv[:, :, MLA_NOPE_DIM:].reshape(kv_rank, MLA_HEADS * MLA_V_DIM))
        gmq = _pad_heads(jnp.tile(g_mla_q[l], MLA_HEADS) * (MLA_QK_DIM ** -0.5 * LOG2E),
                         MLA_HEADS, MLA_QK_DIM).reshape(1, -1)
        gmk = _pad_heads(jnp.tile(g_mla_k[l], MLA_HEADS), MLA_HEADS, MLA_QK_DIM).reshape(1, -1)

        qa, ka, va, qm, km, vm, sgn, sgm = _inproj(
            x2, mod3, g_norm1[l].reshape(1, d), wqkv, wlat, wgate, gq, gk,
            g_q_lat[l].reshape(1, q_rank), g_kv_lat[l].reshape(1, kv_rank), wuq, wuk, wuv,
            gmq, gmk, pos, freq, seq)

        bias = _na_bias(na_rpb[l], n_rows)
        y_na = _na_attention(qa, ka, va, bias, bsz, seq)
        y_mla = _mla_attention(qm, km, vm, bsz, seq)

        x1, h2, lt = _merge(x2, y_na, y_mla, sgn, sgm, _bf(w_proj_na[l]), _bf(w_proj_mla[l]),
                            _bf(w_out[l]), mod3, g_norm2[l].reshape(1, d), w_router[l].T, seq)
        gates_t, q3 = _route(lt, e_bias[l])
        x2 = _moe(h2, gates_t, q3, x1, mod3, _bf(w_sh_gate[l]), _bf(w_sh_up[l]),
                  _bf(w_sh_down[l]), w_exp_gate[l], w_exp_up[l], w_exp_down[l], seq)
    return x2.reshape(bsz, seq, d)
```

```python
import functools

import jax
import jax.numpy as jnp
import numpy as np
from jax import lax
from jax.experimental import pallas as pl
from jax.experimental.pallas import tpu as pltpu

GRID_W = 64
NA_HEADS = 8
NA_HEAD_DIM = 64
NA_WIN_ROWS = 8
NA_WIN_COLS = 16
MLA_HEADS = 8
MLA_NOPE_DIM = 64
MLA_ROPE_DIM = 32
MLA_V_DIM = 64
MLA_QK_DIM = MLA_NOPE_DIM + MLA_ROPE_DIM
ROPE_THETA = 10000.0
N_GROUPS = 8
TOPK_GROUPS = 4
TOP_K = 8
ROUTED_SCALE = 2.5
EPS = 1e-6
NEG_BIG = -1e30

LANES = 128
V7X_VMEM_LIMIT = 56 * 1024 * 1024

NA_QROWS = 4
NA_BAND = 12
NA_BLOCK_TYPES = 3
NA_BLOCKS_PER_STEP = 4
MERGE_SUB = 512
INPROJ_TM = 512
INPROJ_SUB = 256
MLA_PAIRS_PER_STEP = 4
LOG2E = 1.4426950408889634
MOE_TB = 256
ROUTE_TN = 1024
MOE_CHUNK_SHIFT = 4
MOE_CHUNK = 1 << MOE_CHUNK_SHIFT
MOE_TM = 1024
MOE_ROWGROUP = 512
MOE_COPY_UNROLL = 4
MOE_XBUFS = 3

F32 = jnp.float32
BF16 = jnp.bfloat16


def _bf(x):
    return x.astype(BF16)


def _dot(a, b):
    return jnp.dot(a, b, preferred_element_type=F32)


def _dot_nt(a, b):
    return lax.dot_general(a, b, (((1,), (1,)), ((), ())), preferred_element_type=F32)


def _split(x):
    hi = _bf(x)
    lo = _bf(x - hi.astype(F32))
    return hi, lo


def _dot3(a, b):
    ah, al = _split(a)
    bh, bl = _split(b)
    return _dot(ah, bh) + (_dot(ah, bl) + _dot(al, bh))


def _dot3_nt(a, b):
    ah, al = _split(a)
    bh, bl = _split(b)
    return _dot_nt(ah, bh) + (_dot_nt(ah, bl) + _dot_nt(al, bh))


def _sigmoid(x):
    return 1.0 / (1.0 + jnp.exp(-x))


def _silu(x):
    return x * _sigmoid(x)


def _rms(x, n):
    ss = jnp.sum(x * x, axis=-1, keepdims=True)
    return x * lax.rsqrt(ss * (1.0 / n) + EPS)


def _adaln_kernel(c_ref, w_ref, b_ref, o_ref):
    c = c_ref[...]
    o_ref[...] = _dot3(_silu(c), w_ref[...]) + b_ref[...]


def _adaln(c, w, b):
    bsz, d = c.shape
    n = w.shape[1]
    tn = 1024
    return pl.pallas_call(
        _adaln_kernel,
        out_shape=jax.ShapeDtypeStruct((bsz, n), F32),
        grid=(n // tn,),
        in_specs=[
            pl.BlockSpec((bsz, d), lambda j: (0, 0)),
            pl.BlockSpec((d, tn), lambda j: (0, j)),
            pl.BlockSpec((1, tn), lambda j: (0, j)),
        ],
        out_specs=pl.BlockSpec((bsz, tn), lambda j: (0, j)),
        compiler_params=pltpu.CompilerParams(dimension_semantics=("arbitrary",)),
        name="adaln",
    )(c, w, b.reshape(1, n))


def _inproj_kernel(x_ref, shift_ref, scale_ref, g1_ref, wqkv_ref, wlat_ref, wgate_ref,
                   gq_ref, gk_ref, gql_ref, gkvl_ref, wuq_ref, wuk_ref, wuv_ref,
                   gmq_ref, gmk_ref, pos_ref, freq_ref,
                   qa_ref, ka_ref, va_ref, qm_ref, km_ref, vm_ref, sgn_ref, sgm_ref):
    for r0 in range(0, x_ref.shape[0], INPROJ_SUB):
        _inproj_rows(pl.ds(r0, INPROJ_SUB), x_ref, shift_ref, scale_ref, g1_ref, wqkv_ref,
                     wlat_ref, wgate_ref, gq_ref, gk_ref, gql_ref, gkvl_ref, wuq_ref, wuk_ref,
                     wuv_ref, gmq_ref, gmk_ref, pos_ref, freq_ref, qa_ref, ka_ref, va_ref,
                     qm_ref, km_ref, vm_ref, sgn_ref, sgm_ref)


def _inproj_rows(rs, x_ref, shift_ref, scale_ref, g1_ref, wqkv_ref, wlat_ref, wgate_ref,
                 gq_ref, gk_ref, gql_ref, gkvl_ref, wuq_ref, wuk_ref, wuv_ref,
                 gmq_ref, gmk_ref, pos_ref, freq_ref,
                 qa_ref, ka_ref, va_ref, qm_ref, km_ref, vm_ref, sgn_ref, sgm_ref):
    d = x_ref.shape[1]
    x = x_ref[rs, :]
    h = _rms(x, d) * g1_ref[...]
    h = h * (1.0 + scale_ref[0]) + shift_ref[0]
    hb = _bf(h)

    qkv = _dot(hb, wqkv_ref[...])
    lat = _dot(hb, wlat_ref[...])
    gts = _dot(hb, wgate_ref[...])
    sgn_ref[rs, :] = _bf(_sigmoid(gts[:, :d]))
    sgm_ref[rs, :] = _bf(_sigmoid(gts[:, d:]))

    na_w = NA_HEADS * NA_HEAD_DIM
    lane = lax.broadcasted_iota(jnp.int32, (1, LANES), 1)
    lo_half = lane < NA_HEAD_DIM
    for p in range(na_w // LANES):
        sl = slice(p * LANES, (p + 1) * LANES)
        for src_off, g_ref, dst_ref in ((0, gq_ref, qa_ref), (na_w, gk_ref, ka_ref)):
            t = qkv[:, src_off + p * LANES: src_off + (p + 1) * LANES]
            sq = t * t
            s_lo = jnp.sum(jnp.where(lo_half, sq, 0.0), axis=-1, keepdims=True)
            s_hi = jnp.sum(jnp.where(lo_half, 0.0, sq), axis=-1, keepdims=True)
            r = jnp.where(lo_half,
                          lax.rsqrt(s_lo * (1.0 / NA_HEAD_DIM) + EPS),
                          lax.rsqrt(s_hi * (1.0 / NA_HEAD_DIM) + EPS))
            dst_ref[rs, sl] = _bf(t * r * g_ref[:, sl])
    va_ref[rs, :] = _bf(qkv[:, 2 * na_w: 3 * na_w])

    q_rank = gql_ref.shape[1]
    kv_rank = gkvl_ref.shape[1]
    qln = _rms(lat[:, :q_rank], q_rank) * gql_ref[...]
    kvn = _bf(_rms(lat[:, q_rank:q_rank + kv_rank], kv_rank) * gkvl_ref[...])
    qpre = _dot(_bf(qln), wuq_ref[...])
    knope = _dot(kvn, wuk_ref[...])
    vm_ref[rs, :] = _bf(_dot(kvn, wuv_ref[...]))
    krot = lat[:, q_rank + kv_rank:]

    tm = INPROJ_SUB
    half = MLA_ROPE_DIM // 2
    ang_t = freq_ref[...] * pos_ref[:, rs].astype(F32)
    cos_t = jnp.cos(ang_t)
    sin_t = jnp.sin(ang_t)
    l_i = lax.broadcasted_iota(jnp.int32, (LANES, half), 0)
    j_i = lax.broadcasted_iota(jnp.int32, (LANES, half), 1)
    hit = jnp.where((l_i >= MLA_NOPE_DIM) & (l_i < MLA_QK_DIM)
                    & (((l_i - MLA_NOPE_DIM) & (half - 1)) == j_i), 1.0, 0.0)
    first_half = l_i < MLA_NOPE_DIM + half
    eye = jnp.where(lax.broadcasted_iota(jnp.int32, (tm, tm), 0)
                    == lax.broadcasted_iota(jnp.int32, (tm, tm), 1), 1.0, 0.0).astype(BF16)

    def table(sel, vals, fill_nope):
        hi, lo = _split(vals)
        w = _dot(_bf(sel), hi) + _dot(_bf(sel), lo)
        if fill_nope:
            w = jnp.where(lax.broadcasted_iota(jnp.int32, (LANES, tm), 0) < MLA_NOPE_DIM, 1.0, w)
        hi, lo = _split(w)
        return _dot_nt(eye, hi) + _dot_nt(eye, lo)

    c_tab = table(hit, cos_t, True)
    s_up = table(jnp.where(first_half, 0.0, hit), sin_t, False)
    s_dn = table(jnp.where(first_half, -hit, 0.0), sin_t, False)

    def rope(t):
        return t * c_tab + pltpu.roll(t, half, 1) * s_up + pltpu.roll(t, LANES - half, 1) * s_dn

    kr = rope(krot)
    for hd in range(MLA_HEADS):
        sl = slice(hd * LANES, (hd + 1) * LANES)
        qh = rope(qpre[:, sl])
        qm_ref[rs, sl] = _bf(_rms(qh, MLA_QK_DIM) * gmq_ref[:, sl])
        kh = knope[:, sl] + kr
        km_ref[rs, sl] = _bf(_rms(kh, MLA_QK_DIM) * gmk_ref[:, sl])


def _inproj(x2, mod3, g1, wqkv, wlat, wgate, gq, gk, gql, gkvl, wuq, wuk, wuv, gmq, gmk,
            pos, freq, seq):
    t, d = x2.shape
    tm = INPROJ_TM
    per_b = seq // tm
    na_w = NA_HEADS * NA_HEAD_DIM
    mla_w = MLA_HEADS * LANES
    v_w = MLA_HEADS * MLA_V_DIM

    def full(a):
        return pl.BlockSpec(a.shape, lambda i: (0,) * a.ndim)

    def rows(w):
        return pl.BlockSpec((tm, w), lambda i: (i, 0))

    out_shapes = (
        jax.ShapeDtypeStruct((t, na_w), BF16), jax.ShapeDtypeStruct((t, na_w), BF16),
        jax.ShapeDtypeStruct((t, na_w), BF16),
        jax.ShapeDtypeStruct((t, mla_w), BF16), jax.ShapeDtypeStruct((t, mla_w), BF16),
        jax.ShapeDtypeStruct((t, v_w), BF16),
        jax.ShapeDtypeStruct((t, d), BF16), jax.ShapeDtypeStruct((t, d), BF16),
    )
    return pl.pallas_call(
        _inproj_kernel,
        out_shape=out_shapes,
        grid=(t // tm,),
        in_specs=[
            rows(d),
            pl.BlockSpec((1, 1, d), lambda i: (i // per_b, 0, 0)),
            pl.BlockSpec((1, 1, d), lambda i: (i // per_b, 0, 1)),
            full(g1), full(wqkv), full(wlat), full(wgate), full(gq), full(gk), full(gql),
            full(gkvl), full(wuq), full(wuk), full(wuv), full(gmq), full(gmk),
            pl.BlockSpec((1, tm), lambda i: (0, i)),
            full(freq),
        ],
        out_specs=(rows(na_w), rows(na_w), rows(na_w), rows(mla_w), rows(mla_w), rows(v_w),
                   rows(d), rows(d)),
        compiler_params=pltpu.CompilerParams(dimension_semantics=("arbitrary",),
                                             vmem_limit_bytes=V7X_VMEM_LIMIT),
        name="inproj",
    )(x2, mod3, mod3, g1, wqkv, wlat, wgate, gq, gk, gql, gkvl, wuq, wuk, wuv, gmq, gmk,
      pos, freq)


def _na_block_geometry(block_type, n_rows):
    if block_type == 0:
        return 0, 0
    if block_type == 1:
        r0 = NA_QROWS
        return r0, r0 - NA_WIN_ROWS // 2
    return n_rows - NA_QROWS, n_rows - NA_BAND


def _na_bias_kernel(rpb_ref, o_ref, m_ref, *, n_rows):
    hd = pl.program_id(0)
    n_dr = 2 * NA_WIN_ROWS - 1
    n_dc = 2 * NA_WIN_COLS - 1
    qc = lax.broadcasted_iota(jnp.int32, (GRID_W, LANES), 0)
    kc = lax.broadcasted_iota(jnp.int32, (GRID_W, LANES), 1) & (GRID_W - 1)
    dc = jnp.clip(kc - qc, -(NA_WIN_COLS - 1), NA_WIN_COLS - 1) + (NA_WIN_COLS - 1)
    cstart = jnp.clip(qc - NA_WIN_COLS // 2, 0, GRID_W - NA_WIN_COLS)
    col_ok = (kc >= cstart) & (kc < cstart + NA_WIN_COLS)
    for i_dr in range(n_dr):
        acc = jnp.zeros((GRID_W, LANES), F32)
        for t in range(n_dc):
            acc = jnp.where(dc == t, rpb_ref[hd, i_dr * n_dc + t], acc)
        m_ref[i_dr] = jnp.where(col_ok, acc * LOG2E, NEG_BIG)
    neg = jnp.full((GRID_W, LANES), NEG_BIG, F32)
    lo_half = lax.broadcasted_iota(jnp.int32, (GRID_W, LANES), 1) < GRID_W
    kh = NA_WIN_ROWS
    for bt in range(NA_BLOCK_TYPES):
        r0, start = _na_block_geometry(bt, n_rows)
        for i in range(NA_QROWS):
            r = r0 + i
            rs = min(max(r - kh // 2, 0), n_rows - kh)
            for jp in range(NA_BAND // 2):
                halves = []
                for j in (2 * jp, 2 * jp + 1):
                    krow = start + j
                    if rs <= krow < rs + kh:
                        halves.append(m_ref[krow - r + (NA_WIN_ROWS - 1)])
                    else:
                        halves.append(neg)
                tile = jnp.where(lo_half, halves[0], halves[1])
                o_ref[bt, 0, i * GRID_W:(i + 1) * GRID_W, jp * LANES:(jp + 1) * LANES] = tile


def _na_bias(rpb, n_rows):
    heads = rpb.shape[0]
    nq = NA_QROWS * GRID_W
    nk = NA_BAND * GRID_W
    rpb2 = rpb.reshape(heads, -1)
    return pl.pallas_call(
        functools.partial(_na_bias_kernel, n_rows=n_rows),
        out_shape=jax.ShapeDtypeStruct((NA_BLOCK_TYPES, heads, nq, nk), F32),
        grid=(heads,),
        in_specs=[pl.BlockSpec(memory_space=pltpu.SMEM)],
        out_specs=pl.BlockSpec((NA_BLOCK_TYPES, 1, nq, nk), lambda hd: (0, hd, 0, 0)),
        scratch_shapes=[pltpu.VMEM((2 * NA_WIN_ROWS - 1, GRID_W, LANES), F32)],
        compiler_params=pltpu.CompilerParams(dimension_semantics=("arbitrary",)),
        name="na_bias",
    )(rpb2)


def _softmax_pv(s, v_pair, hh, half):
    lane = lax.broadcasted_iota(jnp.int32, (1, LANES), 1)
    mine = (lane < half) if hh == 0 else (lane >= half)
    den_lane = half if hh == 0 else 0
    m = jnp.max(s, axis=-1, keepdims=True)
    p = _bf(jnp.exp2(s - m))
    ones_row = jnp.where(lane == den_lane, 1.0, 0.0).astype(BF16)
    o = _dot(p, jnp.where(mine, v_pair, ones_row))
    den = jnp.sum(jnp.where(lane == den_lane, o, 0.0), axis=-1, keepdims=True)
    return jnp.where(mine, o / den, 0.0)


def _na_kernel(q_ref, k_ref, v_ref, bias_ref, o_ref, *, n_blocks, n_rows):
    nk = NA_BAND * GRID_W
    tq = NA_QROWS * GRID_W
    lo_half = lax.broadcasted_iota(jnp.int32, (1, LANES), 1) < NA_HEAD_DIM
    for u in range(NA_BLOCKS_PER_STEP):
        blk = pl.program_id(1) * NA_BLOCKS_PER_STEP + u
        btype = jnp.where(blk == 0, 0, jnp.where(blk == n_blocks - 1, 2, 1))
        start_row = jnp.where(blk == 0, 0,
                              jnp.where(blk == n_blocks - 1, n_rows - NA_BAND,
                                        blk * NA_QROWS - NA_WIN_ROWS // 2))
        off = pl.multiple_of(start_row * GRID_W, GRID_W)
        rs = pl.ds(u * tq, tq)
        for p in range(NA_HEADS * NA_HEAD_DIM // LANES):
            sl = slice(p * LANES, (p + 1) * LANES)
            qp = q_ref[rs, sl]
            kb = k_ref[pl.ds(off, nk), sl]
            vb = v_ref[pl.ds(off, nk), sl]
            zero = jnp.zeros_like(qp)
            q2 = jnp.concatenate([jnp.where(lo_half, qp, zero), jnp.where(lo_half, zero, qp)],
                                 axis=0)
            s = _dot_nt(q2, kb) + bias_ref[btype, 2 * p:2 * p + 2].reshape(2 * tq, nk)
            m = jnp.max(s, axis=-1, keepdims=True)
            e = jnp.exp2(s - m)
            den = jnp.sum(e, axis=-1, keepdims=True)
            o = _dot(_bf(e), vb) / den
            o_ref[rs, sl] = _bf(jnp.where(lo_half, o[:tq], o[tq:]))


def _na_attention(qa, ka, va, bias, bsz, seq):
    t, w = qa.shape
    n_rows = seq // GRID_W
    n_blocks = n_rows // NA_QROWS
    tq = NA_QROWS * GRID_W
    nk = NA_BAND * GRID_W
    heads = bias.shape[1]

    bps = NA_BLOCKS_PER_STEP
    steps = n_blocks // bps
    return pl.pallas_call(
        functools.partial(_na_kernel, n_blocks=n_blocks, n_rows=n_rows),
        out_shape=jax.ShapeDtypeStruct((t, w), BF16),
        grid=(bsz, steps),
        in_specs=[
            pl.BlockSpec((bps * tq, w), lambda b, j: (b * steps + j, 0)),
            pl.BlockSpec((seq, w), lambda b, j: (b, 0)),
            pl.BlockSpec((seq, w), lambda b, j: (b, 0)),
            pl.BlockSpec(bias.shape, lambda b, j: (0, 0, 0, 0), pipeline_mode=pl.Buffered(1)),
        ],
        out_specs=pl.BlockSpec((bps * tq, w), lambda b, j: (b * steps + j, 0)),
        compiler_params=pltpu.CompilerParams(dimension_semantics=("arbitrary", "arbitrary"),
                                             vmem_limit_bytes=V7X_VMEM_LIMIT),
        name="na_attn",
    )(qa, ka, va, bias)


def _mla_kernel(q_ref, k_ref, v_ref, o_ref):
    for pp in range(MLA_PAIRS_PER_STEP):
        v_pair = v_ref[:, pp * LANES:(pp + 1) * LANES]
        acc = jnp.zeros((q_ref.shape[0], LANES), F32)
        for hh in range(2):
            sl = slice((2 * pp + hh) * LANES, (2 * pp + hh + 1) * LANES)
            s = _dot_nt(q_ref[:, sl], k_ref[:, sl])
            acc = acc + _softmax_pv(s, v_pair, hh, MLA_V_DIM)
        o_ref[:, pp * LANES:(pp + 1) * LANES] = _bf(acc)


def _mla_attention(qm, km, vm, bsz, seq):
    t = qm.shape[0]
    tq = 512
    nq = seq // tq
    groups = MLA_HEADS // (2 * MLA_PAIRS_PER_STEP)
    qk_w = 2 * MLA_PAIRS_PER_STEP * LANES
    v_w = MLA_PAIRS_PER_STEP * LANES
    return pl.pallas_call(
        _mla_kernel,
        out_shape=jax.ShapeDtypeStruct((t, MLA_HEADS * MLA_V_DIM), BF16),
        grid=(bsz, groups, nq),
        in_specs=[
            pl.BlockSpec((tq, qk_w), lambda b, p, i: (b * nq + i, p)),
            pl.BlockSpec((seq, qk_w), lambda b, p, i: (b, p)),
            pl.BlockSpec((seq, v_w), lambda b, p, i: (b, p)),
        ],
        out_specs=pl.BlockSpec((tq, v_w), lambda b, p, i: (b * nq + i, p)),
        compiler_params=pltpu.CompilerParams(
            dimension_semantics=("arbitrary", "arbitrary", "arbitrary"),
            vmem_limit_bytes=V7X_VMEM_LIMIT),
        name="mla_attn",
    )(qm, km, vm)


def _merge_kernel(x_ref, yna_ref, ymla_ref, sgn_ref, sgm_ref, wpn_ref, wpm_ref, wout_ref,
                  gate1_ref, shift2_ref, scale2_ref, g2_ref, wr_ref,
                  x1_ref, h2_ref, lt_ref):
    d = x_ref.shape[1]
    for r0 in range(0, x_ref.shape[0], MERGE_SUB):
        rs = pl.ds(r0, MERGE_SUB)
        merged = (sgn_ref[rs, :].astype(F32) * _dot(yna_ref[rs, :], wpn_ref[...])
                  + sgm_ref[rs, :].astype(F32) * _dot(ymla_ref[rs, :], wpm_ref[...]))
        x1 = x_ref[rs, :] + gate1_ref[0] * _dot(_bf(merged), wout_ref[...])
        x1_ref[rs, :] = x1
        h2 = _rms(x1, d) * g2_ref[...]
        h2 = h2 * (1.0 + scale2_ref[0]) + shift2_ref[0]
        h2_ref[rs, :] = _bf(h2)
        lt_ref[:, rs] = _dot3_nt(wr_ref[...], h2)


def _merge(x2, yna, ymla, sgn, sgm, wpn, wpm, wout, mod3, g2, wr, seq):
    t, d = x2.shape
    tm = 512
    per_b = seq // tm
    n_exp = wr.shape[0]

    def full(a):
        return pl.BlockSpec(a.shape, lambda i: (0,) * a.ndim)

    def rows(w):
        return pl.BlockSpec((tm, w), lambda i: (i, 0))

    def modblk(j):
        return pl.BlockSpec((1, 1, d), lambda i: (i // per_b, 0, j))

    return pl.pallas_call(
        _merge_kernel,
        out_shape=(jax.ShapeDtypeStruct((t, d), F32), jax.ShapeDtypeStruct((t, d), BF16),
                   jax.ShapeDtypeStruct((n_exp, t), F32)),
        grid=(t // tm,),
        in_specs=[rows(d), rows(yna.shape[1]), rows(ymla.shape[1]), rows(d), rows(d),
                  full(wpn), full(wpm), full(wout),
                  modblk(2), modblk(3), modblk(4), full(g2), full(wr)],
        out_specs=(rows(d), rows(d), pl.BlockSpec((n_exp, tm), lambda i: (0, i))),
        compiler_params=pltpu.CompilerParams(dimension_semantics=("arbitrary",),
                                             vmem_limit_bytes=V7X_VMEM_LIMIT),
        name="merge",
    )(x2, yna, ymla, sgn, sgm, wpn, wpm, wout, mod3, mod3, mod3, g2, wr)


def _route_kernel(lt_ref, eb_ref, o_ref, q_ref):
    n_exp, tn = lt_ref.shape
    per_g = n_exp // N_GROUPS
    neg_inf = -jnp.inf
    sc = _sigmoid(lt_ref[...])
    sel = sc + eb_ref[...]
    sc3 = sc.reshape(N_GROUPS, per_g, tn)
    g3 = sel.reshape(N_GROUPS, per_g, tn)
    io = lax.broadcasted_iota(jnp.int32, (N_GROUPS, per_g, tn), 1)
    gio = lax.broadcasted_iota(jnp.int32, (N_GROUPS, per_g, tn), 0)
    eio = gio * per_g + io

    m1 = jnp.max(g3, axis=1, keepdims=True)
    i1 = jnp.min(jnp.where(g3 == m1, io, per_g), axis=1, keepdims=True)
    m2 = jnp.max(jnp.where(io == i1, neg_inf, g3), axis=1, keepdims=True)
    gs = m1 + m2

    g1io = lax.broadcasted_iota(jnp.int32, (N_GROUPS, 1, tn), 0)
    gsel = jnp.zeros((N_GROUPS, 1, tn), F32)
    cur = gs
    for _ in range(TOPK_GROUPS):
        m = jnp.max(cur, axis=0, keepdims=True)
        i = jnp.min(jnp.where(cur == m, g1io, N_GROUPS), axis=0, keepdims=True)
        pick = g1io == i
        gsel = jnp.where(pick, 1.0, gsel)
        cur = jnp.where(pick, neg_inf, cur)

    cur = jnp.where(gsel > 0.0, g3, neg_inf)
    chosen = jnp.zeros((N_GROUPS, per_g, tn), F32)
    for _ in range(TOP_K):
        m = jnp.max(jnp.max(cur, axis=1, keepdims=True), axis=0, keepdims=True)
        cand = jnp.where(cur == m, eio, n_exp)
        i = jnp.min(jnp.min(cand, axis=1, keepdims=True), axis=0, keepdims=True)
        pick = eio == i
        chosen = jnp.where(pick, 1.0, chosen)
        cur = jnp.where(pick, neg_inf, cur)

    w = jnp.where(chosen > 0.0, sc3, 0.0)
    tot = jnp.sum(jnp.sum(w, axis=1, keepdims=True), axis=0, keepdims=True)
    gates = (w / tot * ROUTED_SCALE).reshape(n_exp, tn)
    o_ref[...] = gates
    routed = jnp.where(gates > 0.0, 1.0, 0.0).astype(BF16)
    ones = jnp.ones((8, MOE_TB), BF16)
    for j in range(tn // MOE_TB):
        n_row = _dot_nt(ones, routed[:, j * MOE_TB:(j + 1) * MOE_TB])[0:1]
        q_ref[j] = jnp.floor((n_row + (MOE_CHUNK - 1)) * (1.0 / MOE_CHUNK)).astype(jnp.int32)


def _route(lt, e_bias):
    n_exp, t = lt.shape
    tn = ROUTE_TN
    bps = tn // MOE_TB
    return pl.pallas_call(
        _route_kernel,
        out_shape=(jax.ShapeDtypeStruct((n_exp, t), F32),
                   jax.ShapeDtypeStruct((t // MOE_TB, 1, n_exp), jnp.int32)),
        grid=(t // tn,),
        in_specs=[pl.BlockSpec((n_exp, tn), lambda i: (0, i)),
                  pl.BlockSpec((n_exp, 1), lambda i: (0, 0))],
        out_specs=(pl.BlockSpec((n_exp, tn), lambda i: (0, i)),
                   pl.BlockSpec((bps, 1, n_exp), lambda i: (i, 0, 0))),
        compiler_params=pltpu.CompilerParams(dimension_semantics=("arbitrary",)),
        name="route",
    )(lt, e_bias.reshape(n_exp, 1))


def _for_each_chunk(n, fn):
    shift = MOE_COPY_UNROLL.bit_length() - 1

    def group(j, carry):
        for u in range(MOE_COPY_UNROLL):
            fn(j * MOE_COPY_UNROLL + u)
        return carry
    lax.fori_loop(0, n >> shift, group, 0)
    base = (n >> shift) << shift
    for u in range(MOE_COPY_UNROLL - 1):
        @pl.when(base + u < n)
        def _():
            fn(base + u)


def _dispatch_kernel(dtab_s, nch_s, total_s,
                     gt_ref, qrow_ref, qbrow_ref, h_ref, xs_ref, stage_ref, zero_ref, sem):
    b = pl.program_id(0)
    nb = pl.num_programs(0)
    slot = lax.rem(b, 2)
    n_exp, tb = gt_ref.shape
    rmax = stage_ref.shape[1] * MOE_CHUNK
    cpg = MOE_ROWGROUP // MOE_CHUNK

    routed = gt_ref[...] > 0.0
    before = (lax.broadcasted_iota(jnp.int32, (tb, tb), 0)
              < lax.broadcasted_iota(jnp.int32, (tb, tb), 1))
    pos = _dot(jnp.where(routed, 1.0, 0.0).astype(BF16), jnp.where(before, 1.0, 0.0).astype(BF16))
    posm = _bf(jnp.where(routed, pos, -1.0))
    qrow = qrow_ref[0]
    qbrow = qbrow_ref[0]
    qbrow_f = qbrow.astype(F32)
    h = h_ref[...]
    def sort_rows(g):
        r0 = g * MOE_ROWGROUP
        chunk = (lax.broadcasted_iota(jnp.int32, (MOE_ROWGROUP, n_exp), 0) + r0) >> MOE_CHUNK_SHIFT
        own = jnp.where(chunk >= qbrow, jnp.where(chunk < qbrow + qrow, 1.0, 0.0), 0.0)
        rank = _dot(_bf(own), posm)
        start = jnp.sum(own * qbrow_f, axis=-1, keepdims=True) * MOE_CHUNK
        rel = (lax.broadcasted_iota(jnp.int32, (MOE_ROWGROUP, 1), 0) + r0).astype(F32) - start
        onehot = jnp.where(rank == rel, 1.0, 0.0).astype(BF16)
        rows = _bf(_dot(onehot, h))
        stage_ref[slot, g * cpg:(g + 1) * cpg] = rows.reshape(cpg, MOE_CHUNK, rows.shape[1])

    n_groups = rmax // MOE_ROWGROUP
    n_sure = min(n_groups, (MOE_TB * TOP_K + MOE_ROWGROUP - 1) // MOE_ROWGROUP)
    for g in range(n_sure):
        sort_rows(g)
    for g in range(n_sure, n_groups):
        @pl.when(g * MOE_ROWGROUP < nch_s[b] * MOE_CHUNK)
        def _():
            sort_rows(g)

    def start_chunk(i):
        pltpu.make_async_copy(stage_ref.at[slot, i], xs_ref.at[dtab_s[b, i]], sem.at[slot]).start()
    _for_each_chunk(nch_s[b], start_chunk)

    def wait_chunks(n, sl):
        def wait_chunk(i):
            pltpu.make_async_copy(zero_ref, xs_ref.at[0], sem.at[sl]).wait()
        _for_each_chunk(n, wait_chunk)

    @pl.when(b > 0)
    def _():
        wait_chunks(nch_s[b - 1], 1 - slot)

    @pl.when(b == nb - 1)
    def _():
        zero_ref[...] = jnp.zeros_like(zero_ref)
        n_tail = MOE_TM // MOE_CHUNK
        for c in range(n_tail):
            pltpu.make_async_copy(zero_ref, xs_ref.at[total_s[0] + c], sem.at[slot]).start()
        wait_chunks(nch_s[b] + n_tail, slot)


def _dispatch(gates_t, q, qbase, dtab, nch, total, h2, n_chunks):
    n_exp, t = gates_t.shape
    d = h2.shape[1]
    nb = t // MOE_TB
    rmax = _moe_stage_rows(n_exp)
    grid_spec = pltpu.PrefetchScalarGridSpec(
        num_scalar_prefetch=3,
        grid=(nb,),
        in_specs=[
            pl.BlockSpec((n_exp, MOE_TB), lambda b, *_: (0, b)),
            pl.BlockSpec((1, 1, n_exp), lambda b, *_: (b, 0, 0)),
            pl.BlockSpec((1, 1, n_exp), lambda b, *_: (b, 0, 0)),
            pl.BlockSpec((MOE_TB, d), lambda b, *_: (b, 0)),
        ],
        out_specs=pl.BlockSpec(memory_space=pl.ANY),
        scratch_shapes=[pltpu.VMEM((2, rmax // MOE_CHUNK, MOE_CHUNK, d), BF16),
                        pltpu.VMEM((MOE_CHUNK, d), BF16), pltpu.SemaphoreType.DMA((2,))],
    )
    return pl.pallas_call(
        _dispatch_kernel,
        out_shape=jax.ShapeDtypeStruct((n_chunks, MOE_CHUNK, d), BF16),
        grid_spec=grid_spec,
        compiler_params=pltpu.CompilerParams(dimension_semantics=("arbitrary",),
                                             vmem_limit_bytes=V7X_VMEM_LIMIT),
        name="moe_dispatch",
    )(dtab, nch, total, gates_t, q.reshape(nb, 1, n_exp), qbase.reshape(nb, 1, n_exp), h2)


def _tile_pieces():
    cpt = MOE_TM // MOE_CHUNK
    return [1 << s for s in range(cpt.bit_length() - 1, -1, -1)]


def _expert_kernel(off_s, len_s, next_s, first_s, xs_ref, wg_ref, wu_ref, wd_ref, ys_ref,
                   xbuf, ybuf, wg_b, wu_b, wd_b, state, sem_in, sem_out):
    e = pl.program_id(0)
    n_exp = pl.num_programs(0)
    cpt = MOE_TM // MOE_CHUNK
    d = xbuf.shape[3]
    pieces = _tile_pieces()

    def tile_in(ee, tt, sl):
        return pltpu.make_async_copy(xs_ref.at[pl.ds(off_s[ee] + tt * cpt, cpt)], xbuf.at[sl],
                                     sem_in.at[sl])

    def for_each_piece(valid, fn):
        for k, piece in enumerate(pieces):
            @pl.when((valid & piece) != 0)
            def _():
                fn(k, piece, valid & ~(2 * piece - 1))

    def tile_out(sl, dst_chunk, k, piece, start):
        return pltpu.make_async_copy(ybuf.at[sl, pl.ds(start, piece)],
                                     ys_ref.at[pl.ds(dst_chunk + start, piece)], sem_out.at[sl, k])

    def drain(sl):
        for_each_piece(state[1 + sl], lambda k, piece, start: tile_out(sl, 0, k, piece, start).wait())
        state[1 + sl] = 0

    def following(ee, tt):
        safe = jnp.minimum(ee, n_exp - 1)
        more = tt + 1 < (len_s[safe] + cpt - 1) // cpt
        nxt_e = jnp.where(ee >= n_exp, n_exp, jnp.where(more, ee, next_s[safe]))
        return nxt_e, jnp.where(more, tt + 1, 0)

    def prefetch(ee, tt, xsl):
        @pl.when(ee < n_exp)
        def _():
            tile_in(ee, tt, xsl).start()

    @pl.when(e == 0)
    def _():
        state[0] = 0
        state[1] = 0
        state[2] = 0
        ahead = (first_s[0], 0)
        for k in range(MOE_XBUFS - 1):
            prefetch(ahead[0], ahead[1], k)
            ahead = following(*ahead)

    n_valid = len_s[e]
    n_tiles = (n_valid + cpt - 1) // cpt

    @pl.when(n_valid > 0)
    def _():
        wg_b[...] = _bf(wg_ref[0])
        wu_b[...] = _bf(wu_ref[0])
        wd_b[...] = _bf(wd_ref[0])

        def tile(t, carry):
            g = state[0]
            xsl = lax.rem(g, MOE_XBUFS)
            sl = g & 1
            tile_in(e, t, xsl).wait()
            ahead = (e, t)
            for _ in range(MOE_XBUFS - 1):
                ahead = following(*ahead)
            prefetch(ahead[0], ahead[1], lax.rem(g + MOE_XBUFS - 1, MOE_XBUFS))

            drain(sl)
            valid = jnp.minimum(n_valid - t * cpt, cpt)

            def ffn(n_c):
                x = xbuf[xsl, :n_c].reshape(n_c * MOE_CHUNK, d)
                a = _silu(_dot(x, wg_b[...])) * _dot(x, wu_b[...])
                ybuf[sl, :n_c] = _bf(_dot(_bf(a), wd_b[...])).reshape(n_c, MOE_CHUNK, d)

            @pl.when(valid > cpt // 2)
            def _():
                ffn(cpt)

            @pl.when(valid <= cpt // 2)
            def _():
                ffn(cpt // 2)

            dst = off_s[e] + t * cpt
            for_each_piece(valid, lambda k, piece, start: tile_out(sl, dst, k, piece, start).start())
            state[1 + sl] = valid
            state[0] = state[0] + 1
            return carry
        lax.fori_loop(0, n_tiles, tile, 0)

    @pl.when(e == n_exp - 1)
    def _():
        drain(0)
        drain(1)


def _experts(off, per_exp, nxt, first, xs, wg, wu, wd):
    n_chunks, _, d = xs.shape
    n_exp, _, ff = wg.shape
    cpt = MOE_TM // MOE_CHUNK

    def w_blk(e, *_):
        return (e, 0, 0)

    grid_spec = pltpu.PrefetchScalarGridSpec(
        num_scalar_prefetch=4,
        grid=(n_exp,),
        in_specs=[pl.BlockSpec(memory_space=pl.ANY),
                  pl.BlockSpec((1, d, ff), w_blk), pl.BlockSpec((1, d, ff), w_blk),
                  pl.BlockSpec((1, ff, d), w_blk)],
        out_specs=pl.BlockSpec(memory_space=pl.ANY),
        scratch_shapes=[pltpu.VMEM((MOE_XBUFS, cpt, MOE_CHUNK, d), BF16),
                        pltpu.VMEM((2, cpt, MOE_CHUNK, d), BF16),
                        pltpu.VMEM((d, ff), BF16), pltpu.VMEM((d, ff), BF16),
                        pltpu.VMEM((ff, d), BF16),
                        pltpu.SMEM((3,), jnp.int32),
                        pltpu.SemaphoreType.DMA((MOE_XBUFS,)),
                        pltpu.SemaphoreType.DMA((2, len(_tile_pieces())))],
    )
    return pl.pallas_call(
        _expert_kernel,
        out_shape=jax.ShapeDtypeStruct((n_chunks, MOE_CHUNK, d), BF16),
        grid_spec=grid_spec,
        compiler_params=pltpu.CompilerParams(dimension_semantics=("arbitrary",),
                                             vmem_limit_bytes=V7X_VMEM_LIMIT),
        name="moe_experts",
    )(off, per_exp, nxt, first, xs, wg, wu, wd)


def _combine_kernel(dtab_s, nch_s,
                    gt_ref, qcol_ref, qbcol_ref, h_ref, x1_ref, gate2_ref,
                    wsg_ref, wsu_ref, wsd_ref, ys_ref, o_ref, stage_ref, sem):
    b = pl.program_id(0)
    nb = pl.num_programs(0)
    slot = lax.rem(b, 2)
    n_exp, tb = gt_ref.shape
    rmax = stage_ref.shape[1] * MOE_CHUNK

    def fetch(bb, sl):
        def start_chunk(i):
            pltpu.make_async_copy(ys_ref.at[dtab_s[bb, i]], stage_ref.at[sl, i], sem.at[sl]).start()
        _for_each_chunk(nch_s[bb], start_chunk)

    @pl.when(b == 0)
    def _():
        stage_ref[...] = jnp.zeros_like(stage_ref)
        fetch(0, 0)

    def wait_chunk(i):
        pltpu.make_async_copy(ys_ref.at[0], stage_ref.at[slot, 0], sem.at[slot]).wait()
    _for_each_chunk(nch_s[b], wait_chunk)

    @pl.when(b + 1 < nb)
    def _():
        fetch(b + 1, 1 - slot)

    gt = gt_ref[...]
    routed = jnp.where(gt > 0.0, 1.0, 0.0).astype(BF16)
    i0 = lax.broadcasted_iota(jnp.int32, (tb, tb), 0)
    i1 = lax.broadcasted_iota(jnp.int32, (tb, tb), 1)
    eye = jnp.where(i0 == i1, 1.0, 0.0).astype(BF16)
    routed_t = _dot_nt(eye, routed)
    gates_tok = _dot_nt(eye, _bf(gt))
    earlier = jnp.where(i1 < i0, 1.0, 0.0).astype(BF16)
    pos_t = _dot(earlier, _bf(routed_t))
    posm_t = _bf(jnp.where(routed_t > 0.0, pos_t, -1.0))

    qcol = qcol_ref[0]
    qbcol = qbcol_ref[0]
    h = h_ref[...]
    a = _silu(_dot(h, wsg_ref[...])) * _dot(h, wsu_ref[...])
    shared = _dot(_bf(a), wsd_ref[...])

    def finish(rows):
        chunk = lax.broadcasted_iota(jnp.int32, (n_exp, rows), 1) >> MOE_CHUNK_SHIFT
        own = jnp.where(chunk >= qbcol, jnp.where(chunk < qbcol + qcol, 1.0, 0.0), 0.0)
        own_b = _bf(own)
        rank = _dot(posm_t, own_b)
        wexp = _dot(_bf(gates_tok), own_b)
        start = jnp.sum(own * qbcol.astype(F32), axis=0, keepdims=True) * MOE_CHUNK
        rel = lax.broadcasted_iota(jnp.int32, (1, rows), 1).astype(F32) - start
        weights = _bf(jnp.where(rank == rel, wexp, 0.0))
        staged = stage_ref[slot, :rows // MOE_CHUNK].reshape(rows, o_ref.shape[1])
        o_ref[...] = x1_ref[...] + gate2_ref[0] * (_dot(weights, staged) + shared)

    short = rmax - MOE_ROWGROUP

    @pl.when(nch_s[b] * MOE_CHUNK <= short)
    def _():
        finish(short)

    @pl.when(nch_s[b] * MOE_CHUNK > short)
    def _():
        finish(rmax)


def _combine(gates_t, q, qbase, dtab, nch, h2, x1, mod3, wsg, wsu, wsd, ys, seq):
    n_exp, t = gates_t.shape
    d = h2.shape[1]
    nb = t // MOE_TB
    per_b = seq // MOE_TB
    stage_chunks = _moe_stage_rows(n_exp) // MOE_CHUNK

    def full(a):
        return pl.BlockSpec(a.shape, lambda b, *_: (0,) * a.ndim)

    def rows(w):
        return pl.BlockSpec((MOE_TB, w), lambda b, *_: (b, 0))

    grid_spec = pltpu.PrefetchScalarGridSpec(
        num_scalar_prefetch=2,
        grid=(nb,),
        in_specs=[
            pl.BlockSpec((n_exp, MOE_TB), lambda b, *_: (0, b)),
            pl.BlockSpec((1, n_exp, 1), lambda b, *_: (b, 0, 0)),
            pl.BlockSpec((1, n_exp, 1), lambda b, *_: (b, 0, 0)),
            rows(d), rows(d),
            pl.BlockSpec((1, 1, d), lambda b, *_: (b // per_b, 0, 5)),
            full(wsg), full(wsu), full(wsd),
            pl.BlockSpec(memory_space=pl.ANY),
        ],
        out_specs=rows(d),
        scratch_shapes=[pltpu.VMEM((2, stage_chunks, MOE_CHUNK, d), BF16),
                        pltpu.SemaphoreType.DMA((2,))],
    )
    return pl.pallas_call(
        _combine_kernel,
        out_shape=jax.ShapeDtypeStruct((t, d), F32),
        grid_spec=grid_spec,
        compiler_params=pltpu.CompilerParams(dimension_semantics=("arbitrary",),
                                             vmem_limit_bytes=V7X_VMEM_LIMIT),
        name="moe_combine",
    )(dtab, nch, gates_t, q.reshape(nb, n_exp, 1), qbase.reshape(nb, n_exp, 1),
      h2, x1, mod3, wsg, wsu, wsd, ys)


def _moe_plan(q):
    nb, n_exp = q.shape
    qbase = jnp.cumsum(q, axis=1) - q
    nch = jnp.sum(q, axis=1)
    per_exp = jnp.sum(q, axis=0)
    off = jnp.cumsum(per_exp) - per_exp
    dstq = off[None, :] + jnp.cumsum(q, axis=0) - q
    i = jnp.arange(_moe_block_chunks(n_exp), dtype=jnp.int32)
    ii = i[None, :, None]
    owned = (ii >= qbase[:, None, :]) & (ii < (qbase + q)[:, None, :])
    dtab = i[None, :] + jnp.sum(jnp.where(owned, (dstq - qbase)[:, None, :], 0), axis=2)
    ids = jnp.arange(n_exp, dtype=jnp.int32)
    later = (ids[None, :] > ids[:, None]) & (per_exp[None, :] > 0)
    nxt = jnp.min(jnp.where(later, ids[None, :], n_exp), axis=1)
    first = jnp.min(jnp.where(per_exp > 0, ids, n_exp)).reshape(1)
    total = jnp.sum(per_exp).reshape(1)
    return qbase, dtab, nch, off, per_exp, nxt, first, total


def _moe_block_chunks(n_exp):
    return MOE_TB * TOP_K // MOE_CHUNK + n_exp


def _moe_stage_rows(n_exp):
    return -(-_moe_block_chunks(n_exp) * MOE_CHUNK // MOE_ROWGROUP) * MOE_ROWGROUP


def _moe(h2, gates_t, q3, x1, mod3, wsg, wsu, wsd, wg, wu, wd, seq):
    n_exp, t = gates_t.shape
    nb = t // MOE_TB
    q = q3.reshape(nb, n_exp)
    qbase, dtab, nch, off, per_exp, nxt, first, total = _moe_plan(q)
    n_chunks = nb * _moe_block_chunks(n_exp) + MOE_TM // MOE_CHUNK
    xs = _dispatch(gates_t, q, qbase, dtab, nch, total, h2, n_chunks)
    ys = _experts(off, per_exp, nxt, first, xs, wg, wu, wd)
    return _combine(gates_t, q, qbase, dtab, nch, h2, x1, mod3, wsg, wsu, wsd, ys, seq)


def _pad_heads(w, heads, width):
    lead = w.shape[:-1]
    w = w.reshape(lead + (heads, width))
    w = jnp.pad(w, [(0, 0)] * len(lead) + [(0, 0), (0, LANES - width)])
    return w.reshape(lead + (heads * LANES,))


def kernel(x, c, positions, w_ada, b_ada, g_norm1, w_in, g_na_q, g_na_k, na_rpb, g_q_lat, w_uq,
           g_kv_lat, w_ukv, g_mla_q, g_mla_k, w_proj_na, w_proj_mla, w_out, g_norm2, w_router,
           e_bias, w_exp_gate, w_exp_up, w_exp_down, w_sh_gate, w_sh_up, w_sh_down):
    bsz, seq, d = x.shape
    t = bsz * seq
    depth = w_ada.shape[0]
    na_w = NA_HEADS * NA_HEAD_DIM
    q_rank = g_q_lat.shape[1]
    kv_rank = g_kv_lat.shape[1]
    n_rows = seq // GRID_W

    pos = positions.reshape(1, t)
    half = MLA_ROPE_DIM // 2
    freq = (ROPE_THETA ** (-jnp.arange(half, dtype=F32) / half)).reshape(half, 1)

    x2 = x.reshape(t, d)
    for l in range(depth):
        mod3 = _adaln(c, w_ada[l], b_ada[l]).reshape(bsz, 1, 6 * d)

        wi = w_in[l]
        o_lat = 3 * na_w
        o_rot = o_lat + q_rank + kv_rank
        o_gate = o_rot + MLA_ROPE_DIM
        wqkv = _bf(wi[:, :o_lat])
        w_rot = jnp.pad(wi[:, o_rot:o_gate], ((0, 0), (MLA_NOPE_DIM, LANES - MLA_QK_DIM)))
        wlat = _bf(jnp.concatenate([wi[:, o_lat:o_rot], w_rot], axis=1))
        wgate = _bf(wi[:, o_gate:])
        gq = (jnp.tile(g_na_q[l], NA_HEADS) * (NA_HEAD_DIM ** -0.5 * LOG2E)).reshape(1, na_w)
        gk = jnp.tile(g_na_k[l], NA_HEADS).reshape(1, na_w)
        wuq = _bf(_pad_heads(w_uq[l], MLA_HEADS, MLA_QK_DIM))
        wukv = w_ukv[l].reshape(kv_rank, MLA_HEADS, MLA_NOPE_DIM + MLA_V_DIM)
        wuk = _bf(_pad_heads(wukv[:, :, :MLA_NOPE_DIM].reshape(kv_rank, -1), MLA_HEADS, MLA_NOPE_DIM))
        wuv = _bf(wukv[:, :, MLA_NOPE_DIM:].reshape(kv_rank, MLA_HEADS * MLA_V_DIM))
        gmq = _pad_heads(jnp.tile(g_mla_q[l], MLA_HEADS) * (MLA_QK_DIM ** -0.5 * LOG2E),
                         MLA_HEADS, MLA_QK_DIM).reshape(1, -1)
        gmk = _pad_heads(jnp.tile(g_mla_k[l], MLA_HEADS), MLA_HEADS, MLA_QK_DIM).reshape(1, -1)

        qa, ka, va, qm, km, vm, sgn, sgm = _inproj(
            x2, mod3, g_norm1[l].reshape(1, d), wqkv, wlat, wgate, gq, gk,
            g_q_lat[l].reshape(1, q_rank), g_kv_lat[l].reshape(1, kv_rank), wuq, wuk, wuv,
            gmq, gmk, pos, freq, seq)

        bias = _na_bias(na_rpb[l], n_rows)
        y_na = _na_attention(qa, ka, va, bias, bsz, seq)
        y_mla = _mla_attention(qm, km, vm, bsz, seq)

        x1, h2, lt = _merge(x2, y_na, y_mla, sgn, sgm, _bf(w_proj_na[l]), _bf(w_proj_mla[l]),
                            _bf(w_out[l]), mod3, g_norm2[l].reshape(1, d), w_router[l].T, seq)
        gates_t, q3 = _route(lt, e_bias[l])
        x2 = _moe(h2, gates_t, q3, x1, mod3, _bf(w_sh_gate[l]), _bf(w_sh_up[l]),
                  _bf(w_sh_down[l]), w_exp_gate[l], w_exp_up[l], w_exp_down[l], seq)
    return x2.reshape(bsz, seq, d)
```

```python
import functools

import jax
import jax.numpy as jnp
import numpy as np
from jax import lax
from jax.experimental import pallas as pl
from jax.experimental.pallas import tpu as pltpu

GRID_W = 64
NA_HEADS = 8
NA_HEAD_DIM = 64
NA_WIN_ROWS = 8
NA_WIN_COLS = 16
MLA_HEADS = 8
MLA_NOPE_DIM = 64
MLA_ROPE_DIM = 32
MLA_V_DIM = 64
MLA_QK_DIM = MLA_NOPE_DIM + MLA_ROPE_DIM
ROPE_THETA = 10000.0
N_GROUPS = 8
TOPK_GROUPS = 4
TOP_K = 8
ROUTED_SCALE = 2.5
EPS = 1e-6
NEG_BIG = -1e30

LANES = 128
V7X_VMEM_LIMIT = 56 * 1024 * 1024

NA_QROWS = 4
NA_BAND = 12
NA_BLOCK_TYPES = 3
NA_BLOCKS_PER_STEP = 4
MERGE_SUB = 512
INPROJ_TM = 512
INPROJ_SUB = 256
MLA_PAIRS_PER_STEP = 4
LOG2E = 1.4426950408889634
MOE_TB = 256
ROUTE_TN = 1024
MOE_CHUNK_SHIFT = 4
MOE_CHUNK = 1 << MOE_CHUNK_SHIFT
MOE_TM = 1024
MOE_ROWGROUP = 512
MOE_COPY_UNROLL = 4
MOE_XBUFS = 3

F32 = jnp.float32
BF16 = jnp.bfloat16


def _bf(x):
    return x.astype(BF16)


def _dot(a, b):
    return jnp.dot(a, b, preferred_element_type=F32)


def _dot_nt(a, b):
    return lax.dot_general(a, b, (((1,), (1,)), ((), ())), preferred_element_type=F32)


def _split(x):
    hi = _bf(x)
    lo = _bf(x - hi.astype(F32))
    return hi, lo


def _dot3(a, b):
    ah, al = _split(a)
    bh, bl = _split(b)
    return _dot(ah, bh) + (_dot(ah, bl) + _dot(al, bh))


def _dot3_nt(a, b):
    ah, al = _split(a)
    bh, bl = _split(b)
    return _dot_nt(ah, bh) + (_dot_nt(ah, bl) + _dot_nt(al, bh))


def _sigmoid(x):
    return 1.0 / (1.0 + jnp.exp(-x))


def _silu(x):
    return x * _sigmoid(x)


def _rms(x, n):
    ss = jnp.sum(x * x, axis=-1, keepdims=True)
    return x * lax.rsqrt(ss * (1.0 / n) + EPS)


def _adaln_kernel(c_ref, w_ref, b_ref, o_ref):
    c = c_ref[...]
    o_ref[...] = _dot3(_silu(c), w_ref[...]) + b_ref[...]


def _adaln(c, w, b):
    bsz, d = c.shape
    n = w.shape[1]
    tn = 1024
    return pl.pallas_call(
        _adaln_kernel,
        out_shape=jax.ShapeDtypeStruct((bsz, n), F32),
        grid=(n // tn,),
        in_specs=[
            pl.BlockSpec((bsz, d), lambda j: (0, 0)),
            pl.BlockSpec((d, tn), lambda j: (0, j)),
            pl.BlockSpec((1, tn), lambda j: (0, j)),
        ],
        out_specs=pl.BlockSpec((bsz, tn), lambda j: (0, j)),
        compiler_params=pltpu.CompilerParams(dimension_semantics=("arbitrary",)),
        name="adaln",
    )(c, w, b.reshape(1, n))


def _inproj_kernel(x_ref, shift_ref, scale_ref, g1_ref, wqkv_ref, wlat_ref, wgate_ref,
                   gq_ref, gk_ref, gql_ref, gkvl_ref, wuq_ref, wuk_ref, wuv_ref,
                   gmq_ref, gmk_ref, pos_ref, freq_ref,
                   qa_ref, ka_ref, va_ref, qm_ref, km_ref, vm_ref, sgn_ref, sgm_ref):
    for r0 in range(0, x_ref.shape[0], INPROJ_SUB):
        _inproj_rows(pl.ds(r0, INPROJ_SUB), x_ref, shift_ref, scale_ref, g1_ref, wqkv_ref,
                     wlat_ref, wgate_ref, gq_ref, gk_ref, gql_ref, gkvl_ref, wuq_ref, wuk_ref,
                     wuv_ref, gmq_ref, gmk_ref, pos_ref, freq_ref, qa_ref, ka_ref, va_ref,
                     qm_ref, km_ref, vm_ref, sgn_ref, sgm_ref)


def _inproj_rows(rs, x_ref, shift_ref, scale_ref, g1_ref, wqkv_ref, wlat_ref, wgate_ref,
                 gq_ref, gk_ref, gql_ref, gkvl_ref, wuq_ref, wuk_ref, wuv_ref,
                 gmq_ref, gmk_ref, pos_ref, freq_ref,
                 qa_ref, ka_ref, va_ref, qm_ref, km_ref, vm_ref, sgn_ref, sgm_ref):
    d = x_ref.shape[1]
    x = x_ref[rs, :]
    h = _rms(x, d) * g1_ref[...]
    h = h * (1.0 + scale_ref[0]) + shift_ref[0]
    hb = _bf(h)

    qkv = _dot(hb, wqkv_ref[...])
    lat = _dot(hb, wlat_ref[...])
    gts = _dot(hb, wgate_ref[...])
    sgn_ref[rs, :] = _bf(_sigmoid(gts[:, :d]))
    sgm_ref[rs, :] = _bf(_sigmoid(gts[:, d:]))

    na_w = NA_HEADS * NA_HEAD_DIM
    lane = lax.broadcasted_iota(jnp.int32, (1, LANES), 1)
    lo_half = lane < NA_HEAD_DIM
    for p in range(na_w // LANES):
        sl = slice(p * LANES, (p + 1) * LANES)
        for src_off, g_ref, dst_ref in ((0, gq_ref, qa_ref), (na_w, gk_ref, ka_ref)):
            t = qkv[:, src_off + p * LANES: src_off + (p + 1) * LANES]
            sq = t * t
            s_lo = jnp.sum(jnp.where(lo_half, sq, 0.0), axis=-1, keepdims=True)
            s_hi = jnp.sum(jnp.where(lo_half, 0.0, sq), axis=-1, keepdims=True)
            r = jnp.where(lo_half,
                          lax.rsqrt(s_lo * (1.0 / NA_HEAD_DIM) + EPS),
                          lax.rsqrt(s_hi * (1.0 / NA_HEAD_DIM) + EPS))
            dst_ref[rs, sl] = _bf(t * r * g_ref[:, sl])
    va_ref[rs, :] = _bf(qkv[:, 2 * na_w: 3 * na_w])

    q_rank = gql_ref.shape[1]
    kv_rank = gkvl_ref.shape[1]
    qln = _rms(lat[:, :q_rank], q_rank) * gql_ref[...]
    kvn = _bf(_rms(lat[:, q_rank:q_rank + kv_rank], kv_rank) * gkvl_ref[...])
    qpre = _dot(_bf(qln), wuq_ref[...])
    knope = _dot(kvn, wuk_ref[...])
    vm_ref[rs, :] = _bf(_dot(kvn, wuv_ref[...]))
    krot = lat[:, q_rank + kv_rank:]

    tm = INPROJ_SUB
    half = MLA_ROPE_DIM // 2
    ang_t = freq_ref[...] * pos_ref[:, rs].astype(F32)
    cos_t = jnp.cos(ang_t)
    sin_t = jnp.sin(ang_t)
    l_i = lax.broadcasted_iota(jnp.int32, (LANES, half), 0)
    j_i = lax.broadcasted_iota(jnp.int32, (LANES, half), 1)
    hit = jnp.where((l_i >= MLA_NOPE_DIM) & (l_i < MLA_QK_DIM)
                    & (((l_i - MLA_NOPE_DIM) & (half - 1)) == j_i), 1.0, 0.0)
    first_half = l_i < MLA_NOPE_DIM + half
    eye = jnp.where(lax.broadcasted_iota(jnp.int32, (tm, tm), 0)
                    == lax.broadcasted_iota(jnp.int32, (tm, tm), 1), 1.0, 0.0).astype(BF16)

    def table(sel, vals, fill_nope):
        hi, lo = _split(vals)
        w = _dot(_bf(sel), hi) + _dot(_bf(sel), lo)
        if fill_nope:
            w = jnp.where(lax.broadcasted_iota(jnp.int32, (LANES, tm), 0) < MLA_NOPE_DIM, 1.0, w)
        hi, lo = _split(w)
        return _dot_nt(eye, hi) + _dot_nt(eye, lo)

    c_tab = table(hit, cos_t, True)
    s_up = table(jnp.where(first_half, 0.0, hit), sin_t, False)
    s_dn = table(jnp.where(first_half, -hit, 0.0), sin_t, False)

    def rope(t):
        return t * c_tab + pltpu.roll(t, half, 1) * s_up + pltpu.roll(t, LANES - half, 1) * s_dn

    kr = rope(krot)
    for hd in range(MLA_HEADS):
        sl = slice(hd * LANES, (hd + 1) * LANES)
        qh = rope(qpre[:, sl])
        qm_ref[rs, sl] = _bf(_rms(qh, MLA_QK_DIM) * gmq_ref[:, sl])
        kh = knope[:, sl] + kr
        km_ref[rs, sl] = _bf(_rms(kh, MLA_QK_DIM) * gmk_ref[:, sl])


def _inproj(x2, mod3, g1, wqkv, wlat, wgate, gq, gk, gql, gkvl, wuq, wuk, wuv, gmq, gmk,
            pos, freq, seq):
    t, d = x2.shape
    tm = INPROJ_TM
    per_b = seq // tm
    na_w = NA_HEADS * NA_HEAD_DIM
    mla_w = MLA_HEADS * LANES
    v_w = MLA_HEADS * MLA_V_DIM

    def full(a):
        return pl.BlockSpec(a.shape, lambda i: (0,) * a.ndim)

    def rows(w):
        return pl.BlockSpec((tm, w), lambda i: (i, 0))

    out_shapes = (
        jax.ShapeDtypeStruct((t, na_w), BF16), jax.ShapeDtypeStruct((t, na_w), BF16),
        jax.ShapeDtypeStruct((t, na_w), BF16),
        jax.ShapeDtypeStruct((t, mla_w), BF16), jax.ShapeDtypeStruct((t, mla_w), BF16),
        jax.ShapeDtypeStruct((t, v_w), BF16),
        jax.ShapeDtypeStruct((t, d), BF16), jax.ShapeDtypeStruct((t, d), BF16),
    )
    return pl.pallas_call(
        _inproj_kernel,
        out_shape=out_shapes,
        grid=(t // tm,),
        in_specs=[
            rows(d),
            pl.BlockSpec((1, 1, d), lambda i: (i // per_b, 0, 0)),
            pl.BlockSpec((1, 1, d), lambda i: (i // per_b, 0, 1)),
            full(g1), full(wqkv), full(wlat), full(wgate), full(gq), full(gk), full(gql),
            full(gkvl), full(wuq), full(wuk), full(wuv), full(gmq), full(gmk),
            pl.BlockSpec((1, tm), lambda i: (0, i)),
            full(freq),
        ],
        out_specs=(rows(na_w), rows(na_w), rows(na_w), rows(mla_w), rows(mla_w), rows(v_w),
                   rows(d), rows(d)),
        compiler_params=pltpu.CompilerParams(dimension_semantics=("arbitrary",),
                                             vmem_limit_bytes=V7X_VMEM_LIMIT),
        name="inproj",
    )(x2, mod3, mod3, g1, wqkv, wlat, wgate, gq, gk, gql, gkvl, wuq, wuk, wuv, gmq, gmk,
      pos, freq)


def _na_block_geometry(block_type, n_rows):
    if block_type == 0:
        return 0, 0
    if block_type == 1:
        r0 = NA_QROWS
        return r0, r0 - NA_WIN_ROWS // 2
    return n_rows - NA_QROWS, n_rows - NA_BAND


def _na_bias_kernel(rpb_ref, o_ref, m_ref, *, n_rows):
    hd = pl.program_id(0)
    n_dr = 2 * NA_WIN_ROWS - 1
    n_dc = 2 * NA_WIN_COLS - 1
    qc = lax.broadcasted_iota(jnp.int32, (GRID_W, LANES), 0)
    kc = lax.broadcasted_iota(jnp.int32, (GRID_W, LANES), 1) & (GRID_W - 1)
    dc = jnp.clip(kc - qc, -(NA_WIN_COLS - 1), NA_WIN_COLS - 1) + (NA_WIN_COLS - 1)
    cstart = jnp.clip(qc - NA_WIN_COLS // 2, 0, GRID_W - NA_WIN_COLS)
    col_ok = (kc >= cstart) & (kc < cstart + NA_WIN_COLS)
    for i_dr in range(n_dr):
        acc = jnp.zeros((GRID_W, LANES), F32)
        for t in range(n_dc):
            acc = jnp.where(dc == t, rpb_ref[hd, i_dr * n_dc + t], acc)
        m_ref[i_dr] = jnp.where(col_ok, acc * LOG2E, NEG_BIG)
    neg = jnp.full((GRID_W, LANES), NEG_BIG, F32)
    lo_half = lax.broadcasted_iota(jnp.int32, (GRID_W, LANES), 1) < GRID_W
    kh = NA_WIN_ROWS
    for bt in range(NA_BLOCK_TYPES):
        r0, start = _na_block_geometry(bt, n_rows)
        for i in range(NA_QROWS):
            r = r0 + i
            rs = min(max(r - kh // 2, 0), n_rows - kh)
            for jp in range(NA_BAND // 2):
                halves = []
                for j in (2 * jp, 2 * jp + 1):
                    krow = start + j
                    if rs <= krow < rs + kh:
                        halves.append(m_ref[krow - r + (NA_WIN_ROWS - 1)])
                    else:
                        halves.append(neg)
                tile = jnp.where(lo_half, halves[0], halves[1])
                o_ref[bt, 0, i * GRID_W:(i + 1) * GRID_W, jp * LANES:(jp + 1) * LANES] = tile


def _na_bias(rpb, n_rows):
    heads = rpb.shape[0]
    nq = NA_QROWS * GRID_W
    nk = NA_BAND * GRID_W
    rpb2 = rpb.reshape(heads, -1)
    return pl.pallas_call(
        functools.partial(_na_bias_kernel, n_rows=n_rows),
        out_shape=jax.ShapeDtypeStruct((NA_BLOCK_TYPES, heads, nq, nk), F32),
        grid=(heads,),
        in_specs=[pl.BlockSpec(memory_space=pltpu.SMEM)],
        out_specs=pl.BlockSpec((NA_BLOCK_TYPES, 1, nq, nk), lambda hd: (0, hd, 0, 0)),
        scratch_shapes=[pltpu.VMEM((2 * NA_WIN_ROWS - 1, GRID_W, LANES), F32)],
        compiler_params=pltpu.CompilerParams(dimension_semantics=("arbitrary",)),
        name="na_bias",
    )(rpb2)


def _softmax_pv(s, v_pair, hh, half):
    lane = lax.broadcasted_iota(jnp.int32, (1, LANES), 1)
    mine = (lane < half) if hh == 0 else (lane >= half)
    den_lane = half if hh == 0 else 0
    m = jnp.max(s, axis=-1, keepdims=True)
    p = _bf(jnp.exp2(s - m))
    ones_row = jnp.where(lane == den_lane, 1.0, 0.0).astype(BF16)
    o = _dot(p, jnp.where(mine, v_pair, ones_row))
    den = jnp.sum(jnp.where(lane == den_lane, o, 0.0), axis=-1, keepdims=True)
    return jnp.where(mine, o / den, 0.0)


def _na_kernel(q_ref, k_ref, v_ref, bias_ref, o_ref, *, n_blocks, n_rows):
    nk = NA_BAND * GRID_W
    tq = NA_QROWS * GRID_W
    lo_half = lax.broadcasted_iota(jnp.int32, (1, LANES), 1) < NA_HEAD_DIM
    for u in range(NA_BLOCKS_PER_STEP):
        blk = pl.program_id(1) * NA_BLOCKS_PER_STEP + u
        btype = jnp.where(blk == 0, 0, jnp.where(blk == n_blocks - 1, 2, 1))
        start_row = jnp.where(blk == 0, 0,
                              jnp.where(blk == n_blocks - 1, n_rows - NA_BAND,
                                        blk * NA_QROWS - NA_WIN_ROWS // 2))
        off = pl.multiple_of(start_row * GRID_W, GRID_W)
        rs = pl.ds(u * tq, tq)
        for p in range(NA_HEADS * NA_HEAD_DIM // LANES):
            sl = slice(p * LANES, (p + 1) * LANES)
            qp = q_ref[rs, sl]
            kb = k_ref[pl.ds(off, nk), sl]
            vb = v_ref[pl.ds(off, nk), sl]
            zero = jnp.zeros_like(qp)
            q2 = jnp.concatenate([jnp.where(lo_half, qp, zero), jnp.where(lo_half, zero, qp)],
                                 axis=0)
            s = _dot_nt(q2, kb) + bias_ref[btype, 2 * p:2 * p + 2].reshape(2 * tq, nk)
            m = jnp.max(s, axis=-1, keepdims=True)
            e = jnp.exp2(s - m)
            den = jnp.sum(e, axis=-1, keepdims=True)
            o = _dot(_bf(e), vb) / den
            o_ref[rs, sl] = _bf(jnp.where(lo_half, o[:tq], o[tq:]))


def _na_attention(qa, ka, va, bias, bsz, seq):
    t, w = qa.shape
    n_rows = seq // GRID_W
    n_blocks = n_rows // NA_QROWS
    tq = NA_QROWS * GRID_W
    nk = NA_BAND * GRID_W
    heads = bias.shape[1]

    bps = NA_BLOCKS_PER_STEP
    steps = n_blocks // bps
    return pl.pallas_call(
        functools.partial(_na_kernel, n_blocks=n_blocks, n_rows=n_rows),
        out_shape=jax.ShapeDtypeStruct((t, w), BF16),
        grid=(bsz, steps),
        in_specs=[
            pl.BlockSpec((bps * tq, w), lambda b, j: (b * steps + j, 0)),
            pl.BlockSpec((seq, w), lambda b, j: (b, 0)),
            pl.BlockSpec((seq, w), lambda b, j: (b, 0)),
            pl.BlockSpec(bias.shape, lambda b, j: (0, 0, 0, 0), pipeline_mode=pl.Buffered(1)),
        ],
        out_specs=pl.BlockSpec((bps * tq, w), lambda b, j: (b * steps + j, 0)),
        compiler_params=pltpu.CompilerParams(dimension_semantics=("arbitrary", "arbitrary"),
                                             vmem_limit_bytes=V7X_VMEM_LIMIT),
        name="na_attn",
    )(qa, ka, va, bias)


def _mla_kernel(q_ref, k_ref, v_ref, o_ref):
    for pp in range(MLA_PAIRS_PER_STEP):
        v_pair = v_ref[:, pp * LANES:(pp + 1) * LANES]
        acc = jnp.zeros((q_ref.shape[0], LANES), F32)
        for hh in range(2):
            sl = slice((2 * pp + hh) * LANES, (2 * pp + hh + 1) * LANES)
            s = _dot_nt(q_ref[:, sl], k_ref[:, sl])
            acc = acc + _softmax_pv(s, v_pair, hh, MLA_V_DIM)
        o_ref[:, pp * LANES:(pp + 1) * LANES] = _bf(acc)


def _mla_attention(qm, km, vm, bsz, seq):
    t = qm.shape[0]
    tq = 512
    nq = seq // tq
    groups = MLA_HEADS // (2 * MLA_PAIRS_PER_STEP)
    qk_w = 2 * MLA_PAIRS_PER_STEP * LANES
    v_w = MLA_PAIRS_PER_STEP * LANES
    return pl.pallas_call(
        _mla_kernel,
        out_shape=jax.ShapeDtypeStruct((t, MLA_HEADS * MLA_V_DIM), BF16),
        grid=(bsz, groups, nq),
        in_specs=[
            pl.BlockSpec((tq, qk_w), lambda b, p, i: (b * nq + i, p)),
            pl.BlockSpec((seq, qk_w), lambda b, p, i: (b, p)),
            pl.BlockSpec((seq, v_w), lambda b, p, i: (b, p)),
        ],
        out_specs=pl.BlockSpec((tq, v_w), lambda b, p, i: (b * nq + i, p)),
        compiler_params=pltpu.CompilerParams(
            dimension_semantics=("arbitrary", "arbitrary", "arbitrary"),
            vmem_limit_bytes=V7X_VMEM_LIMIT),
        name="mla_attn",
    )(qm, km, vm)


def _merge_kernel(x_ref, yna_ref, ymla_ref, sgn_ref, sgm_ref, wpn_ref, wpm_ref, wout_ref,
                  gate1_ref, shift2_ref, scale2_ref, g2_ref, wr_ref,
                  x1_ref, h2_ref, lt_ref):
    d = x_ref.shape[1]
    for r0 in range(0, x_ref.shape[0], MERGE_SUB):
        rs = pl.ds(r0, MERGE_SUB)
        merged = (sgn_ref[rs, :].astype(F32) * _dot(yna_ref[rs, :], wpn_ref[...])
                  + sgm_ref[rs, :].astype(F32) * _dot(ymla_ref[rs, :], wpm_ref[...]))
        x1 = x_ref[rs, :] + gate1_ref[0] * _dot(_bf(merged), wout_ref[...])
        x1_ref[rs, :] = x1
        h2 = _rms(x1, d) * g2_ref[...]
        h2 = h2 * (1.0 + scale2_ref[0]) + shift2_ref[0]
        h2_ref[rs, :] = _bf(h2)
        lt_ref[:, rs] = _dot3_nt(wr_ref[...], h2)


def _merge(x2, yna, ymla, sgn, sgm, wpn, wpm, wout, mod3, g2, wr, seq):
    t, d = x2.shape
    tm = 512
    per_b = seq // tm
    n_exp = wr.shape[0]

    def full(a):
        return pl.BlockSpec(a.shape, lambda i: (0,) * a.ndim)

    def rows(w):
        return pl.BlockSpec((tm, w), lambda i: (i, 0))

    def modblk(j):
        return pl.BlockSpec((1, 1, d), lambda i: (i // per_b, 0, j))

    return pl.pallas_call(
        _merge_kernel,
        out_shape=(jax.ShapeDtypeStruct((t, d), F32), jax.ShapeDtypeStruct((t, d), BF16),
                   jax.ShapeDtypeStruct((n_exp, t), F32)),
        grid=(t // tm,),
        in_specs=[rows(d), rows(yna.shape[1]), rows(ymla.shape[1]), rows(d), rows(d),
                  full(wpn), full(wpm), full(wout),
                  modblk(2), modblk(3), modblk(4), full(g2), full(wr)],
        out_specs=(rows(d), rows(d), pl.BlockSpec((n_exp, tm), lambda i: (0, i))),
        compiler_params=pltpu.CompilerParams(dimension_semantics=("arbitrary",),
                                             vmem_limit_bytes=V7X_VMEM_LIMIT),
        name="merge",
    )(x2, yna, ymla, sgn, sgm, wpn, wpm, wout, mod3, mod3, mod3, g2, wr)


def _route_kernel(lt_ref, eb_ref, o_ref, q_ref):
    n_exp, tn = lt_ref.shape
    per_g = n_exp // N_GROUPS
    neg_inf = -jnp.inf
    sc = _sigmoid(lt_ref[...])
    sel = sc + eb_ref[...]
    sc3 = sc.reshape(N_GROUPS, per_g, tn)
    g3 = sel.reshape(N_GROUPS, per_g, tn)
    io = lax.broadcasted_iota(jnp.int32, (N_GROUPS, per_g, tn), 1)
    gio = lax.broadcasted_iota(jnp.int32, (N_GROUPS, per_g, tn), 0)
    eio = gio * per_g + io

    m1 = jnp.max(g3, axis=1, keepdims=True)
    i1 = jnp.min(jnp.where(g3 == m1, io, per_g), axis=1, keepdims=True)
    m2 = jnp.max(jnp.where(io == i1, neg_inf, g3), axis=1, keepdims=True)
    gs = m1 + m2

    g1io = lax.broadcasted_iota(jnp.int32, (N_GROUPS, 1, tn), 0)
    gsel = jnp.zeros((N_GROUPS, 1, tn), F32)
    cur = gs
    for _ in range(TOPK_GROUPS):
        m = jnp.max(cur, axis=0, keepdims=True)
        i = jnp.min(jnp.where(cur == m, g1io, N_GROUPS), axis=0, keepdims=True)
        pick = g1io == i
        gsel = jnp.where(pick, 1.0, gsel)
        cur = jnp.where(pick, neg_inf, cur)

    cur = jnp.where(gsel > 0.0, g3, neg_inf)
    chosen = jnp.zeros((N_GROUPS, per_g, tn), F32)
    for _ in range(TOP_K):
        m = jnp.max(jnp.max(cur, axis=1, keepdims=True), axis=0, keepdims=True)
        cand = jnp.where(cur == m, eio, n_exp)
        i = jnp.min(jnp.min(cand, axis=1, keepdims=True), axis=0, keepdims=True)
        pick = eio == i
        chosen = jnp.where(pick, 1.0, chosen)
        cur = jnp.where(pick, neg_inf, cur)

    w = jnp.where(chosen > 0.0, sc3, 0.0)
    tot = jnp.sum(jnp.sum(w, axis=1, keepdims=True), axis=0, keepdims=True)
    gates = (w / tot * ROUTED_SCALE).reshape(n_exp, tn)
    o_ref[...] = gates
    routed = jnp.where(gates > 0.0, 1.0, 0.0).astype(BF16)
    ones = jnp.ones((8, MOE_TB), BF16)
    for j in range(tn // MOE_TB):
        n_row = _dot_nt(ones, routed[:, j * MOE_TB:(j + 1) * MOE_TB])[0:1]
        q_ref[j] = jnp.floor((n_row + (MOE_CHUNK - 1)) * (1.0 / MOE_CHUNK)).astype(jnp.int32)


def _route(lt, e_bias):
    n_exp, t = lt.shape
    tn = ROUTE_TN
    bps = tn // MOE_TB
    return pl.pallas_call(
        _route_kernel,
        out_shape=(jax.ShapeDtypeStruct((n_exp, t), F32),
                   jax.ShapeDtypeStruct((t // MOE_TB, 1, n_exp), jnp.int32)),
        grid=(t // tn,),
        in_specs=[pl.BlockSpec((n_exp, tn), lambda i: (0, i)),
                  pl.BlockSpec((n_exp, 1), lambda i: (0, 0))],
        out_specs=(pl.BlockSpec((n_exp, tn), lambda i: (0, i)),
                   pl.BlockSpec((bps, 1, n_exp), lambda i: (i, 0, 0))),
        compiler_params=pltpu.CompilerParams(dimension_semantics=("arbitrary",)),
        name="route",
    )(lt, e_bias.reshape(n_exp, 1))


def _for_each_chunk(n, fn):
    shift = MOE_COPY_UNROLL.bit_length() - 1

    def group(j, carry):
        for u in range(MOE_COPY_UNROLL):
            fn(j * MOE_COPY_UNROLL + u)
        return carry
    lax.fori_loop(0, n >> shift, group, 0)
    base = (n >> shift) << shift
    for u in range(MOE_COPY_UNROLL - 1):
        @pl.when(base + u < n)
        def _():
            fn(base + u)


def _dispatch_kernel(src2_s, dst2_s, n2_s, src1_s, dst1_s, n1_s, nch_s, total_s,
                     gt_ref, qrow_ref, qbrow_ref, h_ref, xs_ref, stage_ref, zero_ref, sem):
    b = pl.program_id(0)
    nb = pl.num_programs(0)
    slot = lax.rem(b, 2)
    n_exp, tb = gt_ref.shape
    rmax = stage_ref.shape[1] * MOE_CHUNK
    cpg = MOE_ROWGROUP // MOE_CHUNK

    routed = gt_ref[...] > 0.0
    before = (lax.broadcasted_iota(jnp.int32, (tb, tb), 0)
              < lax.broadcasted_iota(jnp.int32, (tb, tb), 1))
    pos = _dot(jnp.where(routed, 1.0, 0.0).astype(BF16), jnp.where(before, 1.0, 0.0).astype(BF16))
    posm = _bf(jnp.where(routed, pos, -1.0))
    qrow = qrow_ref[0]
    qbrow = qbrow_ref[0]
    qbrow_f = qbrow.astype(F32)
    h = h_ref[...]
    def sort_rows(g):
        r0 = g * MOE_ROWGROUP
        chunk = (lax.broadcasted_iota(jnp.int32, (MOE_ROWGROUP, n_exp), 0) + r0) >> MOE_CHUNK_SHIFT
        own = jnp.where(chunk >= qbrow, jnp.where(chunk < qbrow + qrow, 1.0, 0.0), 0.0)
        rank = _dot(_bf(own), posm)
        start = jnp.sum(own * qbrow_f, axis=-1, keepdims=True) * MOE_CHUNK
        rel = (lax.broadcasted_iota(jnp.int32, (MOE_ROWGROUP, 1), 0) + r0).astype(F32) - start
        onehot = jnp.where(rank == rel, 1.0, 0.0).astype(BF16)
        rows = _bf(_dot(onehot, h))
        stage_ref[slot, g * cpg:(g + 1) * cpg] = rows.reshape(cpg, MOE_CHUNK, rows.shape[1])

    n_groups = rmax // MOE_ROWGROUP
    n_sure = min(n_groups, (MOE_TB * TOP_K + MOE_ROWGROUP - 1) // MOE_ROWGROUP)
    for g in range(n_sure):
        sort_rows(g)
    for g in range(n_sure, n_groups):
        @pl.when(g * MOE_ROWGROUP < nch_s[b] * MOE_CHUNK)
        def _():
            sort_rows(g)

    def copy2(sl, src, dst):
        return pltpu.make_async_copy(stage_ref.at[sl, pl.ds(src, 2)], xs_ref.at[pl.ds(dst, 2)],
                                     sem.at[sl, 0])

    def copy1(sl, src_ref, dst):
        return pltpu.make_async_copy(src_ref, xs_ref.at[dst], sem.at[sl, 1])

    _for_each_chunk(n2_s[b], lambda i: copy2(slot, src2_s[b, i], dst2_s[b, i]).start())
    _for_each_chunk(n1_s[b], lambda i: copy1(slot, stage_ref.at[slot, src1_s[b, i]],
                                             dst1_s[b, i]).start())

    def wait_copies(sl, n2, n1):
        _for_each_chunk(n2, lambda i: copy2(sl, 0, 0).wait())
        _for_each_chunk(n1, lambda i: copy1(sl, zero_ref, 0).wait())

    @pl.when(b > 0)
    def _():
        wait_copies(1 - slot, n2_s[b - 1], n1_s[b - 1])

    @pl.when(b == nb - 1)
    def _():
        zero_ref[...] = jnp.zeros_like(zero_ref)
        n_tail = MOE_TM // MOE_CHUNK
        for c in range(n_tail):
            copy1(slot, zero_ref, total_s[0] + c).start()
        wait_copies(slot, n2_s[b], n1_s[b] + n_tail)


def _dispatch(gates_t, q, qbase, copies, nch, total, h2, n_chunks):
    n_exp, t = gates_t.shape
    d = h2.shape[1]
    nb = t // MOE_TB
    rmax = _moe_stage_rows(n_exp)
    grid_spec = pltpu.PrefetchScalarGridSpec(
        num_scalar_prefetch=8,
        grid=(nb,),
        in_specs=[
            pl.BlockSpec((n_exp, MOE_TB), lambda b, *_: (0, b)),
            pl.BlockSpec((1, 1, n_exp), lambda b, *_: (b, 0, 0)),
            pl.BlockSpec((1, 1, n_exp), lambda b, *_: (b, 0, 0)),
            pl.BlockSpec((MOE_TB, d), lambda b, *_: (b, 0)),
        ],
        out_specs=pl.BlockSpec(memory_space=pl.ANY),
        scratch_shapes=[pltpu.VMEM((2, rmax // MOE_CHUNK, MOE_CHUNK, d), BF16),
                        pltpu.VMEM((MOE_CHUNK, d), BF16), pltpu.SemaphoreType.DMA((2, 2))],
    )
    return pl.pallas_call(
        _dispatch_kernel,
        out_shape=jax.ShapeDtypeStruct((n_chunks, MOE_CHUNK, d), BF16),
        grid_spec=grid_spec,
        compiler_params=pltpu.CompilerParams(dimension_semantics=("arbitrary",),
                                             vmem_limit_bytes=V7X_VMEM_LIMIT),
        name="moe_dispatch",
    )(*copies, nch, total, gates_t, q.reshape(nb, 1, n_exp), qbase.reshape(nb, 1, n_exp), h2)


def _tile_pieces():
    cpt = MOE_TM // MOE_CHUNK
    return [1 << s for s in range(cpt.bit_length() - 1, -1, -1)]


def _expert_kernel(off_s, len_s, next_s, first_s, xs_ref, wg_ref, wu_ref, wd_ref, ys_ref,
                   xbuf, ybuf, wg_b, wu_b, wd_b, state, sem_in, sem_out):
    e = pl.program_id(0)
    n_exp = pl.num_programs(0)
    cpt = MOE_TM // MOE_CHUNK
    d = xbuf.shape[3]
    pieces = _tile_pieces()

    def tile_in(ee, tt, sl):
        return pltpu.make_async_copy(xs_ref.at[pl.ds(off_s[ee] + tt * cpt, cpt)], xbuf.at[sl],
                                     sem_in.at[sl])

    def for_each_piece(valid, fn):
        for k, piece in enumerate(pieces):
            @pl.when((valid & piece) != 0)
            def _():
                fn(k, piece, valid & ~(2 * piece - 1))

    def tile_out(sl, dst_chunk, k, piece, start):
        return pltpu.make_async_copy(ybuf.at[sl, pl.ds(start, piece)],
                                     ys_ref.at[pl.ds(dst_chunk + start, piece)], sem_out.at[sl, k])

    def drain(sl):
        for_each_piece(state[1 + sl], lambda k, piece, start: tile_out(sl, 0, k, piece, start).wait())
        state[1 + sl] = 0

    def following(ee, tt):
        safe = jnp.minimum(ee, n_exp - 1)
        more = tt + 1 < (len_s[safe] + cpt - 1) // cpt
        nxt_e = jnp.where(ee >= n_exp, n_exp, jnp.where(more, ee, next_s[safe]))
        return nxt_e, jnp.where(more, tt + 1, 0)

    def prefetch(ee, tt, xsl):
        @pl.when(ee < n_exp)
        def _():
            tile_in(ee, tt, xsl).start()

    @pl.when(e == 0)
    def _():
        state[0] = 0
        state[1] = 0
        state[2] = 0
        ahead = (first_s[0], 0)
        for k in range(MOE_XBUFS - 1):
            prefetch(ahead[0], ahead[1], k)
            ahead = following(*ahead)

    n_valid = len_s[e]
    n_tiles = (n_valid + cpt - 1) // cpt

    @pl.when(n_valid > 0)
    def _():
        wg_b[...] = _bf(wg_ref[0])
        wu_b[...] = _bf(wu_ref[0])
        wd_b[...] = _bf(wd_ref[0])

        def tile(t, carry):
            g = state[0]
            xsl = lax.rem(g, MOE_XBUFS)
            sl = g & 1
            tile_in(e, t, xsl).wait()
            ahead = (e, t)
            for _ in range(MOE_XBUFS - 1):
                ahead = following(*ahead)
            prefetch(ahead[0], ahead[1], lax.rem(g + MOE_XBUFS - 1, MOE_XBUFS))

            drain(sl)
            valid = jnp.minimum(n_valid - t * cpt, cpt)

            def ffn(n_c):
                x = xbuf[xsl, :n_c].reshape(n_c * MOE_CHUNK, d)
                a = _silu(_dot(x, wg_b[...])) * _dot(x, wu_b[...])
                ybuf[sl, :n_c] = _bf(_dot(_bf(a), wd_b[...])).reshape(n_c, MOE_CHUNK, d)

            @pl.when(valid > cpt // 2)
            def _():
                ffn(cpt)

            @pl.when(valid <= cpt // 2)
            def _():
                ffn(cpt // 2)

            dst = off_s[e] + t * cpt
            for_each_piece(valid, lambda k, piece, start: tile_out(sl, dst, k, piece, start).start())
            state[1 + sl] = valid
            state[0] = state[0] + 1
            return carry
        lax.fori_loop(0, n_tiles, tile, 0)

    @pl.when(e == n_exp - 1)
    def _():
        drain(0)
        drain(1)


def _experts(off, per_exp, nxt, first, xs, wg, wu, wd):
    n_chunks, _, d = xs.shape
    n_exp, _, ff = wg.shape
    cpt = MOE_TM // MOE_CHUNK

    def w_blk(e, *_):
        return (e, 0, 0)

    grid_spec = pltpu.PrefetchScalarGridSpec(
        num_scalar_prefetch=4,
        grid=(n_exp,),
        in_specs=[pl.BlockSpec(memory_space=pl.ANY),
                  pl.BlockSpec((1, d, ff), w_blk), pl.BlockSpec((1, d, ff), w_blk),
                  pl.BlockSpec((1, ff, d), w_blk)],
        out_specs=pl.BlockSpec(memory_space=pl.ANY),
        scratch_shapes=[pltpu.VMEM((MOE_XBUFS, cpt, MOE_CHUNK, d), BF16),
                        pltpu.VMEM((2, cpt, MOE_CHUNK, d), BF16),
                        pltpu.VMEM((d, ff), BF16), pltpu.VMEM((d, ff), BF16),
                        pltpu.VMEM((ff, d), BF16),
                        pltpu.SMEM((3,), jnp.int32),
                        pltpu.SemaphoreType.DMA((MOE_XBUFS,)),
                        pltpu.SemaphoreType.DMA((2, len(_tile_pieces())))],
    )
    return pl.pallas_call(
        _expert_kernel,
        out_shape=jax.ShapeDtypeStruct((n_chunks, MOE_CHUNK, d), BF16),
        grid_spec=grid_spec,
        compiler_params=pltpu.CompilerParams(dimension_semantics=("arbitrary",),
                                             vmem_limit_bytes=V7X_VMEM_LIMIT),
        name="moe_experts",
    )(off, per_exp, nxt, first, xs, wg, wu, wd)


def _combine_kernel(src2_s, dst2_s, n2_s, src1_s, dst1_s, n1_s, nch_s,
                    gt_ref, qcol_ref, qbcol_ref, h_ref, x1_ref, gate2_ref,
                    wsg_ref, wsu_ref, wsd_ref, ys_ref, o_ref, stage_ref, sem):
    b = pl.program_id(0)
    nb = pl.num_programs(0)
    slot = lax.rem(b, 2)
    n_exp, tb = gt_ref.shape
    rmax = stage_ref.shape[1] * MOE_CHUNK

    def copy2(sl, staged, sorted_):
        return pltpu.make_async_copy(ys_ref.at[pl.ds(sorted_, 2)],
                                     stage_ref.at[sl, pl.ds(staged, 2)], sem.at[sl, 0])

    def copy1(sl, staged, sorted_):
        return pltpu.make_async_copy(ys_ref.at[sorted_], stage_ref.at[sl, staged], sem.at[sl, 1])

    def fetch(bb, sl):
        _for_each_chunk(n2_s[bb], lambda i: copy2(sl, src2_s[bb, i], dst2_s[bb, i]).start())
        _for_each_chunk(n1_s[bb], lambda i: copy1(sl, src1_s[bb, i], dst1_s[bb, i]).start())

    @pl.when(b == 0)
    def _():
        stage_ref[...] = jnp.zeros_like(stage_ref)
        fetch(0, 0)

    _for_each_chunk(n2_s[b], lambda i: copy2(slot, 0, 0).wait())
    _for_each_chunk(n1_s[b], lambda i: copy1(slot, 0, 0).wait())

    @pl.when(b + 1 < nb)
    def _():
        fetch(b + 1, 1 - slot)

    gt = gt_ref[...]
    routed = jnp.where(gt > 0.0, 1.0, 0.0).astype(BF16)
    i0 = lax.broadcasted_iota(jnp.int32, (tb, tb), 0)
    i1 = lax.broadcasted_iota(jnp.int32, (tb, tb), 1)
    eye = jnp.where(i0 == i1, 1.0, 0.0).astype(BF16)
    routed_t = _dot_nt(eye, routed)
    gates_tok = _dot_nt(eye, _bf(gt))
    earlier = jnp.where(i1 < i0, 1.0, 0.0).astype(BF16)
    pos_t = _dot(earlier, _bf(routed_t))
    posm_t = _bf(jnp.where(routed_t > 0.0, pos_t, -1.0))

    qcol = qcol_ref[0]
    qbcol = qbcol_ref[0]
    h = h_ref[...]
    a = _silu(_dot(h, wsg_ref[...])) * _dot(h, wsu_ref[...])
    shared = _dot(_bf(a), wsd_ref[...])

    def finish(rows):
        chunk = lax.broadcasted_iota(jnp.int32, (n_exp, rows), 1) >> MOE_CHUNK_SHIFT
        own = jnp.where(chunk >= qbcol, jnp.where(chunk < qbcol + qcol, 1.0, 0.0), 0.0)
        own_b = _bf(own)
        rank = _dot(posm_t, own_b)
        wexp = _dot(_bf(gates_tok), own_b)
        start = jnp.sum(own * qbcol.astype(F32), axis=0, keepdims=True) * MOE_CHUNK
        rel = lax.broadcasted_iota(jnp.int32, (1, rows), 1).astype(F32) - start
        weights = _bf(jnp.where(rank == rel, wexp, 0.0))
        staged = stage_ref[slot, :rows // MOE_CHUNK].reshape(rows, o_ref.shape[1])
        o_ref[...] = x1_ref[...] + gate2_ref[0] * (_dot(weights, staged) + shared)

    short = rmax - MOE_ROWGROUP

    @pl.when(nch_s[b] * MOE_CHUNK <= short)
    def _():
        finish(short)

    @pl.when(nch_s[b] * MOE_CHUNK > short)
    def _():
        finish(rmax)


def _combine(gates_t, q, qbase, copies, nch, h2, x1, mod3, wsg, wsu, wsd, ys, seq):
    n_exp, t = gates_t.shape
    d = h2.shape[1]
    nb = t // MOE_TB
    per_b = seq // MOE_TB
    stage_chunks = _moe_stage_rows(n_exp) // MOE_CHUNK

    def full(a):
        return pl.BlockSpec(a.shape, lambda b, *_: (0,) * a.ndim)

    def rows(w):
        return pl.BlockSpec((MOE_TB, w), lambda b, *_: (b, 0))

    grid_spec = pltpu.PrefetchScalarGridSpec(
        num_scalar_prefetch=7,
        grid=(nb,),
        in_specs=[
            pl.BlockSpec((n_exp, MOE_TB), lambda b, *_: (0, b)),
            pl.BlockSpec((1, n_exp, 1), lambda b, *_: (b, 0, 0)),
            pl.BlockSpec((1, n_exp, 1), lambda b, *_: (b, 0, 0)),
            rows(d), rows(d),
            pl.BlockSpec((1, 1, d), lambda b, *_: (b // per_b, 0, 5)),
            full(wsg), full(wsu), full(wsd),
            pl.BlockSpec(memory_space=pl.ANY),
        ],
        out_specs=rows(d),
        scratch_shapes=[pltpu.VMEM((2, stage_chunks, MOE_CHUNK, d), BF16),
                        pltpu.SemaphoreType.DMA((2, 2))],
    )
    return pl.pallas_call(
        _combine_kernel,
        out_shape=jax.ShapeDtypeStruct((t, d), F32),
        grid_spec=grid_spec,
        compiler_params=pltpu.CompilerParams(dimension_semantics=("arbitrary",),
                                             vmem_limit_bytes=V7X_VMEM_LIMIT),
        name="moe_combine",
    )(*copies, nch, gates_t, q.reshape(nb, n_exp, 1), qbase.reshape(nb, n_exp, 1),
      h2, x1, mod3, wsg, wsu, wsd, ys)


def _moe_plan(q):
    nb, n_exp = q.shape
    qbase = jnp.cumsum(q, axis=1) - q
    nch = jnp.sum(q, axis=1)
    per_exp = jnp.sum(q, axis=0)
    off = jnp.cumsum(per_exp) - per_exp
    dstq = off[None, :] + jnp.cumsum(q, axis=0) - q
    def copy_list(count, length, stage_start, sorted_start, step):
        cum = jnp.cumsum(count, axis=1) - count
        i = jnp.arange(length, dtype=jnp.int32)
        ii = i[None, :, None]
        owned = (ii >= cum[:, None, :]) & (ii < (cum + count)[:, None, :])

        def place(start):
            return step * i[None, :] + jnp.sum(
                jnp.where(owned, (start - step * cum)[:, None, :], 0), axis=2)
        return place(stage_start), place(sorted_start), jnp.sum(count, axis=1)

    pairs = q >> 1
    max_chunks = _moe_block_chunks(n_exp)
    src2, dst2, n2 = copy_list(pairs, max_chunks // 2, qbase, dstq, 2)
    src1, dst1, n1 = copy_list(q & 1, n_exp, qbase + 2 * pairs, dstq + 2 * pairs, 1)
    copies = (src2, dst2, n2, src1, dst1, n1)
    ids = jnp.arange(n_exp, dtype=jnp.int32)
    later = (ids[None, :] > ids[:, None]) & (per_exp[None, :] > 0)
    nxt = jnp.min(jnp.where(later, ids[None, :], n_exp), axis=1)
    first = jnp.min(jnp.where(per_exp > 0, ids, n_exp)).reshape(1)
    total = jnp.sum(per_exp).reshape(1)
    return qbase, copies, nch, off, per_exp, nxt, first, total


def _moe_block_chunks(n_exp):
    return MOE_TB * TOP_K // MOE_CHUNK + n_exp


def _moe_stage_rows(n_exp):
    return -(-_moe_block_chunks(n_exp) * MOE_CHUNK // MOE_ROWGROUP) * MOE_ROWGROUP


def _moe(h2, gates_t, q3, x1, mod3, wsg, wsu, wsd, wg, wu, wd, seq):
    n_exp, t = gates_t.shape
    nb = t // MOE_TB
    q = q3.reshape(nb, n_exp)
    qbase, copies, nch, off, per_exp, nxt, first, total = _moe_plan(q)
    n_chunks = nb * _moe_block_chunks(n_exp) + MOE_TM // MOE_CHUNK
    xs = _dispatch(gates_t, q, qbase, copies, nch, total, h2, n_chunks)
    ys = _experts(off, per_exp, nxt, first, xs, wg, wu, wd)
    return _combine(gates_t, q, qbase, copies, nch, h2, x1, mod3, wsg, wsu, wsd, ys, seq)


def _pad_heads(w, heads, width):
    lead = w.shape[:-1]
    w = w.reshape(lead + (heads, width))
    w = jnp.pad(w, [(0, 0)] * len(lead) + [(0, 0), (0, LANES - width)])
    return w.reshape(lead + (heads * LANES,))


def kernel(x, c, positions, w_ada, b_ada, g_norm1, w_in, g_na_q, g_na_k, na_rpb, g_q_lat, w_uq,
           g_kv_lat, w_ukv, g_mla_q, g_mla_k, w_proj_na, w_proj_mla, w_out, g_norm2, w_router,
           e_bias, w_exp_gate, w_exp_up, w_exp_down, w_sh_gate, w_sh_up, w_sh_down):
    bsz, seq, d = x.shape
    t = bsz * seq
    depth = w_ada.shape[0]
    na_w = NA_HEADS * NA_HEAD_DIM
    q_rank = g_q_lat.shape[1]
    kv_rank = g_kv_lat.shape[1]
    n_rows = seq // GRID_W

    pos = positions.reshape(1, t)
    half = MLA_ROPE_DIM // 2
    freq = (ROPE_THETA ** (-jnp.arange(half, dtype=F32) / half)).reshape(half, 1)

    x2 = x.reshape(t, d)
    for l in range(depth):
        mod3 = _adaln(c, w_ada[l], b_ada[l]).reshape(bsz, 1, 6 * d)

        wi = w_in[l]
        o_lat = 3 * na_w
        o_rot = o_lat + q_rank + kv_rank
        o_gate = o_rot + MLA_ROPE_DIM
        wqkv = _bf(wi[:, :o_lat])
        w_rot = jnp.pad(wi[:, o_rot:o_gate], ((0, 0), (MLA_NOPE_DIM, LANES - MLA_QK_DIM)))
        wlat = _bf(jnp.concatenate([wi[:, o_lat:o_rot], w_rot], axis=1))
        wgate = _bf(wi[:, o_gate:])
        gq = (jnp.tile(g_na_q[l], NA_HEADS) * (NA_HEAD_DIM ** -0.5 * LOG2E)).reshape(1, na_w)
        gk = jnp.tile(g_na_k[l], NA_HEADS).reshape(1, na_w)
        wuq = _bf(_pad_heads(w_uq[l], MLA_HEADS, MLA_QK_DIM))
        wukv = w_ukv[l].reshape(kv_rank, MLA_HEADS, MLA_NOPE_DIM + MLA_V_DIM)
        wuk = _bf(_pad_heads(wukv[:, :, :MLA_NOPE_DIM].reshape(kv_rank, -1), MLA_HEADS, MLA_NOPE_DIM))
        wuv = _bf(wukv[:, :, MLA_NOPE_DIM:].reshape(kv_rank, MLA_HEADS * MLA_V_DIM))
        gmq = _pad_heads(jnp.tile(g_mla_q[l], MLA_HEADS) * (MLA_QK_DIM ** -0.5 * LOG2E),
                         MLA_HEADS, MLA_QK_DIM).reshape(1, -1)
        gmk = _pad_heads(jnp.tile(g_mla_k[l], MLA_HEADS), MLA_HEADS, MLA_QK_DIM).reshape(1, -1)

        qa, ka, va, qm, km, vm, sgn, sgm = _inproj(
            x2, mod3, g_norm1[l].reshape(1, d), wqkv, wlat, wgate, gq, gk,
            g_q_lat[l].reshape(1, q_rank), g_kv_lat[l].reshape(1, kv_rank), wuq, wuk, wuv,
            gmq, gmk, pos, freq, seq)

        bias = _na_bias(na_rpb[l], n_rows)
        y_na = _na_attention(qa, ka, va, bias, bsz, seq)
        y_mla = _mla_attention(qm, km, vm, bsz, seq)

        x1, h2, lt = _merge(x2, y_na, y_mla, sgn, sgm, _bf(w_proj_na[l]), _bf(w_proj_mla[l]),
                            _bf(w_out[l]), mod3, g_norm2[l].reshape(1, d), w_router[l].T, seq)
        gates_t, q3 = _route(lt, e_bias[l])
        x2 = _moe(h2, gates_t, q3, x1, mod3, _bf(w_sh_gate[l]), _bf(w_sh_up[l]),
                  _bf(w_sh_down[l]), w_exp_gate[l], w_exp_up[l], w_exp_down[l], seq)
    return x2.reshape(bsz, seq, d)
```

```python
import functools

import jax
import jax.numpy as jnp
import numpy as np
from jax import lax
from jax.experimental import pallas as pl
from jax.experimental.pallas import tpu as pltpu

GRID_W = 64
NA_HEADS = 8
NA_HEAD_DIM = 64
NA_WIN_ROWS = 8
NA_WIN_COLS = 16
MLA_HEADS = 8
MLA_NOPE_DIM = 64
MLA_ROPE_DIM = 32
MLA_V_DIM = 64
MLA_QK_DIM = MLA_NOPE_DIM + MLA_ROPE_DIM
ROPE_THETA = 10000.0
N_GROUPS = 8
TOPK_GROUPS = 4
TOP_K = 8
ROUTED_SCALE = 2.5
EPS = 1e-6
NEG_BIG = -1e30

LANES = 128
V7X_VMEM_LIMIT = 56 * 1024 * 1024

NA_QROWS = 4
NA_BAND = 12
NA_BLOCK_TYPES = 3
NA_BLOCKS_PER_STEP = 4
MERGE_TM = 1024
MERGE_SUB = 1024
INPROJ_TM = 1024
INPROJ_SUB = 256
MLA_PAIRS_PER_STEP = 4
LOG2E = 1.4426950408889634
MOE_TB = 256
ROUTE_TN = 1024
MOE_CHUNK_SHIFT = 4
MOE_CHUNK = 1 << MOE_CHUNK_SHIFT
MOE_TM = 1024
MOE_ROWGROUP = 512
MOE_COPY_UNROLL = 4
MOE_XBUFS = 3

F32 = jnp.float32
BF16 = jnp.bfloat16


def _bf(x):
    return x.astype(BF16)


def _dot(a, b):
    return jnp.dot(a, b, preferred_element_type=F32)


def _dot_nt(a, b):
    return lax.dot_general(a, b, (((1,), (1,)), ((), ())), preferred_element_type=F32)


def _split(x):
    hi = _bf(x)
    lo = _bf(x - hi.astype(F32))
    return hi, lo


def _dot3(a, b):
    ah, al = _split(a)
    bh, bl = _split(b)
    return _dot(ah, bh) + (_dot(ah, bl) + _dot(al, bh))


def _dot3_nt(a, b):
    m = a.shape[0]
    ah, al = _split(a)
    bh, bl = _split(b)
    both = _dot_nt(jnp.concatenate([ah, al], axis=0), bh)
    return both[:m] + (_dot_nt(ah, bl) + both[m:])


def _sigmoid(x):
    return 1.0 / (1.0 + jnp.exp(-x))


def _silu(x):
    return x * _sigmoid(x)


def _rms(x, n):
    ss = jnp.sum(x * x, axis=-1, keepdims=True)
    return x * lax.rsqrt(ss * (1.0 / n) + EPS)


def _adaln_kernel(c_ref, w_ref, b_ref, o_ref):
    c = c_ref[...]
    o_ref[...] = _dot3(_silu(c), w_ref[...]) + b_ref[...]


def _adaln(c, w, b):
    bsz, d = c.shape
    n = w.shape[1]
    tn = 1024
    return pl.pallas_call(
        _adaln_kernel,
        out_shape=jax.ShapeDtypeStruct((bsz, n), F32),
        grid=(n // tn,),
        in_specs=[
            pl.BlockSpec((bsz, d), lambda j: (0, 0)),
            pl.BlockSpec((d, tn), lambda j: (0, j)),
            pl.BlockSpec((1, tn), lambda j: (0, j)),
        ],
        out_specs=pl.BlockSpec((bsz, tn), lambda j: (0, j)),
        compiler_params=pltpu.CompilerParams(dimension_semantics=("arbitrary",)),
        name="adaln",
    )(c, w, b.reshape(1, n))


def _inproj_kernel(x_ref, shift_ref, scale_ref, g1_ref, wqkv_ref, wlat_ref, wgate_ref,
                   gq_ref, gk_ref, gql_ref, gkvl_ref, wuq_ref, wuk_ref, wuv_ref,
                   gmq_ref, gmk_ref, pos_ref, freq_ref,
                   qa_ref, ka_ref, va_ref, qm_ref, km_ref, vm_ref, sgn_ref, sgm_ref):
    for r0 in range(0, x_ref.shape[0], INPROJ_SUB):
        _inproj_rows(pl.ds(r0, INPROJ_SUB), x_ref, shift_ref, scale_ref, g1_ref, wqkv_ref,
                     wlat_ref, wgate_ref, gq_ref, gk_ref, gql_ref, gkvl_ref, wuq_ref, wuk_ref,
                     wuv_ref, gmq_ref, gmk_ref, pos_ref, freq_ref, qa_ref, ka_ref, va_ref,
                     qm_ref, km_ref, vm_ref, sgn_ref, sgm_ref)


def _inproj_rows(rs, x_ref, shift_ref, scale_ref, g1_ref, wqkv_ref, wlat_ref, wgate_ref,
                 gq_ref, gk_ref, gql_ref, gkvl_ref, wuq_ref, wuk_ref, wuv_ref,
                 gmq_ref, gmk_ref, pos_ref, freq_ref,
                 qa_ref, ka_ref, va_ref, qm_ref, km_ref, vm_ref, sgn_ref, sgm_ref):
    d = x_ref.shape[1]
    x = x_ref[rs, :]
    h = _rms(x, d) * g1_ref[...]
    h = h * (1.0 + scale_ref[0]) + shift_ref[0]
    hb = _bf(h)

    qkv = _dot(hb, wqkv_ref[...])
    lat = _dot(hb, wlat_ref[...])
    gts = _dot(hb, wgate_ref[...])
    sgn_ref[rs, :] = _bf(_sigmoid(gts[:, :d]))
    sgm_ref[rs, :] = _bf(_sigmoid(gts[:, d:]))

    na_w = NA_HEADS * NA_HEAD_DIM
    lane = lax.broadcasted_iota(jnp.int32, (1, LANES), 1)
    lo_half = lane < NA_HEAD_DIM
    for p in range(na_w // LANES):
        sl = slice(p * LANES, (p + 1) * LANES)
        for src_off, g_ref, dst_ref in ((0, gq_ref, qa_ref), (na_w, gk_ref, ka_ref)):
            t = qkv[:, src_off + p * LANES: src_off + (p + 1) * LANES]
            sq = t * t
            s_lo = jnp.sum(jnp.where(lo_half, sq, 0.0), axis=-1, keepdims=True)
            s_hi = jnp.sum(jnp.where(lo_half, 0.0, sq), axis=-1, keepdims=True)
            r = jnp.where(lo_half,
                          lax.rsqrt(s_lo * (1.0 / NA_HEAD_DIM) + EPS),
                          lax.rsqrt(s_hi * (1.0 / NA_HEAD_DIM) + EPS))
            dst_ref[rs, sl] = _bf(t * r * g_ref[:, sl])
    va_ref[rs, :] = _bf(qkv[:, 2 * na_w: 3 * na_w])

    q_rank = gql_ref.shape[1]
    kv_rank = gkvl_ref.shape[1]
    qln = _rms(lat[:, :q_rank], q_rank) * gql_ref[...]
    kvn = _bf(_rms(lat[:, q_rank:q_rank + kv_rank], kv_rank) * gkvl_ref[...])
    qpre = _dot(_bf(qln), wuq_ref[...])
    knope = _dot(kvn, wuk_ref[...])
    vm_ref[rs, :] = _bf(_dot(kvn, wuv_ref[...]))
    krot = lat[:, q_rank + kv_rank:]

    tm = INPROJ_SUB
    half = MLA_ROPE_DIM // 2
    ang_t = freq_ref[...] * pos_ref[:, rs].astype(F32)
    cos_t = jnp.cos(ang_t)
    sin_t = jnp.sin(ang_t)
    l_i = lax.broadcasted_iota(jnp.int32, (LANES, half), 0)
    j_i = lax.broadcasted_iota(jnp.int32, (LANES, half), 1)
    hit = jnp.where((l_i >= MLA_NOPE_DIM) & (l_i < MLA_QK_DIM)
                    & (((l_i - MLA_NOPE_DIM) & (half - 1)) == j_i), 1.0, 0.0)
    first_half = l_i < MLA_NOPE_DIM + half
    eye = jnp.where(lax.broadcasted_iota(jnp.int32, (tm, tm), 0)
                    == lax.broadcasted_iota(jnp.int32, (tm, tm), 1), 1.0, 0.0).astype(BF16)

    def table(sel, vals, fill_nope):
        hi, lo = _split(vals)
        w = _dot(_bf(sel), hi) + _dot(_bf(sel), lo)
        if fill_nope:
            w = jnp.where(lax.broadcasted_iota(jnp.int32, (LANES, tm), 0) < MLA_NOPE_DIM, 1.0, w)
        hi, lo = _split(w)
        return _dot_nt(eye, hi) + _dot_nt(eye, lo)

    c_tab = table(hit, cos_t, True)
    s_up = table(jnp.where(first_half, 0.0, hit), sin_t, False)
    s_dn = table(jnp.where(first_half, -hit, 0.0), sin_t, False)

    def rope(t):
        return t * c_tab + pltpu.roll(t, half, 1) * s_up + pltpu.roll(t, LANES - half, 1) * s_dn

    kr = rope(krot)
    for hd in range(MLA_HEADS):
        sl = slice(hd * LANES, (hd + 1) * LANES)
        qh = rope(qpre[:, sl])
        qm_ref[rs, sl] = _bf(_rms(qh, MLA_QK_DIM) * gmq_ref[:, sl])
        kh = knope[:, sl] + kr
        km_ref[rs, sl] = _bf(_rms(kh, MLA_QK_DIM) * gmk_ref[:, sl])


def _inproj(x2, mod3, g1, wqkv, wlat, wgate, gq, gk, gql, gkvl, wuq, wuk, wuv, gmq, gmk,
            pos, freq, seq):
    t, d = x2.shape
    tm = INPROJ_TM
    per_b = seq // tm
    na_w = NA_HEADS * NA_HEAD_DIM
    mla_w = MLA_HEADS * LANES
    v_w = MLA_HEADS * MLA_V_DIM

    def full(a):
        return pl.BlockSpec(a.shape, lambda i: (0,) * a.ndim, pipeline_mode=pl.Buffered(1))

    def rows(w):
        return pl.BlockSpec((tm, w), lambda i: (i, 0))

    out_shapes = (
        jax.ShapeDtypeStruct((t, na_w), BF16), jax.ShapeDtypeStruct((t, na_w), BF16),
        jax.ShapeDtypeStruct((t, na_w), BF16),
        jax.ShapeDtypeStruct((t, mla_w), BF16), jax.ShapeDtypeStruct((t, mla_w), BF16),
        jax.ShapeDtypeStruct((t, v_w), BF16),
        jax.ShapeDtypeStruct((t, d), BF16), jax.ShapeDtypeStruct((t, d), BF16),
    )
    return pl.pallas_call(
        _inproj_kernel,
        out_shape=out_shapes,
        grid=(t // tm,),
        in_specs=[
            rows(d),
            pl.BlockSpec((1, 1, d), lambda i: (i // per_b, 0, 0)),
            pl.BlockSpec((1, 1, d), lambda i: (i // per_b, 0, 1)),
            full(g1), full(wqkv), full(wlat), full(wgate), full(gq), full(gk), full(gql),
            full(gkvl), full(wuq), full(wuk), full(wuv), full(gmq), full(gmk),
            pl.BlockSpec((1, tm), lambda i: (0, i)),
            full(freq),
        ],
        out_specs=(rows(na_w), rows(na_w), rows(na_w), rows(mla_w), rows(mla_w), rows(v_w),
                   rows(d), rows(d)),
        compiler_params=pltpu.CompilerParams(dimension_semantics=("arbitrary",),
                                             vmem_limit_bytes=V7X_VMEM_LIMIT),
        name="inproj",
    )(x2, mod3, mod3, g1, wqkv, wlat, wgate, gq, gk, gql, gkvl, wuq, wuk, wuv, gmq, gmk,
      pos, freq)


def _na_block_geometry(block_type, n_rows):
    if block_type == 0:
        return 0, 0
    if block_type == 1:
        r0 = NA_QROWS
        return r0, r0 - NA_WIN_ROWS // 2
    return n_rows - NA_QROWS, n_rows - NA_BAND


def _na_bias_kernel(rpb_ref, o_ref, m_ref, *, n_rows):
    hd = pl.program_id(0)
    n_dr = 2 * NA_WIN_ROWS - 1
    n_dc = 2 * NA_WIN_COLS - 1
    qc = lax.broadcasted_iota(jnp.int32, (GRID_W, LANES), 0)
    kc = lax.broadcasted_iota(jnp.int32, (GRID_W, LANES), 1) & (GRID_W - 1)
    dc = jnp.clip(kc - qc, -(NA_WIN_COLS - 1), NA_WIN_COLS - 1) + (NA_WIN_COLS - 1)
    cstart = jnp.clip(qc - NA_WIN_COLS // 2, 0, GRID_W - NA_WIN_COLS)
    col_ok = (kc >= cstart) & (kc < cstart + NA_WIN_COLS)
    for i_dr in range(n_dr):
        acc = jnp.zeros((GRID_W, LANES), F32)
        for t in range(n_dc):
            acc = jnp.where(dc == t, rpb_ref[hd, i_dr * n_dc + t], acc)
        m_ref[i_dr] = jnp.where(col_ok, acc * LOG2E, NEG_BIG)
    neg = jnp.full((GRID_W, LANES), NEG_BIG, F32)
    lo_half = lax.broadcasted_iota(jnp.int32, (GRID_W, LANES), 1) < GRID_W
    kh = NA_WIN_ROWS
    for bt in range(NA_BLOCK_TYPES):
        r0, start = _na_block_geometry(bt, n_rows)
        for i in range(NA_QROWS):
            r = r0 + i
            rs = min(max(r - kh // 2, 0), n_rows - kh)
            for jp in range(NA_BAND // 2):
                halves = []
                for j in (2 * jp, 2 * jp + 1):
                    krow = start + j
                    if rs <= krow < rs + kh:
                        halves.append(m_ref[krow - r + (NA_WIN_ROWS - 1)])
                    else:
                        halves.append(neg)
                tile = jnp.where(lo_half, halves[0], halves[1])
                o_ref[bt, 0, i * GRID_W:(i + 1) * GRID_W, jp * LANES:(jp + 1) * LANES] = tile


def _na_bias(rpb, n_rows):
    heads = rpb.shape[0]
    nq = NA_QROWS * GRID_W
    nk = NA_BAND * GRID_W
    rpb2 = rpb.reshape(heads, -1)
    return pl.pallas_call(
        functools.partial(_na_bias_kernel, n_rows=n_rows),
        out_shape=jax.ShapeDtypeStruct((NA_BLOCK_TYPES, heads, nq, nk), F32),
        grid=(heads,),
        in_specs=[pl.BlockSpec(memory_space=pltpu.SMEM)],
        out_specs=pl.BlockSpec((NA_BLOCK_TYPES, 1, nq, nk), lambda hd: (0, hd, 0, 0)),
        scratch_shapes=[pltpu.VMEM((2 * NA_WIN_ROWS - 1, GRID_W, LANES), F32)],
        compiler_params=pltpu.CompilerParams(dimension_semantics=("arbitrary",)),
        name="na_bias",
    )(rpb2)


def _softmax_pv(s, v_pair, hh, half):
    lane = lax.broadcasted_iota(jnp.int32, (1, LANES), 1)
    mine = (lane < half) if hh == 0 else (lane >= half)
    den_lane = half if hh == 0 else 0
    m = jnp.max(s, axis=-1, keepdims=True)
    p = _bf(jnp.exp2(s - m))
    ones_row = jnp.where(lane == den_lane, 1.0, 0.0).astype(BF16)
    o = _dot(p, jnp.where(mine, v_pair, ones_row))
    den = jnp.sum(jnp.where(lane == den_lane, o, 0.0), axis=-1, keepdims=True)
    return jnp.where(mine, o / den, 0.0)


def _na_kernel(q_ref, k_ref, v_ref, bias_ref, o_ref, *, n_blocks, n_rows):
    nk = NA_BAND * GRID_W
    tq = NA_QROWS * GRID_W
    lo_half = lax.broadcasted_iota(jnp.int32, (1, LANES), 1) < NA_HEAD_DIM
    for u in range(NA_BLOCKS_PER_STEP):
        blk = pl.program_id(1) * NA_BLOCKS_PER_STEP + u
        btype = jnp.where(blk == 0, 0, jnp.where(blk == n_blocks - 1, 2, 1))
        start_row = jnp.where(blk == 0, 0,
                              jnp.where(blk == n_blocks - 1, n_rows - NA_BAND,
                                        blk * NA_QROWS - NA_WIN_ROWS // 2))
        off = pl.multiple_of(start_row * GRID_W, GRID_W)
        rs = pl.ds(u * tq, tq)
        for p in range(NA_HEADS * NA_HEAD_DIM // LANES):
            sl = slice(p * LANES, (p + 1) * LANES)
            qp = q_ref[rs, sl]
            kb = k_ref[pl.ds(off, nk), sl]
            vb = v_ref[pl.ds(off, nk), sl]
            zero = jnp.zeros_like(qp)
            q2 = jnp.concatenate([jnp.where(lo_half, qp, zero), jnp.where(lo_half, zero, qp)],
                                 axis=0)
            s = _dot_nt(q2, kb) + bias_ref[btype, 2 * p:2 * p + 2].reshape(2 * tq, nk)
            m = jnp.max(s, axis=-1, keepdims=True)
            e = jnp.exp2(s - m)
            den = jnp.sum(e, axis=-1, keepdims=True)
            o = _dot(_bf(e), vb) / den
            o_ref[rs, sl] = _bf(jnp.where(lo_half, o[:tq], o[tq:]))


def _na_attention(qa, ka, va, bias, bsz, seq):
    t, w = qa.shape
    n_rows = seq // GRID_W
    n_blocks = n_rows // NA_QROWS
    tq = NA_QROWS * GRID_W
    nk = NA_BAND * GRID_W
    heads = bias.shape[1]

    bps = NA_BLOCKS_PER_STEP
    steps = n_blocks // bps
    return pl.pallas_call(
        functools.partial(_na_kernel, n_blocks=n_blocks, n_rows=n_rows),
        out_shape=jax.ShapeDtypeStruct((t, w), BF16),
        grid=(bsz, steps),
        in_specs=[
            pl.BlockSpec((bps * tq, w), lambda b, j: (b * steps + j, 0)),
            pl.BlockSpec((seq, w), lambda b, j: (b, 0)),
            pl.BlockSpec((seq, w), lambda b, j: (b, 0)),
            pl.BlockSpec(bias.shape, lambda b, j: (0, 0, 0, 0), pipeline_mode=pl.Buffered(1)),
        ],
        out_specs=pl.BlockSpec((bps * tq, w), lambda b, j: (b * steps + j, 0)),
        compiler_params=pltpu.CompilerParams(dimension_semantics=("arbitrary", "arbitrary"),
                                             vmem_limit_bytes=V7X_VMEM_LIMIT),
        name="na_attn",
    )(qa, ka, va, bias)


def _mla_kernel(q_ref, k_ref, v_ref, o_ref):
    for pp in range(MLA_PAIRS_PER_STEP):
        v_pair = v_ref[:, pp * LANES:(pp + 1) * LANES]
        acc = jnp.zeros((q_ref.shape[0], LANES), F32)
        for hh in range(2):
            sl = slice((2 * pp + hh) * LANES, (2 * pp + hh + 1) * LANES)
            s = _dot_nt(q_ref[:, sl], k_ref[:, sl])
            acc = acc + _softmax_pv(s, v_pair, hh, MLA_V_DIM)
        o_ref[:, pp * LANES:(pp + 1) * LANES] = _bf(acc)


def _mla_attention(qm, km, vm, bsz, seq):
    t = qm.shape[0]
    tq = 512
    nq = seq // tq
    groups = MLA_HEADS // (2 * MLA_PAIRS_PER_STEP)
    qk_w = 2 * MLA_PAIRS_PER_STEP * LANES
    v_w = MLA_PAIRS_PER_STEP * LANES
    return pl.pallas_call(
        _mla_kernel,
        out_shape=jax.ShapeDtypeStruct((t, MLA_HEADS * MLA_V_DIM), BF16),
        grid=(bsz, groups, nq),
        in_specs=[
            pl.BlockSpec((tq, qk_w), lambda b, p, i: (b * nq + i, p)),
            pl.BlockSpec((seq, qk_w), lambda b, p, i: (b, p)),
            pl.BlockSpec((seq, v_w), lambda b, p, i: (b, p)),
        ],
        out_specs=pl.BlockSpec((tq, v_w), lambda b, p, i: (b * nq + i, p)),
        compiler_params=pltpu.CompilerParams(
            dimension_semantics=("arbitrary", "arbitrary", "arbitrary"),
            vmem_limit_bytes=V7X_VMEM_LIMIT),
        name="mla_attn",
    )(qm, km, vm)


def _merge_kernel(x_ref, yna_ref, ymla_ref, sgn_ref, sgm_ref, wpn_ref, wpm_ref, wout_ref,
                  gate1_ref, shift2_ref, scale2_ref, g2_ref, wr_ref,
                  x1_ref, h2_ref, lt_ref):
    d = x_ref.shape[1]
    for r0 in range(0, x_ref.shape[0], MERGE_SUB):
        rs = pl.ds(r0, MERGE_SUB)
        merged = (sgn_ref[rs, :].astype(F32) * _dot(yna_ref[rs, :], wpn_ref[...])
                  + sgm_ref[rs, :].astype(F32) * _dot(ymla_ref[rs, :], wpm_ref[...]))
        x1 = x_ref[rs, :] + gate1_ref[0] * _dot(_bf(merged), wout_ref[...])
        x1_ref[rs, :] = x1
        h2 = _rms(x1, d) * g2_ref[...]
        h2 = h2 * (1.0 + scale2_ref[0]) + shift2_ref[0]
        h2_ref[rs, :] = _bf(h2)
        lt_ref[:, rs] = _dot3_nt(wr_ref[...], h2)


def _merge(x2, yna, ymla, sgn, sgm, wpn, wpm, wout, mod3, g2, wr, seq):
    t, d = x2.shape
    tm = MERGE_TM
    per_b = seq // tm
    n_exp = wr.shape[0]

    def full(a):
        return pl.BlockSpec(a.shape, lambda i: (0,) * a.ndim)

    def rows(w):
        return pl.BlockSpec((tm, w), lambda i: (i, 0))

    def modblk(j):
        return pl.BlockSpec((1, 1, d), lambda i: (i // per_b, 0, j))

    return pl.pallas_call(
        _merge_kernel,
        out_shape=(jax.ShapeDtypeStruct((t, d), F32), jax.ShapeDtypeStruct((t, d), BF16),
                   jax.ShapeDtypeStruct((n_exp, t), F32)),
        grid=(t // tm,),
        in_specs=[rows(d), rows(yna.shape[1]), rows(ymla.shape[1]), rows(d), rows(d),
                  full(wpn), full(wpm), full(wout),
                  modblk(2), modblk(3), modblk(4), full(g2), full(wr)],
        out_specs=(rows(d), rows(d), pl.BlockSpec((n_exp, tm), lambda i: (0, i))),
        compiler_params=pltpu.CompilerParams(dimension_semantics=("arbitrary",),
                                             vmem_limit_bytes=V7X_VMEM_LIMIT),
        name="merge",
    )(x2, yna, ymla, sgn, sgm, wpn, wpm, wout, mod3, mod3, mod3, g2, wr)


def _route_kernel(lt_ref, eb_ref, o_ref, q_ref):
    n_exp, tn = lt_ref.shape
    per_g = n_exp // N_GROUPS
    neg_inf = -jnp.inf
    sc = _sigmoid(lt_ref[...])
    sel = sc + eb_ref[...]
    sc3 = sc.reshape(N_GROUPS, per_g, tn)
    g3 = sel.reshape(N_GROUPS, per_g, tn)
    io = lax.broadcasted_iota(jnp.int32, (N_GROUPS, per_g, tn), 1)
    gio = lax.broadcasted_iota(jnp.int32, (N_GROUPS, per_g, tn), 0)
    eio = gio * per_g + io

    m1 = jnp.max(g3, axis=1, keepdims=True)
    i1 = jnp.min(jnp.where(g3 == m1, io, per_g), axis=1, keepdims=True)
    m2 = jnp.max(jnp.where(io == i1, neg_inf, g3), axis=1, keepdims=True)
    gs = m1 + m2

    g1io = lax.broadcasted_iota(jnp.int32, (N_GROUPS, 1, tn), 0)
    gsel = jnp.zeros((N_GROUPS, 1, tn), F32)
    cur = gs
    for _ in range(TOPK_GROUPS):
        m = jnp.max(cur, axis=0, keepdims=True)
        i = jnp.min(jnp.where(cur == m, g1io, N_GROUPS), axis=0, keepdims=True)
        pick = g1io == i
        gsel = jnp.where(pick, 1.0, gsel)
        cur = jnp.where(pick, neg_inf, cur)

    cur = jnp.where(gsel > 0.0, g3, neg_inf)
    chosen = jnp.zeros((N_GROUPS, per_g, tn), F32)
    for _ in range(TOP_K):
        m = jnp.max(jnp.max(cur, axis=1, keepdims=True), axis=0, keepdims=True)
        cand = jnp.where(cur == m, eio, n_exp)
        i = jnp.min(jnp.min(cand, axis=1, keepdims=True), axis=0, keepdims=True)
        pick = eio == i
        chosen = jnp.where(pick, 1.0, chosen)
        cur = jnp.where(pick, neg_inf, cur)

    w = jnp.where(chosen > 0.0, sc3, 0.0)
    tot = jnp.sum(jnp.sum(w, axis=1, keepdims=True), axis=0, keepdims=True)
    gates = (w / tot * ROUTED_SCALE).reshape(n_exp, tn)
    o_ref[...] = gates
    routed = jnp.where(gates > 0.0, 1.0, 0.0).astype(BF16)
    ones = jnp.ones((8, MOE_TB), BF16)
    for j in range(tn // MOE_TB):
        n_row = _dot_nt(ones, routed[:, j * MOE_TB:(j + 1) * MOE_TB])[0:1]
        q_ref[j] = jnp.floor((n_row + (MOE_CHUNK - 1)) * (1.0 / MOE_CHUNK)).astype(jnp.int32)


def _route(lt, e_bias):
    n_exp, t = lt.shape
    tn = ROUTE_TN
    bps = tn // MOE_TB
    return pl.pallas_call(
        _route_kernel,
        out_shape=(jax.ShapeDtypeStruct((n_exp, t), F32),
                   jax.ShapeDtypeStruct((t // MOE_TB, 1, n_exp), jnp.int32)),
        grid=(t // tn,),
        in_specs=[pl.BlockSpec((n_exp, tn), lambda i: (0, i)),
                  pl.BlockSpec((n_exp, 1), lambda i: (0, 0))],
        out_specs=(pl.BlockSpec((n_exp, tn), lambda i: (0, i)),
                   pl.BlockSpec((bps, 1, n_exp), lambda i: (i, 0, 0))),
        compiler_params=pltpu.CompilerParams(dimension_semantics=("arbitrary",)),
        name="route",
    )(lt, e_bias.reshape(n_exp, 1))


def _for_each_chunk(n, fn):
    shift = MOE_COPY_UNROLL.bit_length() - 1

    def group(j, carry):
        for u in range(MOE_COPY_UNROLL):
            fn(j * MOE_COPY_UNROLL + u)
        return carry
    lax.fori_loop(0, n >> shift, group, 0)
    base = (n >> shift) << shift
    for u in range(MOE_COPY_UNROLL - 1):
        @pl.when(base + u < n)
        def _():
            fn(base + u)


def _dispatch_kernel(src2_s, dst2_s, n2_s, src1_s, dst1_s, n1_s, nch_s, total_s,
                     gt_ref, qrow_ref, qbrow_ref, h_ref, xs_ref, stage_ref, zero_ref, sem):
    b = pl.program_id(0)
    nb = pl.num_programs(0)
    slot = lax.rem(b, 2)
    n_exp, tb = gt_ref.shape
    rmax = stage_ref.shape[1] * MOE_CHUNK
    cpg = MOE_ROWGROUP // MOE_CHUNK

    routed = gt_ref[...] > 0.0
    before = (lax.broadcasted_iota(jnp.int32, (tb, tb), 0)
              < lax.broadcasted_iota(jnp.int32, (tb, tb), 1))
    pos = _dot(jnp.where(routed, 1.0, 0.0).astype(BF16), jnp.where(before, 1.0, 0.0).astype(BF16))
    posm = _bf(jnp.where(routed, pos, -1.0))
    qrow = qrow_ref[0]
    qbrow = qbrow_ref[0]
    qbrow_f = qbrow.astype(F32)
    h = h_ref[...]
    def sort_rows(g):
        r0 = g * MOE_ROWGROUP
        chunk = (lax.broadcasted_iota(jnp.int32, (MOE_ROWGROUP, n_exp), 0) + r0) >> MOE_CHUNK_SHIFT
        own = jnp.where(chunk >= qbrow, jnp.where(chunk < qbrow + qrow, 1.0, 0.0), 0.0)
        rank = _dot(_bf(own), posm)
        start = jnp.sum(own * qbrow_f, axis=-1, keepdims=True) * MOE_CHUNK
        rel = (lax.broadcasted_iota(jnp.int32, (MOE_ROWGROUP, 1), 0) + r0).astype(F32) - start
        onehot = jnp.where(rank == rel, 1.0, 0.0).astype(BF16)
        rows = _bf(_dot(onehot, h))
        stage_ref[slot, g * cpg:(g + 1) * cpg] = rows.reshape(cpg, MOE_CHUNK, rows.shape[1])

    n_groups = rmax // MOE_ROWGROUP
    n_sure = min(n_groups, (MOE_TB * TOP_K + MOE_ROWGROUP - 1) // MOE_ROWGROUP)
    for g in range(n_sure):
        sort_rows(g)
    for g in range(n_sure, n_groups):
        @pl.when(g * MOE_ROWGROUP < nch_s[b] * MOE_CHUNK)
        def _():
            sort_rows(g)

    def copy2(sl, src, dst):
        return pltpu.make_async_copy(stage_ref.at[sl, pl.ds(src, 2)], xs_ref.at[pl.ds(dst, 2)],
                                     sem.at[sl, 0])

    def copy1(sl, src_ref, dst):
        return pltpu.make_async_copy(src_ref, xs_ref.at[dst], sem.at[sl, 1])

    _for_each_chunk(n2_s[b], lambda i: copy2(slot, src2_s[b, i], dst2_s[b, i]).start())
    _for_each_chunk(n1_s[b], lambda i: copy1(slot, stage_ref.at[slot, src1_s[b, i]],
                                             dst1_s[b, i]).start())

    def wait_copies(sl, n2, n1):
        _for_each_chunk(n2, lambda i: copy2(sl, 0, 0).wait())
        _for_each_chunk(n1, lambda i: copy1(sl, zero_ref, 0).wait())

    @pl.when(b > 0)
    def _():
        wait_copies(1 - slot, n2_s[b - 1], n1_s[b - 1])

    @pl.when(b == nb - 1)
    def _():
        zero_ref[...] = jnp.zeros_like(zero_ref)
        n_tail = MOE_TM // MOE_CHUNK
        for c in range(n_tail):
            copy1(slot, zero_ref, total_s[0] + c).start()
        wait_copies(slot, n2_s[b], n1_s[b] + n_tail)


def _dispatch(gates_t, q, qbase, copies, nch, total, h2, n_chunks):
    n_exp, t = gates_t.shape
    d = h2.shape[1]
    nb = t // MOE_TB
    rmax = _moe_stage_rows(n_exp)
    grid_spec = pltpu.PrefetchScalarGridSpec(
        num_scalar_prefetch=8,
        grid=(nb,),
        in_specs=[
            pl.BlockSpec((n_exp, MOE_TB), lambda b, *_: (0, b)),
            pl.BlockSpec((1, 1, n_exp), lambda b, *_: (b, 0, 0)),
            pl.BlockSpec((1, 1, n_exp), lambda b, *_: (b, 0, 0)),
            pl.BlockSpec((MOE_TB, d), lambda b, *_: (b, 0)),
        ],
        out_specs=pl.BlockSpec(memory_space=pl.ANY),
        scratch_shapes=[pltpu.VMEM((2, rmax // MOE_CHUNK, MOE_CHUNK, d), BF16),
                        pltpu.VMEM((MOE_CHUNK, d), BF16), pltpu.SemaphoreType.DMA((2, 2))],
    )
    return pl.pallas_call(
        _dispatch_kernel,
        out_shape=jax.ShapeDtypeStruct((n_chunks, MOE_CHUNK, d), BF16),
        grid_spec=grid_spec,
        compiler_params=pltpu.CompilerParams(dimension_semantics=("arbitrary",),
                                             vmem_limit_bytes=V7X_VMEM_LIMIT),
        name="moe_dispatch",
    )(*copies, nch, total, gates_t, q.reshape(nb, 1, n_exp), qbase.reshape(nb, 1, n_exp), h2)


def _tile_pieces():
    cpt = MOE_TM // MOE_CHUNK
    return [1 << s for s in range(cpt.bit_length() - 1, -1, -1)]


def _expert_kernel(off_s, len_s, next_s, first_s, xs_ref, wg_ref, wu_ref, wd_ref, ys_ref,
                   xbuf, ybuf, wg_b, wu_b, wd_b, state, sem_in, sem_out):
    e = pl.program_id(0)
    n_exp = pl.num_programs(0)
    cpt = MOE_TM // MOE_CHUNK
    d = xbuf.shape[3]
    pieces = _tile_pieces()

    def tile_in(ee, tt, sl):
        return pltpu.make_async_copy(xs_ref.at[pl.ds(off_s[ee] + tt * cpt, cpt)], xbuf.at[sl],
                                     sem_in.at[sl])

    def for_each_piece(valid, fn):
        for k, piece in enumerate(pieces):
            @pl.when((valid & piece) != 0)
            def _():
                fn(k, piece, valid & ~(2 * piece - 1))

    def tile_out(sl, dst_chunk, k, piece, start):
        return pltpu.make_async_copy(ybuf.at[sl, pl.ds(start, piece)],
                                     ys_ref.at[pl.ds(dst_chunk + start, piece)], sem_out.at[sl, k])

    def drain(sl):
        for_each_piece(state[1 + sl], lambda k, piece, start: tile_out(sl, 0, k, piece, start).wait())
        state[1 + sl] = 0

    def following(ee, tt):
        safe = jnp.minimum(ee, n_exp - 1)
        more = tt + 1 < (len_s[safe] + cpt - 1) // cpt
        nxt_e = jnp.where(ee >= n_exp, n_exp, jnp.where(more, ee, next_s[safe]))
        return nxt_e, jnp.where(more, tt + 1, 0)

    def prefetch(ee, tt, xsl):
        @pl.when(ee < n_exp)
        def _():
            tile_in(ee, tt, xsl).start()

    @pl.when(e == 0)
    def _():
        state[0] = 0
        state[1] = 0
        state[2] = 0
        ahead = (first_s[0], 0)
        for k in range(MOE_XBUFS - 1):
            prefetch(ahead[0], ahead[1], k)
            ahead = following(*ahead)

    n_valid = len_s[e]
    n_tiles = (n_valid + cpt - 1) // cpt

    @pl.when(n_valid > 0)
    def _():
        wg_b[...] = _bf(wg_ref[0])
        wu_b[...] = _bf(wu_ref[0])
        wd_b[...] = _bf(wd_ref[0])

        def tile(t, carry):
            g = state[0]
            xsl = lax.rem(g, MOE_XBUFS)
            sl = g & 1
            tile_in(e, t, xsl).wait()
            ahead = (e, t)
            for _ in range(MOE_XBUFS - 1):
                ahead = following(*ahead)
            prefetch(ahead[0], ahead[1], lax.rem(g + MOE_XBUFS - 1, MOE_XBUFS))

            drain(sl)
            valid = jnp.minimum(n_valid - t * cpt, cpt)

            def ffn(n_c):
                x = xbuf[xsl, :n_c].reshape(n_c * MOE_CHUNK, d)
                a = _silu(_dot(x, wg_b[...])) * _dot(x, wu_b[...])
                ybuf[sl, :n_c] = _bf(_dot(_bf(a), wd_b[...])).reshape(n_c, MOE_CHUNK, d)

            @pl.when(valid > cpt // 2)
            def _():
                ffn(cpt)

            @pl.when(valid <= cpt // 2)
            def _():
                ffn(cpt // 2)

            dst = off_s[e] + t * cpt
            for_each_piece(valid, lambda k, piece, start: tile_out(sl, dst, k, piece, start).start())
            state[1 + sl] = valid
            state[0] = state[0] + 1
            return carry
        lax.fori_loop(0, n_tiles, tile, 0)

    @pl.when(e == n_exp - 1)
    def _():
        drain(0)
        drain(1)


def _experts(off, per_exp, nxt, first, xs, wg, wu, wd):
    n_chunks, _, d = xs.shape
    n_exp, _, ff = wg.shape
    cpt = MOE_TM // MOE_CHUNK

    def w_blk(e, *_):
        return (e, 0, 0)

    grid_spec = pltpu.PrefetchScalarGridSpec(
        num_scalar_prefetch=4,
        grid=(n_exp,),
        in_specs=[pl.BlockSpec(memory_space=pl.ANY),
                  pl.BlockSpec((1, d, ff), w_blk), pl.BlockSpec((1, d, ff), w_blk),
                  pl.BlockSpec((1, ff, d), w_blk)],
        out_specs=pl.BlockSpec(memory_space=pl.ANY),
        scratch_shapes=[pltpu.VMEM((MOE_XBUFS, cpt, MOE_CHUNK, d), BF16),
                        pltpu.VMEM((2, cpt, MOE_CHUNK, d), BF16),
                        pltpu.VMEM((d, ff), BF16), pltpu.VMEM((d, ff), BF16),
                        pltpu.VMEM((ff, d), BF16),
                        pltpu.SMEM((3,), jnp.int32),
                        pltpu.SemaphoreType.DMA((MOE_XBUFS,)),
                        pltpu.SemaphoreType.DMA((2, len(_tile_pieces())))],
    )
    return pl.pallas_call(
        _expert_kernel,
        out_shape=jax.ShapeDtypeStruct((n_chunks, MOE_CHUNK, d), BF16),
        grid_spec=grid_spec,
        compiler_params=pltpu.CompilerParams(dimension_semantics=("arbitrary",),
                                             vmem_limit_bytes=V7X_VMEM_LIMIT),
        name="moe_experts",
    )(off, per_exp, nxt, first, xs, wg, wu, wd)


def _combine_kernel(src2_s, dst2_s, n2_s, src1_s, dst1_s, n1_s, nch_s,
                    gt_ref, qcol_ref, qbcol_ref, h_ref, x1_ref, gate2_ref,
                    wsg_ref, wsu_ref, wsd_ref, ys_ref, o_ref, stage_ref, sem):
    b = pl.program_id(0)
    nb = pl.num_programs(0)
    slot = lax.rem(b, 2)
    n_exp, tb = gt_ref.shape
    rmax = stage_ref.shape[1] * MOE_CHUNK

    def copy2(sl, staged, sorted_):
        return pltpu.make_async_copy(ys_ref.at[pl.ds(sorted_, 2)],
                                     stage_ref.at[sl, pl.ds(staged, 2)], sem.at[sl, 0])

    def copy1(sl, staged, sorted_):
        return pltpu.make_async_copy(ys_ref.at[sorted_], stage_ref.at[sl, staged], sem.at[sl, 1])

    def fetch(bb, sl):
        _for_each_chunk(n2_s[bb], lambda i: copy2(sl, src2_s[bb, i], dst2_s[bb, i]).start())
        _for_each_chunk(n1_s[bb], lambda i: copy1(sl, src1_s[bb, i], dst1_s[bb, i]).start())

    @pl.when(b == 0)
    def _():
        stage_ref[...] = jnp.zeros_like(stage_ref)
        fetch(0, 0)

    _for_each_chunk(n2_s[b], lambda i: copy2(slot, 0, 0).wait())
    _for_each_chunk(n1_s[b], lambda i: copy1(slot, 0, 0).wait())

    @pl.when(b + 1 < nb)
    def _():
        fetch(b + 1, 1 - slot)

    gt = gt_ref[...]
    routed = jnp.where(gt > 0.0, 1.0, 0.0).astype(BF16)
    i0 = lax.broadcasted_iota(jnp.int32, (tb, tb), 0)
    i1 = lax.broadcasted_iota(jnp.int32, (tb, tb), 1)
    eye = jnp.where(i0 == i1, 1.0, 0.0).astype(BF16)
    routed_t = _dot_nt(eye, routed)
    gates_tok = _dot_nt(eye, _bf(gt))
    earlier = jnp.where(i1 < i0, 1.0, 0.0).astype(BF16)
    pos_t = _dot(earlier, _bf(routed_t))
    posm_t = _bf(jnp.where(routed_t > 0.0, pos_t, -1.0))

    qcol = qcol_ref[0]
    qbcol = qbcol_ref[0]
    h = h_ref[...]
    a = _silu(_dot(h, wsg_ref[...])) * _dot(h, wsu_ref[...])
    shared = _dot(_bf(a), wsd_ref[...])

    def finish(rows):
        chunk = lax.broadcasted_iota(jnp.int32, (n_exp, rows), 1) >> MOE_CHUNK_SHIFT
        own = jnp.where(chunk >= qbcol, jnp.where(chunk < qbcol + qcol, 1.0, 0.0), 0.0)
        own_b = _bf(own)
        rank = _dot(posm_t, own_b)
        wexp = _dot(_bf(gates_tok), own_b)
        start = jnp.sum(own * qbcol.astype(F32), axis=0, keepdims=True) * MOE_CHUNK
        rel = lax.broadcasted_iota(jnp.int32, (1, rows), 1).astype(F32) - start
        weights = _bf(jnp.where(rank == rel, wexp, 0.0))
        staged = stage_ref[slot, :rows // MOE_CHUNK].reshape(rows, o_ref.shape[1])
        o_ref[...] = x1_ref[...] + gate2_ref[0] * (_dot(weights, staged) + shared)

    short = rmax - MOE_ROWGROUP

    @pl.when(nch_s[b] * MOE_CHUNK <= short)
    def _():
        finish(short)

    @pl.when(nch_s[b] * MOE_CHUNK > short)
    def _():
        finish(rmax)


def _combine(gates_t, q, qbase, copies, nch, h2, x1, mod3, wsg, wsu, wsd, ys, seq):
    n_exp, t = gates_t.shape
    d = h2.shape[1]
    nb = t // MOE_TB
    per_b = seq // MOE_TB
    stage_chunks = _moe_stage_rows(n_exp) // MOE_CHUNK

    def full(a):
        return pl.BlockSpec(a.shape, lambda b, *_: (0,) * a.ndim)

    def rows(w):
        return pl.BlockSpec((MOE_TB, w), lambda b, *_: (b, 0))

    grid_spec = pltpu.PrefetchScalarGridSpec(
        num_scalar_prefetch=7,
        grid=(nb,),
        in_specs=[
            pl.BlockSpec((n_exp, MOE_TB), lambda b, *_: (0, b)),
            pl.BlockSpec((1, n_exp, 1), lambda b, *_: (b, 0, 0)),
            pl.BlockSpec((1, n_exp, 1), lambda b, *_: (b, 0, 0)),
            rows(d), rows(d),
            pl.BlockSpec((1, 1, d), lambda b, *_: (b // per_b, 0, 5)),
            full(wsg), full(wsu), full(wsd),
            pl.BlockSpec(memory_space=pl.ANY),
        ],
        out_specs=rows(d),
        scratch_shapes=[pltpu.VMEM((2, stage_chunks, MOE_CHUNK, d), BF16),
                        pltpu.SemaphoreType.DMA((2, 2))],
    )
    return pl.pallas_call(
        _combine_kernel,
        out_shape=jax.ShapeDtypeStruct((t, d), F32),
        grid_spec=grid_spec,
        compiler_params=pltpu.CompilerParams(dimension_semantics=("arbitrary",),
                                             vmem_limit_bytes=V7X_VMEM_LIMIT),
        name="moe_combine",
    )(*copies, nch, gates_t, q.reshape(nb, n_exp, 1), qbase.reshape(nb, n_exp, 1),
      h2, x1, mod3, wsg, wsu, wsd, ys)


def _moe_plan(q):
    nb, n_exp = q.shape
    qbase = jnp.cumsum(q, axis=1) - q
    nch = jnp.sum(q, axis=1)
    per_exp = jnp.sum(q, axis=0)
    off = jnp.cumsum(per_exp) - per_exp
    dstq = off[None, :] + jnp.cumsum(q, axis=0) - q
    def copy_list(count, length, stage_start, sorted_start, step):
        cum = jnp.cumsum(count, axis=1) - count
        i = jnp.arange(length, dtype=jnp.int32)
        ii = i[None, :, None]
        owned = (ii >= cum[:, None, :]) & (ii < (cum + count)[:, None, :])

        def place(start):
            return step * i[None, :] + jnp.sum(
                jnp.where(owned, (start - step * cum)[:, None, :], 0), axis=2)
        return place(stage_start), place(sorted_start), jnp.sum(count, axis=1)

    pairs = q >> 1
    max_chunks = _moe_block_chunks(n_exp)
    src2, dst2, n2 = copy_list(pairs, max_chunks // 2, qbase, dstq, 2)
    src1, dst1, n1 = copy_list(q & 1, n_exp, qbase + 2 * pairs, dstq + 2 * pairs, 1)
    copies = (src2, dst2, n2, src1, dst1, n1)
    ids = jnp.arange(n_exp, dtype=jnp.int32)
    later = (ids[None, :] > ids[:, None]) & (per_exp[None, :] > 0)
    nxt = jnp.min(jnp.where(later, ids[None, :], n_exp), axis=1)
    first = jnp.min(jnp.where(per_exp > 0, ids, n_exp)).reshape(1)
    total = jnp.sum(per_exp).reshape(1)
    return qbase, copies, nch, off, per_exp, nxt, first, total


def _moe_block_chunks(n_exp):
    return MOE_TB * TOP_K // MOE_CHUNK + n_exp


def _moe_stage_rows(n_exp):
    return -(-_moe_block_chunks(n_exp) * MOE_CHUNK // MOE_ROWGROUP) * MOE_ROWGROUP


def _moe(h2, gates_t, q3, x1, mod3, wsg, wsu, wsd, wg, wu, wd, seq):
    n_exp, t = gates_t.shape
    nb = t // MOE_TB
    q = q3.reshape(nb, n_exp)
    qbase, copies, nch, off, per_exp, nxt, first, total = _moe_plan(q)
    n_chunks = nb * _moe_block_chunks(n_exp) + MOE_TM // MOE_CHUNK
    xs = _dispatch(gates_t, q, qbase, copies, nch, total, h2, n_chunks)
    ys = _experts(off, per_exp, nxt, first, xs, wg, wu, wd)
    return _combine(gates_t, q, qbase, copies, nch, h2, x1, mod3, wsg, wsu, wsd, ys, seq)


def _pad_heads(w, heads, width):
    lead = w.shape[:-1]
    w = w.reshape(lead + (heads, width))
    w = jnp.pad(w, [(0, 0)] * len(lead) + [(0, 0), (0, LANES - width)])
    return w.reshape(lead + (heads * LANES,))


def kernel(x, c, positions, w_ada, b_ada, g_norm1, w_in, g_na_q, g_na_k, na_rpb, g_q_lat, w_uq,
           g_kv_lat, w_ukv, g_mla_q, g_mla_k, w_proj_na, w_proj_mla, w_out, g_norm2, w_router,
           e_bias, w_exp_gate, w_exp_up, w_exp_down, w_sh_gate, w_sh_up, w_sh_down):
    bsz, seq, d = x.shape
    t = bsz * seq
    depth = w_ada.shape[0]
    na_w = NA_HEADS * NA_HEAD_DIM
    q_rank = g_q_lat.shape[1]
    kv_rank = g_kv_lat.shape[1]
    n_rows = seq // GRID_W

    pos = positions.reshape(1, t)
    half = MLA_ROPE_DIM // 2
    freq = (ROPE_THETA ** (-jnp.arange(half, dtype=F32) / half)).reshape(half, 1)

    x2 = x.reshape(t, d)
    for l in range(depth):
        mod3 = _adaln(c, w_ada[l], b_ada[l]).reshape(bsz, 1, 6 * d)

        wi = w_in[l]
        o_lat = 3 * na_w
        o_rot = o_lat + q_rank + kv_rank
        o_gate = o_rot + MLA_ROPE_DIM
        wqkv = _bf(wi[:, :o_lat])
        w_rot = jnp.pad(wi[:, o_rot:o_gate], ((0, 0), (MLA_NOPE_DIM, LANES - MLA_QK_DIM)))
        wlat = _bf(jnp.concatenate([wi[:, o_lat:o_rot], w_rot], axis=1))
        wgate = _bf(wi[:, o_gate:])
        gq = (jnp.tile(g_na_q[l], NA_HEADS) * (NA_HEAD_DIM ** -0.5 * LOG2E)).reshape(1, na_w)
        gk = jnp.tile(g_na_k[l], NA_HEADS).reshape(1, na_w)
        wuq = _bf(_pad_heads(w_uq[l], MLA_HEADS, MLA_QK_DIM))
        wukv = w_ukv[l].reshape(kv_rank, MLA_HEADS, MLA_NOPE_DIM + MLA_V_DIM)
        wuk = _bf(_pad_heads(wukv[:, :, :MLA_NOPE_DIM].reshape(kv_rank, -1), MLA_HEADS, MLA_NOPE_DIM))
        wuv = _bf(wukv[:, :, MLA_NOPE_DIM:].reshape(kv_rank, MLA_HEADS * MLA_V_DIM))
        gmq = _pad_heads(jnp.tile(g_mla_q[l], MLA_HEADS) * (MLA_QK_DIM ** -0.5 * LOG2E),
                         MLA_HEADS, MLA_QK_DIM).reshape(1, -1)
        gmk = _pad_heads(jnp.tile(g_mla_k[l], MLA_HEADS), MLA_HEADS, MLA_QK_DIM).reshape(1, -1)

        qa, ka, va, qm, km, vm, sgn, sgm = _inproj(
            x2, mod3, g_norm1[l].reshape(1, d), wqkv, wlat, wgate, gq, gk,
            g_q_lat[l].reshape(1, q_rank), g_kv_lat[l].reshape(1, kv_rank), wuq, wuk, wuv,
            gmq, gmk, pos, freq, seq)

        bias = _na_bias(na_rpb[l], n_rows)
        y_na = _na_attention(qa, ka, va, bias, bsz, seq)
        y_mla = _mla_attention(qm, km, vm, bsz, seq)

        x1, h2, lt = _merge(x2, y_na, y_mla, sgn, sgm, _bf(w_proj_na[l]), _bf(w_proj_mla[l]),
                            _bf(w_out[l]), mod3, g_norm2[l].reshape(1, d), w_router[l].T, seq)
        gates_t, q3 = _route(lt, e_bias[l])
        x2 = _moe(h2, gates_t, q3, x1, mod3, _bf(w_sh_gate[l]), _bf(w_sh_up[l]),
                  _bf(w_sh_down[l]), w_exp_gate[l], w_exp_up[l], w_exp_down[l], seq)
    return x2.reshape(bsz, seq, d)
```

```python
import functools

import jax
import jax.numpy as jnp
import numpy as np
from jax import lax
from jax.experimental import pallas as pl
from jax.experimental.pallas import tpu as pltpu

GRID_W = 64
NA_HEADS = 8
NA_HEAD_DIM = 64
NA_WIN_ROWS = 8
NA_WIN_COLS = 16
MLA_HEADS = 8
MLA_NOPE_DIM = 64
MLA_ROPE_DIM = 32
MLA_V_DIM = 64
MLA_QK_DIM = MLA_NOPE_DIM + MLA_ROPE_DIM
ROPE_THETA = 10000.0
N_GROUPS = 8
TOPK_GROUPS = 4
TOP_K = 8
ROUTED_SCALE = 2.5
EPS = 1e-6
NEG_BIG = -1e30

LANES = 128
V7X_VMEM_LIMIT = 56 * 1024 * 1024

NA_QROWS = 4
NA_BAND = 12
NA_BLOCK_TYPES = 3
NA_BLOCKS_PER_STEP = 4
MERGE_TM = 1024
MERGE_SUB = 1024
INPROJ_TM = 1024
INPROJ_SUB = 256
MLA_TQ = 1024
MLA_SUB = 512
MLA_PAIRS_PER_STEP = 4
LOG2E = 1.4426950408889634
MOE_TB = 256
ROUTE_TN = 1024
MOE_CHUNK_SHIFT = 4
MOE_CHUNK = 1 << MOE_CHUNK_SHIFT
MOE_TM = 1024
MOE_TILE_ARMS = 2
MOE_ROWGROUP = 512
MOE_COPY_UNROLL = 4
MOE_XBUFS = 3

F32 = jnp.float32
BF16 = jnp.bfloat16


def _bf(x):
    return x.astype(BF16)


def _dot(a, b):
    return jnp.dot(a, b, preferred_element_type=F32)


def _dot_nt(a, b):
    return lax.dot_general(a, b, (((1,), (1,)), ((), ())), preferred_element_type=F32)


def _split(x):
    hi = _bf(x)
    lo = _bf(x - hi.astype(F32))
    return hi, lo


def _dot3(a, b):
    ah, al = _split(a)
    bh, bl = _split(b)
    return _dot(ah, bh) + (_dot(ah, bl) + _dot(al, bh))


def _dot3_nt(a, b):
    m = a.shape[0]
    ah, al = _split(a)
    bh, bl = _split(b)
    both = _dot_nt(jnp.concatenate([ah, al], axis=0), bh)
    return both[:m] + (_dot_nt(ah, bl) + both[m:])


def _sigmoid(x):
    return 1.0 / (1.0 + jnp.exp(-x))


def _silu(x):
    return x * _sigmoid(x)


def _rms(x, n):
    ss = jnp.sum(x * x, axis=-1, keepdims=True)
    return x * lax.rsqrt(ss * (1.0 / n) + EPS)


def _adaln_kernel(c_ref, w_ref, b_ref, o_ref):
    c = c_ref[...]
    o_ref[...] = _dot3(_silu(c), w_ref[...]) + b_ref[...]


def _adaln(c, w, b):
    bsz, d = c.shape
    n = w.shape[1]
    tn = 1024
    return pl.pallas_call(
        _adaln_kernel,
        out_shape=jax.ShapeDtypeStruct((bsz, n), F32),
        grid=(n // tn,),
        in_specs=[
            pl.BlockSpec((bsz, d), lambda j: (0, 0)),
            pl.BlockSpec((d, tn), lambda j: (0, j)),
            pl.BlockSpec((1, tn), lambda j: (0, j)),
        ],
        out_specs=pl.BlockSpec((bsz, tn), lambda j: (0, j)),
        compiler_params=pltpu.CompilerParams(dimension_semantics=("arbitrary",)),
        name="adaln",
    )(c, w, b.reshape(1, n))


def _inproj_kernel(x_ref, shift_ref, scale_ref, g1_ref, wqkv_ref, wlat_ref, wgate_ref,
                   gq_ref, gk_ref, gql_ref, gkvl_ref, wuq_ref, wuk_ref, wuv_ref,
                   gmq_ref, gmk_ref, pos_ref, freq_ref,
                   qa_ref, ka_ref, va_ref, qm_ref, km_ref, vm_ref, sgn_ref, sgm_ref):
    for r0 in range(0, x_ref.shape[0], INPROJ_SUB):
        _inproj_rows(pl.ds(r0, INPROJ_SUB), x_ref, shift_ref, scale_ref, g1_ref, wqkv_ref,
                     wlat_ref, wgate_ref, gq_ref, gk_ref, gql_ref, gkvl_ref, wuq_ref, wuk_ref,
                     wuv_ref, gmq_ref, gmk_ref, pos_ref, freq_ref, qa_ref, ka_ref, va_ref,
                     qm_ref, km_ref, vm_ref, sgn_ref, sgm_ref)


def _inproj_rows(rs, x_ref, shift_ref, scale_ref, g1_ref, wqkv_ref, wlat_ref, wgate_ref,
                 gq_ref, gk_ref, gql_ref, gkvl_ref, wuq_ref, wuk_ref, wuv_ref,
                 gmq_ref, gmk_ref, pos_ref, freq_ref,
                 qa_ref, ka_ref, va_ref, qm_ref, km_ref, vm_ref, sgn_ref, sgm_ref):
    d = x_ref.shape[1]
    x = x_ref[rs, :]
    h = _rms(x, d) * g1_ref[...]
    h = h * (1.0 + scale_ref[0]) + shift_ref[0]
    hb = _bf(h)

    qkv = _dot(hb, wqkv_ref[...])
    lat = _dot(hb, wlat_ref[...])
    gts = _dot(hb, wgate_ref[...])
    sgn_ref[rs, :] = _bf(_sigmoid(gts[:, :d]))
    sgm_ref[rs, :] = _bf(_sigmoid(gts[:, d:]))

    na_w = NA_HEADS * NA_HEAD_DIM
    lane = lax.broadcasted_iota(jnp.int32, (1, LANES), 1)
    lo_half = lane < NA_HEAD_DIM
    for p in range(na_w // LANES):
        sl = slice(p * LANES, (p + 1) * LANES)
        for src_off, g_ref, dst_ref in ((0, gq_ref, qa_ref), (na_w, gk_ref, ka_ref)):
            t = qkv[:, src_off + p * LANES: src_off + (p + 1) * LANES]
            sq = t * t
            s_lo = jnp.sum(jnp.where(lo_half, sq, 0.0), axis=-1, keepdims=True)
            s_hi = jnp.sum(jnp.where(lo_half, 0.0, sq), axis=-1, keepdims=True)
            r = jnp.where(lo_half,
                          lax.rsqrt(s_lo * (1.0 / NA_HEAD_DIM) + EPS),
                          lax.rsqrt(s_hi * (1.0 / NA_HEAD_DIM) + EPS))
            dst_ref[rs, sl] = _bf(t * r * g_ref[:, sl])
    va_ref[rs, :] = _bf(qkv[:, 2 * na_w: 3 * na_w])

    q_rank = gql_ref.shape[1]
    kv_rank = gkvl_ref.shape[1]
    qln = _rms(lat[:, :q_rank], q_rank) * gql_ref[...]
    kvn = _bf(_rms(lat[:, q_rank:q_rank + kv_rank], kv_rank) * gkvl_ref[...])
    qpre = _dot(_bf(qln), wuq_ref[...])
    knope = _dot(kvn, wuk_ref[...])
    vm_ref[rs, :] = _bf(_dot(kvn, wuv_ref[...]))
    krot = lat[:, q_rank + kv_rank:]

    tm = INPROJ_SUB
    half = MLA_ROPE_DIM // 2
    ang_t = freq_ref[...] * pos_ref[:, rs].astype(F32)
    cos_t = jnp.cos(ang_t)
    sin_t = jnp.sin(ang_t)
    l_i = lax.broadcasted_iota(jnp.int32, (LANES, half), 0)
    j_i = lax.broadcasted_iota(jnp.int32, (LANES, half), 1)
    hit = jnp.where((l_i >= MLA_NOPE_DIM) & (l_i < MLA_QK_DIM)
                    & (((l_i - MLA_NOPE_DIM) & (half - 1)) == j_i), 1.0, 0.0)
    first_half = l_i < MLA_NOPE_DIM + half
    eye = jnp.where(lax.broadcasted_iota(jnp.int32, (tm, tm), 0)
                    == lax.broadcasted_iota(jnp.int32, (tm, tm), 1), 1.0, 0.0).astype(BF16)

    def table(sel, vals, fill_nope):
        hi, lo = _split(vals)
        w = _dot(_bf(sel), hi) + _dot(_bf(sel), lo)
        if fill_nope:
            w = jnp.where(lax.broadcasted_iota(jnp.int32, (LANES, tm), 0) < MLA_NOPE_DIM, 1.0, w)
        hi, lo = _split(w)
        return _dot_nt(eye, hi) + _dot_nt(eye, lo)

    c_tab = table(hit, cos_t, True)
    s_up = table(jnp.where(first_half, 0.0, hit), sin_t, False)
    s_dn = table(jnp.where(first_half, -hit, 0.0), sin_t, False)

    def rope(t):
        return t * c_tab + pltpu.roll(t, half, 1) * s_up + pltpu.roll(t, LANES - half, 1) * s_dn

    kr = rope(krot)
    for hd in range(MLA_HEADS):
        sl = slice(hd * LANES, (hd + 1) * LANES)
        qh = rope(qpre[:, sl])
        qm_ref[rs, sl] = _bf(_rms(qh, MLA_QK_DIM) * gmq_ref[:, sl])
        kh = knope[:, sl] + kr
        km_ref[rs, sl] = _bf(_rms(kh, MLA_QK_DIM) * gmk_ref[:, sl])


def _inproj(x2, mod3, g1, wqkv, wlat, wgate, gq, gk, gql, gkvl, wuq, wuk, wuv, gmq, gmk,
            pos, freq, seq):
    t, d = x2.shape
    tm = INPROJ_TM
    per_b = seq // tm
    na_w = NA_HEADS * NA_HEAD_DIM
    mla_w = MLA_HEADS * LANES
    v_w = MLA_HEADS * MLA_V_DIM

    def full(a):
        return pl.BlockSpec(a.shape, lambda i: (0,) * a.ndim, pipeline_mode=pl.Buffered(1))

    def rows(w):
        return pl.BlockSpec((tm, w), lambda i: (i, 0))

    out_shapes = (
        jax.ShapeDtypeStruct((t, na_w), BF16), jax.ShapeDtypeStruct((t, na_w), BF16),
        jax.ShapeDtypeStruct((t, na_w), BF16),
        jax.ShapeDtypeStruct((t, mla_w), BF16), jax.ShapeDtypeStruct((t, mla_w), BF16),
        jax.ShapeDtypeStruct((t, v_w), BF16),
        jax.ShapeDtypeStruct((t, d), BF16), jax.ShapeDtypeStruct((t, d), BF16),
    )
    return pl.pallas_call(
        _inproj_kernel,
        out_shape=out_shapes,
        grid=(t // tm,),
        in_specs=[
            rows(d),
            pl.BlockSpec((1, 1, d), lambda i: (i // per_b, 0, 0)),
            pl.BlockSpec((1, 1, d), lambda i: (i // per_b, 0, 1)),
            full(g1), full(wqkv), full(wlat), full(wgate), full(gq), full(gk), full(gql),
            full(gkvl), full(wuq), full(wuk), full(wuv), full(gmq), full(gmk),
            pl.BlockSpec((1, tm), lambda i: (0, i)),
            full(freq),
        ],
        out_specs=(rows(na_w), rows(na_w), rows(na_w), rows(mla_w), rows(mla_w), rows(v_w),
                   rows(d), rows(d)),
        compiler_params=pltpu.CompilerParams(dimension_semantics=("arbitrary",),
                                             vmem_limit_bytes=V7X_VMEM_LIMIT),
        name="inproj",
    )(x2, mod3, mod3, g1, wqkv, wlat, wgate, gq, gk, gql, gkvl, wuq, wuk, wuv, gmq, gmk,
      pos, freq)


def _na_block_geometry(block_type, n_rows):
    if block_type == 0:
        return 0, 0
    if block_type == 1:
        r0 = NA_QROWS
        return r0, r0 - NA_WIN_ROWS // 2
    return n_rows - NA_QROWS, n_rows - NA_BAND


def _na_bias_kernel(rpb_ref, o_ref, m_ref, *, n_rows):
    hd = pl.program_id(0)
    n_dr = 2 * NA_WIN_ROWS - 1
    n_dc = 2 * NA_WIN_COLS - 1
    qc = lax.broadcasted_iota(jnp.int32, (GRID_W, LANES), 0)
    kc = lax.broadcasted_iota(jnp.int32, (GRID_W, LANES), 1) & (GRID_W - 1)
    dc = jnp.clip(kc - qc, -(NA_WIN_COLS - 1), NA_WIN_COLS - 1) + (NA_WIN_COLS - 1)
    cstart = jnp.clip(qc - NA_WIN_COLS // 2, 0, GRID_W - NA_WIN_COLS)
    col_ok = (kc >= cstart) & (kc < cstart + NA_WIN_COLS)
    for i_dr in range(n_dr):
        acc = jnp.zeros((GRID_W, LANES), F32)
        for t in range(n_dc):
            acc = jnp.where(dc == t, rpb_ref[hd, i_dr * n_dc + t], acc)
        m_ref[i_dr] = jnp.where(col_ok, acc * LOG2E, NEG_BIG)
    neg = jnp.full((GRID_W, LANES), NEG_BIG, F32)
    lo_half = lax.broadcasted_iota(jnp.int32, (GRID_W, LANES), 1) < GRID_W
    kh = NA_WIN_ROWS
    for bt in range(NA_BLOCK_TYPES):
        r0, start = _na_block_geometry(bt, n_rows)
        for i in range(NA_QROWS):
            r = r0 + i
            rs = min(max(r - kh // 2, 0), n_rows - kh)
            for jp in range(NA_BAND // 2):
                halves = []
                for j in (2 * jp, 2 * jp + 1):
                    krow = start + j
                    if rs <= krow < rs + kh:
                        halves.append(m_ref[krow - r + (NA_WIN_ROWS - 1)])
                    else:
                        halves.append(neg)
                tile = jnp.where(lo_half, halves[0], halves[1])
                o_ref[bt, 0, i * GRID_W:(i + 1) * GRID_W, jp * LANES:(jp + 1) * LANES] = tile


def _na_bias(rpb, n_rows):
    heads = rpb.shape[0]
    nq = NA_QROWS * GRID_W
    nk = NA_BAND * GRID_W
    rpb2 = rpb.reshape(heads, -1)
    return pl.pallas_call(
        functools.partial(_na_bias_kernel, n_rows=n_rows),
        out_shape=jax.ShapeDtypeStruct((NA_BLOCK_TYPES, heads, nq, nk), F32),
        grid=(heads,),
        in_specs=[pl.BlockSpec(memory_space=pltpu.SMEM)],
        out_specs=pl.BlockSpec((NA_BLOCK_TYPES, 1, nq, nk), lambda hd: (0, hd, 0, 0)),
        scratch_shapes=[pltpu.VMEM((2 * NA_WIN_ROWS - 1, GRID_W, LANES), F32)],
        compiler_params=pltpu.CompilerParams(dimension_semantics=("arbitrary",)),
        name="na_bias",
    )(rpb2)


def _softmax_pv(s, v_pair, hh, half):
    lane = lax.broadcasted_iota(jnp.int32, (1, LANES), 1)
    mine = (lane < half) if hh == 0 else (lane >= half)
    den_lane = half if hh == 0 else 0
    m = jnp.max(s, axis=-1, keepdims=True)
    p = _bf(jnp.exp2(s - m))
    ones_row = jnp.where(lane == den_lane, 1.0, 0.0).astype(BF16)
    o = _dot(p, jnp.where(mine, v_pair, ones_row))
    den = jnp.sum(jnp.where(lane == den_lane, o, 0.0), axis=-1, keepdims=True)
    return jnp.where(mine, o / den, 0.0)


def _na_kernel(q_ref, k_ref, v_ref, bias_ref, o_ref, *, n_blocks, n_rows):
    nk = NA_BAND * GRID_W
    tq = NA_QROWS * GRID_W
    lo_half = lax.broadcasted_iota(jnp.int32, (1, LANES), 1) < NA_HEAD_DIM
    for u in range(NA_BLOCKS_PER_STEP):
        blk = pl.program_id(1) * NA_BLOCKS_PER_STEP + u
        btype = jnp.where(blk == 0, 0, jnp.where(blk == n_blocks - 1, 2, 1))
        start_row = jnp.where(blk == 0, 0,
                              jnp.where(blk == n_blocks - 1, n_rows - NA_BAND,
                                        blk * NA_QROWS - NA_WIN_ROWS // 2))
        off = pl.multiple_of(start_row * GRID_W, GRID_W)
        rs = pl.ds(u * tq, tq)
        for p in range(NA_HEADS * NA_HEAD_DIM // LANES):
            sl = slice(p * LANES, (p + 1) * LANES)
            qp = q_ref[rs, sl]
            kb = k_ref[pl.ds(off, nk), sl]
            vb = v_ref[pl.ds(off, nk), sl]
            zero = jnp.zeros_like(qp)
            q2 = jnp.concatenate([jnp.where(lo_half, qp, zero), jnp.where(lo_half, zero, qp)],
                                 axis=0)
            s = _dot_nt(q2, kb) + bias_ref[btype, 2 * p:2 * p + 2].reshape(2 * tq, nk)
            m = jnp.max(s, axis=-1, keepdims=True)
            e = jnp.exp2(s - m)
            den = jnp.sum(e, axis=-1, keepdims=True)
            o = _dot(_bf(e), vb) / den
            o_ref[rs, sl] = _bf(jnp.where(lo_half, o[:tq], o[tq:]))


def _na_attention(qa, ka, va, bias, bsz, seq):
    t, w = qa.shape
    n_rows = seq // GRID_W
    n_blocks = n_rows // NA_QROWS
    tq = NA_QROWS * GRID_W
    nk = NA_BAND * GRID_W
    heads = bias.shape[1]

    bps = NA_BLOCKS_PER_STEP
    steps = n_blocks // bps
    return pl.pallas_call(
        functools.partial(_na_kernel, n_blocks=n_blocks, n_rows=n_rows),
        out_shape=jax.ShapeDtypeStruct((t, w), BF16),
        grid=(bsz, steps),
        in_specs=[
            pl.BlockSpec((bps * tq, w), lambda b, j: (b * steps + j, 0)),
            pl.BlockSpec((seq, w), lambda b, j: (b, 0)),
            pl.BlockSpec((seq, w), lambda b, j: (b, 0)),
            pl.BlockSpec(bias.shape, lambda b, j: (0, 0, 0, 0), pipeline_mode=pl.Buffered(1)),
        ],
        out_specs=pl.BlockSpec((bps * tq, w), lambda b, j: (b * steps + j, 0)),
        compiler_params=pltpu.CompilerParams(dimension_semantics=("arbitrary", "arbitrary"),
                                             vmem_limit_bytes=V7X_VMEM_LIMIT),
        name="na_attn",
    )(qa, ka, va, bias)


def _mla_kernel(q_ref, k_ref, v_ref, o_ref):
    for r0 in range(0, q_ref.shape[0], MLA_SUB):
        rs = pl.ds(r0, MLA_SUB)
        for pp in range(MLA_PAIRS_PER_STEP):
            v_pair = v_ref[:, pp * LANES:(pp + 1) * LANES]
            acc = jnp.zeros((MLA_SUB, LANES), F32)
            for hh in range(2):
                sl = slice((2 * pp + hh) * LANES, (2 * pp + hh + 1) * LANES)
                s = _dot_nt(q_ref[rs, sl], k_ref[:, sl])
                acc = acc + _softmax_pv(s, v_pair, hh, MLA_V_DIM)
            o_ref[rs, pp * LANES:(pp + 1) * LANES] = _bf(acc)


def _mla_attention(qm, km, vm, bsz, seq):
    t = qm.shape[0]
    tq = MLA_TQ
    nq = seq // tq
    groups = MLA_HEADS // (2 * MLA_PAIRS_PER_STEP)
    qk_w = 2 * MLA_PAIRS_PER_STEP * LANES
    v_w = MLA_PAIRS_PER_STEP * LANES
    return pl.pallas_call(
        _mla_kernel,
        out_shape=jax.ShapeDtypeStruct((t, MLA_HEADS * MLA_V_DIM), BF16),
        grid=(bsz, groups, nq),
        in_specs=[
            pl.BlockSpec((tq, qk_w), lambda b, p, i: (b * nq + i, p)),
            pl.BlockSpec((seq, qk_w), lambda b, p, i: (b, p)),
            pl.BlockSpec((seq, v_w), lambda b, p, i: (b, p)),
        ],
        out_specs=pl.BlockSpec((tq, v_w), lambda b, p, i: (b * nq + i, p)),
        compiler_params=pltpu.CompilerParams(
            dimension_semantics=("arbitrary", "arbitrary", "arbitrary"),
            vmem_limit_bytes=V7X_VMEM_LIMIT),
        name="mla_attn",
    )(qm, km, vm)


def _merge_kernel(x_ref, yna_ref, ymla_ref, sgn_ref, sgm_ref, wpn_ref, wpm_ref, wout_ref,
                  gate1_ref, shift2_ref, scale2_ref, g2_ref, wr_ref,
                  x1_ref, h2_ref, lt_ref):
    d = x_ref.shape[1]
    for r0 in range(0, x_ref.shape[0], MERGE_SUB):
        rs = pl.ds(r0, MERGE_SUB)
        merged = (sgn_ref[rs, :].astype(F32) * _dot(yna_ref[rs, :], wpn_ref[...])
                  + sgm_ref[rs, :].astype(F32) * _dot(ymla_ref[rs, :], wpm_ref[...]))
        x1 = x_ref[rs, :] + gate1_ref[0] * _dot(_bf(merged), wout_ref[...])
        x1_ref[rs, :] = x1
        h2 = _rms(x1, d) * g2_ref[...]
        h2 = h2 * (1.0 + scale2_ref[0]) + shift2_ref[0]
        h2_ref[rs, :] = _bf(h2)
        lt_ref[:, rs] = _dot3_nt(wr_ref[...], h2)


def _merge(x2, yna, ymla, sgn, sgm, wpn, wpm, wout, mod3, g2, wr, seq):
    t, d = x2.shape
    tm = MERGE_TM
    per_b = seq // tm
    n_exp = wr.shape[0]

    def full(a):
        return pl.BlockSpec(a.shape, lambda i: (0,) * a.ndim)

    def rows(w):
        return pl.BlockSpec((tm, w), lambda i: (i, 0))

    def modblk(j):
        return pl.BlockSpec((1, 1, d), lambda i: (i // per_b, 0, j))

    return pl.pallas_call(
        _merge_kernel,
        out_shape=(jax.ShapeDtypeStruct((t, d), F32), jax.ShapeDtypeStruct((t, d), BF16),
                   jax.ShapeDtypeStruct((n_exp, t), F32)),
        grid=(t // tm,),
        in_specs=[rows(d), rows(yna.shape[1]), rows(ymla.shape[1]), rows(d), rows(d),
                  full(wpn), full(wpm), full(wout),
                  modblk(2), modblk(3), modblk(4), full(g2), full(wr)],
        out_specs=(rows(d), rows(d), pl.BlockSpec((n_exp, tm), lambda i: (0, i))),
        compiler_params=pltpu.CompilerParams(dimension_semantics=("arbitrary",),
                                             vmem_limit_bytes=V7X_VMEM_LIMIT),
        name="merge",
    )(x2, yna, ymla, sgn, sgm, wpn, wpm, wout, mod3, mod3, mod3, g2, wr)


def _route_kernel(lt_ref, eb_ref, o_ref, q_ref):
    n_exp, tn = lt_ref.shape
    per_g = n_exp // N_GROUPS
    neg_inf = -jnp.inf
    sc = _sigmoid(lt_ref[...])
    sel = sc + eb_ref[...]
    sc3 = sc.reshape(N_GROUPS, per_g, tn)
    g3 = sel.reshape(N_GROUPS, per_g, tn)
    io = lax.broadcasted_iota(jnp.int32, (N_GROUPS, per_g, tn), 1)
    gio = lax.broadcasted_iota(jnp.int32, (N_GROUPS, per_g, tn), 0)
    eio = gio * per_g + io

    m1 = jnp.max(g3, axis=1, keepdims=True)
    i1 = jnp.min(jnp.where(g3 == m1, io, per_g), axis=1, keepdims=True)
    m2 = jnp.max(jnp.where(io == i1, neg_inf, g3), axis=1, keepdims=True)
    gs = m1 + m2

    g1io = lax.broadcasted_iota(jnp.int32, (N_GROUPS, 1, tn), 0)
    gsel = jnp.zeros((N_GROUPS, 1, tn), F32)
    cur = gs
    for _ in range(TOPK_GROUPS):
        m = jnp.max(cur, axis=0, keepdims=True)
        i = jnp.min(jnp.where(cur == m, g1io, N_GROUPS), axis=0, keepdims=True)
        pick = g1io == i
        gsel = jnp.where(pick, 1.0, gsel)
        cur = jnp.where(pick, neg_inf, cur)

    cur = jnp.where(gsel > 0.0, g3, neg_inf)
    chosen = jnp.zeros((N_GROUPS, per_g, tn), F32)
    for _ in range(TOP_K):
        m = jnp.max(jnp.max(cur, axis=1, keepdims=True), axis=0, keepdims=True)
        cand = jnp.where(cur == m, eio, n_exp)
        i = jnp.min(jnp.min(cand, axis=1, keepdims=True), axis=0, keepdims=True)
        pick = eio == i
        chosen = jnp.where(pick, 1.0, chosen)
        cur = jnp.where(pick, neg_inf, cur)

    w = jnp.where(chosen > 0.0, sc3, 0.0)
    tot = jnp.sum(jnp.sum(w, axis=1, keepdims=True), axis=0, keepdims=True)
    gates = (w / tot * ROUTED_SCALE).reshape(n_exp, tn)
    o_ref[...] = gates
    routed = jnp.where(gates > 0.0, 1.0, 0.0).astype(BF16)
    ones = jnp.ones((8, MOE_TB), BF16)
    for j in range(tn // MOE_TB):
        n_row = _dot_nt(ones, routed[:, j * MOE_TB:(j + 1) * MOE_TB])[0:1]
        q_ref[j] = jnp.floor((n_row + (MOE_CHUNK - 1)) * (1.0 / MOE_CHUNK)).astype(jnp.int32)


def _route(lt, e_bias):
    n_exp, t = lt.shape
    tn = ROUTE_TN
    bps = tn // MOE_TB
    return pl.pallas_call(
        _route_kernel,
        out_shape=(jax.ShapeDtypeStruct((n_exp, t), F32),
                   jax.ShapeDtypeStruct((t // MOE_TB, 1, n_exp), jnp.int32)),
        grid=(t // tn,),
        in_specs=[pl.BlockSpec((n_exp, tn), lambda i: (0, i)),
                  pl.BlockSpec((n_exp, 1), lambda i: (0, 0))],
        out_specs=(pl.BlockSpec((n_exp, tn), lambda i: (0, i)),
                   pl.BlockSpec((bps, 1, n_exp), lambda i: (i, 0, 0))),
        compiler_params=pltpu.CompilerParams(dimension_semantics=("arbitrary",)),
        name="route",
    )(lt, e_bias.reshape(n_exp, 1))


def _for_each_chunk(n, fn):
    shift = MOE_COPY_UNROLL.bit_length() - 1

    def group(j, carry):
        for u in range(MOE_COPY_UNROLL):
            fn(j * MOE_COPY_UNROLL + u)
        return carry
    lax.fori_loop(0, n >> shift, group, 0)
    base = (n >> shift) << shift
    for u in range(MOE_COPY_UNROLL - 1):
        @pl.when(base + u < n)
        def _():
            fn(base + u)


def _dispatch_kernel(src2_s, dst2_s, n2_s, src1_s, dst1_s, n1_s, nch_s, total_s,
                     gt_ref, qrow_ref, qbrow_ref, h_ref, xs_ref, stage_ref, zero_ref, sem):
    b = pl.program_id(0)
    nb = pl.num_programs(0)
    slot = lax.rem(b, 2)
    n_exp, tb = gt_ref.shape
    rmax = stage_ref.shape[1] * MOE_CHUNK
    cpg = MOE_ROWGROUP // MOE_CHUNK

    routed = gt_ref[...] > 0.0
    before = (lax.broadcasted_iota(jnp.int32, (tb, tb), 0)
              < lax.broadcasted_iota(jnp.int32, (tb, tb), 1))
    pos = _dot(jnp.where(routed, 1.0, 0.0).astype(BF16), jnp.where(before, 1.0, 0.0).astype(BF16))
    posm = _bf(jnp.where(routed, pos, -1.0))
    qrow = qrow_ref[0]
    qbrow = qbrow_ref[0]
    qbrow_f = qbrow.astype(F32)
    h = h_ref[...]
    def sort_rows(g):
        r0 = g * MOE_ROWGROUP
        chunk = (lax.broadcasted_iota(jnp.int32, (MOE_ROWGROUP, n_exp), 0) + r0) >> MOE_CHUNK_SHIFT
        own = jnp.where(chunk >= qbrow, jnp.where(chunk < qbrow + qrow, 1.0, 0.0), 0.0)
        rank = _dot(_bf(own), posm)
        start = jnp.sum(own * qbrow_f, axis=-1, keepdims=True) * MOE_CHUNK
        rel = (lax.broadcasted_iota(jnp.int32, (MOE_ROWGROUP, 1), 0) + r0).astype(F32) - start
        onehot = jnp.where(rank == rel, 1.0, 0.0).astype(BF16)
        rows = _bf(_dot(onehot, h))
        stage_ref[slot, g * cpg:(g + 1) * cpg] = rows.reshape(cpg, MOE_CHUNK, rows.shape[1])

    n_groups = rmax // MOE_ROWGROUP
    n_sure = min(n_groups, (MOE_TB * TOP_K + MOE_ROWGROUP - 1) // MOE_ROWGROUP)
    for g in range(n_sure):
        sort_rows(g)
    for g in range(n_sure, n_groups):
        @pl.when(g * MOE_ROWGROUP < nch_s[b] * MOE_CHUNK)
        def _():
            sort_rows(g)

    def copy2(sl, src, dst):
        return pltpu.make_async_copy(stage_ref.at[sl, pl.ds(src, 2)], xs_ref.at[pl.ds(dst, 2)],
                                     sem.at[sl, 0])

    def copy1(sl, src_ref, dst):
        return pltpu.make_async_copy(src_ref, xs_ref.at[dst], sem.at[sl, 1])

    _for_each_chunk(n2_s[b], lambda i: copy2(slot, src2_s[b, i], dst2_s[b, i]).start())
    _for_each_chunk(n1_s[b], lambda i: copy1(slot, stage_ref.at[slot, src1_s[b, i]],
                                             dst1_s[b, i]).start())

    def wait_copies(sl, n2, n1):
        _for_each_chunk(n2, lambda i: copy2(sl, 0, 0).wait())
        _for_each_chunk(n1, lambda i: copy1(sl, zero_ref, 0).wait())

    @pl.when(b > 0)
    def _():
        wait_copies(1 - slot, n2_s[b - 1], n1_s[b - 1])

    @pl.when(b == nb - 1)
    def _():
        zero_ref[...] = jnp.zeros_like(zero_ref)
        n_tail = MOE_TM // MOE_CHUNK
        for c in range(n_tail):
            copy1(slot, zero_ref, total_s[0] + c).start()
        wait_copies(slot, n2_s[b], n1_s[b] + n_tail)


def _dispatch(gates_t, q, qbase, copies, nch, total, h2, n_chunks):
    n_exp, t = gates_t.shape
    d = h2.shape[1]
    nb = t // MOE_TB
    rmax = _moe_stage_rows(n_exp)
    grid_spec = pltpu.PrefetchScalarGridSpec(
        num_scalar_prefetch=8,
        grid=(nb,),
        in_specs=[
            pl.BlockSpec((n_exp, MOE_TB), lambda b, *_: (0, b)),
            pl.BlockSpec((1, 1, n_exp), lambda b, *_: (b, 0, 0)),
            pl.BlockSpec((1, 1, n_exp), lambda b, *_: (b, 0, 0)),
            pl.BlockSpec((MOE_TB, d), lambda b, *_: (b, 0)),
        ],
        out_specs=pl.BlockSpec(memory_space=pl.ANY),
        scratch_shapes=[pltpu.VMEM((2, rmax // MOE_CHUNK, MOE_CHUNK, d), BF16),
                        pltpu.VMEM((MOE_CHUNK, d), BF16), pltpu.SemaphoreType.DMA((2, 2))],
    )
    return pl.pallas_call(
        _dispatch_kernel,
        out_shape=jax.ShapeDtypeStruct((n_chunks, MOE_CHUNK, d), BF16),
        grid_spec=grid_spec,
        compiler_params=pltpu.CompilerParams(dimension_semantics=("arbitrary",),
                                             vmem_limit_bytes=V7X_VMEM_LIMIT),
        name="moe_dispatch",
    )(*copies, nch, total, gates_t, q.reshape(nb, 1, n_exp), qbase.reshape(nb, 1, n_exp), h2)


def _tile_pieces():
    cpt = MOE_TM // MOE_CHUNK
    return [1 << s for s in range(cpt.bit_length() - 1, -1, -1)]


def _expert_kernel(off_s, len_s, next_s, first_s, xs_ref, wg_ref, wu_ref, wd_ref, ys_ref,
                   xbuf, ybuf, wg_b, wu_b, wd_b, state, sem_in, sem_out):
    e = pl.program_id(0)
    n_exp = pl.num_programs(0)
    cpt = MOE_TM // MOE_CHUNK
    d = xbuf.shape[3]
    pieces = _tile_pieces()

    def tile_in(ee, tt, sl):
        return pltpu.make_async_copy(xs_ref.at[pl.ds(off_s[ee] + tt * cpt, cpt)], xbuf.at[sl],
                                     sem_in.at[sl])

    def for_each_piece(valid, fn):
        for k, piece in enumerate(pieces):
            @pl.when((valid & piece) != 0)
            def _():
                fn(k, piece, valid & ~(2 * piece - 1))

    def tile_out(sl, dst_chunk, k, piece, start):
        return pltpu.make_async_copy(ybuf.at[sl, pl.ds(start, piece)],
                                     ys_ref.at[pl.ds(dst_chunk + start, piece)], sem_out.at[sl, k])

    def drain(sl):
        for_each_piece(state[1 + sl], lambda k, piece, start: tile_out(sl, 0, k, piece, start).wait())
        state[1 + sl] = 0

    def following(ee, tt):
        safe = jnp.minimum(ee, n_exp - 1)
        more = tt + 1 < (len_s[safe] + cpt - 1) // cpt
        nxt_e = jnp.where(ee >= n_exp, n_exp, jnp.where(more, ee, next_s[safe]))
        return nxt_e, jnp.where(more, tt + 1, 0)

    def prefetch(ee, tt, xsl):
        @pl.when(ee < n_exp)
        def _():
            tile_in(ee, tt, xsl).start()

    @pl.when(e == 0)
    def _():
        state[0] = 0
        state[1] = 0
        state[2] = 0
        ahead = (first_s[0], 0)
        for k in range(MOE_XBUFS - 1):
            prefetch(ahead[0], ahead[1], k)
            ahead = following(*ahead)

    n_valid = len_s[e]
    n_tiles = (n_valid + cpt - 1) // cpt

    @pl.when(n_valid > 0)
    def _():
        wg_b[...] = _bf(wg_ref[0])
        wu_b[...] = _bf(wu_ref[0])
        wd_b[...] = _bf(wd_ref[0])

        def tile(t, carry):
            g = state[0]
            xsl = lax.rem(g, MOE_XBUFS)
            sl = g & 1
            tile_in(e, t, xsl).wait()
            ahead = (e, t)
            for _ in range(MOE_XBUFS - 1):
                ahead = following(*ahead)
            prefetch(ahead[0], ahead[1], lax.rem(g + MOE_XBUFS - 1, MOE_XBUFS))

            drain(sl)
            valid = jnp.minimum(n_valid - t * cpt, cpt)

            def ffn(n_c):
                x = xbuf[xsl, :n_c].reshape(n_c * MOE_CHUNK, d)
                a = _silu(_dot(x, wg_b[...])) * _dot(x, wu_b[...])
                ybuf[sl, :n_c] = _bf(_dot(_bf(a), wd_b[...])).reshape(n_c, MOE_CHUNK, d)

            sizes = [cpt >> k for k in range(MOE_TILE_ARMS)]
            for k, n_c in enumerate(sizes):
                covers_more = valid > (sizes[k + 1] if k + 1 < len(sizes) else 0)
                fits = valid <= n_c

                @pl.when(jnp.logical_and(covers_more, fits))
                def _():
                    ffn(n_c)

            dst = off_s[e] + t * cpt
            for_each_piece(valid, lambda k, piece, start: tile_out(sl, dst, k, piece, start).start())
            state[1 + sl] = valid
            state[0] = state[0] + 1
            return carry
        lax.fori_loop(0, n_tiles, tile, 0)

    @pl.when(e == n_exp - 1)
    def _():
        drain(0)
        drain(1)


def _experts(off, per_exp, nxt, first, xs, wg, wu, wd):
    n_chunks, _, d = xs.shape
    n_exp, _, ff = wg.shape
    cpt = MOE_TM // MOE_CHUNK

    def w_blk(e, *_):
        return (e, 0, 0)

    grid_spec = pltpu.PrefetchScalarGridSpec(
        num_scalar_prefetch=4,
        grid=(n_exp,),
        in_specs=[pl.BlockSpec(memory_space=pl.ANY),
                  pl.BlockSpec((1, d, ff), w_blk), pl.BlockSpec((1, d, ff), w_blk),
                  pl.BlockSpec((1, ff, d), w_blk)],
        out_specs=pl.BlockSpec(memory_space=pl.ANY),
        scratch_shapes=[pltpu.VMEM((MOE_XBUFS, cpt, MOE_CHUNK, d), BF16),
                        pltpu.VMEM((2, cpt, MOE_CHUNK, d), BF16),
                        pltpu.VMEM((d, ff), BF16), pltpu.VMEM((d, ff), BF16),
                        pltpu.VMEM((ff, d), BF16),
                        pltpu.SMEM((3,), jnp.int32),
                        pltpu.SemaphoreType.DMA((MOE_XBUFS,)),
                        pltpu.SemaphoreType.DMA((2, len(_tile_pieces())))],
    )
    return pl.pallas_call(
        _expert_kernel,
        out_shape=jax.ShapeDtypeStruct((n_chunks, MOE_CHUNK, d), BF16),
        grid_spec=grid_spec,
        compiler_params=pltpu.CompilerParams(dimension_semantics=("arbitrary",),
                                             vmem_limit_bytes=V7X_VMEM_LIMIT),
        name="moe_experts",
    )(off, per_exp, nxt, first, xs, wg, wu, wd)


def _combine_kernel(src2_s, dst2_s, n2_s, src1_s, dst1_s, n1_s, nch_s,
                    gt_ref, qcol_ref, qbcol_ref, h_ref, x1_ref, gate2_ref,
                    wsg_ref, wsu_ref, wsd_ref, ys_ref, o_ref, stage_ref, sem):
    b = pl.program_id(0)
    nb = pl.num_programs(0)
    slot = lax.rem(b, 2)
    n_exp, tb = gt_ref.shape
    rmax = stage_ref.shape[1] * MOE_CHUNK

    def copy2(sl, staged, sorted_):
        return pltpu.make_async_copy(ys_ref.at[pl.ds(sorted_, 2)],
                                     stage_ref.at[sl, pl.ds(staged, 2)], sem.at[sl, 0])

    def copy1(sl, staged, sorted_):
        return pltpu.make_async_copy(ys_ref.at[sorted_], stage_ref.at[sl, staged], sem.at[sl, 1])

    def fetch(bb, sl):
        _for_each_chunk(n2_s[bb], lambda i: copy2(sl, src2_s[bb, i], dst2_s[bb, i]).start())
        _for_each_chunk(n1_s[bb], lambda i: copy1(sl, src1_s[bb, i], dst1_s[bb, i]).start())

    @pl.when(b == 0)
    def _():
        stage_ref[...] = jnp.zeros_like(stage_ref)
        fetch(0, 0)

    _for_each_chunk(n2_s[b], lambda i: copy2(slot, 0, 0).wait())
    _for_each_chunk(n1_s[b], lambda i: copy1(slot, 0, 0).wait())

    @pl.when(b + 1 < nb)
    def _():
        fetch(b + 1, 1 - slot)

    gt = gt_ref[...]
    routed = jnp.where(gt > 0.0, 1.0, 0.0).astype(BF16)
    i0 = lax.broadcasted_iota(jnp.int32, (tb, tb), 0)
    i1 = lax.broadcasted_iota(jnp.int32, (tb, tb), 1)
    eye = jnp.where(i0 == i1, 1.0, 0.0).astype(BF16)
    routed_t = _dot_nt(eye, routed)
    gates_tok = _dot_nt(eye, _bf(gt))
    earlier = jnp.where(i1 < i0, 1.0, 0.0).astype(BF16)
    pos_t = _dot(earlier, _bf(routed_t))
    posm_t = _bf(jnp.where(routed_t > 0.0, pos_t, -1.0))

    qcol = qcol_ref[0]
    qbcol = qbcol_ref[0]
    h = h_ref[...]
    a = _silu(_dot(h, wsg_ref[...])) * _dot(h, wsu_ref[...])
    shared = _dot(_bf(a), wsd_ref[...])

    def finish(rows):
        chunk = lax.broadcasted_iota(jnp.int32, (n_exp, rows), 1) >> MOE_CHUNK_SHIFT
        own = jnp.where(chunk >= qbcol, jnp.where(chunk < qbcol + qcol, 1.0, 0.0), 0.0)
        own_b = _bf(own)
        rank = _dot(posm_t, own_b)
        wexp = _dot(_bf(gates_tok), own_b)
        start = jnp.sum(own * qbcol.astype(F32), axis=0, keepdims=True) * MOE_CHUNK
        rel = lax.broadcasted_iota(jnp.int32, (1, rows), 1).astype(F32) - start
        weights = _bf(jnp.where(rank == rel, wexp, 0.0))
        staged = stage_ref[slot, :rows // MOE_CHUNK].reshape(rows, o_ref.shape[1])
        o_ref[...] = x1_ref[...] + gate2_ref[0] * (_dot(weights, staged) + shared)

    short = rmax - MOE_ROWGROUP

    @pl.when(nch_s[b] * MOE_CHUNK <= short)
    def _():
        finish(short)

    @pl.when(nch_s[b] * MOE_CHUNK > short)
    def _():
        finish(rmax)


def _combine(gates_t, q, qbase, copies, nch, h2, x1, mod3, wsg, wsu, wsd, ys, seq):
    n_exp, t = gates_t.shape
    d = h2.shape[1]
    nb = t // MOE_TB
    per_b = seq // MOE_TB
    stage_chunks = _moe_stage_rows(n_exp) // MOE_CHUNK

    def full(a):
        return pl.BlockSpec(a.shape, lambda b, *_: (0,) * a.ndim)

    def rows(w):
        return pl.BlockSpec((MOE_TB, w), lambda b, *_: (b, 0))

    grid_spec = pltpu.PrefetchScalarGridSpec(
        num_scalar_prefetch=7,
        grid=(nb,),
        in_specs=[
            pl.BlockSpec((n_exp, MOE_TB), lambda b, *_: (0, b)),
            pl.BlockSpec((1, n_exp, 1), lambda b, *_: (b, 0, 0)),
            pl.BlockSpec((1, n_exp, 1), lambda b, *_: (b, 0, 0)),
            rows(d), rows(d),
            pl.BlockSpec((1, 1, d), lambda b, *_: (b // per_b, 0, 5)),
            full(wsg), full(wsu), full(wsd),
            pl.BlockSpec(memory_space=pl.ANY),
        ],
        out_specs=rows(d),
        scratch_shapes=[pltpu.VMEM((2, stage_chunks, MOE_CHUNK, d), BF16),
                        pltpu.SemaphoreType.DMA((2, 2))],
    )
    return pl.pallas_call(
        _combine_kernel,
        out_shape=jax.ShapeDtypeStruct((t, d), F32),
        grid_spec=grid_spec,
        compiler_params=pltpu.CompilerParams(dimension_semantics=("arbitrary",),
                                             vmem_limit_bytes=V7X_VMEM_LIMIT),
        name="moe_combine",
    )(*copies, nch, gates_t, q.reshape(nb, n_exp, 1), qbase.reshape(nb, n_exp, 1),
      h2, x1, mod3, wsg, wsu, wsd, ys)


def _moe_plan(q):
    nb, n_exp = q.shape
    qbase = jnp.cumsum(q, axis=1) - q
    nch = jnp.sum(q, axis=1)
    per_exp = jnp.sum(q, axis=0)
    off = jnp.cumsum(per_exp) - per_exp
    dstq = off[None, :] + jnp.cumsum(q, axis=0) - q
    def copy_list(count, length, stage_start, sorted_start, step):
        cum = jnp.cumsum(count, axis=1) - count
        i = jnp.arange(length, dtype=jnp.int32)
        ii = i[None, :, None]
        owned = (ii >= cum[:, None, :]) & (ii < (cum + count)[:, None, :])

        def place(start):
            return step * i[None, :] + jnp.sum(
                jnp.where(owned, (start - step * cum)[:, None, :], 0), axis=2)
        return place(stage_start), place(sorted_start), jnp.sum(count, axis=1)

    pairs = q >> 1
    max_chunks = _moe_block_chunks(n_exp)
    src2, dst2, n2 = copy_list(pairs, max_chunks // 2, qbase, dstq, 2)
    src1, dst1, n1 = copy_list(q & 1, n_exp, qbase + 2 * pairs, dstq + 2 * pairs, 1)
    copies = (src2, dst2, n2, src1, dst1, n1)
    ids = jnp.arange(n_exp, dtype=jnp.int32)
    later = (ids[None, :] > ids[:, None]) & (per_exp[None, :] > 0)
    nxt = jnp.min(jnp.where(later, ids[None, :], n_exp), axis=1)
    first = jnp.min(jnp.where(per_exp > 0, ids, n_exp)).reshape(1)
    total = jnp.sum(per_exp).reshape(1)
    return qbase, copies, nch, off, per_exp, nxt, first, total


def _moe_block_chunks(n_exp):
    return MOE_TB * TOP_K // MOE_CHUNK + n_exp


def _moe_stage_rows(n_exp):
    return -(-_moe_block_chunks(n_exp) * MOE_CHUNK // MOE_ROWGROUP) * MOE_ROWGROUP


def _moe(h2, gates_t, q3, x1, mod3, wsg, wsu, wsd, wg, wu, wd, seq):
    n_exp, t = gates_t.shape
    nb = t // MOE_TB
    q = q3.reshape(nb, n_exp)
    qbase, copies, nch, off, per_exp, nxt, first, total = _moe_plan(q)
    n_chunks = nb * _moe_block_chunks(n_exp) + MOE_TM // MOE_CHUNK
    xs = _dispatch(gates_t, q, qbase, copies, nch, total, h2, n_chunks)
    ys = _experts(off, per_exp, nxt, first, xs, wg, wu, wd)
    return _combine(gates_t, q, qbase, copies, nch, h2, x1, mod3, wsg, wsu, wsd, ys, seq)


def _pad_heads(w, heads, width):
    lead = w.shape[:-1]
    w = w.reshape(lead + (heads, width))
    w = jnp.pad(w, [(0, 0)] * len(lead) + [(0, 0), (0, LANES - width)])
    return w.reshape(lead + (heads * LANES,))


def kernel(x, c, positions, w_ada, b_ada, g_norm1, w_in, g_na_q, g_na_k, na_rpb, g_q_lat, w_uq,
           g_kv_lat, w_ukv, g_mla_q, g_mla_k, w_proj_na, w_proj_mla, w_out, g_norm2, w_router,
           e_bias, w_exp_gate, w_exp_up, w_exp_down, w_sh_gate, w_sh_up, w_sh_down):
    bsz, seq, d = x.shape
    t = bsz * seq
    depth = w_ada.shape[0]
    na_w = NA_HEADS * NA_HEAD_DIM
    q_rank = g_q_lat.shape[1]
    kv_rank = g_kv_lat.shape[1]
    n_rows = seq // GRID_W

    pos = positions.reshape(1, t)
    half = MLA_ROPE_DIM // 2
    freq = (ROPE_THETA ** (-jnp.arange(half, dtype=F32) / half)).reshape(half, 1)

    x2 = x.reshape(t, d)
    for l in range(depth):
        mod3 = _adaln(c, w_ada[l], b_ada[l]).reshape(bsz, 1, 6 * d)

        wi = w_in[l]
        o_lat = 3 * na_w
        o_rot = o_lat + q_rank + kv_rank
        o_gate = o_rot + MLA_ROPE_DIM
        wqkv = _bf(wi[:, :o_lat])
        w_rot = jnp.pad(wi[:, o_rot:o_gate], ((0, 0), (MLA_NOPE_DIM, LANES - MLA_QK_DIM)))
        wlat = _bf(jnp.concatenate([wi[:, o_lat:o_rot], w_rot], axis=1))
        wgate = _bf(wi[:, o_gate:])
        gq = (jnp.tile(g_na_q[l], NA_HEADS) * (NA_HEAD_DIM ** -0.5 * LOG2E)).reshape(1, na_w)
        gk = jnp.tile(g_na_k[l], NA_HEADS).reshape(1, na_w)
        wuq = _bf(_pad_heads(w_uq[l], MLA_HEADS, MLA_QK_DIM))
        wukv = w_ukv[l].reshape(kv_rank, MLA_HEADS, MLA_NOPE_DIM + MLA_V_DIM)
        wuk = _bf(_pad_heads(wukv[:, :, :MLA_NOPE_DIM].reshape(kv_rank, -1), MLA_HEADS, MLA_NOPE_DIM))
        wuv = _bf(wukv[:, :, MLA_NOPE_DIM:].reshape(kv_rank, MLA_HEADS * MLA_V_DIM))
        gmq = _pad_heads(jnp.tile(g_mla_q[l], MLA_HEADS) * (MLA_QK_DIM ** -0.5 * LOG2E),
                         MLA_HEADS, MLA_QK_DIM).reshape(1, -1)
        gmk = _pad_heads(jnp.tile(g_mla_k[l], MLA_HEADS), MLA_HEADS, MLA_QK_DIM).reshape(1, -1)

        qa, ka, va, qm, km, vm, sgn, sgm = _inproj(
            x2, mod3, g_norm1[l].reshape(1, d), wqkv, wlat, wgate, gq, gk,
            g_q_lat[l].reshape(1, q_rank), g_kv_lat[l].reshape(1, kv_rank), wuq, wuk, wuv,
            gmq, gmk, pos, freq, seq)

        bias = _na_bias(na_rpb[l], n_rows)
        y_na = _na_attention(qa, ka, va, bias, bsz, seq)
        y_mla = _mla_attention(qm, km, vm, bsz, seq)

        x1, h2, lt = _merge(x2, y_na, y_mla, sgn, sgm, _bf(w_proj_na[l]), _bf(w_proj_mla[l]),
                            _bf(w_out[l]), mod3, g_norm2[l].reshape(1, d), w_router[l].T, seq)
        gates_t, q3 = _route(lt, e_bias[l])
        x2 = _moe(h2, gates_t, q3, x1, mod3, _bf(w_sh_gate[l]), _bf(w_sh_up[l]),
                  _bf(w_sh_down[l]), w_exp_gate[l], w_exp_up[l], w_exp_down[l], seq)
    return x2.reshape(bsz, seq, d)
```

```python
import functools

import jax
import jax.numpy as jnp
from jax import lax
from jax.experimental import pallas as pl
from jax.experimental.pallas import tpu as pltpu

GRID_W = 64
NA_HEADS = 8
NA_HEAD_DIM = 64
NA_WIN_ROWS = 8
NA_WIN_COLS = 16
MLA_HEADS = 8
MLA_NOPE_DIM = 64
MLA_ROPE_DIM = 32
MLA_V_DIM = 64
MLA_QK_DIM = MLA_NOPE_DIM + MLA_ROPE_DIM
ROPE_THETA = 10000.0
N_GROUPS = 8
TOPK_GROUPS = 4
TOP_K = 8
ROUTED_SCALE = 2.5
EPS = 1e-6
NEG_BIG = -1e30

LANES = 128
V7X_VMEM_LIMIT = 56 * 1024 * 1024

NA_QROWS = 4
NA_BAND = 12
NA_BLOCK_TYPES = 3
NA_BLOCKS_PER_STEP = 4
MERGE_TM = 1024
MERGE_SUB = 1024
INPROJ_TM = 1024
INPROJ_SUB = 256
MLA_TQ = 1024
MLA_SUB = 512
MLA_PAIRS_PER_STEP = 4
LOG2E = 1.4426950408889634
MOE_TB = 256
ROUTE_TN = 1024
MOE_CHUNK_SHIFT = 4
MOE_CHUNK = 1 << MOE_CHUNK_SHIFT
MOE_TM = 1024
MOE_TILE_ARMS = 2
MOE_ROWGROUP = 512
MOE_COPY_UNROLL = 4
MOE_XBUFS = 3

F32 = jnp.float32
BF16 = jnp.bfloat16


def _bf(x):
    return x.astype(BF16)


def _dot(a, b):
    return jnp.dot(a, b, preferred_element_type=F32)


def _dot_nt(a, b):
    return lax.dot_general(a, b, (((1,), (1,)), ((), ())), preferred_element_type=F32)


def _split(x):
    hi = _bf(x)
    lo = _bf(x - hi.astype(F32))
    return hi, lo


def _dot3(a, b):
    ah, al = _split(a)
    bh, bl = _split(b)
    return _dot(ah, bh) + (_dot(ah, bl) + _dot(al, bh))


def _dot3_nt(a, b):
    m = a.shape[0]
    ah, al = _split(a)
    bh, bl = _split(b)
    both = _dot_nt(jnp.concatenate([ah, al], axis=0), bh)
    return both[:m] + (_dot_nt(ah, bl) + both[m:])


def _sigmoid(x):
    return 1.0 / (1.0 + jnp.exp(-x))


def _silu(x):
    return x * _sigmoid(x)


def _rms(x, n):
    ss = jnp.sum(x * x, axis=-1, keepdims=True)
    return x * lax.rsqrt(ss * (1.0 / n) + EPS)


def _adaln_kernel(c_ref, w_ref, b_ref, o_ref):
    c = c_ref[...]
    o_ref[...] = _dot3(_silu(c), w_ref[...]) + b_ref[...]


def _adaln(c, w, b):
    bsz, d = c.shape
    n = w.shape[1]
    tn = 1024
    return pl.pallas_call(
        _adaln_kernel,
        out_shape=jax.ShapeDtypeStruct((bsz, n), F32),
        grid=(n // tn,),
        in_specs=[
            pl.BlockSpec((bsz, d), lambda j: (0, 0)),
            pl.BlockSpec((d, tn), lambda j: (0, j)),
            pl.BlockSpec((1, tn), lambda j: (0, j)),
        ],
        out_specs=pl.BlockSpec((bsz, tn), lambda j: (0, j)),
        compiler_params=pltpu.CompilerParams(dimension_semantics=("arbitrary",)),
        name="adaln",
    )(c, w, b.reshape(1, n))


def _inproj_kernel(x_ref, shift_ref, scale_ref, g1_ref, wqkv_ref, wlat_ref, wgate_ref,
                   gq_ref, gk_ref, gql_ref, gkvl_ref, wuq_ref, wuk_ref, wuv_ref,
                   gmq_ref, gmk_ref, pos_ref, freq_ref,
                   qa_ref, ka_ref, va_ref, qm_ref, km_ref, vm_ref, sgn_ref, sgm_ref):
    for r0 in range(0, x_ref.shape[0], INPROJ_SUB):
        _inproj_rows(pl.ds(r0, INPROJ_SUB), x_ref, shift_ref, scale_ref, g1_ref, wqkv_ref,
                     wlat_ref, wgate_ref, gq_ref, gk_ref, gql_ref, gkvl_ref, wuq_ref, wuk_ref,
                     wuv_ref, gmq_ref, gmk_ref, pos_ref, freq_ref, qa_ref, ka_ref, va_ref,
                     qm_ref, km_ref, vm_ref, sgn_ref, sgm_ref)


def _inproj_rows(rs, x_ref, shift_ref, scale_ref, g1_ref, wqkv_ref, wlat_ref, wgate_ref,
                 gq_ref, gk_ref, gql_ref, gkvl_ref, wuq_ref, wuk_ref, wuv_ref,
                 gmq_ref, gmk_ref, pos_ref, freq_ref,
                 qa_ref, ka_ref, va_ref, qm_ref, km_ref, vm_ref, sgn_ref, sgm_ref):
    d = x_ref.shape[1]
    x = x_ref[rs, :]
    h = _rms(x, d) * g1_ref[...]
    h = h * (1.0 + scale_ref[0]) + shift_ref[0]
    hb = _bf(h)

    qkv = _dot(hb, wqkv_ref[...])
    lat = _dot(hb, wlat_ref[...])
    gts = _dot(hb, wgate_ref[...])
    sgn_ref[rs, :] = _bf(_sigmoid(gts[:, :d]))
    sgm_ref[rs, :] = _bf(_sigmoid(gts[:, d:]))

    na_w = NA_HEADS * NA_HEAD_DIM
    lane = lax.broadcasted_iota(jnp.int32, (1, LANES), 1)
    lo_half = lane < NA_HEAD_DIM
    for p in range(na_w // LANES):
        sl = slice(p * LANES, (p + 1) * LANES)
        for src_off, g_ref, dst_ref in ((0, gq_ref, qa_ref), (na_w, gk_ref, ka_ref)):
            t = qkv[:, src_off + p * LANES: src_off + (p + 1) * LANES]
            sq = t * t
            s_lo = jnp.sum(jnp.where(lo_half, sq, 0.0), axis=-1, keepdims=True)
            s_hi = jnp.sum(jnp.where(lo_half, 0.0, sq), axis=-1, keepdims=True)
            r = jnp.where(lo_half,
                          lax.rsqrt(s_lo * (1.0 / NA_HEAD_DIM) + EPS),
                          lax.rsqrt(s_hi * (1.0 / NA_HEAD_DIM) + EPS))
            dst_ref[rs, sl] = _bf(t * r * g_ref[:, sl])
    va_ref[rs, :] = _bf(qkv[:, 2 * na_w: 3 * na_w])

    q_rank = gql_ref.shape[1]
    kv_rank = gkvl_ref.shape[1]
    qln = _rms(lat[:, :q_rank], q_rank) * gql_ref[...]
    kvn = _bf(_rms(lat[:, q_rank:q_rank + kv_rank], kv_rank) * gkvl_ref[...])
    qpre = _dot(_bf(qln), wuq_ref[...])
    knope = _dot(kvn, wuk_ref[...])
    vm_ref[rs, :] = _bf(_dot(kvn, wuv_ref[...]))
    krot = lat[:, q_rank + kv_rank:]

    tm = INPROJ_SUB
    half = MLA_ROPE_DIM // 2
    ang_t = freq_ref[...] * pos_ref[:, rs].astype(F32)
    cos_t = jnp.cos(ang_t)
    sin_t = jnp.sin(ang_t)
    l_i = lax.broadcasted_iota(jnp.int32, (LANES, half), 0)
    j_i = lax.broadcasted_iota(jnp.int32, (LANES, half), 1)
    hit = jnp.where((l_i >= MLA_NOPE_DIM) & (l_i < MLA_QK_DIM)
                    & (((l_i - MLA_NOPE_DIM) & (half - 1)) == j_i), 1.0, 0.0)
    first_half = l_i < MLA_NOPE_DIM + half
    eye = jnp.where(lax.broadcasted_iota(jnp.int32, (tm, tm), 0)
                    == lax.broadcasted_iota(jnp.int32, (tm, tm), 1), 1.0, 0.0).astype(BF16)

    def table(sel, vals, fill_nope):
        hi, lo = _split(vals)
        w = _dot(_bf(sel), hi) + _dot(_bf(sel), lo)
        if fill_nope:
            w = jnp.where(lax.broadcasted_iota(jnp.int32, (LANES, tm), 0) < MLA_NOPE_DIM, 1.0, w)
        hi, lo = _split(w)
        return _dot_nt(eye, hi) + _dot_nt(eye, lo)

    c_tab = table(hit, cos_t, True)
    s_up = table(jnp.where(first_half, 0.0, hit), sin_t, False)
    s_dn = table(jnp.where(first_half, -hit, 0.0), sin_t, False)

    def rope(t):
        return t * c_tab + pltpu.roll(t, half, 1) * s_up + pltpu.roll(t, LANES - half, 1) * s_dn

    kr = rope(krot)
    for hd in range(MLA_HEADS):
        sl = slice(hd * LANES, (hd + 1) * LANES)
        qh = rope(qpre[:, sl])
        qm_ref[rs, sl] = _bf(_rms(qh, MLA_QK_DIM) * gmq_ref[:, sl])
        kh = knope[:, sl] + kr
        km_ref[rs, sl] = _bf(_rms(kh, MLA_QK_DIM) * gmk_ref[:, sl])


def _inproj(x2, mod3, g1, wqkv, wlat, wgate, gq, gk, gql, gkvl, wuq, wuk, wuv, gmq, gmk,
            pos, freq, seq):
    t, d = x2.shape
    tm = INPROJ_TM
    per_b = seq // tm
    na_w = NA_HEADS * NA_HEAD_DIM
    mla_w = MLA_HEADS * LANES
    v_w = MLA_HEADS * MLA_V_DIM

    def full(a):
        return pl.BlockSpec(a.shape, lambda i: (0,) * a.ndim, pipeline_mode=pl.Buffered(1))

    def rows(w):
        return pl.BlockSpec((tm, w), lambda i: (i, 0))

    out_shapes = (
        jax.ShapeDtypeStruct((t, na_w), BF16), jax.ShapeDtypeStruct((t, na_w), BF16),
        jax.ShapeDtypeStruct((t, na_w), BF16),
        jax.ShapeDtypeStruct((t, mla_w), BF16), jax.ShapeDtypeStruct((t, mla_w), BF16),
        jax.ShapeDtypeStruct((t, v_w), BF16),
        jax.ShapeDtypeStruct((t, d), BF16), jax.ShapeDtypeStruct((t, d), BF16),
    )
    return pl.pallas_call(
        _inproj_kernel,
        out_shape=out_shapes,
        grid=(t // tm,),
        in_specs=[
            rows(d),
            pl.BlockSpec((1, 1, d), lambda i: (i // per_b, 0, 0)),
            pl.BlockSpec((1, 1, d), lambda i: (i // per_b, 0, 1)),
            full(g1), full(wqkv), full(wlat), full(wgate), full(gq), full(gk), full(gql),
            full(gkvl), full(wuq), full(wuk), full(wuv), full(gmq), full(gmk),
            pl.BlockSpec((1, tm), lambda i: (0, i)),
            full(freq),
        ],
        out_specs=(rows(na_w), rows(na_w), rows(na_w), rows(mla_w), rows(mla_w), rows(v_w),
                   rows(d), rows(d)),
        compiler_params=pltpu.CompilerParams(dimension_semantics=("arbitrary",),
                                             vmem_limit_bytes=V7X_VMEM_LIMIT),
        name="inproj",
    )(x2, mod3, mod3, g1, wqkv, wlat, wgate, gq, gk, gql, gkvl, wuq, wuk, wuv, gmq, gmk,
      pos, freq)


def _na_block_geometry(block_type, n_rows):
    if block_type == 0:
        return 0, 0
    if block_type == 1:
        r0 = NA_QROWS
        return r0, r0 - NA_WIN_ROWS // 2
    return n_rows - NA_QROWS, n_rows - NA_BAND


def _na_bias_kernel(rpb_ref, o_ref, m_ref, *, n_rows):
    hd = pl.program_id(0)
    n_dr = 2 * NA_WIN_ROWS - 1
    n_dc = 2 * NA_WIN_COLS - 1
    qc = lax.broadcasted_iota(jnp.int32, (GRID_W, LANES), 0)
    kc = lax.broadcasted_iota(jnp.int32, (GRID_W, LANES), 1) & (GRID_W - 1)
    dc = jnp.clip(kc - qc, -(NA_WIN_COLS - 1), NA_WIN_COLS - 1) + (NA_WIN_COLS - 1)
    cstart = jnp.clip(qc - NA_WIN_COLS // 2, 0, GRID_W - NA_WIN_COLS)
    col_ok = (kc >= cstart) & (kc < cstart + NA_WIN_COLS)
    for i_dr in range(n_dr):
        acc = jnp.zeros((GRID_W, LANES), F32)
        for t in range(n_dc):
            acc = jnp.where(dc == t, rpb_ref[hd, i_dr * n_dc + t], acc)
        m_ref[i_dr] = jnp.where(col_ok, acc * LOG2E, NEG_BIG)
    neg = jnp.full((GRID_W, LANES), NEG_BIG, F32)
    lo_half = lax.broadcasted_iota(jnp.int32, (GRID_W, LANES), 1) < GRID_W
    kh = NA_WIN_ROWS
    for bt in range(NA_BLOCK_TYPES):
        r0, start = _na_block_geometry(bt, n_rows)
        for i in range(NA_QROWS):
            r = r0 + i
            rs = min(max(r - kh // 2, 0), n_rows - kh)
            for jp in range(NA_BAND // 2):
                halves = []
                for j in (2 * jp, 2 * jp + 1):
                    krow = start + j
                    if rs <= krow < rs + kh:
                        halves.append(m_ref[krow - r + (NA_WIN_ROWS - 1)])
                    else:
                        halves.append(neg)
                tile = jnp.where(lo_half, halves[0], halves[1])
                o_ref[bt, 0, i * GRID_W:(i + 1) * GRID_W, jp * LANES:(jp + 1) * LANES] = tile


def _na_bias(rpb, n_rows):
    heads = rpb.shape[0]
    nq = NA_QROWS * GRID_W
    nk = NA_BAND * GRID_W
    rpb2 = rpb.reshape(heads, -1)
    return pl.pallas_call(
        functools.partial(_na_bias_kernel, n_rows=n_rows),
        out_shape=jax.ShapeDtypeStruct((NA_BLOCK_TYPES, heads, nq, nk), F32),
        grid=(heads,),
        in_specs=[pl.BlockSpec(memory_space=pltpu.SMEM)],
        out_specs=pl.BlockSpec((NA_BLOCK_TYPES, 1, nq, nk), lambda hd: (0, hd, 0, 0)),
        scratch_shapes=[pltpu.VMEM((2 * NA_WIN_ROWS - 1, GRID_W, LANES), F32)],
        compiler_params=pltpu.CompilerParams(dimension_semantics=("arbitrary",)),
        name="na_bias",
    )(rpb2)


def _softmax_pv(s, v_pair, hh, half):
    lane = lax.broadcasted_iota(jnp.int32, (1, LANES), 1)
    mine = (lane < half) if hh == 0 else (lane >= half)
    den_lane = half if hh == 0 else 0
    m = jnp.max(s, axis=-1, keepdims=True)
    p = _bf(jnp.exp2(s - m))
    ones_row = jnp.where(lane == den_lane, 1.0, 0.0).astype(BF16)
    o = _dot(p, jnp.where(mine, v_pair, ones_row))
    den = jnp.sum(jnp.where(lane == den_lane, o, 0.0), axis=-1, keepdims=True)
    return jnp.where(mine, o / den, 0.0)


def _na_kernel(q_ref, k_ref, v_ref, bias_ref, o_ref, *, n_blocks, n_rows):
    nk = NA_BAND * GRID_W
    tq = NA_QROWS * GRID_W
    lo_half = lax.broadcasted_iota(jnp.int32, (1, LANES), 1) < NA_HEAD_DIM
    for u in range(NA_BLOCKS_PER_STEP):
        blk = pl.program_id(1) * NA_BLOCKS_PER_STEP + u
        btype = jnp.where(blk == 0, 0, jnp.where(blk == n_blocks - 1, 2, 1))
        start_row = jnp.where(blk == 0, 0,
                              jnp.where(blk == n_blocks - 1, n_rows - NA_BAND,
                                        blk * NA_QROWS - NA_WIN_ROWS // 2))
        off = pl.multiple_of(start_row * GRID_W, GRID_W)
        rs = pl.ds(u * tq, tq)
        for p in range(NA_HEADS * NA_HEAD_DIM // LANES):
            sl = slice(p * LANES, (p + 1) * LANES)
            qp = q_ref[rs, sl]
            kb = k_ref[pl.ds(off, nk), sl]
            vb = v_ref[pl.ds(off, nk), sl]
            zero = jnp.zeros_like(qp)
            q2 = jnp.concatenate([jnp.where(lo_half, qp, zero), jnp.where(lo_half, zero, qp)],
                                 axis=0)
            s = _dot_nt(q2, kb) + bias_ref[btype, 2 * p:2 * p + 2].reshape(2 * tq, nk)
            m = jnp.max(s, axis=-1, keepdims=True)
            e = jnp.exp2(s - m)
            den = jnp.sum(e, axis=-1, keepdims=True)
            o = _dot(_bf(e), vb) / den
            o_ref[rs, sl] = _bf(jnp.where(lo_half, o[:tq], o[tq:]))


def _na_attention(qa, ka, va, bias, bsz, seq):
    t, w = qa.shape
    n_rows = seq // GRID_W
    n_blocks = n_rows // NA_QROWS
    tq = NA_QROWS * GRID_W
    nk = NA_BAND * GRID_W
    heads = bias.shape[1]

    bps = NA_BLOCKS_PER_STEP
    steps = n_blocks // bps
    return pl.pallas_call(
        functools.partial(_na_kernel, n_blocks=n_blocks, n_rows=n_rows),
        out_shape=jax.ShapeDtypeStruct((t, w), BF16),
        grid=(bsz, steps),
        in_specs=[
            pl.BlockSpec((bps * tq, w), lambda b, j: (b * steps + j, 0)),
            pl.BlockSpec((seq, w), lambda b, j: (b, 0)),
            pl.BlockSpec((seq, w), lambda b, j: (b, 0)),
            pl.BlockSpec(bias.shape, lambda b, j: (0, 0, 0, 0), pipeline_mode=pl.Buffered(1)),
        ],
        out_specs=pl.BlockSpec((bps * tq, w), lambda b, j: (b * steps + j, 0)),
        compiler_params=pltpu.CompilerParams(dimension_semantics=("arbitrary", "arbitrary"),
                                             vmem_limit_bytes=V7X_VMEM_LIMIT),
        name="na_attn",
    )(qa, ka, va, bias)


def _mla_kernel(q_ref, k_ref, v_ref, o_ref):
    for r0 in range(0, q_ref.shape[0], MLA_SUB):
        rs = pl.ds(r0, MLA_SUB)
        for pp in range(MLA_PAIRS_PER_STEP):
            v_pair = v_ref[:, pp * LANES:(pp + 1) * LANES]
            acc = jnp.zeros((MLA_SUB, LANES), F32)
            for hh in range(2):
                sl = slice((2 * pp + hh) * LANES, (2 * pp + hh + 1) * LANES)
                s = _dot_nt(q_ref[rs, sl], k_ref[:, sl])
                acc = acc + _softmax_pv(s, v_pair, hh, MLA_V_DIM)
            o_ref[rs, pp * LANES:(pp + 1) * LANES] = _bf(acc)


def _mla_attention(qm, km, vm, bsz, seq):
    t = qm.shape[0]
    tq = MLA_TQ
    nq = seq // tq
    groups = MLA_HEADS // (2 * MLA_PAIRS_PER_STEP)
    qk_w = 2 * MLA_PAIRS_PER_STEP * LANES
    v_w = MLA_PAIRS_PER_STEP * LANES
    return pl.pallas_call(
        _mla_kernel,
        out_shape=jax.ShapeDtypeStruct((t, MLA_HEADS * MLA_V_DIM), BF16),
        grid=(bsz, groups, nq),
        in_specs=[
            pl.BlockSpec((tq, qk_w), lambda b, p, i: (b * nq + i, p)),
            pl.BlockSpec((seq, qk_w), lambda b, p, i: (b, p)),
            pl.BlockSpec((seq, v_w), lambda b, p, i: (b, p)),
        ],
        out_specs=pl.BlockSpec((tq, v_w), lambda b, p, i: (b * nq + i, p)),
        compiler_params=pltpu.CompilerParams(
            dimension_semantics=("arbitrary", "arbitrary", "arbitrary"),
            vmem_limit_bytes=V7X_VMEM_LIMIT),
        name="mla_attn",
    )(qm, km, vm)


def _merge_kernel(x_ref, yna_ref, ymla_ref, sgn_ref, sgm_ref, wpn_ref, wpm_ref, wout_ref,
                  gate1_ref, shift2_ref, scale2_ref, g2_ref, wr_ref,
                  x1_ref, h2_ref, lt_ref):
    d = x_ref.shape[1]
    for r0 in range(0, x_ref.shape[0], MERGE_SUB):
        rs = pl.ds(r0, MERGE_SUB)
        merged = (sgn_ref[rs, :].astype(F32) * _dot(yna_ref[rs, :], wpn_ref[...])
                  + sgm_ref[rs, :].astype(F32) * _dot(ymla_ref[rs, :], wpm_ref[...]))
        x1 = x_ref[rs, :] + gate1_ref[0] * _dot(_bf(merged), wout_ref[...])
        x1_ref[rs, :] = x1
        h2 = _rms(x1, d) * g2_ref[...]
        h2 = h2 * (1.0 + scale2_ref[0]) + shift2_ref[0]
        h2_ref[rs, :] = _bf(h2)
        lt_ref[:, rs] = _dot3_nt(wr_ref[...], h2)


def _merge(x2, yna, ymla, sgn, sgm, wpn, wpm, wout, mod3, g2, wr, seq):
    t, d = x2.shape
    tm = MERGE_TM
    per_b = seq // tm
    n_exp = wr.shape[0]

    def full(a):
        return pl.BlockSpec(a.shape, lambda i: (0,) * a.ndim)

    def rows(w):
        return pl.BlockSpec((tm, w), lambda i: (i, 0))

    def modblk(j):
        return pl.BlockSpec((1, 1, d), lambda i: (i // per_b, 0, j))

    return pl.pallas_call(
        _merge_kernel,
        out_shape=(jax.ShapeDtypeStruct((t, d), F32), jax.ShapeDtypeStruct((t, d), BF16),
                   jax.ShapeDtypeStruct((n_exp, t), F32)),
        grid=(t // tm,),
        in_specs=[rows(d), rows(yna.shape[1]), rows(ymla.shape[1]), rows(d), rows(d),
                  full(wpn), full(wpm), full(wout),
                  modblk(2), modblk(3), modblk(4), full(g2), full(wr)],
        out_specs=(rows(d), rows(d), pl.BlockSpec((n_exp, tm), lambda i: (0, i))),
        compiler_params=pltpu.CompilerParams(dimension_semantics=("arbitrary",),
                                             vmem_limit_bytes=V7X_VMEM_LIMIT),
        name="merge",
    )(x2, yna, ymla, sgn, sgm, wpn, wpm, wout, mod3, mod3, mod3, g2, wr)


def _route_kernel(lt_ref, eb_ref, o_ref, q_ref):
    n_exp, tn = lt_ref.shape
    per_g = n_exp // N_GROUPS
    neg_inf = -jnp.inf
    sc = _sigmoid(lt_ref[...])
    sel = sc + eb_ref[...]
    sc3 = sc.reshape(N_GROUPS, per_g, tn)
    g3 = sel.reshape(N_GROUPS, per_g, tn)
    io = lax.broadcasted_iota(jnp.int32, (N_GROUPS, per_g, tn), 1)
    gio = lax.broadcasted_iota(jnp.int32, (N_GROUPS, per_g, tn), 0)
    eio = gio * per_g + io

    m1 = jnp.max(g3, axis=1, keepdims=True)
    i1 = jnp.min(jnp.where(g3 == m1, io, per_g), axis=1, keepdims=True)
    m2 = jnp.max(jnp.where(io == i1, neg_inf, g3), axis=1, keepdims=True)
    gs = m1 + m2

    g1io = lax.broadcasted_iota(jnp.int32, (N_GROUPS, 1, tn), 0)
    gsel = jnp.zeros((N_GROUPS, 1, tn), F32)
    cur = gs
    for _ in range(TOPK_GROUPS):
        m = jnp.max(cur, axis=0, keepdims=True)
        i = jnp.min(jnp.where(cur == m, g1io, N_GROUPS), axis=0, keepdims=True)
        pick = g1io == i
        gsel = jnp.where(pick, 1.0, gsel)
        cur = jnp.where(pick, neg_inf, cur)

    cur = jnp.where(gsel > 0.0, g3, neg_inf)
    chosen = jnp.zeros((N_GROUPS, per_g, tn), F32)
    for _ in range(TOP_K):
        m = jnp.max(jnp.max(cur, axis=1, keepdims=True), axis=0, keepdims=True)
        cand = jnp.where(cur == m, eio, n_exp)
        i = jnp.min(jnp.min(cand, axis=1, keepdims=True), axis=0, keepdims=True)
        pick = eio == i
        chosen = jnp.where(pick, 1.0, chosen)
        cur = jnp.where(pick, neg_inf, cur)

    w = jnp.where(chosen > 0.0, sc3, 0.0)
    tot = jnp.sum(jnp.sum(w, axis=1, keepdims=True), axis=0, keepdims=True)
    gates = (w / tot * ROUTED_SCALE).reshape(n_exp, tn)
    o_ref[...] = gates
    routed = jnp.where(gates > 0.0, 1.0, 0.0).astype(BF16)
    ones = jnp.ones((8, MOE_TB), BF16)
    for j in range(tn // MOE_TB):
        n_row = _dot_nt(ones, routed[:, j * MOE_TB:(j + 1) * MOE_TB])[0:1]
        q_ref[j] = jnp.floor((n_row + (MOE_CHUNK - 1)) * (1.0 / MOE_CHUNK)).astype(jnp.int32)


def _route(lt, e_bias):
    n_exp, t = lt.shape
    tn = ROUTE_TN
    bps = tn // MOE_TB
    return pl.pallas_call(
        _route_kernel,
        out_shape=(jax.ShapeDtypeStruct((n_exp, t), F32),
                   jax.ShapeDtypeStruct((t // MOE_TB, 1, n_exp), jnp.int32)),
        grid=(t // tn,),
        in_specs=[pl.BlockSpec((n_exp, tn), lambda i: (0, i)),
                  pl.BlockSpec((n_exp, 1), lambda i: (0, 0))],
        out_specs=(pl.BlockSpec((n_exp, tn), lambda i: (0, i)),
                   pl.BlockSpec((bps, 1, n_exp), lambda i: (i, 0, 0))),
        compiler_params=pltpu.CompilerParams(dimension_semantics=("arbitrary",)),
        name="route",
    )(lt, e_bias.reshape(n_exp, 1))


def _for_each_chunk(n, fn):
    shift = MOE_COPY_UNROLL.bit_length() - 1

    def group(j, carry):
        for u in range(MOE_COPY_UNROLL):
            fn(j * MOE_COPY_UNROLL + u)
        return carry
    lax.fori_loop(0, n >> shift, group, 0)
    base = (n >> shift) << shift
    for u in range(MOE_COPY_UNROLL - 1):
        @pl.when(base + u < n)
        def _():
            fn(base + u)


def _dispatch_kernel(src2_s, dst2_s, n2_s, src1_s, dst1_s, n1_s, nch_s, total_s,
                     gt_ref, qrow_ref, qbrow_ref, h_ref, xs_ref, stage_ref, zero_ref, sem):
    b = pl.program_id(0)
    nb = pl.num_programs(0)
    slot = lax.rem(b, 2)
    n_exp, tb = gt_ref.shape
    rmax = stage_ref.shape[1] * MOE_CHUNK
    cpg = MOE_ROWGROUP // MOE_CHUNK

    routed = gt_ref[...] > 0.0
    before = (lax.broadcasted_iota(jnp.int32, (tb, tb), 0)
              < lax.broadcasted_iota(jnp.int32, (tb, tb), 1))
    pos = _dot(jnp.where(routed, 1.0, 0.0).astype(BF16), jnp.where(before, 1.0, 0.0).astype(BF16))
    posm = _bf(jnp.where(routed, pos, -1.0))
    qrow = qrow_ref[0]
    qbrow = qbrow_ref[0]
    qbrow_f = qbrow.astype(F32)
    h = h_ref[...]

    def sort_rows(g):
        r0 = g * MOE_ROWGROUP
        chunk = (lax.broadcasted_iota(jnp.int32, (MOE_ROWGROUP, n_exp), 0) + r0) >> MOE_CHUNK_SHIFT
        own = jnp.where(chunk >= qbrow, jnp.where(chunk < qbrow + qrow, 1.0, 0.0), 0.0)
        rank = _dot(_bf(own), posm)
        start = jnp.sum(own * qbrow_f, axis=-1, keepdims=True) * MOE_CHUNK
        rel = (lax.broadcasted_iota(jnp.int32, (MOE_ROWGROUP, 1), 0) + r0).astype(F32) - start
        onehot = jnp.where(rank == rel, 1.0, 0.0).astype(BF16)
        rows = _bf(_dot(onehot, h))
        stage_ref[slot, g * cpg:(g + 1) * cpg] = rows.reshape(cpg, MOE_CHUNK, rows.shape[1])

    n_groups = rmax // MOE_ROWGROUP
    n_sure = min(n_groups, (MOE_TB * TOP_K + MOE_ROWGROUP - 1) // MOE_ROWGROUP)
    for g in range(n_sure):
        sort_rows(g)
    for g in range(n_sure, n_groups):
        @pl.when(g * MOE_ROWGROUP < nch_s[b] * MOE_CHUNK)
        def _():
            sort_rows(g)

    def copy2(sl, src, dst):
        return pltpu.make_async_copy(stage_ref.at[sl, pl.ds(src, 2)], xs_ref.at[pl.ds(dst, 2)],
                                     sem.at[sl, 0])

    def copy1(sl, src_ref, dst):
        return pltpu.make_async_copy(src_ref, xs_ref.at[dst], sem.at[sl, 1])

    _for_each_chunk(n2_s[b], lambda i: copy2(slot, src2_s[b, i], dst2_s[b, i]).start())
    _for_each_chunk(n1_s[b], lambda i: copy1(slot, stage_ref.at[slot, src1_s[b, i]],
                                             dst1_s[b, i]).start())

    def wait_copies(sl, n2, n1):
        _for_each_chunk(n2, lambda i: copy2(sl, 0, 0).wait())
        _for_each_chunk(n1, lambda i: copy1(sl, zero_ref, 0).wait())

    @pl.when(b > 0)
    def _():
        wait_copies(1 - slot, n2_s[b - 1], n1_s[b - 1])

    @pl.when(b == nb - 1)
    def _():
        zero_ref[...] = jnp.zeros_like(zero_ref)
        n_tail = MOE_TM // MOE_CHUNK
        for c in range(n_tail):
            copy1(slot, zero_ref, total_s[0] + c).start()
        wait_copies(slot, n2_s[b], n1_s[b] + n_tail)


def _dispatch(gates_t, q, qbase, copies, nch, total, h2, n_chunks):
    n_exp, t = gates_t.shape
    d = h2.shape[1]
    nb = t // MOE_TB
    rmax = _moe_stage_rows(n_exp)
    grid_spec = pltpu.PrefetchScalarGridSpec(
        num_scalar_prefetch=8,
        grid=(nb,),
        in_specs=[
            pl.BlockSpec((n_exp, MOE_TB), lambda b, *_: (0, b)),
            pl.BlockSpec((1, 1, n_exp), lambda b, *_: (b, 0, 0)),
            pl.BlockSpec((1, 1, n_exp), lambda b, *_: (b, 0, 0)),
            pl.BlockSpec((MOE_TB, d), lambda b, *_: (b, 0)),
        ],
        out_specs=pl.BlockSpec(memory_space=pl.ANY),
        scratch_shapes=[pltpu.VMEM((2, rmax // MOE_CHUNK, MOE_CHUNK, d), BF16),
                        pltpu.VMEM((MOE_CHUNK, d), BF16), pltpu.SemaphoreType.DMA((2, 2))],
    )
    return pl.pallas_call(
        _dispatch_kernel,
        out_shape=jax.ShapeDtypeStruct((n_chunks, MOE_CHUNK, d), BF16),
        grid_spec=grid_spec,
        compiler_params=pltpu.CompilerParams(dimension_semantics=("arbitrary",),
                                             vmem_limit_bytes=V7X_VMEM_LIMIT),
        name="moe_dispatch",
    )(*copies, nch, total, gates_t, q.reshape(nb, 1, n_exp), qbase.reshape(nb, 1, n_exp), h2)


def _tile_pieces():
    cpt = MOE_TM // MOE_CHUNK
    return [1 << s for s in range(cpt.bit_length() - 1, -1, -1)]


def _expert_kernel(off_s, len_s, next_s, first_s, xs_ref, wg_ref, wu_ref, wd_ref, ys_ref,
                   xbuf, ybuf, wg_b, wu_b, wd_b, state, sem_in, sem_out):
    e = pl.program_id(0)
    n_exp = pl.num_programs(0)
    cpt = MOE_TM // MOE_CHUNK
    d = xbuf.shape[3]
    pieces = _tile_pieces()

    def tile_in(ee, tt, sl):
        return pltpu.make_async_copy(xs_ref.at[pl.ds(off_s[ee] + tt * cpt, cpt)], xbuf.at[sl],
                                     sem_in.at[sl])

    def for_each_piece(valid, fn):
        for k, piece in enumerate(pieces):
            @pl.when((valid & piece) != 0)
            def _():
                fn(k, piece, valid & ~(2 * piece - 1))

    def tile_out(sl, dst_chunk, k, piece, start):
        return pltpu.make_async_copy(ybuf.at[sl, pl.ds(start, piece)],
                                     ys_ref.at[pl.ds(dst_chunk + start, piece)], sem_out.at[sl, k])

    def drain(sl):
        for_each_piece(state[1 + sl], lambda k, piece, start: tile_out(sl, 0, k, piece, start).wait())
        state[1 + sl] = 0

    def following(ee, tt):
        safe = jnp.minimum(ee, n_exp - 1)
        more = tt + 1 < (len_s[safe] + cpt - 1) // cpt
        nxt_e = jnp.where(ee >= n_exp, n_exp, jnp.where(more, ee, next_s[safe]))
        return nxt_e, jnp.where(more, tt + 1, 0)

    def prefetch(ee, tt, xsl):
        @pl.when(ee < n_exp)
        def _():
            tile_in(ee, tt, xsl).start()

    @pl.when(e == 0)
    def _():
        state[0] = 0
        state[1] = 0
        state[2] = 0
        ahead = (first_s[0], 0)
        for k in range(MOE_XBUFS - 1):
            prefetch(ahead[0], ahead[1], k)
            ahead = following(*ahead)

    n_valid = len_s[e]
    n_tiles = (n_valid + cpt - 1) // cpt

    @pl.when(n_valid > 0)
    def _():
        wg_b[...] = _bf(wg_ref[0])
        wu_b[...] = _bf(wu_ref[0])
        wd_b[...] = _bf(wd_ref[0])

        def tile(t, carry):
            g = state[0]
            xsl = lax.rem(g, MOE_XBUFS)
            sl = g & 1
            tile_in(e, t, xsl).wait()
            ahead = (e, t)
            for _ in range(MOE_XBUFS - 1):
                ahead = following(*ahead)
            prefetch(ahead[0], ahead[1], lax.rem(g + MOE_XBUFS - 1, MOE_XBUFS))

            drain(sl)
            valid = jnp.minimum(n_valid - t * cpt, cpt)

            def ffn(n_c):
                x = xbuf[xsl, :n_c].reshape(n_c * MOE_CHUNK, d)
                a = _silu(_dot(x, wg_b[...])) * _dot(x, wu_b[...])
                ybuf[sl, :n_c] = _bf(_dot(_bf(a), wd_b[...])).reshape(n_c, MOE_CHUNK, d)

            sizes = [cpt >> k for k in range(MOE_TILE_ARMS)]
            for k, n_c in enumerate(sizes):
                covers_more = valid > (sizes[k + 1] if k + 1 < len(sizes) else 0)
                fits = valid <= n_c

                @pl.when(jnp.logical_and(covers_more, fits))
                def _():
                    ffn(n_c)

            dst = off_s[e] + t * cpt
            for_each_piece(valid, lambda k, piece, start: tile_out(sl, dst, k, piece, start).start())
            state[1 + sl] = valid
            state[0] = state[0] + 1
            return carry
        lax.fori_loop(0, n_tiles, tile, 0)

    @pl.when(e == n_exp - 1)
    def _():
        drain(0)
        drain(1)


def _experts(off, per_exp, nxt, first, xs, wg, wu, wd):
    n_chunks, _, d = xs.shape
    n_exp, _, ff = wg.shape
    cpt = MOE_TM // MOE_CHUNK

    def w_blk(e, *_):
        return (e, 0, 0)

    grid_spec = pltpu.PrefetchScalarGridSpec(
        num_scalar_prefetch=4,
        grid=(n_exp,),
        in_specs=[pl.BlockSpec(memory_space=pl.ANY),
                  pl.BlockSpec((1, d, ff), w_blk), pl.BlockSpec((1, d, ff), w_blk),
                  pl.BlockSpec((1, ff, d), w_blk)],
        out_specs=pl.BlockSpec(memory_space=pl.ANY),
        scratch_shapes=[pltpu.VMEM((MOE_XBUFS, cpt, MOE_CHUNK, d), BF16),
                        pltpu.VMEM((2, cpt, MOE_CHUNK, d), BF16),
                        pltpu.VMEM((d, ff), BF16), pltpu.VMEM((d, ff), BF16),
                        pltpu.VMEM((ff, d), BF16),
                        pltpu.SMEM((3,), jnp.int32),
                        pltpu.SemaphoreType.DMA((MOE_XBUFS,)),
                        pltpu.SemaphoreType.DMA((2, len(_tile_pieces())))],
    )
    return pl.pallas_call(
        _expert_kernel,
        out_shape=jax.ShapeDtypeStruct((n_chunks, MOE_CHUNK, d), BF16),
        grid_spec=grid_spec,
        compiler_params=pltpu.CompilerParams(dimension_semantics=("arbitrary",),
                                             vmem_limit_bytes=V7X_VMEM_LIMIT),
        name="moe_experts",
    )(off, per_exp, nxt, first, xs, wg, wu, wd)


def _combine_kernel(src2_s, dst2_s, n2_s, src1_s, dst1_s, n1_s, nch_s,
                    gt_ref, qcol_ref, qbcol_ref, h_ref, x1_ref, gate2_ref,
                    wsg_ref, wsu_ref, wsd_ref, ys_ref, o_ref, stage_ref, sem):
    b = pl.program_id(0)
    nb = pl.num_programs(0)
    slot = lax.rem(b, 2)
    n_exp, tb = gt_ref.shape
    rmax = stage_ref.shape[1] * MOE_CHUNK

    def copy2(sl, staged, sorted_):
        return pltpu.make_async_copy(ys_ref.at[pl.ds(sorted_, 2)],
                                     stage_ref.at[sl, pl.ds(staged, 2)], sem.at[sl, 0])

    def copy1(sl, staged, sorted_):
        return pltpu.make_async_copy(ys_ref.at[sorted_], stage_ref.at[sl, staged], sem.at[sl, 1])

    def fetch(bb, sl):
        _for_each_chunk(n2_s[bb], lambda i: copy2(sl, src2_s[bb, i], dst2_s[bb, i]).start())
        _for_each_chunk(n1_s[bb], lambda i: copy1(sl, src1_s[bb, i], dst1_s[bb, i]).start())

    @pl.when(b == 0)
    def _():
        stage_ref[...] = jnp.zeros_like(stage_ref)
        fetch(0, 0)

    _for_each_chunk(n2_s[b], lambda i: copy2(slot, 0, 0).wait())
    _for_each_chunk(n1_s[b], lambda i: copy1(slot, 0, 0).wait())

    @pl.when(b + 1 < nb)
    def _():
        fetch(b + 1, 1 - slot)

    gt = gt_ref[...]
    routed = jnp.where(gt > 0.0, 1.0, 0.0).astype(BF16)
    i0 = lax.broadcasted_iota(jnp.int32, (tb, tb), 0)
    i1 = lax.broadcasted_iota(jnp.int32, (tb, tb), 1)
    eye = jnp.where(i0 == i1, 1.0, 0.0).astype(BF16)
    routed_t = _dot_nt(eye, routed)
    gates_tok = _dot_nt(eye, _bf(gt))
    earlier = jnp.where(i1 < i0, 1.0, 0.0).astype(BF16)
    pos_t = _dot(earlier, _bf(routed_t))
    posm_t = _bf(jnp.where(routed_t > 0.0, pos_t, -1.0))

    qcol = qcol_ref[0]
    qbcol = qbcol_ref[0]
    h = h_ref[...]
    a = _silu(_dot(h, wsg_ref[...])) * _dot(h, wsu_ref[...])
    shared = _dot(_bf(a), wsd_ref[...])

    def finish(rows):
        chunk = lax.broadcasted_iota(jnp.int32, (n_exp, rows), 1) >> MOE_CHUNK_SHIFT
        own = jnp.where(chunk >= qbcol, jnp.where(chunk < qbcol + qcol, 1.0, 0.0), 0.0)
        own_b = _bf(own)
        rank = _dot(posm_t, own_b)
        wexp = _dot(_bf(gates_tok), own_b)
        start = jnp.sum(own * qbcol.astype(F32), axis=0, keepdims=True) * MOE_CHUNK
        rel = lax.broadcasted_iota(jnp.int32, (1, rows), 1).astype(F32) - start
        weights = _bf(jnp.where(rank == rel, wexp, 0.0))
        staged = stage_ref[slot, :rows // MOE_CHUNK].reshape(rows, o_ref.shape[1])
        o_ref[...] = x1_ref[...] + gate2_ref[0] * (_dot(weights, staged) + shared)

    short = rmax - MOE_ROWGROUP

    @pl.when(nch_s[b] * MOE_CHUNK <= short)
    def _():
        finish(short)

    @pl.when(nch_s[b] * MOE_CHUNK > short)
    def _():
        finish(rmax)


def _combine(gates_t, q, qbase, copies, nch, h2, x1, mod3, wsg, wsu, wsd, ys, seq):
    n_exp, t = gates_t.shape
    d = h2.shape[1]
    nb = t // MOE_TB
    per_b = seq // MOE_TB
    stage_chunks = _moe_stage_rows(n_exp) // MOE_CHUNK

    def full(a):
        return pl.BlockSpec(a.shape, lambda b, *_: (0,) * a.ndim)

    def rows(w):
        return pl.BlockSpec((MOE_TB, w), lambda b, *_: (b, 0))

    grid_spec = pltpu.PrefetchScalarGridSpec(
        num_scalar_prefetch=7,
        grid=(nb,),
        in_specs=[
            pl.BlockSpec((n_exp, MOE_TB), lambda b, *_: (0, b)),
            pl.BlockSpec((1, n_exp, 1), lambda b, *_: (b, 0, 0)),
            pl.BlockSpec((1, n_exp, 1), lambda b, *_: (b, 0, 0)),
            rows(d), rows(d),
            pl.BlockSpec((1, 1, d), lambda b, *_: (b // per_b, 0, 5)),
            full(wsg), full(wsu), full(wsd),
            pl.BlockSpec(memory_space=pl.ANY),
        ],
        out_specs=rows(d),
        scratch_shapes=[pltpu.VMEM((2, stage_chunks, MOE_CHUNK, d), BF16),
                        pltpu.SemaphoreType.DMA((2, 2))],
    )
    return pl.pallas_call(
        _combine_kernel,
        out_shape=jax.ShapeDtypeStruct((t, d), F32),
        grid_spec=grid_spec,
        compiler_params=pltpu.CompilerParams(dimension_semantics=("arbitrary",),
                                             vmem_limit_bytes=V7X_VMEM_LIMIT),
        name="moe_combine",
    )(*copies, nch, gates_t, q.reshape(nb, n_exp, 1), qbase.reshape(nb, n_exp, 1),
      h2, x1, mod3, wsg, wsu, wsd, ys)


def _moe_plan(q):
    nb, n_exp = q.shape
    qbase = jnp.cumsum(q, axis=1) - q
    nch = jnp.sum(q, axis=1)
    per_exp = jnp.sum(q, axis=0)
    off = jnp.cumsum(per_exp) - per_exp
    dstq = off[None, :] + jnp.cumsum(q, axis=0) - q
    def copy_list(count, length, stage_start, sorted_start, step):
        cum = jnp.cumsum(count, axis=1) - count
        i = jnp.arange(length, dtype=jnp.int32)
        ii = i[None, :, None]
        owned = (ii >= cum[:, None, :]) & (ii < (cum + count)[:, None, :])

        def place(start):
            return step * i[None, :] + jnp.sum(
                jnp.where(owned, (start - step * cum)[:, None, :], 0), axis=2)
        return place(stage_start), place(sorted_start), jnp.sum(count, axis=1)

    pairs = q >> 1
    max_chunks = _moe_block_chunks(n_exp)
    src2, dst2, n2 = copy_list(pairs, max_chunks // 2, qbase, dstq, 2)
    src1, dst1, n1 = copy_list(q & 1, n_exp, qbase + 2 * pairs, dstq + 2 * pairs, 1)
    copies = (src2, dst2, n2, src1, dst1, n1)
    ids = jnp.arange(n_exp, dtype=jnp.int32)
    later = (ids[None, :] > ids[:, None]) & (per_exp[None, :] > 0)
    nxt = jnp.min(jnp.where(later, ids[None, :], n_exp), axis=1)
    first = jnp.min(jnp.where(per_exp > 0, ids, n_exp)).reshape(1)
    total = jnp.sum(per_exp).reshape(1)
    return qbase, copies, nch, off, per_exp, nxt, first, total


def _moe_block_chunks(n_exp):
    return MOE_TB * TOP_K // MOE_CHUNK + n_exp


def _moe_stage_rows(n_exp):
    return -(-_moe_block_chunks(n_exp) * MOE_CHUNK // MOE_ROWGROUP) * MOE_ROWGROUP


def _moe(h2, gates_t, q3, x1, mod3, wsg, wsu, wsd, wg, wu, wd, seq):
    n_exp, t = gates_t.shape
    nb = t // MOE_TB
    q = q3.reshape(nb, n_exp)
    qbase, copies, nch, off, per_exp, nxt, first, total = _moe_plan(q)
    n_chunks = nb * _moe_block_chunks(n_exp) + MOE_TM // MOE_CHUNK
    xs = _dispatch(gates_t, q, qbase, copies, nch, total, h2, n_chunks)
    ys = _experts(off, per_exp, nxt, first, xs, wg, wu, wd)
    return _combine(gates_t, q, qbase, copies, nch, h2, x1, mod3, wsg, wsu, wsd, ys, seq)


def _pad_heads(w, heads, width):
    lead = w.shape[:-1]
    w = w.reshape(lead + (heads, width))
    w = jnp.pad(w, [(0, 0)] * len(lead) + [(0, 0), (0, LANES - width)])
    return w.reshape(lead + (heads * LANES,))


def kernel(x, c, positions, w_ada, b_ada, g_norm1, w_in, g_na_q, g_na_k, na_rpb, g_q_lat, w_uq,
           g_kv_lat, w_ukv, g_mla_q, g_mla_k, w_proj_na, w_proj_mla, w_out, g_norm2, w_router,
           e_bias, w_exp_gate, w_exp_up, w_exp_down, w_sh_gate, w_sh_up, w_sh_down):
    bsz, seq, d = x.shape
    t = bsz * seq
    depth = w_ada.shape[0]
    na_w = NA_HEADS * NA_HEAD_DIM
    q_rank = g_q_lat.shape[1]
    kv_rank = g_kv_lat.shape[1]
    n_rows = seq // GRID_W

    pos = positions.reshape(1, t)
    half = MLA_ROPE_DIM // 2
    freq = (ROPE_THETA ** (-jnp.arange(half, dtype=F32) / half)).reshape(half, 1)

    x2 = x.reshape(t, d)
    for l in range(depth):
        mod3 = _adaln(c, w_ada[l], b_ada[l]).reshape(bsz, 1, 6 * d)

        wi = w_in[l]
        o_lat = 3 * na_w
        o_rot = o_lat + q_rank + kv_rank
        o_gate = o_rot + MLA_ROPE_DIM
        wqkv = _bf(wi[:, :o_lat])
        w_rot = jnp.pad(wi[:, o_rot:o_gate], ((0, 0), (MLA_NOPE_DIM, LANES - MLA_QK_DIM)))
        wlat = _bf(jnp.concatenate([wi[:, o_lat:o_rot], w_rot], axis=1))
        wgate = _bf(wi[:, o_gate:])
        gq = (jnp.tile(g_na_q[l], NA_HEADS) * (NA_HEAD_DIM ** -0.5 * LOG2E)).reshape(1, na_w)
        gk = jnp.tile(g_na_k[l], NA_HEADS).reshape(1, na_w)
        wuq = _bf(_pad_heads(w_uq[l], MLA_HEADS, MLA_QK_DIM))
        wukv = w_ukv[l].reshape(kv_rank, MLA_HEADS, MLA_NOPE_DIM + MLA_V_DIM)
        wuk = _bf(_pad_heads(wukv[:, :, :MLA_NOPE_DIM].reshape(kv_rank, -1), MLA_HEADS, MLA_NOPE_DIM))
        wuv = _bf(wukv[:, :, MLA_NOPE_DIM:].reshape(kv_rank, MLA_HEADS * MLA_V_DIM))
        gmq = _pad_heads(jnp.tile(g_mla_q[l], MLA_HEADS) * (MLA_QK_DIM ** -0.5 * LOG2E),
                         MLA_HEADS, MLA_QK_DIM).reshape(1, -1)
        gmk = _pad_heads(jnp.tile(g_mla_k[l], MLA_HEADS), MLA_HEADS, MLA_QK_DIM).reshape(1, -1)

        qa, ka, va, qm, km, vm, sgn, sgm = _inproj(
            x2, mod3, g_norm1[l].reshape(1, d), wqkv, wlat, wgate, gq, gk,
            g_q_lat[l].reshape(1, q_rank), g_kv_lat[l].reshape(1, kv_rank), wuq, wuk, wuv,
            gmq, gmk, pos, freq, seq)

        bias = _na_bias(na_rpb[l], n_rows)
        y_na = _na_attention(qa, ka, va, bias, bsz, seq)
        y_mla = _mla_attention(qm, km, vm, bsz, seq)

        x1, h2, lt = _merge(x2, y_na, y_mla, sgn, sgm, _bf(w_proj_na[l]), _bf(w_proj_mla[l]),
                            _bf(w_out[l]), mod3, g_norm2[l].reshape(1, d), w_router[l].T, seq)
        gates_t, q3 = _route(lt, e_bias[l])
        x2 = _moe(h2, gates_t, q3, x1, mod3, _bf(w_sh_gate[l]), _bf(w_sh_up[l]),
                  _bf(w_sh_down[l]), w_exp_gate[l], w_exp_up[l], w_exp_down[l], seq)
    return x2.reshape(bsz, seq, d)
```

```python
import functools

import jax
import jax.numpy as jnp
from jax import lax
from jax.experimental import pallas as pl
from jax.experimental.pallas import tpu as pltpu

GRID_W = 64
NA_HEADS = 8
NA_HEAD_DIM = 64
NA_WIN_ROWS = 8
NA_WIN_COLS = 16
MLA_HEADS = 8
MLA_NOPE_DIM = 64
MLA_ROPE_DIM = 32
MLA_V_DIM = 64
MLA_QK_DIM = MLA_NOPE_DIM + MLA_ROPE_DIM
ROPE_THETA = 10000.0
N_GROUPS = 8
TOPK_GROUPS = 4
TOP_K = 8
ROUTED_SCALE = 2.5
EPS = 1e-6
NEG_BIG = -1e30

LANES = 128
V7X_VMEM_LIMIT = 56 * 1024 * 1024

NA_QROWS = 4
NA_BAND = 12
NA_BLOCK_TYPES = 3
NA_BLOCKS_PER_STEP = 4
MERGE_TM = 1024
MERGE_SUB = 1024
INPROJ_TM = 1024
INPROJ_SUB = 256
MLA_TQ = 1024
MLA_SUB = 512
MLA_PAIRS_PER_STEP = 4
LOG2E = 1.4426950408889634
MOE_TB = 256
ROUTE_TN = 2048
MOE_CHUNK_SHIFT = 4
MOE_CHUNK = 1 << MOE_CHUNK_SHIFT
MOE_TM = 1024
MOE_TILE_ARMS = 2
MOE_ROWGROUP = 512
MOE_COPY_UNROLL = 4
MOE_XBUFS = 3

F32 = jnp.float32
BF16 = jnp.bfloat16


def _bf(x):
    return x.astype(BF16)


def _dot(a, b):
    return jnp.dot(a, b, preferred_element_type=F32)


def _dot_nt(a, b):
    return lax.dot_general(a, b, (((1,), (1,)), ((), ())), preferred_element_type=F32)


def _split(x):
    hi = _bf(x)
    lo = _bf(x - hi.astype(F32))
    return hi, lo


def _dot3(a, b):
    ah, al = _split(a)
    bh, bl = _split(b)
    return _dot(ah, bh) + (_dot(ah, bl) + _dot(al, bh))


def _dot3_nt(a, b):
    m = a.shape[0]
    ah, al = _split(a)
    bh, bl = _split(b)
    both = _dot_nt(jnp.concatenate([ah, al], axis=0), bh)
    return both[:m] + (_dot_nt(ah, bl) + both[m:])


def _sigmoid(x):
    return 1.0 / (1.0 + jnp.exp(-x))


def _silu(x):
    return x * _sigmoid(x)


def _rms(x, n):
    ss = jnp.sum(x * x, axis=-1, keepdims=True)
    return x * lax.rsqrt(ss * (1.0 / n) + EPS)


def _adaln_kernel(c_ref, w_ref, b_ref, o_ref):
    c = c_ref[...]
    o_ref[...] = _dot3(_silu(c), w_ref[...]) + b_ref[...]


def _adaln(c, w, b):
    bsz, d = c.shape
    n = w.shape[1]
    tn = 1024
    return pl.pallas_call(
        _adaln_kernel,
        out_shape=jax.ShapeDtypeStruct((bsz, n), F32),
        grid=(n // tn,),
        in_specs=[
            pl.BlockSpec((bsz, d), lambda j: (0, 0)),
            pl.BlockSpec((d, tn), lambda j: (0, j)),
            pl.BlockSpec((1, tn), lambda j: (0, j)),
        ],
        out_specs=pl.BlockSpec((bsz, tn), lambda j: (0, j)),
        compiler_params=pltpu.CompilerParams(dimension_semantics=("arbitrary",)),
        name="adaln",
    )(c, w, b.reshape(1, n))


def _inproj_kernel(x_ref, shift_ref, scale_ref, g1_ref, wqkv_ref, wlat_ref, wgate_ref,
                   gq_ref, gk_ref, gql_ref, gkvl_ref, wuq_ref, wuk_ref, wuv_ref,
                   gmq_ref, gmk_ref, pos_ref, freq_ref,
                   qa_ref, ka_ref, va_ref, qm_ref, km_ref, vm_ref, sgn_ref, sgm_ref):
    for r0 in range(0, x_ref.shape[0], INPROJ_SUB):
        _inproj_rows(pl.ds(r0, INPROJ_SUB), x_ref, shift_ref, scale_ref, g1_ref, wqkv_ref,
                     wlat_ref, wgate_ref, gq_ref, gk_ref, gql_ref, gkvl_ref, wuq_ref, wuk_ref,
                     wuv_ref, gmq_ref, gmk_ref, pos_ref, freq_ref, qa_ref, ka_ref, va_ref,
                     qm_ref, km_ref, vm_ref, sgn_ref, sgm_ref)


def _inproj_rows(rs, x_ref, shift_ref, scale_ref, g1_ref, wqkv_ref, wlat_ref, wgate_ref,
                 gq_ref, gk_ref, gql_ref, gkvl_ref, wuq_ref, wuk_ref, wuv_ref,
                 gmq_ref, gmk_ref, pos_ref, freq_ref,
                 qa_ref, ka_ref, va_ref, qm_ref, km_ref, vm_ref, sgn_ref, sgm_ref):
    d = x_ref.shape[1]
    x = x_ref[rs, :]
    h = _rms(x, d) * g1_ref[...]
    h = h * (1.0 + scale_ref[0]) + shift_ref[0]
    hb = _bf(h)

    qkv = _dot(hb, wqkv_ref[...])
    lat = _dot(hb, wlat_ref[...])
    gts = _dot(hb, wgate_ref[...])
    sgn_ref[rs, :] = _bf(_sigmoid(gts[:, :d]))
    sgm_ref[rs, :] = _bf(_sigmoid(gts[:, d:]))

    na_w = NA_HEADS * NA_HEAD_DIM
    lane = lax.broadcasted_iota(jnp.int32, (1, LANES), 1)
    lo_half = lane < NA_HEAD_DIM
    for p in range(na_w // LANES):
        sl = slice(p * LANES, (p + 1) * LANES)
        for src_off, g_ref, dst_ref in ((0, gq_ref, qa_ref), (na_w, gk_ref, ka_ref)):
            t = qkv[:, src_off + p * LANES: src_off + (p + 1) * LANES]
            sq = t * t
            s_lo = jnp.sum(jnp.where(lo_half, sq, 0.0), axis=-1, keepdims=True)
            s_hi = jnp.sum(jnp.where(lo_half, 0.0, sq), axis=-1, keepdims=True)
            r = jnp.where(lo_half,
                          lax.rsqrt(s_lo * (1.0 / NA_HEAD_DIM) + EPS),
                          lax.rsqrt(s_hi * (1.0 / NA_HEAD_DIM) + EPS))
            dst_ref[rs, sl] = _bf(t * r * g_ref[:, sl])
    va_ref[rs, :] = _bf(qkv[:, 2 * na_w: 3 * na_w])

    q_rank = gql_ref.shape[1]
    kv_rank = gkvl_ref.shape[1]
    qln = _rms(lat[:, :q_rank], q_rank) * gql_ref[...]
    kvn = _bf(_rms(lat[:, q_rank:q_rank + kv_rank], kv_rank) * gkvl_ref[...])
    qpre = _dot(_bf(qln), wuq_ref[...])
    knope = _dot(kvn, wuk_ref[...])
    vm_ref[rs, :] = _bf(_dot(kvn, wuv_ref[...]))
    krot = lat[:, q_rank + kv_rank:]

    tm = INPROJ_SUB
    half = MLA_ROPE_DIM // 2
    ang_t = freq_ref[...] * pos_ref[:, rs].astype(F32)
    cos_t = jnp.cos(ang_t)
    sin_t = jnp.sin(ang_t)
    l_i = lax.broadcasted_iota(jnp.int32, (LANES, half), 0)
    j_i = lax.broadcasted_iota(jnp.int32, (LANES, half), 1)
    hit = jnp.where((l_i >= MLA_NOPE_DIM) & (l_i < MLA_QK_DIM)
                    & (((l_i - MLA_NOPE_DIM) & (half - 1)) == j_i), 1.0, 0.0)
    first_half = l_i < MLA_NOPE_DIM + half
    eye = jnp.where(lax.broadcasted_iota(jnp.int32, (tm, tm), 0)
                    == lax.broadcasted_iota(jnp.int32, (tm, tm), 1), 1.0, 0.0).astype(BF16)

    def table(sel, vals, fill_nope):
        hi, lo = _split(vals)
        w = _dot(_bf(sel), hi) + _dot(_bf(sel), lo)
        if fill_nope:
            w = jnp.where(lax.broadcasted_iota(jnp.int32, (LANES, tm), 0) < MLA_NOPE_DIM, 1.0, w)
        hi, lo = _split(w)
        return _dot_nt(eye, hi) + _dot_nt(eye, lo)

    c_tab = table(hit, cos_t, True)
    s_up = table(jnp.where(first_half, 0.0, hit), sin_t, False)
    s_dn = table(jnp.where(first_half, -hit, 0.0), sin_t, False)

    def rope(t):
        return t * c_tab + pltpu.roll(t, half, 1) * s_up + pltpu.roll(t, LANES - half, 1) * s_dn

    kr = rope(krot)
    for hd in range(MLA_HEADS):
        sl = slice(hd * LANES, (hd + 1) * LANES)
        qh = rope(qpre[:, sl])
        qm_ref[rs, sl] = _bf(_rms(qh, MLA_QK_DIM) * gmq_ref[:, sl])
        kh = knope[:, sl] + kr
        km_ref[rs, sl] = _bf(_rms(kh, MLA_QK_DIM) * gmk_ref[:, sl])


def _inproj(x2, mod3, g1, wqkv, wlat, wgate, gq, gk, gql, gkvl, wuq, wuk, wuv, gmq, gmk,
            pos, freq, seq):
    t, d = x2.shape
    tm = INPROJ_TM
    per_b = seq // tm
    na_w = NA_HEADS * NA_HEAD_DIM
    mla_w = MLA_HEADS * LANES
    v_w = MLA_HEADS * MLA_V_DIM

    def full(a):
        return pl.BlockSpec(a.shape, lambda i: (0,) * a.ndim, pipeline_mode=pl.Buffered(1))

    def rows(w):
        return pl.BlockSpec((tm, w), lambda i: (i, 0))

    out_shapes = (
        jax.ShapeDtypeStruct((t, na_w), BF16), jax.ShapeDtypeStruct((t, na_w), BF16),
        jax.ShapeDtypeStruct((t, na_w), BF16),
        jax.ShapeDtypeStruct((t, mla_w), BF16), jax.ShapeDtypeStruct((t, mla_w), BF16),
        jax.ShapeDtypeStruct((t, v_w), BF16),
        jax.ShapeDtypeStruct((t, d), BF16), jax.ShapeDtypeStruct((t, d), BF16),
    )
    return pl.pallas_call(
        _inproj_kernel,
        out_shape=out_shapes,
        grid=(t // tm,),
        in_specs=[
            rows(d),
            pl.BlockSpec((1, 1, d), lambda i: (i // per_b, 0, 0)),
            pl.BlockSpec((1, 1, d), lambda i: (i // per_b, 0, 1)),
            full(g1), full(wqkv), full(wlat), full(wgate), full(gq), full(gk), full(gql),
            full(gkvl), full(wuq), full(wuk), full(wuv), full(gmq), full(gmk),
            pl.BlockSpec((1, tm), lambda i: (0, i)),
            full(freq),
        ],
        out_specs=(rows(na_w), rows(na_w), rows(na_w), rows(mla_w), rows(mla_w), rows(v_w),
                   rows(d), rows(d)),
        compiler_params=pltpu.CompilerParams(dimension_semantics=("arbitrary",),
                                             vmem_limit_bytes=V7X_VMEM_LIMIT),
        name="inproj",
    )(x2, mod3, mod3, g1, wqkv, wlat, wgate, gq, gk, gql, gkvl, wuq, wuk, wuv, gmq, gmk,
      pos, freq)


def _na_block_geometry(block_type, n_rows):
    if block_type == 0:
        return 0, 0
    if block_type == 1:
        r0 = NA_QROWS
        return r0, r0 - NA_WIN_ROWS // 2
    return n_rows - NA_QROWS, n_rows - NA_BAND


def _na_bias_kernel(rpb_ref, o_ref, m_ref, *, n_rows):
    hd = pl.program_id(0)
    n_dr = 2 * NA_WIN_ROWS - 1
    n_dc = 2 * NA_WIN_COLS - 1
    qc = lax.broadcasted_iota(jnp.int32, (GRID_W, LANES), 0)
    kc = lax.broadcasted_iota(jnp.int32, (GRID_W, LANES), 1) & (GRID_W - 1)
    dc = jnp.clip(kc - qc, -(NA_WIN_COLS - 1), NA_WIN_COLS - 1) + (NA_WIN_COLS - 1)
    cstart = jnp.clip(qc - NA_WIN_COLS // 2, 0, GRID_W - NA_WIN_COLS)
    col_ok = (kc >= cstart) & (kc < cstart + NA_WIN_COLS)
    for i_dr in range(n_dr):
        acc = jnp.zeros((GRID_W, LANES), F32)
        for t in range(n_dc):
            acc = jnp.where(dc == t, rpb_ref[hd, i_dr * n_dc + t], acc)
        m_ref[i_dr] = jnp.where(col_ok, acc * LOG2E, NEG_BIG)
    neg = jnp.full((GRID_W, LANES), NEG_BIG, F32)
    lo_half = lax.broadcasted_iota(jnp.int32, (GRID_W, LANES), 1) < GRID_W
    kh = NA_WIN_ROWS
    for bt in range(NA_BLOCK_TYPES):
        r0, start = _na_block_geometry(bt, n_rows)
        for i in range(NA_QROWS):
            r = r0 + i
            rs = min(max(r - kh // 2, 0), n_rows - kh)
            for jp in range(NA_BAND // 2):
                halves = []
                for j in (2 * jp, 2 * jp + 1):
                    krow = start + j
                    if rs <= krow < rs + kh:
                        halves.append(m_ref[krow - r + (NA_WIN_ROWS - 1)])
                    else:
                        halves.append(neg)
                tile = jnp.where(lo_half, halves[0], halves[1])
                o_ref[bt, 0, i * GRID_W:(i + 1) * GRID_W, jp * LANES:(jp + 1) * LANES] = tile


def _na_bias(rpb, n_rows):
    heads = rpb.shape[0]
    nq = NA_QROWS * GRID_W
    nk = NA_BAND * GRID_W
    rpb2 = rpb.reshape(heads, -1)
    return pl.pallas_call(
        functools.partial(_na_bias_kernel, n_rows=n_rows),
        out_shape=jax.ShapeDtypeStruct((NA_BLOCK_TYPES, heads, nq, nk), F32),
        grid=(heads,),
        in_specs=[pl.BlockSpec(memory_space=pltpu.SMEM)],
        out_specs=pl.BlockSpec((NA_BLOCK_TYPES, 1, nq, nk), lambda hd: (0, hd, 0, 0)),
        scratch_shapes=[pltpu.VMEM((2 * NA_WIN_ROWS - 1, GRID_W, LANES), F32)],
        compiler_params=pltpu.CompilerParams(dimension_semantics=("arbitrary",)),
        name="na_bias",
    )(rpb2)


def _softmax_pv(s, v_pair, hh, half):
    lane = lax.broadcasted_iota(jnp.int32, (1, LANES), 1)
    mine = (lane < half) if hh == 0 else (lane >= half)
    den_lane = half if hh == 0 else 0
    m = jnp.max(s, axis=-1, keepdims=True)
    p = _bf(jnp.exp2(s - m))
    ones_row = jnp.where(lane == den_lane, 1.0, 0.0).astype(BF16)
    o = _dot(p, jnp.where(mine, v_pair, ones_row))
    den = jnp.sum(jnp.where(lane == den_lane, o, 0.0), axis=-1, keepdims=True)
    return jnp.where(mine, o / den, 0.0)


def _na_kernel(q_ref, k_ref, v_ref, bias_ref, o_ref, *, n_blocks, n_rows):
    nk = NA_BAND * GRID_W
    tq = NA_QROWS * GRID_W
    lo_half = lax.broadcasted_iota(jnp.int32, (1, LANES), 1) < NA_HEAD_DIM
    for u in range(NA_BLOCKS_PER_STEP):
        blk = pl.program_id(1) * NA_BLOCKS_PER_STEP + u
        btype = jnp.where(blk == 0, 0, jnp.where(blk == n_blocks - 1, 2, 1))
        start_row = jnp.where(blk == 0, 0,
                              jnp.where(blk == n_blocks - 1, n_rows - NA_BAND,
                                        blk * NA_QROWS - NA_WIN_ROWS // 2))
        off = pl.multiple_of(start_row * GRID_W, GRID_W)
        rs = pl.ds(u * tq, tq)
        for p in range(NA_HEADS * NA_HEAD_DIM // LANES):
            sl = slice(p * LANES, (p + 1) * LANES)
            qp = q_ref[rs, sl]
            kb = k_ref[pl.ds(off, nk), sl]
            vb = v_ref[pl.ds(off, nk), sl]
            zero = jnp.zeros_like(qp)
            q2 = jnp.concatenate([jnp.where(lo_half, qp, zero), jnp.where(lo_half, zero, qp)],
                                 axis=0)
            s = _dot_nt(q2, kb) + bias_ref[btype, 2 * p:2 * p + 2].reshape(2 * tq, nk)
            m = jnp.max(s, axis=-1, keepdims=True)
            e = jnp.exp2(s - m)
            den = jnp.sum(e, axis=-1, keepdims=True)
            o = _dot(_bf(e), vb) / den
            o_ref[rs, sl] = _bf(jnp.where(lo_half, o[:tq], o[tq:]))


def _na_attention(qa, ka, va, bias, bsz, seq):
    t, w = qa.shape
    n_rows = seq // GRID_W
    n_blocks = n_rows // NA_QROWS
    tq = NA_QROWS * GRID_W
    nk = NA_BAND * GRID_W
    heads = bias.shape[1]

    bps = NA_BLOCKS_PER_STEP
    steps = n_blocks // bps
    return pl.pallas_call(
        functools.partial(_na_kernel, n_blocks=n_blocks, n_rows=n_rows),
        out_shape=jax.ShapeDtypeStruct((t, w), BF16),
        grid=(bsz, steps),
        in_specs=[
            pl.BlockSpec((bps * tq, w), lambda b, j: (b * steps + j, 0)),
            pl.BlockSpec((seq, w), lambda b, j: (b, 0)),
            pl.BlockSpec((seq, w), lambda b, j: (b, 0)),
            pl.BlockSpec(bias.shape, lambda b, j: (0, 0, 0, 0), pipeline_mode=pl.Buffered(1)),
        ],
        out_specs=pl.BlockSpec((bps * tq, w), lambda b, j: (b * steps + j, 0)),
        compiler_params=pltpu.CompilerParams(dimension_semantics=("arbitrary", "arbitrary"),
                                             vmem_limit_bytes=V7X_VMEM_LIMIT),
        name="na_attn",
    )(qa, ka, va, bias)


def _mla_kernel(q_ref, k_ref, v_ref, o_ref):
    for r0 in range(0, q_ref.shape[0], MLA_SUB):
        rs = pl.ds(r0, MLA_SUB)
        for pp in range(MLA_PAIRS_PER_STEP):
            v_pair = v_ref[:, pp * LANES:(pp + 1) * LANES]
            acc = jnp.zeros((MLA_SUB, LANES), F32)
            for hh in range(2):
                sl = slice((2 * pp + hh) * LANES, (2 * pp + hh + 1) * LANES)
                s = _dot_nt(q_ref[rs, sl], k_ref[:, sl])
                acc = acc + _softmax_pv(s, v_pair, hh, MLA_V_DIM)
            o_ref[rs, pp * LANES:(pp + 1) * LANES] = _bf(acc)


def _mla_attention(qm, km, vm, bsz, seq):
    t = qm.shape[0]
    tq = MLA_TQ
    nq = seq // tq
    groups = MLA_HEADS // (2 * MLA_PAIRS_PER_STEP)
    qk_w = 2 * MLA_PAIRS_PER_STEP * LANES
    v_w = MLA_PAIRS_PER_STEP * LANES
    return pl.pallas_call(
        _mla_kernel,
        out_shape=jax.ShapeDtypeStruct((t, MLA_HEADS * MLA_V_DIM), BF16),
        grid=(bsz, groups, nq),
        in_specs=[
            pl.BlockSpec((tq, qk_w), lambda b, p, i: (b * nq + i, p)),
            pl.BlockSpec((seq, qk_w), lambda b, p, i: (b, p)),
            pl.BlockSpec((seq, v_w), lambda b, p, i: (b, p)),
        ],
        out_specs=pl.BlockSpec((tq, v_w), lambda b, p, i: (b * nq + i, p)),
        compiler_params=pltpu.CompilerParams(
            dimension_semantics=("arbitrary", "arbitrary", "arbitrary"),
            vmem_limit_bytes=V7X_VMEM_LIMIT),
        name="mla_attn",
    )(qm, km, vm)


def _merge_kernel(x_ref, yna_ref, ymla_ref, sgn_ref, sgm_ref, wpn_ref, wpm_ref, wout_ref,
                  gate1_ref, shift2_ref, scale2_ref, g2_ref, wr_ref,
                  x1_ref, h2_ref, lt_ref):
    d = x_ref.shape[1]
    for r0 in range(0, x_ref.shape[0], MERGE_SUB):
        rs = pl.ds(r0, MERGE_SUB)
        merged = (sgn_ref[rs, :].astype(F32) * _dot(yna_ref[rs, :], wpn_ref[...])
                  + sgm_ref[rs, :].astype(F32) * _dot(ymla_ref[rs, :], wpm_ref[...]))
        x1 = x_ref[rs, :] + gate1_ref[0] * _dot(_bf(merged), wout_ref[...])
        x1_ref[rs, :] = x1
        h2 = _rms(x1, d) * g2_ref[...]
        h2 = h2 * (1.0 + scale2_ref[0]) + shift2_ref[0]
        h2_ref[rs, :] = _bf(h2)
        lt_ref[:, rs] = _dot3_nt(wr_ref[...], h2)


def _merge(x2, yna, ymla, sgn, sgm, wpn, wpm, wout, mod3, g2, wr, seq):
    t, d = x2.shape
    tm = MERGE_TM
    per_b = seq // tm
    n_exp = wr.shape[0]

    def full(a):
        return pl.BlockSpec(a.shape, lambda i: (0,) * a.ndim)

    def rows(w):
        return pl.BlockSpec((tm, w), lambda i: (i, 0))

    def modblk(j):
        return pl.BlockSpec((1, 1, d), lambda i: (i // per_b, 0, j))

    return pl.pallas_call(
        _merge_kernel,
        out_shape=(jax.ShapeDtypeStruct((t, d), F32), jax.ShapeDtypeStruct((t, d), BF16),
                   jax.ShapeDtypeStruct((n_exp, t), F32)),
        grid=(t // tm,),
        in_specs=[rows(d), rows(yna.shape[1]), rows(ymla.shape[1]), rows(d), rows(d),
                  full(wpn), full(wpm), full(wout),
                  modblk(2), modblk(3), modblk(4), full(g2), full(wr)],
        out_specs=(rows(d), rows(d), pl.BlockSpec((n_exp, tm), lambda i: (0, i))),
        compiler_params=pltpu.CompilerParams(dimension_semantics=("arbitrary",),
                                             vmem_limit_bytes=V7X_VMEM_LIMIT),
        name="merge",
    )(x2, yna, ymla, sgn, sgm, wpn, wpm, wout, mod3, mod3, mod3, g2, wr)


def _route_kernel(lt_ref, eb_ref, o_ref, q_ref):
    n_exp, tn = lt_ref.shape
    per_g = n_exp // N_GROUPS
    neg_inf = -jnp.inf
    sc = _sigmoid(lt_ref[...])
    sel = sc + eb_ref[...]
    sc3 = sc.reshape(N_GROUPS, per_g, tn)
    g3 = sel.reshape(N_GROUPS, per_g, tn)
    io = lax.broadcasted_iota(jnp.int32, (N_GROUPS, per_g, tn), 1)
    gio = lax.broadcasted_iota(jnp.int32, (N_GROUPS, per_g, tn), 0)
    eio = gio * per_g + io

    m1 = jnp.max(g3, axis=1, keepdims=True)
    i1 = jnp.min(jnp.where(g3 == m1, io, per_g), axis=1, keepdims=True)
    m2 = jnp.max(jnp.where(io == i1, neg_inf, g3), axis=1, keepdims=True)
    gs = m1 + m2

    g1io = lax.broadcasted_iota(jnp.int32, (N_GROUPS, 1, tn), 0)
    gsel = jnp.zeros((N_GROUPS, 1, tn), F32)
    cur = gs
    for _ in range(TOPK_GROUPS):
        m = jnp.max(cur, axis=0, keepdims=True)
        i = jnp.min(jnp.where(cur == m, g1io, N_GROUPS), axis=0, keepdims=True)
        pick = g1io == i
        gsel = jnp.where(pick, 1.0, gsel)
        cur = jnp.where(pick, neg_inf, cur)

    cur = jnp.where(gsel > 0.0, g3, neg_inf)
    chosen = jnp.zeros((N_GROUPS, per_g, tn), F32)
    for _ in range(TOP_K):
        m = jnp.max(jnp.max(cur, axis=1, keepdims=True), axis=0, keepdims=True)
        cand = jnp.where(cur == m, eio, n_exp)
        i = jnp.min(jnp.min(cand, axis=1, keepdims=True), axis=0, keepdims=True)
        pick = eio == i
        chosen = jnp.where(pick, 1.0, chosen)
        cur = jnp.where(pick, neg_inf, cur)

    w = jnp.where(chosen > 0.0, sc3, 0.0)
    tot = jnp.sum(jnp.sum(w, axis=1, keepdims=True), axis=0, keepdims=True)
    gates = (w / tot * ROUTED_SCALE).reshape(n_exp, tn)
    o_ref[...] = gates
    routed = jnp.where(gates > 0.0, 1.0, 0.0).astype(BF16)
    ones = jnp.ones((8, MOE_TB), BF16)
    for j in range(tn // MOE_TB):
        n_row = _dot_nt(ones, routed[:, j * MOE_TB:(j + 1) * MOE_TB])[0:1]
        q_ref[j] = jnp.floor((n_row + (MOE_CHUNK - 1)) * (1.0 / MOE_CHUNK)).astype(jnp.int32)


def _route(lt, e_bias):
    n_exp, t = lt.shape
    tn = ROUTE_TN
    bps = tn // MOE_TB
    return pl.pallas_call(
        _route_kernel,
        out_shape=(jax.ShapeDtypeStruct((n_exp, t), F32),
                   jax.ShapeDtypeStruct((t // MOE_TB, 1, n_exp), jnp.int32)),
        grid=(t // tn,),
        in_specs=[pl.BlockSpec((n_exp, tn), lambda i: (0, i)),
                  pl.BlockSpec((n_exp, 1), lambda i: (0, 0))],
        out_specs=(pl.BlockSpec((n_exp, tn), lambda i: (0, i)),
                   pl.BlockSpec((bps, 1, n_exp), lambda i: (i, 0, 0))),
        compiler_params=pltpu.CompilerParams(dimension_semantics=("arbitrary",)),
        name="route",
    )(lt, e_bias.reshape(n_exp, 1))


def _for_each_chunk(n, fn):
    shift = MOE_COPY_UNROLL.bit_length() - 1

    def group(j, carry):
        for u in range(MOE_COPY_UNROLL):
            fn(j * MOE_COPY_UNROLL + u)
        return carry
    lax.fori_loop(0, n >> shift, group, 0)
    base = (n >> shift) << shift
    for u in range(MOE_COPY_UNROLL - 1):
        @pl.when(base + u < n)
        def _():
            fn(base + u)


def _dispatch_kernel(src2_s, dst2_s, n2_s, src1_s, dst1_s, n1_s, nch_s, total_s,
                     gt_ref, qrow_ref, qbrow_ref, h_ref, xs_ref, stage_ref, zero_ref, sem):
    b = pl.program_id(0)
    nb = pl.num_programs(0)
    slot = lax.rem(b, 2)
    n_exp, tb = gt_ref.shape
    rmax = stage_ref.shape[1] * MOE_CHUNK
    cpg = MOE_ROWGROUP // MOE_CHUNK

    routed = gt_ref[...] > 0.0
    before = (lax.broadcasted_iota(jnp.int32, (tb, tb), 0)
              < lax.broadcasted_iota(jnp.int32, (tb, tb), 1))
    pos = _dot(jnp.where(routed, 1.0, 0.0).astype(BF16), jnp.where(before, 1.0, 0.0).astype(BF16))
    posm = _bf(jnp.where(routed, pos, -1.0))
    qrow = qrow_ref[0]
    qbrow = qbrow_ref[0]
    qbrow_f = qbrow.astype(F32)
    h = h_ref[...]

    def sort_rows(g):
        r0 = g * MOE_ROWGROUP
        chunk = (lax.broadcasted_iota(jnp.int32, (MOE_ROWGROUP, n_exp), 0) + r0) >> MOE_CHUNK_SHIFT
        own = jnp.where(chunk >= qbrow, jnp.where(chunk < qbrow + qrow, 1.0, 0.0), 0.0)
        rank = _dot(_bf(own), posm)
        start = jnp.sum(own * qbrow_f, axis=-1, keepdims=True) * MOE_CHUNK
        rel = (lax.broadcasted_iota(jnp.int32, (MOE_ROWGROUP, 1), 0) + r0).astype(F32) - start
        onehot = jnp.where(rank == rel, 1.0, 0.0).astype(BF16)
        rows = _bf(_dot(onehot, h))
        stage_ref[slot, g * cpg:(g + 1) * cpg] = rows.reshape(cpg, MOE_CHUNK, rows.shape[1])

    n_groups = rmax // MOE_ROWGROUP
    n_sure = min(n_groups, MOE_TB * TOP_K // MOE_ROWGROUP + 1)
    for g in range(n_sure):
        sort_rows(g)
    for g in range(n_sure, n_groups):
        @pl.when(g * MOE_ROWGROUP < nch_s[b] * MOE_CHUNK)
        def _():
            sort_rows(g)

    def copy2(sl, src, dst):
        return pltpu.make_async_copy(stage_ref.at[sl, pl.ds(src, 2)], xs_ref.at[pl.ds(dst, 2)],
                                     sem.at[sl, 0])

    def copy1(sl, src_ref, dst):
        return pltpu.make_async_copy(src_ref, xs_ref.at[dst], sem.at[sl, 1])

    _for_each_chunk(n2_s[b], lambda i: copy2(slot, src2_s[b, i], dst2_s[b, i]).start())
    _for_each_chunk(n1_s[b], lambda i: copy1(slot, stage_ref.at[slot, src1_s[b, i]],
                                             dst1_s[b, i]).start())

    def wait_copies(sl, n2, n1):
        _for_each_chunk(n2, lambda i: copy2(sl, 0, 0).wait())
        _for_each_chunk(n1, lambda i: copy1(sl, zero_ref, 0).wait())

    @pl.when(b > 0)
    def _():
        wait_copies(1 - slot, n2_s[b - 1], n1_s[b - 1])

    @pl.when(b == nb - 1)
    def _():
        zero_ref[...] = jnp.zeros_like(zero_ref)
        n_tail = MOE_TM // MOE_CHUNK
        for c in range(n_tail):
            copy1(slot, zero_ref, total_s[0] + c).start()
        wait_copies(slot, n2_s[b], n1_s[b] + n_tail)


def _dispatch(gates_t, q, qbase, copies, nch, total, h2, n_chunks):
    n_exp, t = gates_t.shape
    d = h2.shape[1]
    nb = t // MOE_TB
    rmax = _moe_stage_rows(n_exp)
    grid_spec = pltpu.PrefetchScalarGridSpec(
        num_scalar_prefetch=8,
        grid=(nb,),
        in_specs=[
            pl.BlockSpec((n_exp, MOE_TB), lambda b, *_: (0, b)),
            pl.BlockSpec((1, 1, n_exp), lambda b, *_: (b, 0, 0)),
            pl.BlockSpec((1, 1, n_exp), lambda b, *_: (b, 0, 0)),
            pl.BlockSpec((MOE_TB, d), lambda b, *_: (b, 0)),
        ],
        out_specs=pl.BlockSpec(memory_space=pl.ANY),
        scratch_shapes=[pltpu.VMEM((2, rmax // MOE_CHUNK, MOE_CHUNK, d), BF16),
                        pltpu.VMEM((MOE_CHUNK, d), BF16), pltpu.SemaphoreType.DMA((2, 2))],
    )
    return pl.pallas_call(
        _dispatch_kernel,
        out_shape=jax.ShapeDtypeStruct((n_chunks, MOE_CHUNK, d), BF16),
        grid_spec=grid_spec,
        compiler_params=pltpu.CompilerParams(dimension_semantics=("arbitrary",),
                                             vmem_limit_bytes=V7X_VMEM_LIMIT),
        name="moe_dispatch",
    )(*copies, nch, total, gates_t, q.reshape(nb, 1, n_exp), qbase.reshape(nb, 1, n_exp), h2)


def _tile_pieces():
    cpt = MOE_TM // MOE_CHUNK
    return [1 << s for s in range(cpt.bit_length() - 1, -1, -1)]


def _expert_kernel(off_s, len_s, next_s, first_s, xs_ref, wg_ref, wu_ref, wd_ref, ys_ref,
                   xbuf, ybuf, wg_b, wu_b, wd_b, state, sem_in, sem_out):
    e = pl.program_id(0)
    n_exp = pl.num_programs(0)
    cpt = MOE_TM // MOE_CHUNK
    d = xbuf.shape[3]
    pieces = _tile_pieces()

    def tile_in(ee, tt, sl):
        return pltpu.make_async_copy(xs_ref.at[pl.ds(off_s[ee] + tt * cpt, cpt)], xbuf.at[sl],
                                     sem_in.at[sl])

    def for_each_piece(valid, fn):
        for k, piece in enumerate(pieces):
            @pl.when((valid & piece) != 0)
            def _():
                fn(k, piece, valid & ~(2 * piece - 1))

    def tile_out(sl, dst_chunk, k, piece, start):
        return pltpu.make_async_copy(ybuf.at[sl, pl.ds(start, piece)],
                                     ys_ref.at[pl.ds(dst_chunk + start, piece)], sem_out.at[sl, k])

    def drain(sl):
        for_each_piece(state[1 + sl], lambda k, piece, start: tile_out(sl, 0, k, piece, start).wait())
        state[1 + sl] = 0

    def following(ee, tt):
        safe = jnp.minimum(ee, n_exp - 1)
        more = tt + 1 < (len_s[safe] + cpt - 1) // cpt
        nxt_e = jnp.where(ee >= n_exp, n_exp, jnp.where(more, ee, next_s[safe]))
        return nxt_e, jnp.where(more, tt + 1, 0)

    def prefetch(ee, tt, xsl):
        @pl.when(ee < n_exp)
        def _():
            tile_in(ee, tt, xsl).start()

    @pl.when(e == 0)
    def _():
        state[0] = 0
        state[1] = 0
        state[2] = 0
        ahead = (first_s[0], 0)
        for k in range(MOE_XBUFS - 1):
            prefetch(ahead[0], ahead[1], k)
            ahead = following(*ahead)

    n_valid = len_s[e]
    n_tiles = (n_valid + cpt - 1) // cpt

    @pl.when(n_valid > 0)
    def _():
        wg_b[...] = _bf(wg_ref[0])
        wu_b[...] = _bf(wu_ref[0])
        wd_b[...] = _bf(wd_ref[0])

        def tile(t, carry):
            g = state[0]
            xsl = lax.rem(g, MOE_XBUFS)
            sl = g & 1
            tile_in(e, t, xsl).wait()
            ahead = (e, t)
            for _ in range(MOE_XBUFS - 1):
                ahead = following(*ahead)
            prefetch(ahead[0], ahead[1], lax.rem(g + MOE_XBUFS - 1, MOE_XBUFS))

            drain(sl)
            valid = jnp.minimum(n_valid - t * cpt, cpt)

            def ffn(n_c):
                x = xbuf[xsl, :n_c].reshape(n_c * MOE_CHUNK, d)
                a = _silu(_dot(x, wg_b[...])) * _dot(x, wu_b[...])
                ybuf[sl, :n_c] = _bf(_dot(_bf(a), wd_b[...])).reshape(n_c, MOE_CHUNK, d)

            sizes = [cpt >> k for k in range(MOE_TILE_ARMS)]
            for k, n_c in enumerate(sizes):
                covers_more = valid > (sizes[k + 1] if k + 1 < len(sizes) else 0)
                fits = valid <= n_c

                @pl.when(jnp.logical_and(covers_more, fits))
                def _():
                    ffn(n_c)

            dst = off_s[e] + t * cpt
            for_each_piece(valid, lambda k, piece, start: tile_out(sl, dst, k, piece, start).start())
            state[1 + sl] = valid
            state[0] = state[0] + 1
            return carry
        lax.fori_loop(0, n_tiles, tile, 0)

    @pl.when(e == n_exp - 1)
    def _():
        drain(0)
        drain(1)


def _experts(off, per_exp, nxt, first, xs, wg, wu, wd):
    n_chunks, _, d = xs.shape
    n_exp, _, ff = wg.shape
    cpt = MOE_TM // MOE_CHUNK

    def w_blk(e, *_):
        return (e, 0, 0)

    grid_spec = pltpu.PrefetchScalarGridSpec(
        num_scalar_prefetch=4,
        grid=(n_exp,),
        in_specs=[pl.BlockSpec(memory_space=pl.ANY),
                  pl.BlockSpec((1, d, ff), w_blk), pl.BlockSpec((1, d, ff), w_blk),
                  pl.BlockSpec((1, ff, d), w_blk)],
        out_specs=pl.BlockSpec(memory_space=pl.ANY),
        scratch_shapes=[pltpu.VMEM((MOE_XBUFS, cpt, MOE_CHUNK, d), BF16),
                        pltpu.VMEM((2, cpt, MOE_CHUNK, d), BF16),
                        pltpu.VMEM((d, ff), BF16), pltpu.VMEM((d, ff), BF16),
                        pltpu.VMEM((ff, d), BF16),
                        pltpu.SMEM((3,), jnp.int32),
                        pltpu.SemaphoreType.DMA((MOE_XBUFS,)),
                        pltpu.SemaphoreType.DMA((2, len(_tile_pieces())))],
    )
    return pl.pallas_call(
        _expert_kernel,
        out_shape=jax.ShapeDtypeStruct((n_chunks, MOE_CHUNK, d), BF16),
        grid_spec=grid_spec,
        compiler_params=pltpu.CompilerParams(dimension_semantics=("arbitrary",),
                                             vmem_limit_bytes=V7X_VMEM_LIMIT),
        name="moe_experts",
    )(off, per_exp, nxt, first, xs, wg, wu, wd)


def _combine_kernel(src2_s, dst2_s, n2_s, src1_s, dst1_s, n1_s, nch_s,
                    gt_ref, qcol_ref, qbcol_ref, h_ref, x1_ref, gate2_ref,
                    wsg_ref, wsu_ref, wsd_ref, ys_ref, o_ref, stage_ref, sem):
    b = pl.program_id(0)
    nb = pl.num_programs(0)
    slot = lax.rem(b, 2)
    n_exp, tb = gt_ref.shape
    rmax = stage_ref.shape[1] * MOE_CHUNK

    def copy2(sl, staged, sorted_):
        return pltpu.make_async_copy(ys_ref.at[pl.ds(sorted_, 2)],
                                     stage_ref.at[sl, pl.ds(staged, 2)], sem.at[sl, 0])

    def copy1(sl, staged, sorted_):
        return pltpu.make_async_copy(ys_ref.at[sorted_], stage_ref.at[sl, staged], sem.at[sl, 1])

    def fetch(bb, sl):
        _for_each_chunk(n2_s[bb], lambda i: copy2(sl, src2_s[bb, i], dst2_s[bb, i]).start())
        _for_each_chunk(n1_s[bb], lambda i: copy1(sl, src1_s[bb, i], dst1_s[bb, i]).start())

    @pl.when(b == 0)
    def _():
        stage_ref[...] = jnp.zeros_like(stage_ref)
        fetch(0, 0)

    _for_each_chunk(n2_s[b], lambda i: copy2(slot, 0, 0).wait())
    _for_each_chunk(n1_s[b], lambda i: copy1(slot, 0, 0).wait())

    @pl.when(b + 1 < nb)
    def _():
        fetch(b + 1, 1 - slot)

    gt = gt_ref[...]
    routed = jnp.where(gt > 0.0, 1.0, 0.0).astype(BF16)
    i0 = lax.broadcasted_iota(jnp.int32, (tb, tb), 0)
    i1 = lax.broadcasted_iota(jnp.int32, (tb, tb), 1)
    eye = jnp.where(i0 == i1, 1.0, 0.0).astype(BF16)
    routed_t = _dot_nt(eye, routed)
    gates_tok = _dot_nt(eye, _bf(gt))
    earlier = jnp.where(i1 < i0, 1.0, 0.0).astype(BF16)
    pos_t = _dot(earlier, _bf(routed_t))
    posm_t = _bf(jnp.where(routed_t > 0.0, pos_t, -1.0))

    qcol = qcol_ref[0]
    qbcol = qbcol_ref[0]
    h = h_ref[...]
    a = _silu(_dot(h, wsg_ref[...])) * _dot(h, wsu_ref[...])
    shared = _dot(_bf(a), wsd_ref[...])

    def finish(rows):
        chunk = lax.broadcasted_iota(jnp.int32, (n_exp, rows), 1) >> MOE_CHUNK_SHIFT
        own = jnp.where(chunk >= qbcol, jnp.where(chunk < qbcol + qcol, 1.0, 0.0), 0.0)
        own_b = _bf(own)
        rank = _dot(posm_t, own_b)
        wexp = _dot(_bf(gates_tok), own_b)
        start = jnp.sum(own * qbcol.astype(F32), axis=0, keepdims=True) * MOE_CHUNK
        rel = lax.broadcasted_iota(jnp.int32, (1, rows), 1).astype(F32) - start
        weights = _bf(jnp.where(rank == rel, wexp, 0.0))
        staged = stage_ref[slot, :rows // MOE_CHUNK].reshape(rows, o_ref.shape[1])
        o_ref[...] = x1_ref[...] + gate2_ref[0] * (_dot(weights, staged) + shared)

    short = rmax - MOE_ROWGROUP

    @pl.when(nch_s[b] * MOE_CHUNK <= short)
    def _():
        finish(short)

    @pl.when(nch_s[b] * MOE_CHUNK > short)
    def _():
        finish(rmax)


def _combine(gates_t, q, qbase, copies, nch, h2, x1, mod3, wsg, wsu, wsd, ys, seq):
    n_exp, t = gates_t.shape
    d = h2.shape[1]
    nb = t // MOE_TB
    per_b = seq // MOE_TB
    stage_chunks = _moe_stage_rows(n_exp) // MOE_CHUNK

    def full(a):
        return pl.BlockSpec(a.shape, lambda b, *_: (0,) * a.ndim)

    def rows(w):
        return pl.BlockSpec((MOE_TB, w), lambda b, *_: (b, 0))

    grid_spec = pltpu.PrefetchScalarGridSpec(
        num_scalar_prefetch=7,
        grid=(nb,),
        in_specs=[
            pl.BlockSpec((n_exp, MOE_TB), lambda b, *_: (0, b)),
            pl.BlockSpec((1, n_exp, 1), lambda b, *_: (b, 0, 0)),
            pl.BlockSpec((1, n_exp, 1), lambda b, *_: (b, 0, 0)),
            rows(d), rows(d),
            pl.BlockSpec((1, 1, d), lambda b, *_: (b // per_b, 0, 5)),
            full(wsg), full(wsu), full(wsd),
            pl.BlockSpec(memory_space=pl.ANY),
        ],
        out_specs=rows(d),
        scratch_shapes=[pltpu.VMEM((2, stage_chunks, MOE_CHUNK, d), BF16),
                        pltpu.SemaphoreType.DMA((2, 2))],
    )
    return pl.pallas_call(
        _combine_kernel,
        out_shape=jax.ShapeDtypeStruct((t, d), F32),
        grid_spec=grid_spec,
        compiler_params=pltpu.CompilerParams(dimension_semantics=("arbitrary",),
                                             vmem_limit_bytes=V7X_VMEM_LIMIT),
        name="moe_combine",
    )(*copies, nch, gates_t, q.reshape(nb, n_exp, 1), qbase.reshape(nb, n_exp, 1),
      h2, x1, mod3, wsg, wsu, wsd, ys)


def _moe_plan(q):
    nb, n_exp = q.shape
    qbase = jnp.cumsum(q, axis=1) - q
    nch = jnp.sum(q, axis=1)
    per_exp = jnp.sum(q, axis=0)
    off = jnp.cumsum(per_exp) - per_exp
    dstq = off[None, :] + jnp.cumsum(q, axis=0) - q
    def copy_list(count, length, stage_start, sorted_start, step):
        cum = jnp.cumsum(count, axis=1) - count
        i = jnp.arange(length, dtype=jnp.int32)
        ii = i[None, :, None]
        owned = (ii >= cum[:, None, :]) & (ii < (cum + count)[:, None, :])

        def place(start):
            return step * i[None, :] + jnp.sum(
                jnp.where(owned, (start - step * cum)[:, None, :], 0), axis=2)
        return place(stage_start), place(sorted_start), jnp.sum(count, axis=1)

    pairs = q >> 1
    max_chunks = _moe_block_chunks(n_exp)
    src2, dst2, n2 = copy_list(pairs, max_chunks // 2, qbase, dstq, 2)
    src1, dst1, n1 = copy_list(q & 1, n_exp, qbase + 2 * pairs, dstq + 2 * pairs, 1)
    copies = (src2, dst2, n2, src1, dst1, n1)
    ids = jnp.arange(n_exp, dtype=jnp.int32)
    later = (ids[None, :] > ids[:, None]) & (per_exp[None, :] > 0)
    nxt = jnp.min(jnp.where(later, ids[None, :], n_exp), axis=1)
    first = jnp.min(jnp.where(per_exp > 0, ids, n_exp)).reshape(1)
    total = jnp.sum(per_exp).reshape(1)
    return qbase, copies, nch, off, per_exp, nxt, first, total


def _moe_block_chunks(n_exp):
    return MOE_TB * TOP_K // MOE_CHUNK + n_exp


def _moe_stage_rows(n_exp):
    return -(-_moe_block_chunks(n_exp) * MOE_CHUNK // MOE_ROWGROUP) * MOE_ROWGROUP


def _moe(h2, gates_t, q3, x1, mod3, wsg, wsu, wsd, wg, wu, wd, seq):
    n_exp, t = gates_t.shape
    nb = t // MOE_TB
    q = q3.reshape(nb, n_exp)
    qbase, copies, nch, off, per_exp, nxt, first, total = _moe_plan(q)
    n_chunks = nb * _moe_block_chunks(n_exp) + MOE_TM // MOE_CHUNK
    xs = _dispatch(gates_t, q, qbase, copies, nch, total, h2, n_chunks)
    ys = _experts(off, per_exp, nxt, first, xs, wg, wu, wd)
    return _combine(gates_t, q, qbase, copies, nch, h2, x1, mod3, wsg, wsu, wsd, ys, seq)


def _pad_heads(w, heads, width):
    lead = w.shape[:-1]
    w = w.reshape(lead + (heads, width))
    w = jnp.pad(w, [(0, 0)] * len(lead) + [(0, 0), (0, LANES - width)])
    return w.reshape(lead + (heads * LANES,))


def kernel(x, c, positions, w_ada, b_ada, g_norm1, w_in, g_na_q, g_na_k, na_rpb, g_q_lat, w_uq,
           g_kv_lat, w_ukv, g_mla_q, g_mla_k, w_proj_na, w_proj_mla, w_out, g_norm2, w_router,
           e_bias, w_exp_gate, w_exp_up, w_exp_down, w_sh_gate, w_sh_up, w_sh_down):
    bsz, seq, d = x.shape
    t = bsz * seq
    depth = w_ada.shape[0]
    na_w = NA_HEADS * NA_HEAD_DIM
    q_rank = g_q_lat.shape[1]
    kv_rank = g_kv_lat.shape[1]
    n_rows = seq // GRID_W

    pos = positions.reshape(1, t)
    half = MLA_ROPE_DIM // 2
    freq = (ROPE_THETA ** (-jnp.arange(half, dtype=F32) / half)).reshape(half, 1)

    x2 = x.reshape(t, d)
    for l in range(depth):
        mod3 = _adaln(c, w_ada[l], b_ada[l]).reshape(bsz, 1, 6 * d)

        wi = w_in[l]
        o_lat = 3 * na_w
        o_rot = o_lat + q_rank + kv_rank
        o_gate = o_rot + MLA_ROPE_DIM
        wqkv = _bf(wi[:, :o_lat])
        w_rot = jnp.pad(wi[:, o_rot:o_gate], ((0, 0), (MLA_NOPE_DIM, LANES - MLA_QK_DIM)))
        wlat = _bf(jnp.concatenate([wi[:, o_lat:o_rot], w_rot], axis=1))
        wgate = _bf(wi[:, o_gate:])
        gq = (jnp.tile(g_na_q[l], NA_HEADS) * (NA_HEAD_DIM ** -0.5 * LOG2E)).reshape(1, na_w)
        gk = jnp.tile(g_na_k[l], NA_HEADS).reshape(1, na_w)
        wuq = _bf(_pad_heads(w_uq[l], MLA_HEADS, MLA_QK_DIM))
        wukv = w_ukv[l].reshape(kv_rank, MLA_HEADS, MLA_NOPE_DIM + MLA_V_DIM)
        wuk = _bf(_pad_heads(wukv[:, :, :MLA_NOPE_DIM].reshape(kv_rank, -1), MLA_HEADS, MLA_NOPE_DIM))
        wuv = _bf(wukv[:, :, MLA_NOPE_DIM:].reshape(kv_rank, MLA_HEADS * MLA_V_DIM))
        gmq = _pad_heads(jnp.tile(g_mla_q[l], MLA_HEADS) * (MLA_QK_DIM ** -0.5 * LOG2E),
                         MLA_HEADS, MLA_QK_DIM).reshape(1, -1)
        gmk = _pad_heads(jnp.tile(g_mla_k[l], MLA_HEADS), MLA_HEADS, MLA_QK_DIM).reshape(1, -1)

        qa, ka, va, qm, km, vm, sgn, sgm = _inproj(
            x2, mod3, g_norm1[l].reshape(1, d), wqkv, wlat, wgate, gq, gk,
            g_q_lat[l].reshape(1, q_rank), g_kv_lat[l].reshape(1, kv_rank), wuq, wuk, wuv,
            gmq, gmk, pos, freq, seq)

        bias = _na_bias(na_rpb[l], n_rows)
        y_na = _na_attention(qa, ka, va, bias, bsz, seq)
        y_mla = _mla_attention(qm, km, vm, bsz, seq)

        x1, h2, lt = _merge(x2, y_na, y_mla, sgn, sgm, _bf(w_proj_na[l]), _bf(w_proj_mla[l]),
                            _bf(w_out[l]), mod3, g_norm2[l].reshape(1, d), w_router[l].T, seq)
        gates_t, q3 = _route(lt, e_bias[l])
        x2 = _moe(h2, gates_t, q3, x1, mod3, _bf(w_sh_gate[l]), _bf(w_sh_up[l]),
                  _bf(w_sh_down[l]), w_exp_gate[l], w_exp_up[l], w_exp_down[l], seq)
    return x2.reshape(bsz, seq, d)
```

```python
import functools

import jax
import jax.numpy as jnp
from jax import lax
from jax.experimental import pallas as pl
from jax.experimental.pallas import tpu as pltpu

GRID_W = 64
NA_HEADS = 8
NA_HEAD_DIM = 64
NA_WIN_ROWS = 8
NA_WIN_COLS = 16
MLA_HEADS = 8
MLA_NOPE_DIM = 64
MLA_ROPE_DIM = 32
MLA_V_DIM = 64
MLA_QK_DIM = MLA_NOPE_DIM + MLA_ROPE_DIM
ROPE_THETA = 10000.0
N_GROUPS = 8
TOPK_GROUPS = 4
TOP_K = 8
ROUTED_SCALE = 2.5
EPS = 1e-6
NEG_BIG = -1e30

LANES = 128
V7X_VMEM_LIMIT = 56 * 1024 * 1024

NA_QROWS = 4
NA_BAND = 12
NA_BLOCK_TYPES = 3
NA_BLOCKS_PER_STEP = 4
MERGE_TM = 1024
MERGE_SUB = 1024
INPROJ_TM = 1024
INPROJ_SUB = 256
MLA_TQ = 1024
MLA_SUB = 512
MLA_PAIRS_PER_STEP = 4
LOG2E = 1.4426950408889634
MOE_TB = 256
ROUTE_TN = 2048
MOE_CHUNK_SHIFT = 4
MOE_CHUNK = 1 << MOE_CHUNK_SHIFT
MOE_TM = 1024
MOE_TILE_ARMS = 2
MOE_ROWGROUP = 512
MOE_COPY_UNROLL = 4
MOE_XBUFS = 3

F32 = jnp.float32
BF16 = jnp.bfloat16


def _bf(x):
    return x.astype(BF16)


def _dot(a, b):
    return jnp.dot(a, b, preferred_element_type=F32)


def _dot_nt(a, b):
    return lax.dot_general(a, b, (((1,), (1,)), ((), ())), preferred_element_type=F32)


def _split(x):
    hi = _bf(x)
    lo = _bf(x - hi.astype(F32))
    return hi, lo


def _dot3(a, b):
    ah, al = _split(a)
    bh, bl = _split(b)
    return _dot(ah, bh) + (_dot(ah, bl) + _dot(al, bh))


def _dot3_nt(a, b):
    m = a.shape[0]
    ah, al = _split(a)
    bh, bl = _split(b)
    both = _dot_nt(jnp.concatenate([ah, al], axis=0), bh)
    return both[:m] + (_dot_nt(ah, bl) + both[m:])


def _sigmoid(x):
    return 1.0 / (1.0 + jnp.exp(-x))


def _silu(x):
    return x * _sigmoid(x)


def _rms(x, n):
    ss = jnp.sum(x * x, axis=-1, keepdims=True)
    return x * lax.rsqrt(ss * (1.0 / n) + EPS)


def _adaln_kernel(c_ref, w_ref, b_ref, o_ref):
    c = c_ref[...]
    o_ref[...] = _dot3(_silu(c), w_ref[...]) + b_ref[...]


def _adaln(c, w, b):
    bsz, d = c.shape
    n = w.shape[1]
    tn = 1024
    return pl.pallas_call(
        _adaln_kernel,
        out_shape=jax.ShapeDtypeStruct((bsz, n), F32),
        grid=(n // tn,),
        in_specs=[
            pl.BlockSpec((bsz, d), lambda j: (0, 0)),
            pl.BlockSpec((d, tn), lambda j: (0, j)),
            pl.BlockSpec((1, tn), lambda j: (0, j)),
        ],
        out_specs=pl.BlockSpec((bsz, tn), lambda j: (0, j)),
        compiler_params=pltpu.CompilerParams(dimension_semantics=("arbitrary",)),
        name="adaln",
    )(c, w, b.reshape(1, n))


def _inproj_kernel(x_ref, shift_ref, scale_ref, g1_ref, wqkv_ref, wlat_ref, wgate_ref,
                   gq_ref, gk_ref, gql_ref, gkvl_ref, wuq_ref, wuk_ref, wuv_ref,
                   gmq_ref, gmk_ref, pos_ref, freq_ref,
                   qa_ref, ka_ref, va_ref, qm_ref, km_ref, vm_ref, sgn_ref, sgm_ref):
    for r0 in range(0, x_ref.shape[0], INPROJ_SUB):
        _inproj_rows(pl.ds(r0, INPROJ_SUB), x_ref, shift_ref, scale_ref, g1_ref, wqkv_ref,
                     wlat_ref, wgate_ref, gq_ref, gk_ref, gql_ref, gkvl_ref, wuq_ref, wuk_ref,
                     wuv_ref, gmq_ref, gmk_ref, pos_ref, freq_ref, qa_ref, ka_ref, va_ref,
                     qm_ref, km_ref, vm_ref, sgn_ref, sgm_ref)


def _inproj_rows(rs, x_ref, shift_ref, scale_ref, g1_ref, wqkv_ref, wlat_ref, wgate_ref,
                 gq_ref, gk_ref, gql_ref, gkvl_ref, wuq_ref, wuk_ref, wuv_ref,
                 gmq_ref, gmk_ref, pos_ref, freq_ref,
                 qa_ref, ka_ref, va_ref, qm_ref, km_ref, vm_ref, sgn_ref, sgm_ref):
    d = x_ref.shape[1]
    x = x_ref[rs, :]
    h = _rms(x, d) * g1_ref[...]
    h = h * (1.0 + scale_ref[0]) + shift_ref[0]
    hb = _bf(h)

    qkv = _dot(hb, wqkv_ref[...])
    lat = _dot(hb, wlat_ref[...])
    gts = _dot(hb, wgate_ref[...])
    sgn_ref[rs, :] = _bf(_sigmoid(gts[:, :d]))
    sgm_ref[rs, :] = _bf(_sigmoid(gts[:, d:]))

    na_w = NA_HEADS * NA_HEAD_DIM
    lane = lax.broadcasted_iota(jnp.int32, (1, LANES), 1)
    lo_half = lane < NA_HEAD_DIM
    for p in range(na_w // LANES):
        sl = slice(p * LANES, (p + 1) * LANES)
        for src_off, g_ref, dst_ref in ((0, gq_ref, qa_ref), (na_w, gk_ref, ka_ref)):
            t = qkv[:, src_off + p * LANES: src_off + (p + 1) * LANES]
            sq = t * t
            s_lo = jnp.sum(jnp.where(lo_half, sq, 0.0), axis=-1, keepdims=True)
            s_hi = jnp.sum(jnp.where(lo_half, 0.0, sq), axis=-1, keepdims=True)
            r = jnp.where(lo_half,
                          lax.rsqrt(s_lo * (1.0 / NA_HEAD_DIM) + EPS),
                          lax.rsqrt(s_hi * (1.0 / NA_HEAD_DIM) + EPS))
            dst_ref[rs, sl] = _bf(t * r * g_ref[:, sl])
    va_ref[rs, :] = _bf(qkv[:, 2 * na_w: 3 * na_w])

    q_rank = gql_ref.shape[1]
    kv_rank = gkvl_ref.shape[1]
    qln = _rms(lat[:, :q_rank], q_rank) * gql_ref[...]
    kvn = _bf(_rms(lat[:, q_rank:q_rank + kv_rank], kv_rank) * gkvl_ref[...])
    qpre = _dot(_bf(qln), wuq_ref[...])
    knope = _dot(kvn, wuk_ref[...])
    vm_ref[rs, :] = _bf(_dot(kvn, wuv_ref[...]))
    krot = lat[:, q_rank + kv_rank:]

    tm = INPROJ_SUB
    half = MLA_ROPE_DIM // 2
    ang_t = freq_ref[...] * pos_ref[:, rs].astype(F32)
    cos_t = jnp.cos(ang_t)
    sin_t = jnp.sin(ang_t)
    l_i = lax.broadcasted_iota(jnp.int32, (LANES, half), 0)
    j_i = lax.broadcasted_iota(jnp.int32, (LANES, half), 1)
    hit = jnp.where((l_i >= MLA_NOPE_DIM) & (l_i < MLA_QK_DIM)
                    & (((l_i - MLA_NOPE_DIM) & (half - 1)) == j_i), 1.0, 0.0)
    first_half = l_i < MLA_NOPE_DIM + half
    eye = jnp.where(lax.broadcasted_iota(jnp.int32, (tm, tm), 0)
                    == lax.broadcasted_iota(jnp.int32, (tm, tm), 1), 1.0, 0.0).astype(BF16)

    def table(sel, vals, fill_nope):
        hi, lo = _split(vals)
        w = _dot(_bf(sel), hi) + _dot(_bf(sel), lo)
        if fill_nope:
            w = jnp.where(lax.broadcasted_iota(jnp.int32, (LANES, tm), 0) < MLA_NOPE_DIM, 1.0, w)
        hi, lo = _split(w)
        return _dot_nt(eye, hi) + _dot_nt(eye, lo)

    c_tab = table(hit, cos_t, True)
    s_up = table(jnp.where(first_half, 0.0, hit), sin_t, False)
    s_dn = table(jnp.where(first_half, -hit, 0.0), sin_t, False)

    def rope(t):
        return t * c_tab + pltpu.roll(t, half, 1) * s_up + pltpu.roll(t, LANES - half, 1) * s_dn

    kr = rope(krot)
    for hd in range(MLA_HEADS):
        sl = slice(hd * LANES, (hd + 1) * LANES)
        qh = rope(qpre[:, sl])
        qm_ref[rs, sl] = _bf(_rms(qh, MLA_QK_DIM) * gmq_ref[:, sl])
        kh = knope[:, sl] + kr
        km_ref[rs, sl] = _bf(_rms(kh, MLA_QK_DIM) * gmk_ref[:, sl])


def _inproj(x2, mod3, g1, wqkv, wlat, wgate, gq, gk, gql, gkvl, wuq, wuk, wuv, gmq, gmk,
            pos, freq, seq):
    t, d = x2.shape
    tm = INPROJ_TM
    per_b = seq // tm
    na_w = NA_HEADS * NA_HEAD_DIM
    mla_w = MLA_HEADS * LANES
    v_w = MLA_HEADS * MLA_V_DIM

    def full(a):
        return pl.BlockSpec(a.shape, lambda i: (0,) * a.ndim, pipeline_mode=pl.Buffered(1))

    def rows(w):
        return pl.BlockSpec((tm, w), lambda i: (i, 0))

    out_shapes = (
        jax.ShapeDtypeStruct((t, na_w), BF16), jax.ShapeDtypeStruct((t, na_w), BF16),
        jax.ShapeDtypeStruct((t, na_w), BF16),
        jax.ShapeDtypeStruct((t, mla_w), BF16), jax.ShapeDtypeStruct((t, mla_w), BF16),
        jax.ShapeDtypeStruct((t, v_w), BF16),
        jax.ShapeDtypeStruct((t, d), BF16), jax.ShapeDtypeStruct((t, d), BF16),
    )
    return pl.pallas_call(
        _inproj_kernel,
        out_shape=out_shapes,
        grid=(t // tm,),
        in_specs=[
            rows(d),
            pl.BlockSpec((1, 1, d), lambda i: (i // per_b, 0, 0)),
            pl.BlockSpec((1, 1, d), lambda i: (i // per_b, 0, 1)),
            full(g1), full(wqkv), full(wlat), full(wgate), full(gq), full(gk), full(gql),
            full(gkvl), full(wuq), full(wuk), full(wuv), full(gmq), full(gmk),
            pl.BlockSpec((1, tm), lambda i: (0, i)),
            full(freq),
        ],
        out_specs=(rows(na_w), rows(na_w), rows(na_w), rows(mla_w), rows(mla_w), rows(v_w),
                   rows(d), rows(d)),
        compiler_params=pltpu.CompilerParams(dimension_semantics=("arbitrary",),
                                             vmem_limit_bytes=V7X_VMEM_LIMIT),
        name="inproj",
    )(x2, mod3, mod3, g1, wqkv, wlat, wgate, gq, gk, gql, gkvl, wuq, wuk, wuv, gmq, gmk,
      pos, freq)


def _na_block_geometry(block_type, n_rows):
    if block_type == 0:
        return 0, 0
    if block_type == 1:
        r0 = NA_QROWS
        return r0, r0 - NA_WIN_ROWS // 2
    return n_rows - NA_QROWS, n_rows - NA_BAND


def _na_bias_kernel(rpb_ref, o_ref, m_ref, *, n_rows):
    hd = pl.program_id(0)
    n_dr = 2 * NA_WIN_ROWS - 1
    n_dc = 2 * NA_WIN_COLS - 1
    qc = lax.broadcasted_iota(jnp.int32, (GRID_W, LANES), 0)
    kc = lax.broadcasted_iota(jnp.int32, (GRID_W, LANES), 1) & (GRID_W - 1)
    dc = jnp.clip(kc - qc, -(NA_WIN_COLS - 1), NA_WIN_COLS - 1) + (NA_WIN_COLS - 1)
    cstart = jnp.clip(qc - NA_WIN_COLS // 2, 0, GRID_W - NA_WIN_COLS)
    col_ok = (kc >= cstart) & (kc < cstart + NA_WIN_COLS)
    for i_dr in range(n_dr):
        acc = jnp.zeros((GRID_W, LANES), F32)
        for t in range(n_dc):
            acc = jnp.where(dc == t, rpb_ref[hd, i_dr * n_dc + t], acc)
        m_ref[i_dr] = jnp.where(col_ok, acc * LOG2E, NEG_BIG)
    neg = jnp.full((GRID_W, LANES), NEG_BIG, F32)
    lo_half = lax.broadcasted_iota(jnp.int32, (GRID_W, LANES), 1) < GRID_W
    kh = NA_WIN_ROWS
    for bt in range(NA_BLOCK_TYPES):
        r0, start = _na_block_geometry(bt, n_rows)
        for i in range(NA_QROWS):
            r = r0 + i
            rs = min(max(r - kh // 2, 0), n_rows - kh)
            for jp in range(NA_BAND // 2):
                halves = []
                for j in (2 * jp, 2 * jp + 1):
                    krow = start + j
                    if rs <= krow < rs + kh:
                        halves.append(m_ref[krow - r + (NA_WIN_ROWS - 1)])
                    else:
                        halves.append(neg)
                tile = jnp.where(lo_half, halves[0], halves[1])
                o_ref[bt, 0, i * GRID_W:(i + 1) * GRID_W, jp * LANES:(jp + 1) * LANES] = tile


def _na_bias(rpb, n_rows):
    heads = rpb.shape[0]
    nq = NA_QROWS * GRID_W
    nk = NA_BAND * GRID_W
    rpb2 = rpb.reshape(heads, -1)
    return pl.pallas_call(
        functools.partial(_na_bias_kernel, n_rows=n_rows),
        out_shape=jax.ShapeDtypeStruct((NA_BLOCK_TYPES, heads, nq, nk), F32),
        grid=(heads,),
        in_specs=[pl.BlockSpec(memory_space=pltpu.SMEM)],
        out_specs=pl.BlockSpec((NA_BLOCK_TYPES, 1, nq, nk), lambda hd: (0, hd, 0, 0)),
        scratch_shapes=[pltpu.VMEM((2 * NA_WIN_ROWS - 1, GRID_W, LANES), F32)],
        compiler_params=pltpu.CompilerParams(dimension_semantics=("arbitrary",)),
        name="na_bias",
    )(rpb2)


def _softmax_pv(s, v_pair, hh, half):
    lane = lax.broadcasted_iota(jnp.int32, (1, LANES), 1)
    mine = (lane < half) if hh == 0 else (lane >= half)
    den_lane = half if hh == 0 else 0
    m = jnp.max(s, axis=-1, keepdims=True)
    p = _bf(jnp.exp2(s - m))
    ones_row = jnp.where(lane == den_lane, 1.0, 0.0).astype(BF16)
    o = _dot(p, jnp.where(mine, v_pair, ones_row))
    den = jnp.sum(jnp.where(lane == den_lane, o, 0.0), axis=-1, keepdims=True)
    return jnp.where(mine, o / den, 0.0)


def _na_kernel(q_ref, k_ref, v_ref, bias_ref, o_ref, *, n_blocks, n_rows):
    nk = NA_BAND * GRID_W
    tq = NA_QROWS * GRID_W
    lo_half = lax.broadcasted_iota(jnp.int32, (1, LANES), 1) < NA_HEAD_DIM
    for u in range(NA_BLOCKS_PER_STEP):
        blk = pl.program_id(1) * NA_BLOCKS_PER_STEP + u
        btype = jnp.where(blk == 0, 0, jnp.where(blk == n_blocks - 1, 2, 1))
        start_row = jnp.where(blk == 0, 0,
                              jnp.where(blk == n_blocks - 1, n_rows - NA_BAND,
                                        blk * NA_QROWS - NA_WIN_ROWS // 2))
        off = pl.multiple_of(start_row * GRID_W, GRID_W)
        rs = pl.ds(u * tq, tq)
        for p in range(NA_HEADS * NA_HEAD_DIM // LANES):
            sl = slice(p * LANES, (p + 1) * LANES)
            qp = q_ref[rs, sl]
            kb = k_ref[pl.ds(off, nk), sl]
            vb = v_ref[pl.ds(off, nk), sl]
            zero = jnp.zeros_like(qp)
            q2 = jnp.concatenate([jnp.where(lo_half, qp, zero), jnp.where(lo_half, zero, qp)],
                                 axis=0)
            s = _dot_nt(q2, kb) + bias_ref[btype, 2 * p:2 * p + 2].reshape(2 * tq, nk)
            m = jnp.max(s, axis=-1, keepdims=True)
            e = jnp.exp2(s - m)
            den = jnp.sum(e, axis=-1, keepdims=True)
            o = _dot(_bf(e), vb) / den
            o_ref[rs, sl] = _bf(jnp.where(lo_half, o[:tq], o[tq:]))


def _na_attention(qa, ka, va, bias, bsz, seq):
    t, w = qa.shape
    n_rows = seq // GRID_W
    n_blocks = n_rows // NA_QROWS
    tq = NA_QROWS * GRID_W
    nk = NA_BAND * GRID_W
    heads = bias.shape[1]

    bps = NA_BLOCKS_PER_STEP
    steps = n_blocks // bps
    return pl.pallas_call(
        functools.partial(_na_kernel, n_blocks=n_blocks, n_rows=n_rows),
        out_shape=jax.ShapeDtypeStruct((t, w), BF16),
        grid=(bsz, steps),
        in_specs=[
            pl.BlockSpec((bps * tq, w), lambda b, j: (b * steps + j, 0)),
            pl.BlockSpec((seq, w), lambda b, j: (b, 0)),
            pl.BlockSpec((seq, w), lambda b, j: (b, 0)),
            pl.BlockSpec(bias.shape, lambda b, j: (0, 0, 0, 0), pipeline_mode=pl.Buffered(1)),
        ],
        out_specs=pl.BlockSpec((bps * tq, w), lambda b, j: (b * steps + j, 0)),
        compiler_params=pltpu.CompilerParams(dimension_semantics=("arbitrary", "arbitrary"),
                                             vmem_limit_bytes=V7X_VMEM_LIMIT),
        name="na_attn",
    )(qa, ka, va, bias)


def _mla_kernel(q_ref, k_ref, v_ref, o_ref):
    for r0 in range(0, q_ref.shape[0], MLA_SUB):
        rs = pl.ds(r0, MLA_SUB)
        for pp in range(MLA_PAIRS_PER_STEP):
            v_pair = v_ref[:, pp * LANES:(pp + 1) * LANES]
            acc = jnp.zeros((MLA_SUB, LANES), F32)
            for hh in range(2):
                sl = slice((2 * pp + hh) * LANES, (2 * pp + hh + 1) * LANES)
                s = _dot_nt(q_ref[rs, sl], k_ref[:, sl])
                acc = acc + _softmax_pv(s, v_pair, hh, MLA_V_DIM)
            o_ref[rs, pp * LANES:(pp + 1) * LANES] = _bf(acc)


def _mla_attention(qm, km, vm, bsz, seq):
    t = qm.shape[0]
    tq = MLA_TQ
    nq = seq // tq
    groups = MLA_HEADS // (2 * MLA_PAIRS_PER_STEP)
    qk_w = 2 * MLA_PAIRS_PER_STEP * LANES
    v_w = MLA_PAIRS_PER_STEP * LANES
    return pl.pallas_call(
        _mla_kernel,
        out_shape=jax.ShapeDtypeStruct((t, MLA_HEADS * MLA_V_DIM), BF16),
        grid=(bsz, groups, nq),
        in_specs=[
            pl.BlockSpec((tq, qk_w), lambda b, p, i: (b * nq + i, p)),
            pl.BlockSpec((seq, qk_w), lambda b, p, i: (b, p)),
            pl.BlockSpec((seq, v_w), lambda b, p, i: (b, p)),
        ],
        out_specs=pl.BlockSpec((tq, v_w), lambda b, p, i: (b * nq + i, p)),
        compiler_params=pltpu.CompilerParams(
            dimension_semantics=("arbitrary", "arbitrary", "arbitrary"),
            vmem_limit_bytes=V7X_VMEM_LIMIT),
        name="mla_attn",
    )(qm, km, vm)


def _merge_kernel(x_ref, yna_ref, ymla_ref, sgn_ref, sgm_ref, wpn_ref, wpm_ref, wout_ref,
                  gate1_ref, shift2_ref, scale2_ref, g2_ref, wr_ref,
                  x1_ref, h2_ref, lt_ref):
    d = x_ref.shape[1]
    for r0 in range(0, x_ref.shape[0], MERGE_SUB):
        rs = pl.ds(r0, MERGE_SUB)
        merged = (sgn_ref[rs, :].astype(F32) * _dot(yna_ref[rs, :], wpn_ref[...])
                  + sgm_ref[rs, :].astype(F32) * _dot(ymla_ref[rs, :], wpm_ref[...]))
        x1 = x_ref[rs, :] + gate1_ref[0] * _dot(_bf(merged), wout_ref[...])
        x1_ref[rs, :] = x1
        h2 = _rms(x1, d) * g2_ref[...]
        h2 = h2 * (1.0 + scale2_ref[0]) + shift2_ref[0]
        h2_ref[rs, :] = _bf(h2)
        lt_ref[:, rs] = _dot3_nt(wr_ref[...], h2)


def _merge(x2, yna, ymla, sgn, sgm, wpn, wpm, wout, mod3, g2, wr, seq):
    t, d = x2.shape
    tm = MERGE_TM
    per_b = seq // tm
    n_exp = wr.shape[0]

    def full(a):
        return pl.BlockSpec(a.shape, lambda i: (0,) * a.ndim)

    def rows(w):
        return pl.BlockSpec((tm, w), lambda i: (i, 0))

    def modblk(j):
        return pl.BlockSpec((1, 1, d), lambda i: (i // per_b, 0, j))

    return pl.pallas_call(
        _merge_kernel,
        out_shape=(jax.ShapeDtypeStruct((t, d), F32), jax.ShapeDtypeStruct((t, d), BF16),
                   jax.ShapeDtypeStruct((n_exp, t), F32)),
        grid=(t // tm,),
        in_specs=[rows(d), rows(yna.shape[1]), rows(ymla.shape[1]), rows(d), rows(d),
                  full(wpn), full(wpm), full(wout),
                  modblk(2), modblk(3), modblk(4), full(g2), full(wr)],
        out_specs=(rows(d), rows(d), pl.BlockSpec((n_exp, tm), lambda i: (0, i))),
        compiler_params=pltpu.CompilerParams(dimension_semantics=("arbitrary",),
                                             vmem_limit_bytes=V7X_VMEM_LIMIT),
        name="merge",
    )(x2, yna, ymla, sgn, sgm, wpn, wpm, wout, mod3, mod3, mod3, g2, wr)


def _route_kernel(lt_ref, eb_ref, o_ref, q_ref):
    n_exp, tn = lt_ref.shape
    per_g = n_exp // N_GROUPS
    neg_inf = -jnp.inf
    sc = _sigmoid(lt_ref[...])
    sel = sc + eb_ref[...]
    sc3 = sc.reshape(N_GROUPS, per_g, tn)
    g3 = sel.reshape(N_GROUPS, per_g, tn)
    io = lax.broadcasted_iota(jnp.int32, (N_GROUPS, per_g, tn), 1)
    gio = lax.broadcasted_iota(jnp.int32, (N_GROUPS, per_g, tn), 0)
    eio = gio * per_g + io

    m1 = jnp.max(g3, axis=1, keepdims=True)
    i1 = jnp.min(jnp.where(g3 == m1, io, per_g), axis=1, keepdims=True)
    m2 = jnp.max(jnp.where(io == i1, neg_inf, g3), axis=1, keepdims=True)
    gs = m1 + m2

    g1io = lax.broadcasted_iota(jnp.int32, (N_GROUPS, 1, tn), 0)
    gsel = jnp.zeros((N_GROUPS, 1, tn), F32)
    cur = gs
    for _ in range(TOPK_GROUPS):
        m = jnp.max(cur, axis=0, keepdims=True)
        i = jnp.min(jnp.where(cur == m, g1io, N_GROUPS), axis=0, keepdims=True)
        pick = g1io == i
        gsel = jnp.where(pick, 1.0, gsel)
        cur = jnp.where(pick, neg_inf, cur)

    cur = jnp.where(gsel > 0.0, g3, neg_inf)
    chosen = jnp.zeros((N_GROUPS, per_g, tn), F32)
    for _ in range(TOP_K):
        m = jnp.max(jnp.max(cur, axis=1, keepdims=True), axis=0, keepdims=True)
        cand = jnp.where(cur == m, eio, n_exp)
        i = jnp.min(jnp.min(cand, axis=1, keepdims=True), axis=0, keepdims=True)
        pick = eio == i
        chosen = jnp.where(pick, 1.0, chosen)
        cur = jnp.where(pick, neg_inf, cur)

    w = jnp.where(chosen > 0.0, sc3, 0.0)
    tot = jnp.sum(jnp.sum(w, axis=1, keepdims=True), axis=0, keepdims=True)
    gates = (w / tot * ROUTED_SCALE).reshape(n_exp, tn)
    o_ref[...] = gates
    routed = jnp.where(gates > 0.0, 1.0, 0.0).astype(BF16)
    ones = jnp.ones((8, MOE_TB), BF16)
    for j in range(tn // MOE_TB):
        n_row = _dot_nt(ones, routed[:, j * MOE_TB:(j + 1) * MOE_TB])[0:1]
        q_ref[j] = jnp.floor((n_row + (MOE_CHUNK - 1)) * (1.0 / MOE_CHUNK)).astype(jnp.int32)


def _route(lt, e_bias):
    n_exp, t = lt.shape
    tn = ROUTE_TN
    bps = tn // MOE_TB
    return pl.pallas_call(
        _route_kernel,
        out_shape=(jax.ShapeDtypeStruct((n_exp, t), F32),
                   jax.ShapeDtypeStruct((t // MOE_TB, 1, n_exp), jnp.int32)),
        grid=(t // tn,),
        in_specs=[pl.BlockSpec((n_exp, tn), lambda i: (0, i)),
                  pl.BlockSpec((n_exp, 1), lambda i: (0, 0))],
        out_specs=(pl.BlockSpec((n_exp, tn), lambda i: (0, i)),
                   pl.BlockSpec((bps, 1, n_exp), lambda i: (i, 0, 0))),
        compiler_params=pltpu.CompilerParams(dimension_semantics=("arbitrary",)),
        name="route",
    )(lt, e_bias.reshape(n_exp, 1))


def _for_each_chunk(n, fn):
    shift = MOE_COPY_UNROLL.bit_length() - 1

    def group(j, carry):
        for u in range(MOE_COPY_UNROLL):
            fn(j * MOE_COPY_UNROLL + u)
        return carry
    lax.fori_loop(0, n >> shift, group, 0)
    base = (n >> shift) << shift
    for u in range(MOE_COPY_UNROLL - 1):
        @pl.when(base + u < n)
        def _():
            fn(base + u)


def _start_each_copy(n, copy_of):
    shift = MOE_COPY_UNROLL.bit_length() - 1

    def group(j, carry):
        for u in range(MOE_COPY_UNROLL):
            copy_of(j * MOE_COPY_UNROLL + u).start(priority=u % 2)
        return carry
    lax.fori_loop(0, n >> shift, group, 0)
    base = (n >> shift) << shift
    for u in range(MOE_COPY_UNROLL - 1):
        @pl.when(base + u < n)
        def _():
            copy_of(base + u).start(priority=u % 2)


def _dispatch_kernel(src2_s, dst2_s, n2_s, src1_s, dst1_s, n1_s, nch_s, total_s,
                     gt_ref, qrow_ref, qbrow_ref, h_ref, xs_ref, stage_ref, zero_ref, sem):
    b = pl.program_id(0)
    nb = pl.num_programs(0)
    slot = lax.rem(b, 2)
    n_exp, tb = gt_ref.shape
    rmax = stage_ref.shape[1] * MOE_CHUNK
    cpg = MOE_ROWGROUP // MOE_CHUNK

    routed = gt_ref[...] > 0.0
    before = (lax.broadcasted_iota(jnp.int32, (tb, tb), 0)
              < lax.broadcasted_iota(jnp.int32, (tb, tb), 1))
    pos = _dot(jnp.where(routed, 1.0, 0.0).astype(BF16), jnp.where(before, 1.0, 0.0).astype(BF16))
    posm = _bf(jnp.where(routed, pos, -1.0))
    qrow = qrow_ref[0]
    qbrow = qbrow_ref[0]
    qbrow_f = qbrow.astype(F32)
    h = h_ref[...]

    def sort_rows(g):
        r0 = g * MOE_ROWGROUP
        chunk = (lax.broadcasted_iota(jnp.int32, (MOE_ROWGROUP, n_exp), 0) + r0) >> MOE_CHUNK_SHIFT
        own = jnp.where(chunk >= qbrow, jnp.where(chunk < qbrow + qrow, 1.0, 0.0), 0.0)
        rank = _dot(_bf(own), posm)
        start = jnp.sum(own * qbrow_f, axis=-1, keepdims=True) * MOE_CHUNK
        rel = (lax.broadcasted_iota(jnp.int32, (MOE_ROWGROUP, 1), 0) + r0).astype(F32) - start
        onehot = jnp.where(rank == rel, 1.0, 0.0).astype(BF16)
        rows = _bf(_dot(onehot, h))
        stage_ref[slot, g * cpg:(g + 1) * cpg] = rows.reshape(cpg, MOE_CHUNK, rows.shape[1])

    n_groups = rmax // MOE_ROWGROUP
    n_sure = min(n_groups, MOE_TB * TOP_K // MOE_ROWGROUP + 1)
    for g in range(n_sure):
        sort_rows(g)
    for g in range(n_sure, n_groups):
        @pl.when(g * MOE_ROWGROUP < nch_s[b] * MOE_CHUNK)
        def _():
            sort_rows(g)

    def copy2(sl, src, dst):
        return pltpu.make_async_copy(stage_ref.at[sl, pl.ds(src, 2)], xs_ref.at[pl.ds(dst, 2)],
                                     sem.at[sl, 0])

    def copy1(sl, src_ref, dst):
        return pltpu.make_async_copy(src_ref, xs_ref.at[dst], sem.at[sl, 1])

    _start_each_copy(n2_s[b], lambda i: copy2(slot, src2_s[b, i], dst2_s[b, i]))
    _start_each_copy(n1_s[b], lambda i: copy1(slot, stage_ref.at[slot, src1_s[b, i]],
                                              dst1_s[b, i]))

    def wait_copies(sl, n2, n1):
        _for_each_chunk(n2, lambda i: copy2(sl, 0, 0).wait())
        _for_each_chunk(n1, lambda i: copy1(sl, zero_ref, 0).wait())

    @pl.when(b > 0)
    def _():
        wait_copies(1 - slot, n2_s[b - 1], n1_s[b - 1])

    @pl.when(b == nb - 1)
    def _():
        zero_ref[...] = jnp.zeros_like(zero_ref)
        n_tail = MOE_TM // MOE_CHUNK
        for c in range(n_tail):
            copy1(slot, zero_ref, total_s[0] + c).start()
        wait_copies(slot, n2_s[b], n1_s[b] + n_tail)


def _dispatch(gates_t, q, qbase, copies, nch, total, h2, n_chunks):
    n_exp, t = gates_t.shape
    d = h2.shape[1]
    nb = t // MOE_TB
    rmax = _moe_stage_rows(n_exp)
    grid_spec = pltpu.PrefetchScalarGridSpec(
        num_scalar_prefetch=8,
        grid=(nb,),
        in_specs=[
            pl.BlockSpec((n_exp, MOE_TB), lambda b, *_: (0, b)),
            pl.BlockSpec((1, 1, n_exp), lambda b, *_: (b, 0, 0)),
            pl.BlockSpec((1, 1, n_exp), lambda b, *_: (b, 0, 0)),
            pl.BlockSpec((MOE_TB, d), lambda b, *_: (b, 0)),
        ],
        out_specs=pl.BlockSpec(memory_space=pl.ANY),
        scratch_shapes=[pltpu.VMEM((2, rmax // MOE_CHUNK, MOE_CHUNK, d), BF16),
                        pltpu.VMEM((MOE_CHUNK, d), BF16), pltpu.SemaphoreType.DMA((2, 2))],
    )
    return pl.pallas_call(
        _dispatch_kernel,
        out_shape=jax.ShapeDtypeStruct((n_chunks, MOE_CHUNK, d), BF16),
        grid_spec=grid_spec,
        compiler_params=pltpu.CompilerParams(dimension_semantics=("arbitrary",),
                                             vmem_limit_bytes=V7X_VMEM_LIMIT),
        name="moe_dispatch",
    )(*copies, nch, total, gates_t, q.reshape(nb, 1, n_exp), qbase.reshape(nb, 1, n_exp), h2)


def _tile_pieces():
    cpt = MOE_TM // MOE_CHUNK
    return [1 << s for s in range(cpt.bit_length() - 1, -1, -1)]


def _expert_kernel(off_s, len_s, next_s, first_s, xs_ref, wg_ref, wu_ref, wd_ref, ys_ref,
                   xbuf, ybuf, wg_b, wu_b, wd_b, state, sem_in, sem_out):
    e = pl.program_id(0)
    n_exp = pl.num_programs(0)
    cpt = MOE_TM // MOE_CHUNK
    d = xbuf.shape[3]
    pieces = _tile_pieces()

    def tile_in(ee, tt, sl):
        return pltpu.make_async_copy(xs_ref.at[pl.ds(off_s[ee] + tt * cpt, cpt)], xbuf.at[sl],
                                     sem_in.at[sl])

    def for_each_piece(valid, fn):
        for k, piece in enumerate(pieces):
            @pl.when((valid & piece) != 0)
            def _():
                fn(k, piece, valid & ~(2 * piece - 1))

    def tile_out(sl, dst_chunk, k, piece, start):
        return pltpu.make_async_copy(ybuf.at[sl, pl.ds(start, piece)],
                                     ys_ref.at[pl.ds(dst_chunk + start, piece)], sem_out.at[sl, k])

    def drain(sl):
        for_each_piece(state[1 + sl], lambda k, piece, start: tile_out(sl, 0, k, piece, start).wait())
        state[1 + sl] = 0

    def following(ee, tt):
        safe = jnp.minimum(ee, n_exp - 1)
        more = tt + 1 < (len_s[safe] + cpt - 1) // cpt
        nxt_e = jnp.where(ee >= n_exp, n_exp, jnp.where(more, ee, next_s[safe]))
        return nxt_e, jnp.where(more, tt + 1, 0)

    def prefetch(ee, tt, xsl):
        @pl.when(ee < n_exp)
        def _():
            tile_in(ee, tt, xsl).start()

    @pl.when(e == 0)
    def _():
        state[0] = 0
        state[1] = 0
        state[2] = 0
        ahead = (first_s[0], 0)
        for k in range(MOE_XBUFS - 1):
            prefetch(ahead[0], ahead[1], k)
            ahead = following(*ahead)

    n_valid = len_s[e]
    n_tiles = (n_valid + cpt - 1) // cpt

    @pl.when(n_valid > 0)
    def _():
        wg_b[...] = _bf(wg_ref[0])
        wu_b[...] = _bf(wu_ref[0])
        wd_b[...] = _bf(wd_ref[0])

        def tile(t, carry):
            g = state[0]
            xsl = lax.rem(g, MOE_XBUFS)
            sl = g & 1
            tile_in(e, t, xsl).wait()
            ahead = (e, t)
            for _ in range(MOE_XBUFS - 1):
                ahead = following(*ahead)
            prefetch(ahead[0], ahead[1], lax.rem(g + MOE_XBUFS - 1, MOE_XBUFS))

            drain(sl)
            valid = jnp.minimum(n_valid - t * cpt, cpt)

            def ffn(n_c):
                x = xbuf[xsl, :n_c].reshape(n_c * MOE_CHUNK, d)
                a = _silu(_dot(x, wg_b[...])) * _dot(x, wu_b[...])
                ybuf[sl, :n_c] = _bf(_dot(_bf(a), wd_b[...])).reshape(n_c, MOE_CHUNK, d)

            sizes = [cpt >> k for k in range(MOE_TILE_ARMS)]
            for k, n_c in enumerate(sizes):
                covers_more = valid > (sizes[k + 1] if k + 1 < len(sizes) else 0)
                fits = valid <= n_c

                @pl.when(jnp.logical_and(covers_more, fits))
                def _():
                    ffn(n_c)

            dst = off_s[e] + t * cpt
            for_each_piece(valid, lambda k, piece, start: tile_out(sl, dst, k, piece, start).start())
            state[1 + sl] = valid
            state[0] = state[0] + 1
            return carry
        lax.fori_loop(0, n_tiles, tile, 0)

    @pl.when(e == n_exp - 1)
    def _():
        drain(0)
        drain(1)


def _experts(off, per_exp, nxt, first, xs, wg, wu, wd):
    n_chunks, _, d = xs.shape
    n_exp, _, ff = wg.shape
    cpt = MOE_TM // MOE_CHUNK

    def w_blk(e, *_):
        return (e, 0, 0)

    grid_spec = pltpu.PrefetchScalarGridSpec(
        num_scalar_prefetch=4,
        grid=(n_exp,),
        in_specs=[pl.BlockSpec(memory_space=pl.ANY),
                  pl.BlockSpec((1, d, ff), w_blk), pl.BlockSpec((1, d, ff), w_blk),
                  pl.BlockSpec((1, ff, d), w_blk)],
        out_specs=pl.BlockSpec(memory_space=pl.ANY),
        scratch_shapes=[pltpu.VMEM((MOE_XBUFS, cpt, MOE_CHUNK, d), BF16),
                        pltpu.VMEM((2, cpt, MOE_CHUNK, d), BF16),
                        pltpu.VMEM((d, ff), BF16), pltpu.VMEM((d, ff), BF16),
                        pltpu.VMEM((ff, d), BF16),
                        pltpu.SMEM((3,), jnp.int32),
                        pltpu.SemaphoreType.DMA((MOE_XBUFS,)),
                        pltpu.SemaphoreType.DMA((2, len(_tile_pieces())))],
    )
    return pl.pallas_call(
        _expert_kernel,
        out_shape=jax.ShapeDtypeStruct((n_chunks, MOE_CHUNK, d), BF16),
        grid_spec=grid_spec,
        compiler_params=pltpu.CompilerParams(dimension_semantics=("arbitrary",),
                                             vmem_limit_bytes=V7X_VMEM_LIMIT),
        name="moe_experts",
    )(off, per_exp, nxt, first, xs, wg, wu, wd)


def _combine_kernel(src2_s, dst2_s, n2_s, src1_s, dst1_s, n1_s, nch_s,
                    gt_ref, qcol_ref, qbcol_ref, h_ref, x1_ref, gate2_ref,
                    wsg_ref, wsu_ref, wsd_ref, ys_ref, o_ref, stage_ref, sem):
    b = pl.program_id(0)
    nb = pl.num_programs(0)
    slot = lax.rem(b, 2)
    n_exp, tb = gt_ref.shape
    rmax = stage_ref.shape[1] * MOE_CHUNK

    def copy2(sl, staged, sorted_):
        return pltpu.make_async_copy(ys_ref.at[pl.ds(sorted_, 2)],
                                     stage_ref.at[sl, pl.ds(staged, 2)], sem.at[sl, 0])

    def copy1(sl, staged, sorted_):
        return pltpu.make_async_copy(ys_ref.at[sorted_], stage_ref.at[sl, staged], sem.at[sl, 1])

    def fetch(bb, sl):
        _start_each_copy(n2_s[bb], lambda i: copy2(sl, src2_s[bb, i], dst2_s[bb, i]))
        _start_each_copy(n1_s[bb], lambda i: copy1(sl, src1_s[bb, i], dst1_s[bb, i]))

    @pl.when(b == 0)
    def _():
        stage_ref[...] = jnp.zeros_like(stage_ref)
        fetch(0, 0)

    _for_each_chunk(n2_s[b], lambda i: copy2(slot, 0, 0).wait())
    _for_each_chunk(n1_s[b], lambda i: copy1(slot, 0, 0).wait())

    @pl.when(b + 1 < nb)
    def _():
        fetch(b + 1, 1 - slot)

    gt = gt_ref[...]
    routed = jnp.where(gt > 0.0, 1.0, 0.0).astype(BF16)
    i0 = lax.broadcasted_iota(jnp.int32, (tb, tb), 0)
    i1 = lax.broadcasted_iota(jnp.int32, (tb, tb), 1)
    eye = jnp.where(i0 == i1, 1.0, 0.0).astype(BF16)
    routed_t = _dot_nt(eye, routed)
    gates_tok = _dot_nt(eye, _bf(gt))
    earlier = jnp.where(i1 < i0, 1.0, 0.0).astype(BF16)
    pos_t = _dot(earlier, _bf(routed_t))
    posm_t = _bf(jnp.where(routed_t > 0.0, pos_t, -1.0))

    qcol = qcol_ref[0]
    qbcol = qbcol_ref[0]
    h = h_ref[...]
    a = _silu(_dot(h, wsg_ref[...])) * _dot(h, wsu_ref[...])
    shared = _dot(_bf(a), wsd_ref[...])

    def finish(rows):
        chunk = lax.broadcasted_iota(jnp.int32, (n_exp, rows), 1) >> MOE_CHUNK_SHIFT
        own = jnp.where(chunk >= qbcol, jnp.where(chunk < qbcol + qcol, 1.0, 0.0), 0.0)
        own_b = _bf(own)
        rank = _dot(posm_t, own_b)
        wexp = _dot(_bf(gates_tok), own_b)
        start = jnp.sum(own * qbcol.astype(F32), axis=0, keepdims=True) * MOE_CHUNK
        rel = lax.broadcasted_iota(jnp.int32, (1, rows), 1).astype(F32) - start
        weights = _bf(jnp.where(rank == rel, wexp, 0.0))
        staged = stage_ref[slot, :rows // MOE_CHUNK].reshape(rows, o_ref.shape[1])
        o_ref[...] = x1_ref[...] + gate2_ref[0] * (_dot(weights, staged) + shared)

    short = rmax - MOE_ROWGROUP

    @pl.when(nch_s[b] * MOE_CHUNK <= short)
    def _():
        finish(short)

    @pl.when(nch_s[b] * MOE_CHUNK > short)
    def _():
        finish(rmax)


def _combine(gates_t, q, qbase, copies, nch, h2, x1, mod3, wsg, wsu, wsd, ys, seq):
    n_exp, t = gates_t.shape
    d = h2.shape[1]
    nb = t // MOE_TB
    per_b = seq // MOE_TB
    stage_chunks = _moe_stage_rows(n_exp) // MOE_CHUNK

    def full(a):
        return pl.BlockSpec(a.shape, lambda b, *_: (0,) * a.ndim)

    def rows(w):
        return pl.BlockSpec((MOE_TB, w), lambda b, *_: (b, 0))

    grid_spec = pltpu.PrefetchScalarGridSpec(
        num_scalar_prefetch=7,
        grid=(nb,),
        in_specs=[
            pl.BlockSpec((n_exp, MOE_TB), lambda b, *_: (0, b)),
            pl.BlockSpec((1, n_exp, 1), lambda b, *_: (b, 0, 0)),
            pl.BlockSpec((1, n_exp, 1), lambda b, *_: (b, 0, 0)),
            rows(d), rows(d),
            pl.BlockSpec((1, 1, d), lambda b, *_: (b // per_b, 0, 5)),
            full(wsg), full(wsu), full(wsd),
            pl.BlockSpec(memory_space=pl.ANY),
        ],
        out_specs=rows(d),
        scratch_shapes=[pltpu.VMEM((2, stage_chunks, MOE_CHUNK, d), BF16),
                        pltpu.SemaphoreType.DMA((2, 2))],
    )
    return pl.pallas_call(
        _combine_kernel,
        out_shape=jax.ShapeDtypeStruct((t, d), F32),
        grid_spec=grid_spec,
        compiler_params=pltpu.CompilerParams(dimension_semantics=("arbitrary",),
                                             vmem_limit_bytes=V7X_VMEM_LIMIT),
        name="moe_combine",
    )(*copies, nch, gates_t, q.reshape(nb, n_exp, 1), qbase.reshape(nb, n_exp, 1),
      h2, x1, mod3, wsg, wsu, wsd, ys)


def _moe_plan(q):
    nb, n_exp = q.shape
    qbase = jnp.cumsum(q, axis=1) - q
    nch = jnp.sum(q, axis=1)
    per_exp = jnp.sum(q, axis=0)
    off = jnp.cumsum(per_exp) - per_exp
    dstq = off[None, :] + jnp.cumsum(q, axis=0) - q
    def copy_list(count, length, stage_start, sorted_start, step):
        cum = jnp.cumsum(count, axis=1) - count
        i = jnp.arange(length, dtype=jnp.int32)
        ii = i[None, :, None]
        owned = (ii >= cum[:, None, :]) & (ii < (cum + count)[:, None, :])

        def place(start):
            return step * i[None, :] + jnp.sum(
                jnp.where(owned, (start - step * cum)[:, None, :], 0), axis=2)
        return place(stage_start), place(sorted_start), jnp.sum(count, axis=1)

    pairs = q >> 1
    max_chunks = _moe_block_chunks(n_exp)
    src2, dst2, n2 = copy_list(pairs, max_chunks // 2, qbase, dstq, 2)
    src1, dst1, n1 = copy_list(q & 1, n_exp, qbase + 2 * pairs, dstq + 2 * pairs, 1)
    copies = (src2, dst2, n2, src1, dst1, n1)
    ids = jnp.arange(n_exp, dtype=jnp.int32)
    later = (ids[None, :] > ids[:, None]) & (per_exp[None, :] > 0)
    nxt = jnp.min(jnp.where(later, ids[None, :], n_exp), axis=1)
    first = jnp.min(jnp.where(per_exp > 0, ids, n_exp)).reshape(1)
    total = jnp.sum(per_exp).reshape(1)
    return qbase, copies, nch, off, per_exp, nxt, first, total


def _moe_block_chunks(n_exp):
    return MOE_TB * TOP_K // MOE_CHUNK + n_exp


def _moe_stage_rows(n_exp):
    return -(-_moe_block_chunks(n_exp) * MOE_CHUNK // MOE_ROWGROUP) * MOE_ROWGROUP


def _moe(h2, gates_t, q3, x1, mod3, wsg, wsu, wsd, wg, wu, wd, seq):
    n_exp, t = gates_t.shape
    nb = t // MOE_TB
    q = q3.reshape(nb, n_exp)
    qbase, copies, nch, off, per_exp, nxt, first, total = _moe_plan(q)
    n_chunks = nb * _moe_block_chunks(n_exp) + MOE_TM // MOE_CHUNK
    xs = _dispatch(gates_t, q, qbase, copies, nch, total, h2, n_chunks)
    ys = _experts(off, per_exp, nxt, first, xs, wg, wu, wd)
    return _combine(gates_t, q, qbase, copies, nch, h2, x1, mod3, wsg, wsu, wsd, ys, seq)


def _pad_heads(w, heads, width):
    lead = w.shape[:-1]
    w = w.reshape(lead + (heads, width))
    w = jnp.pad(w, [(0, 0)] * len(lead) + [(0, 0), (0, LANES - width)])
    return w.reshape(lead + (heads * LANES,))


def kernel(x, c, positions, w_ada, b_ada, g_norm1, w_in, g_na_q, g_na_k, na_rpb, g_q_lat, w_uq,
           g_kv_lat, w_ukv, g_mla_q, g_mla_k, w_proj_na, w_proj_mla, w_out, g_norm2, w_router,
           e_bias, w_exp_gate, w_exp_up, w_exp_down, w_sh_gate, w_sh_up, w_sh_down):
    bsz, seq, d = x.shape
    t = bsz * seq
    depth = w_ada.shape[0]
    na_w = NA_HEADS * NA_HEAD_DIM
    q_rank = g_q_lat.shape[1]
    kv_rank = g_kv_lat.shape[1]
    n_rows = seq // GRID_W

    pos = positions.reshape(1, t)
    half = MLA_ROPE_DIM // 2
    freq = (ROPE_THETA ** (-jnp.arange(half, dtype=F32) / half)).reshape(half, 1)

    x2 = x.reshape(t, d)
    for l in range(depth):
        mod3 = _adaln(c, w_ada[l], b_ada[l]).reshape(bsz, 1, 6 * d)

        wi = w_in[l]
        o_lat = 3 * na_w
        o_rot = o_lat + q_rank + kv_rank
        o_gate = o_rot + MLA_ROPE_DIM
        wqkv = _bf(wi[:, :o_lat])
        w_rot = jnp.pad(wi[:, o_rot:o_gate], ((0, 0), (MLA_NOPE_DIM, LANES - MLA_QK_DIM)))
        wlat = _bf(jnp.concatenate([wi[:, o_lat:o_rot], w_rot], axis=1))
        wgate = _bf(wi[:, o_gate:])
        gq = (jnp.tile(g_na_q[l], NA_HEADS) * (NA_HEAD_DIM ** -0.5 * LOG2E)).reshape(1, na_w)
        gk = jnp.tile(g_na_k[l], NA_HEADS).reshape(1, na_w)
        wuq = _bf(_pad_heads(w_uq[l], MLA_HEADS, MLA_QK_DIM))
        wukv = w_ukv[l].reshape(kv_rank, MLA_HEADS, MLA_NOPE_DIM + MLA_V_DIM)
        wuk = _bf(_pad_heads(wukv[:, :, :MLA_NOPE_DIM].reshape(kv_rank, -1), MLA_HEADS, MLA_NOPE_DIM))
        wuv = _bf(wukv[:, :, MLA_NOPE_DIM:].reshape(kv_rank, MLA_HEADS * MLA_V_DIM))
        gmq = _pad_heads(jnp.tile(g_mla_q[l], MLA_HEADS) * (MLA_QK_DIM ** -0.5 * LOG2E),
                         MLA_HEADS, MLA_QK_DIM).reshape(1, -1)
        gmk = _pad_heads(jnp.tile(g_mla_k[l], MLA_HEADS), MLA_HEADS, MLA_QK_DIM).reshape(1, -1)

        qa, ka, va, qm, km, vm, sgn, sgm = _inproj(
            x2, mod3, g_norm1[l].reshape(1, d), wqkv, wlat, wgate, gq, gk,
            g_q_lat[l].reshape(1, q_rank), g_kv_lat[l].reshape(1, kv_rank), wuq, wuk, wuv,
            gmq, gmk, pos, freq, seq)

        bias = _na_bias(na_rpb[l], n_rows)
        y_na = _na_attention(qa, ka, va, bias, bsz, seq)
        y_mla = _mla_attention(qm, km, vm, bsz, seq)

        x1, h2, lt = _merge(x2, y_na, y_mla, sgn, sgm, _bf(w_proj_na[l]), _bf(w_proj_mla[l]),
                            _bf(w_out[l]), mod3, g_norm2[l].reshape(1, d), w_router[l].T, seq)
        gates_t, q3 = _route(lt, e_bias[l])
        x2 = _moe(h2, gates_t, q3, x1, mod3, _bf(w_sh_gate[l]), _bf(w_sh_up[l]),
                  _bf(w_sh_down[l]), w_exp_gate[l], w_exp_up[l], w_exp_down[l], seq)
    return x2.reshape(bsz, seq, d)
```
